```python
import jax, jax.numpy as jnp
from jax import lax
import numpy as np

D_MODEL = 1024
BATCH = 8
SEQ = 16384
DEPTH = 4

CHUNK = 64
Q_BLOCK = 128
SGU_CHUNK = 128
HEAD_DIM = 64
D_ATTN = D_MODEL // 2
N_ATTN_HEADS = D_ATTN // HEAD_DIM
D_SGU = D_MODEL // 2
SGU_GROUP_DIM = 64
N_SGU_GROUPS = D_SGU // SGU_GROUP_DIM
D_MIX = D_ATTN + D_SGU
D_IN = 3 * D_ATTN + N_ATTN_HEADS + 2 * D_SGU
D_FF = -(-8 * D_MODEL // (3 * 256)) * 256
EPS = 1e-6

kernel_name = 'fox_gmlp_hybrid_trunk'


def rms_norm(x, g):
    xf = x.astype(jnp.float32)
    y = xf * lax.rsqrt(jnp.mean(xf * xf, axis=-1, keepdims=True) + EPS)
    return (y * g.astype(jnp.float32)).astype(x.dtype)


def layer_norm(x, g, b):
    xf = x.astype(jnp.float32)
    mu = jnp.mean(xf, axis=-1, keepdims=True)
    xc = xf - mu
    y = xc * lax.rsqrt(jnp.mean(xc * xc, axis=-1, keepdims=True) + EPS)
    return (y * g.astype(jnp.float32) + b.astype(jnp.float32)).astype(x.dtype)


def forgetting_attention(q, k, v, log_f):
    B, S, H, Dh = q.shape
    nb = S // Q_BLOCK
    c_bhs = jnp.cumsum(log_f, axis=1).transpose(0, 2, 1)
    q_blocks = q.reshape(B, nb, Q_BLOCK, H, Dh).transpose(1, 0, 2, 3, 4)
    cq_blocks = c_bhs.reshape(B, H, nb, Q_BLOCK).transpose(2, 0, 1, 3)
    starts = jnp.arange(nb, dtype=jnp.int32) * Q_BLOCK
    key_pos = jnp.arange(S, dtype=jnp.int32)
    scale = Dh ** -0.5

    def one_block(args):
        qb, cqb, start = args
        s = jnp.einsum('bqhd,bkhd->bhqk', qb, k).astype(jnp.float32) * scale
        s = s + cqb[..., :, None] - c_bhs[..., None, :]
        q_pos = start + jnp.arange(Q_BLOCK, dtype=jnp.int32)
        causal = key_pos[None, :] <= q_pos[:, None]
        s = jnp.where(causal, s, -jnp.inf)
        p = jax.nn.softmax(s, axis=-1).astype(v.dtype)
        return jnp.einsum('bhqk,bkhd->bqhd', p, v)

    out = lax.map(one_block, (q_blocks, cq_blocks, starts))
    return out.transpose(1, 0, 2, 3, 4).reshape(B, S, H * Dh)


def spatial_gating(z, ln_g, ln_b, w_s, b_s):
    B, S, _ = z.shape
    zu, zv = jnp.split(z, 2, axis=-1)
    zv = layer_norm(zv, ln_g, ln_b)
    nc = S // SGU_CHUNK
    zv = zv.reshape(B, nc, SGU_CHUNK, N_SGU_GROUPS, SGU_GROUP_DIM)
    pos = jnp.arange(SGU_CHUNK, dtype=jnp.int32) // CHUNK
    mask = (pos[None, :] <= pos[:, None]).astype(w_s.dtype)
    w = w_s * mask[None]
    mixed = jnp.einsum('gij,bcjgd->bcigd', w, zv) + b_s.T[None, None, :, :, None]
    return zu * mixed.reshape(B, S, D_SGU)


def _fwd_setup_inputs(seed: int = 0) -> dict:
    key = jax.random.key(seed)
    ks = jax.random.split(key, 14)
    f32 = jnp.float32
    nrm = lambda k, shape, s: jax.random.normal(k, shape, f32) * s
    head_bias = jnp.linspace(1.0, 5.0, N_ATTN_HEADS, dtype=f32)
    return {
        'x': jax.random.normal(ks[0], (BATCH, SEQ, D_MODEL), f32),
        'mix_norm_g': 1.0 + nrm(ks[1], (DEPTH, D_MODEL), 0.05),
        'w_in': nrm(ks[2], (DEPTH, D_MODEL, D_IN), D_MODEL ** -0.5),
        'b_f': head_bias[None, :] + nrm(ks[3], (DEPTH, N_ATTN_HEADS), 0.1),
        'sgu_ln_g': 1.0 + nrm(ks[4], (DEPTH, D_SGU), 0.05),
        'sgu_ln_b': nrm(ks[5], (DEPTH, D_SGU), 0.02),
        'w_s': nrm(ks[6], (DEPTH, N_SGU_GROUPS, SGU_CHUNK, SGU_CHUNK), 0.5 * SGU_CHUNK ** -0.5),
        'b_s': 1.0 + nrm(ks[7], (DEPTH, N_SGU_GROUPS, SGU_CHUNK), 0.1),
        'out_norm_g': 1.0 + nrm(ks[8], (DEPTH, D_MIX), 0.05),
        'w_out': nrm(ks[9], (DEPTH, D_MIX, D_MODEL), D_MIX ** -0.5),
        'ffn_norm_g': 1.0 + nrm(ks[10], (DEPTH, D_MODEL), 0.05),
        'w_gate_up': nrm(ks[11], (DEPTH, D_MODEL, 2 * D_FF), D_MODEL ** -0.5),
        'w_down': nrm(ks[12], (DEPTH, D_FF, D_MODEL), D_FF ** -0.5),
        'final_norm_g': 1.0 + nrm(ks[13], (D_MODEL,), 0.05),
    }


def _fwd_reference(x, mix_norm_g, w_in, b_f, sgu_ln_g, sgu_ln_b, w_s, b_s, out_norm_g, w_out,
              ffn_norm_g, w_gate_up, w_down, final_norm_g):
    B, S, _ = x.shape
    for l in range(DEPTH):
        xn = rms_norm(x, mix_norm_g[l])
        h = xn @ w_in[l]
        q, k, v, f_logit, z = jnp.split(
            h, [D_ATTN, 2 * D_ATTN, 3 * D_ATTN, 3 * D_ATTN + N_ATTN_HEADS], axis=-1)
        q = q.reshape(B, S, N_ATTN_HEADS, HEAD_DIM)
        k = k.reshape(B, S, N_ATTN_HEADS, HEAD_DIM)
        v = v.reshape(B, S, N_ATTN_HEADS, HEAD_DIM)
        log_f = jax.nn.log_sigmoid(f_logit.astype(jnp.float32) + b_f[l].astype(jnp.float32))
        attn = forgetting_attention(q, k, v, log_f)
        sgu = spatial_gating(jax.nn.gelu(z, approximate=False),
                             sgu_ln_g[l], sgu_ln_b[l], w_s[l], b_s[l])
        merged = jnp.concatenate(
            [rms_norm(attn, out_norm_g[l, :D_ATTN]), rms_norm(sgu, out_norm_g[l, D_ATTN:])], axis=-1)
        x = x + merged @ w_out[l]
        xn = rms_norm(x, ffn_norm_g[l])
        gate, up = jnp.split(xn @ w_gate_up[l], 2, axis=-1)
        x = x + (jax.nn.silu(gate) * up) @ w_down[l]
    return rms_norm(x, final_norm_g)


import jax as _jax
import jax.numpy as _jnp

TWIN_FORMAT = 'train_step'
FWD_PARAMS = ['x', 'mix_norm_g', 'w_in', 'b_f', 'sgu_ln_g', 'sgu_ln_b', 'w_s', 'b_s', 'out_norm_g', 'w_out', 'ffn_norm_g', 'w_gate_up', 'w_down', 'final_norm_g']
TWIN_WEIGHTS = ['mix_norm_g', 'w_in', 'b_f', 'sgu_ln_g', 'sgu_ln_b', 'w_s', 'b_s', 'out_norm_g', 'w_out', 'ffn_norm_g', 'w_gate_up', 'w_down', 'final_norm_g']
TWIN_DIFF_INPUT = 'x'
TWIN_INPUTS = ['x', 'mix_norm_g', 'w_in', 'b_f', 'sgu_ln_g', 'sgu_ln_b', 'w_s', 'b_s', 'out_norm_g', 'w_out', 'ffn_norm_g', 'w_gate_up', 'w_down', 'final_norm_g', 'loss_target', 'm_mix_norm_g', 'm_w_in', 'm_b_f', 'm_sgu_ln_g', 'm_sgu_ln_b', 'm_w_s', 'm_b_s', 'm_out_norm_g', 'm_w_out', 'm_ffn_norm_g', 'm_w_gate_up', 'm_w_down', 'm_final_norm_g', 'v_mix_norm_g', 'v_w_in', 'v_b_f', 'v_sgu_ln_g', 'v_sgu_ln_b', 'v_w_s', 'v_b_s', 'v_out_norm_g', 'v_w_out', 'v_ffn_norm_g', 'v_w_gate_up', 'v_w_down', 'v_final_norm_g']
TWIN_OUTPUTS = ['loss', 'grad_x', 'grad_mix_norm_g', 'grad_w_in', 'grad_b_f', 'grad_sgu_ln_g', 'grad_sgu_ln_b', 'grad_w_s', 'grad_b_s', 'grad_out_norm_g', 'grad_w_out', 'grad_ffn_norm_g', 'grad_w_gate_up', 'grad_w_down', 'grad_final_norm_g', 'delta_mix_norm_g', 'delta_w_in', 'delta_b_f', 'delta_sgu_ln_g', 'delta_sgu_ln_b', 'delta_w_s', 'delta_b_s', 'delta_out_norm_g', 'delta_w_out', 'delta_ffn_norm_g', 'delta_w_gate_up', 'delta_w_down', 'delta_final_norm_g', 'new_m_mix_norm_g', 'new_m_w_in', 'new_m_b_f', 'new_m_sgu_ln_g', 'new_m_sgu_ln_b', 'new_m_w_s', 'new_m_b_s', 'new_m_out_norm_g', 'new_m_w_out', 'new_m_ffn_norm_g', 'new_m_w_gate_up', 'new_m_w_down', 'new_m_final_norm_g', 'new_v_mix_norm_g', 'new_v_w_in', 'new_v_b_f', 'new_v_sgu_ln_g', 'new_v_sgu_ln_b', 'new_v_w_s', 'new_v_b_s', 'new_v_out_norm_g', 'new_v_w_out', 'new_v_ffn_norm_g', 'new_v_w_gate_up', 'new_v_w_down', 'new_v_final_norm_g']
TWIN_LEAF_KINDS = {'loss': 'loss', 'grad_x': 'grad_x', 'grad_mix_norm_g': 'grad_w', 'grad_w_in': 'grad_w', 'grad_b_f': 'grad_w', 'grad_sgu_ln_g': 'grad_w', 'grad_sgu_ln_b': 'grad_w', 'grad_w_s': 'grad_w', 'grad_b_s': 'grad_w', 'grad_out_norm_g': 'grad_w', 'grad_w_out': 'grad_w', 'grad_ffn_norm_g': 'grad_w', 'grad_w_gate_up': 'grad_w', 'grad_w_down': 'grad_w', 'grad_final_norm_g': 'grad_w', 'delta_mix_norm_g': 'delta_w', 'delta_w_in': 'delta_w', 'delta_b_f': 'delta_w', 'delta_sgu_ln_g': 'delta_w', 'delta_sgu_ln_b': 'delta_w', 'delta_w_s': 'delta_w', 'delta_b_s': 'delta_w', 'delta_out_norm_g': 'delta_w', 'delta_w_out': 'delta_w', 'delta_ffn_norm_g': 'delta_w', 'delta_w_gate_up': 'delta_w', 'delta_w_down': 'delta_w', 'delta_final_norm_g': 'delta_w', 'new_m_mix_norm_g': 'new_m', 'new_m_w_in': 'new_m', 'new_m_b_f': 'new_m', 'new_m_sgu_ln_g': 'new_m', 'new_m_sgu_ln_b': 'new_m', 'new_m_w_s': 'new_m', 'new_m_b_s': 'new_m', 'new_m_out_norm_g': 'new_m', 'new_m_w_out': 'new_m', 'new_m_ffn_norm_g': 'new_m', 'new_m_w_gate_up': 'new_m', 'new_m_w_down': 'new_m', 'new_m_final_norm_g': 'new_m', 'new_v_mix_norm_g': 'new_v', 'new_v_w_in': 'new_v', 'new_v_b_f': 'new_v', 'new_v_sgu_ln_g': 'new_v', 'new_v_sgu_ln_b': 'new_v', 'new_v_w_s': 'new_v', 'new_v_b_s': 'new_v', 'new_v_out_norm_g': 'new_v', 'new_v_w_out': 'new_v', 'new_v_ffn_norm_g': 'new_v', 'new_v_w_gate_up': 'new_v', 'new_v_w_down': 'new_v', 'new_v_final_norm_g': 'new_v'}


def _forward(args):
    return _fwd_reference(*[args[k] for k in FWD_PARAMS])


def _output_shape():
    def fwd():
        inp = _fwd_setup_inputs(0)
        return _fwd_reference(*[inp[k] for k in FWD_PARAMS])
    out = _jax.eval_shape(fwd)
    return out.shape, out.dtype

N_MICROBATCH = 1
ADAM_LR = 0.001
ADAM_B1 = 0.9
ADAM_B2 = 0.999
ADAM_EPS = 1e-08
ADAM_WD = 0.01
ADAM_STEP = 10
PER_EXAMPLE_BATCH_AXIS = {'x': 0, 'loss_target': 0}
SHARED_INPUTS = []
_WEIGHT_DTYPES = {'mix_norm_g': _jnp.float32, 'w_in': _jnp.float32, 'b_f': _jnp.float32, 'sgu_ln_g': _jnp.float32, 'sgu_ln_b': _jnp.float32, 'w_s': _jnp.float32, 'b_s': _jnp.float32, 'out_norm_g': _jnp.float32, 'w_out': _jnp.float32, 'ffn_norm_g': _jnp.float32, 'w_gate_up': _jnp.float32, 'w_down': _jnp.float32, 'final_norm_g': _jnp.float32}
MOMENT_SCALE = {'mix_norm_g': 4.906462e-01, 'w_in': 3.018517e-01, 'b_f': 1.089542e+00, 'sgu_ln_g': 8.846094e-02, 'sgu_ln_b': 8.696744e-02, 'w_s': 1.279229e-01, 'b_s': 1.415919e-01, 'out_norm_g': 7.840217e-01, 'w_out': 7.495235e-01, 'ffn_norm_g': 2.197440e-01, 'w_gate_up': 8.783931e-02, 'w_down': 1.541277e-01, 'final_norm_g': 1.310203e+02}


def _to_microbatches(a, axis):
    t = _jnp.moveaxis(a, axis, 0)
    t = t.reshape((N_MICROBATCH, t.shape[0] // N_MICROBATCH) + t.shape[1:])
    return _jnp.moveaxis(t, 1, axis + 1)


def setup_inputs(seed: int = 0) -> dict:
    inp = _fwd_setup_inputs(seed)
    key = _jax.random.fold_in(_jax.random.key(seed), 7919)
    shape, _ = _output_shape()
    out = dict(inp)
    out["loss_target"] = _jax.random.normal(_jax.random.fold_in(key, 0), shape, _jnp.float32)
    for i, name in enumerate(TWIN_WEIGHTS):
        w = inp[name].astype(_jnp.float32)
        if MOMENT_SCALE is None:
            s = _jnp.sqrt(_jnp.mean(_jnp.square(w)) + 1e-30)
        else:
            s = MOMENT_SCALE[name]
        km, kv = _jax.random.split(_jax.random.fold_in(key, i + 1))
        out[name] = w
        out["m_" + name] = s * _jax.random.normal(km, w.shape, _jnp.float32)
        out["v_" + name] = (s * s) * _jax.random.uniform(kv, w.shape, _jnp.float32, 0.5, 1.5)
    if N_MICROBATCH > 1:
        for name, axis in PER_EXAMPLE_BATCH_AXIS.items():
            out[name] = _to_microbatches(out[name], axis)
    return {'x': out['x'], 'mix_norm_g': out['mix_norm_g'], 'w_in': out['w_in'], 'b_f': out['b_f'], 'sgu_ln_g': out['sgu_ln_g'], 'sgu_ln_b': out['sgu_ln_b'], 'w_s': out['w_s'], 'b_s': out['b_s'], 'out_norm_g': out['out_norm_g'], 'w_out': out['w_out'], 'ffn_norm_g': out['ffn_norm_g'], 'w_gate_up': out['w_gate_up'], 'w_down': out['w_down'], 'final_norm_g': out['final_norm_g'], 'loss_target': out['loss_target'], 'm_mix_norm_g': out['m_mix_norm_g'], 'm_w_in': out['m_w_in'], 'm_b_f': out['m_b_f'], 'm_sgu_ln_g': out['m_sgu_ln_g'], 'm_sgu_ln_b': out['m_sgu_ln_b'], 'm_w_s': out['m_w_s'], 'm_b_s': out['m_b_s'], 'm_out_norm_g': out['m_out_norm_g'], 'm_w_out': out['m_w_out'], 'm_ffn_norm_g': out['m_ffn_norm_g'], 'm_w_gate_up': out['m_w_gate_up'], 'm_w_down': out['m_w_down'], 'm_final_norm_g': out['m_final_norm_g'], 'v_mix_norm_g': out['v_mix_norm_g'], 'v_w_in': out['v_w_in'], 'v_b_f': out['v_b_f'], 'v_sgu_ln_g': out['v_sgu_ln_g'], 'v_sgu_ln_b': out['v_sgu_ln_b'], 'v_w_s': out['v_w_s'], 'v_b_s': out['v_b_s'], 'v_out_norm_g': out['v_out_norm_g'], 'v_w_out': out['v_w_out'], 'v_ffn_norm_g': out['v_ffn_norm_g'], 'v_w_gate_up': out['v_w_gate_up'], 'v_w_down': out['v_w_down'], 'v_final_norm_g': out['v_final_norm_g']}


def _loss(weights, diff, rest, loss_target):
    with _jax.named_scope("forward"):
        args = {**rest, TWIN_DIFF_INPUT: diff, **{k: w.astype(_WEIGHT_DTYPES[k]) for k, w in weights.items()}}
        y = _forward(args)
    with _jax.named_scope("loss_head"):
        err = _jnp.square(y.astype(_jnp.float32) - loss_target)
        return 0.5 * _jnp.sum(_jnp.mean(err, axis=-1)) if err.ndim else 0.5 * err


def _adamw(w, g, m, v):
    m = ADAM_B1 * m + (1.0 - ADAM_B1) * g
    v = ADAM_B2 * v + (1.0 - ADAM_B2) * _jnp.square(g)
    m_hat = m / (1.0 - ADAM_B1 ** ADAM_STEP)
    v_hat = v / (1.0 - ADAM_B2 ** ADAM_STEP)
    delta = -ADAM_LR * (m_hat / (_jnp.sqrt(v_hat) + ADAM_EPS) + ADAM_WD * w)
    return delta, m, v


def reference(x, mix_norm_g, w_in, b_f, sgu_ln_g, sgu_ln_b, w_s, b_s, out_norm_g, w_out, ffn_norm_g, w_gate_up, w_down, final_norm_g, loss_target, m_mix_norm_g, m_w_in, m_b_f, m_sgu_ln_g, m_sgu_ln_b, m_w_s, m_b_s, m_out_norm_g, m_w_out, m_ffn_norm_g, m_w_gate_up, m_w_down, m_final_norm_g, v_mix_norm_g, v_w_in, v_b_f, v_sgu_ln_g, v_sgu_ln_b, v_w_s, v_b_s, v_out_norm_g, v_w_out, v_ffn_norm_g, v_w_gate_up, v_w_down, v_final_norm_g):
    given = dict(x=x, mix_norm_g=mix_norm_g, w_in=w_in, b_f=b_f, sgu_ln_g=sgu_ln_g, sgu_ln_b=sgu_ln_b, w_s=w_s, b_s=b_s, out_norm_g=out_norm_g, w_out=w_out, ffn_norm_g=ffn_norm_g, w_gate_up=w_gate_up, w_down=w_down, final_norm_g=final_norm_g, loss_target=loss_target, m_mix_norm_g=m_mix_norm_g, m_w_in=m_w_in, m_b_f=m_b_f, m_sgu_ln_g=m_sgu_ln_g, m_sgu_ln_b=m_sgu_ln_b, m_w_s=m_w_s, m_b_s=m_b_s, m_out_norm_g=m_out_norm_g, m_w_out=m_w_out, m_ffn_norm_g=m_ffn_norm_g, m_w_gate_up=m_w_gate_up, m_w_down=m_w_down, m_final_norm_g=m_final_norm_g, v_mix_norm_g=v_mix_norm_g, v_w_in=v_w_in, v_b_f=v_b_f, v_sgu_ln_g=v_sgu_ln_g, v_sgu_ln_b=v_sgu_ln_b, v_w_s=v_w_s, v_b_s=v_b_s, v_out_norm_g=v_out_norm_g, v_w_out=v_w_out, v_ffn_norm_g=v_ffn_norm_g, v_w_gate_up=v_w_gate_up, v_w_down=v_w_down, v_final_norm_g=v_final_norm_g)
    weights = {n: given[n] for n in TWIN_WEIGHTS}
    shared = {n: given[n] for n in SHARED_INPUTS}
    per_example = {n: given[n] for n in ['x']}
    grad_fn = _jax.value_and_grad(_loss, argnums=(0, 1))

    def one_microbatch(ex, loss_target):
        ex = dict(ex)
        diff = ex.pop(TWIN_DIFF_INPUT)
        return grad_fn(weights, diff, {**shared, **ex}, loss_target)

    if N_MICROBATCH == 1:
        loss, (grad_w, grad_x) = one_microbatch(per_example, given["loss_target"])
    else:
        def body(carry, xs):
            loss_sum, grad_sum = carry
            l_k, (gw_k, gx_k) = one_microbatch(xs[0], xs[1])
            with _jax.named_scope("update"):
                return (loss_sum + l_k, _jax.tree.map(_jnp.add, grad_sum, gw_k)), gx_k

        init = (_jnp.zeros((), _jnp.float32), _jax.tree.map(_jnp.zeros_like, weights))
        (loss, grad_w), grad_x = _jax.lax.scan(body, init, (per_example, given["loss_target"]))
    with _jax.named_scope("update"):
        delta_w, new_m, new_v = {}, {}, {}
        for n in TWIN_WEIGHTS:
            delta_w[n], new_m[n], new_v[n] = _adamw(weights[n], grad_w[n], given["m_" + n], given["v_" + n])
    return (loss, grad_x, *[grad_w[n] for n in TWIN_WEIGHTS], *[delta_w[n] for n in TWIN_WEIGHTS],
            *[new_m[n] for n in TWIN_WEIGHTS], *[new_v[n] for n in TWIN_WEIGHTS])
```

```python
import jax
import jax.numpy as jnp
from jax import lax
from jax.experimental import pallas as pl
from jax.experimental.pallas import tpu as pltpu

F32 = jnp.float32
BF16 = jnp.bfloat16

D_MODEL = 1024
D_HALF = 512
N_HEADS = 8
HEAD_DIM = 64
SGU_CHUNK = 128
CAUSAL_CHUNK = 64
D_FF = 2816
D_IN = 2568
D_IN_PAD = 2688
F_COL_BLOCK = 2560 // 128
EPS = 1e-6
NEG = -1e30
LANES = 128
VMEM_LIMIT = 56 * 1024 * 1024

ADAM_LR = 0.001
ADAM_B1 = 0.9
ADAM_B2 = 0.999
ADAM_EPS = 1e-08
ADAM_WD = 0.01
ADAM_STEP = 10

MESH = pl.DeviceIdType.MESH
ANY = pl.BlockSpec(memory_space=pl.ANY)


def _params(*sem):
    return pltpu.CompilerParams(dimension_semantics=sem, vmem_limit_bytes=VMEM_LIMIT)


def _matmul(a, b, *, trans_b=False, tm, tn, out_dtype, residual=None, name):
    m, k = a.shape
    n = b.shape[0] if trans_b else b.shape[1]
    dims = (((1,), (1,)), ((), ())) if trans_b else (((1,), (0,)), ((), ()))

    def body(*refs):
        a_ref, b_ref = refs[0], refs[1]
        o_ref = refs[-1]
        acc = lax.dot_general(a_ref[...].astype(BF16), b_ref[...].astype(BF16), dims,
                              preferred_element_type=F32)
        if residual is not None:
            acc = acc + refs[2][...]
        o_ref[...] = acc.astype(out_dtype)

    b_spec = pl.BlockSpec((tn, k), lambda i, j: (j, 0)) if trans_b else pl.BlockSpec((k, tn), lambda i, j: (0, j))
    in_specs = [pl.BlockSpec((tm, k), lambda i, j: (i, 0)), b_spec]
    args = [a, b]
    if residual is not None:
        in_specs.append(pl.BlockSpec((tm, tn), lambda i, j: (i, j)))
        args.append(residual)
    return pl.pallas_call(
        body, name=name, grid=(m // tm, n // tn), in_specs=in_specs,
        out_specs=pl.BlockSpec((tm, tn), lambda i, j: (i, j)),
        out_shape=jax.ShapeDtypeStruct((m, n), out_dtype),
        compiler_params=_params("parallel", "parallel"),
    )(*args)


def _matmul_tn(a, b, *, tm, tn, ts, name):
    s, m = a.shape
    n = b.shape[1]

    def body(a_ref, b_ref, o_ref):
        @pl.when(pl.program_id(2) == 0)
        def _():
            o_ref[...] = jnp.zeros_like(o_ref)

        o_ref[...] += lax.dot_general(a_ref[...].astype(BF16), b_ref[...].astype(BF16),
                                      (((0,), (0,)), ((), ())), preferred_element_type=F32)

    return pl.pallas_call(
        body, name=name, grid=(m // tm, n // tn, s // ts),
        in_specs=[pl.BlockSpec((ts, tm), lambda i, j, t: (t, i)), pl.BlockSpec((ts, tn), lambda i, j, t: (t, j))],
        out_specs=pl.BlockSpec((tm, tn), lambda i, j, t: (i, j)),
        out_shape=jax.ShapeDtypeStruct((m, n), F32),
        compiler_params=_params("parallel", "parallel", "arbitrary"),
    )(a, b)


def _rms_fwd(xs, g, *, out_dtype, tr, name):
    s = xs[0].shape[0]
    widths = [x.shape[1] for x in xs]
    wsum = sum(widths)
    nx = len(xs)

    def body(*refs):
        g_ref, o_ref = refs[nx], refs[nx + 1]
        off = 0
        for x_ref, w in zip(refs[:nx], widths):
            x = x_ref[...]
            r = lax.rsqrt(jnp.mean(x * x, axis=-1, keepdims=True) + EPS)
            o_ref[:, off:off + w] = (x * r * g_ref[:, off:off + w]).astype(out_dtype)
            off += w

    return pl.pallas_call(
        body, name=name, grid=(s // tr,),
        in_specs=[pl.BlockSpec((tr, w), lambda i: (i, 0)) for w in widths] + [pl.BlockSpec((1, wsum), lambda i: (0, 0))],
        out_specs=pl.BlockSpec((tr, wsum), lambda i: (i, 0)),
        out_shape=jax.ShapeDtypeStruct((s, wsum), out_dtype),
        compiler_params=_params("parallel"),
    )(*xs, g)


def _rms_bwd(dy, xs, g, *, residual=None, tr, name):
    s = xs[0].shape[0]
    widths = [x.shape[1] for x in xs]
    wsum = sum(widths)
    nx = len(xs)
    nin = 2 + nx + (residual is not None)

    def body(*refs):
        dy_ref, g_ref = refs[0], refs[1 + nx]
        dx_refs, dg_ref = refs[nin:nin + nx], refs[nin + nx]

        @pl.when(pl.program_id(0) == 0)
        def _():
            dg_ref[...] = jnp.zeros_like(dg_ref)

        off = 0
        for idx, (x_ref, w) in enumerate(zip(refs[1:1 + nx], widths)):
            x = x_ref[...]
            r = lax.rsqrt(jnp.mean(x * x, axis=-1, keepdims=True) + EPS)
            xh = x * r
            dyv = dy_ref[:, off:off + w]
            dxh = dyv * g_ref[:, off:off + w]
            dx = r * (dxh - xh * jnp.mean(dxh * xh, axis=-1, keepdims=True))
            if residual is not None and idx == 0:
                dx = dx + refs[2 + nx][...]
            dx_refs[idx][...] = dx
            dg_ref[:, off:off + w] += jnp.sum(dyv * xh, axis=0, keepdims=True)
            off += w

    in_specs = ([pl.BlockSpec((tr, wsum), lambda i: (i, 0))]
                + [pl.BlockSpec((tr, w), lambda i: (i, 0)) for w in widths]
                + [pl.BlockSpec((1, wsum), lambda i: (0, 0))])
    args = [dy, *xs, g]
    if residual is not None:
        in_specs.append(pl.BlockSpec((tr, widths[0]), lambda i: (i, 0)))
        args.append(residual)
    outs = pl.pallas_call(
        body, name=name, grid=(s // tr,), in_specs=in_specs,
        out_specs=[pl.BlockSpec((tr, w), lambda i: (i, 0)) for w in widths] + [pl.BlockSpec((1, wsum), lambda i: (0, 0))],
        out_shape=[jax.ShapeDtypeStruct((s, w), F32) for w in widths] + [jax.ShapeDtypeStruct((1, wsum), F32)],
        compiler_params=_params("arbitrary"),
    )(*args)
    return outs[:nx], outs[nx]


def _loss_head(x, tgt, g, *, tr, name):
    s, d = x.shape

    def body(x_ref, t_ref, g_ref, loss_ref, dx_ref, dg_ref):
        @pl.when(pl.program_id(0) == 0)
        def _():
            loss_ref[...] = jnp.zeros_like(loss_ref)
            dg_ref[...] = jnp.zeros_like(dg_ref)

        xv = x_ref[...]
        r = lax.rsqrt(jnp.mean(xv * xv, axis=-1, keepdims=True) + EPS)
        xh = xv * r
        err = xh * g_ref[...] - t_ref[...]
        loss_ref[...] += 0.5 * jnp.sum(jnp.mean(err * err, axis=-1, keepdims=True))
        dy = err * (1.0 / d)
        dxh = dy * g_ref[...]
        dx_ref[...] = r * (dxh - xh * jnp.mean(dxh * xh, axis=-1, keepdims=True))
        dg_ref[...] += jnp.sum(dy * xh, axis=0, keepdims=True)

    return pl.pallas_call(
        body, name=name, grid=(s // tr,),
        in_specs=[pl.BlockSpec((tr, d), lambda i: (i, 0)), pl.BlockSpec((tr, d), lambda i: (i, 0)),
                  pl.BlockSpec((1, d), lambda i: (0, 0))],
        out_specs=[pl.BlockSpec((8, LANES), lambda i: (0, 0)), pl.BlockSpec((tr, d), lambda i: (i, 0)),
                   pl.BlockSpec((1, d), lambda i: (0, 0))],
        out_shape=[jax.ShapeDtypeStruct((8, LANES), F32), jax.ShapeDtypeStruct((s, d), F32),
                   jax.ShapeDtypeStruct((1, d), F32)],
        compiler_params=_params("arbitrary"),
    )(x, tgt, g)


def _gates_fwd(h, bf_pad, *, tr, name):
    s = h.shape[0]

    def body(fl_ref, b_ref, c_ref, carry_ref):
        @pl.when(pl.program_id(0) == 0)
        def _():
            carry_ref[...] = jnp.zeros_like(carry_ref)

        lf = jax.nn.log_sigmoid(fl_ref[...] + b_ref[...])
        row = lax.broadcasted_iota(jnp.int32, (tr, tr), 0)
        col = lax.broadcasted_iota(jnp.int32, (tr, tr), 1)
        tri = (col <= row).astype(F32)
        c_ref[...] = jnp.dot(tri, lf, precision=lax.Precision.HIGHEST, preferred_element_type=F32) + carry_ref[...]
        carry_ref[...] += jnp.sum(lf, axis=0, keepdims=True)

    return pl.pallas_call(
        body, name=name, grid=(s // tr,),
        in_specs=[pl.BlockSpec((tr, LANES), lambda i: (i, F_COL_BLOCK)), pl.BlockSpec((1, LANES), lambda i: (0, 0))],
        out_specs=pl.BlockSpec((tr, LANES), lambda i: (i, 0)),
        out_shape=jax.ShapeDtypeStruct((s, LANES), F32),
        scratch_shapes=[pltpu.VMEM((1, LANES), F32)],
        compiler_params=_params("arbitrary"),
    )(h, bf_pad)


def _gates_bwd(dc, h, bf_pad, *, tr, name):
    s = h.shape[0]
    n = s // tr

    def body(dc_ref, fl_ref, b_ref, dfl_ref, db_ref, carry_ref):
        @pl.when(pl.program_id(0) == 0)
        def _():
            carry_ref[...] = jnp.zeros_like(carry_ref)
            db_ref[...] = jnp.zeros_like(db_ref)

        dcv = dc_ref[...]
        row = lax.broadcasted_iota(jnp.int32, (tr, tr), 0)
        col = lax.broadcasted_iota(jnp.int32, (tr, tr), 1)
        triu = (col >= row).astype(F32)
        dlf = jnp.dot(triu, dcv, precision=lax.Precision.HIGHEST, preferred_element_type=F32) + carry_ref[...]
        carry_ref[...] += jnp.sum(dcv, axis=0, keepdims=True)
        dfl = dlf * jax.nn.sigmoid(-(fl_ref[...] + b_ref[...]))
        dfl_ref[...] = dfl.astype(dfl_ref.dtype)
        db_ref[...] += jnp.sum(dfl, axis=0, keepdims=True)

    return pl.pallas_call(
        body, name=name, grid=(n,),
        in_specs=[pl.BlockSpec((tr, LANES), lambda i: (n - 1 - i, 0)),
                  pl.BlockSpec((tr, LANES), lambda i: (n - 1 - i, F_COL_BLOCK)),
                  pl.BlockSpec((1, LANES), lambda i: (0, 0))],
        out_specs=[pl.BlockSpec((tr, LANES), lambda i: (n - 1 - i, 0)), pl.BlockSpec((1, LANES), lambda i: (0, 0))],
        out_shape=[jax.ShapeDtypeStruct((s, LANES), BF16), jax.ShapeDtypeStruct((1, LANES), F32)],
        scratch_shapes=[pltpu.VMEM((1, LANES), F32)],
        compiler_params=_params("arbitrary"),
    )(dc, h, bf_pad)


def _attn_fwd(q4, k4, v4, cq4, ck4, *, name):
    nh, nq, tq, dh = q4.shape
    nk, tk = k4.shape[1], k4.shape[2]
    assert tq == tk

    def body(q_ref, k_ref, v_ref, cq_ref, ck_ref, o_ref, lse_ref):
        i = pl.program_id(1)
        q = q_ref[0, 0]
        cq = cq_ref[0, 0]

        def block(j, carry, masked):
            m, l, acc = carry
            s = lax.dot_general(q, k_ref[0, j], (((1,), (1,)), ((), ())), preferred_element_type=F32)
            s = s + (cq - ck_ref[0, j])
            if masked:
                row = lax.broadcasted_iota(jnp.int32, (tq, tk), 0)
                col = lax.broadcasted_iota(jnp.int32, (tq, tk), 1)
                s = jnp.where(col <= row, s, NEG)
            m_new = jnp.maximum(m, jnp.max(s, axis=1, keepdims=True))
            p = jnp.exp(s - m_new)
            alpha = jnp.exp(m - m_new)
            l = alpha * l + jnp.sum(p, axis=1, keepdims=True)
            acc = alpha * acc + jnp.dot(p.astype(BF16), v_ref[0, j], preferred_element_type=F32)
            return m_new, l, acc

        init = (jnp.full((tq, 1), NEG, F32), jnp.zeros((tq, 1), F32), jnp.zeros((tq, dh), F32))
        carry = lax.fori_loop(0, i, lambda j, c: block(j, c, False), init)
        m, l, acc = block(i, carry, True)
        o_ref[0, 0] = acc / l
        lse_ref[0, 0] = m + jnp.log(l)

    return pl.pallas_call(
        body, name=name, grid=(nh, nq),
        in_specs=[pl.BlockSpec((1, 1, tq, dh), lambda h, i: (h, i, 0, 0)),
                  pl.BlockSpec((1, nk, tk, dh), lambda h, i: (h, 0, 0, 0)),
                  pl.BlockSpec((1, nk, tk, dh), lambda h, i: (h, 0, 0, 0)),
                  pl.BlockSpec((1, 1, tq, 1), lambda h, i: (h, i, 0, 0)),
                  pl.BlockSpec((1, nk, 1, tk), lambda h, i: (h, 0, 0, 0))],
        out_specs=[pl.BlockSpec((1, 1, tq, dh), lambda h, i: (h, i, 0, 0)),
                   pl.BlockSpec((1, 1, tq, 1), lambda h, i: (h, i, 0, 0))],
        out_shape=[jax.ShapeDtypeStruct((nh, nq, tq, dh), F32), jax.ShapeDtypeStruct((nh, nq, tq, 1), F32)],
        compiler_params=_params("parallel", "arbitrary"),
    )(q4, k4, v4, cq4, ck4)


def _attn_delta(do4, o4, *, name):
    nh, nq, tq, dh = o4.shape

    def body(do_ref, o_ref, d_ref):
        d_ref[0, 0] = jnp.sum(do_ref[0, 0] * o_ref[0, 0], axis=1, keepdims=True)

    spec = pl.BlockSpec((1, 1, tq, dh), lambda h, i: (h, i, 0, 0))
    return pl.pallas_call(
        body, name=name, grid=(nh, nq), in_specs=[spec, spec],
        out_specs=pl.BlockSpec((1, 1, tq, 1), lambda h, i: (h, i, 0, 0)),
        out_shape=jax.ShapeDtypeStruct((nh, nq, tq, 1), F32),
        compiler_params=_params("parallel", "parallel"),
    )(do4, o4)


def _attn_bwd(q4, do4, k4, v4, ckc4, cqr4, lse4, dl4, *, name):
    nh, nq, tq, dh = q4.shape
    nk, tk = k4.shape[1], k4.shape[2]
    assert tq == tk

    def body(q_ref, do_ref, k_ref, v_ref, ckc_ref, cqr_ref, lse_ref, dl_ref, dq_ref, dk_ref, dv_ref, dck_ref, dcq_ref):
        j = pl.program_id(1)

        @pl.when(j == 0)
        def _():
            dq_ref[...] = jnp.zeros_like(dq_ref)
            dcq_ref[...] = jnp.zeros_like(dcq_ref)

        kj = k_ref[0, 0]
        vj = v_ref[0, 0]
        kj8 = kj * 0.125
        ckc = ckc_ref[0, 0]

        def block(i, carry, masked):
            dk, dv, dc = carry
            qi = q_ref[0, i]
            doi = do_ref[0, i]
            st = lax.dot_general(kj, qi, (((1,), (1,)), ((), ())), preferred_element_type=F32)
            st = st + ((cqr_ref[0, i] - lse_ref[0, i]) - ckc)
            if masked:
                key = lax.broadcasted_iota(jnp.int32, (tk, tq), 0)
                qry = lax.broadcasted_iota(jnp.int32, (tk, tq), 1)
                st = jnp.where(key <= qry, st, NEG)
            pt = jnp.exp(st)
            dv = dv + jnp.dot(pt.astype(BF16), doi, preferred_element_type=F32)
            dpt = lax.dot_general(vj, doi, (((1,), (1,)), ((), ())), preferred_element_type=F32)
            dst = pt * (dpt - dl_ref[0, i])
            dc = dc - jnp.sum(dst, axis=1, keepdims=True)
            dcq_ref[0, i] += jnp.sum(dst, axis=0, keepdims=True)
            dsb = dst.astype(BF16)
            dk = dk + jnp.dot(dsb, qi, preferred_element_type=F32)
            dq_ref[0, i] += lax.dot_general(dsb, kj8, (((0,), (0,)), ((), ())), preferred_element_type=F32)
            return dk, dv, dc

        init = (jnp.zeros((tk, dh), F32), jnp.zeros((tk, dh), F32), jnp.zeros((tk, 1), F32))
        carry = block(j, init, True)
        dk, dv, dc = lax.fori_loop(j + 1, nq, lambda i, c: block(i, c, False), carry)
        dk_ref[0, 0] = dk
        dv_ref[0, 0] = dv
        dck_ref[0, 0] = dc

    full_q = pl.BlockSpec((1, nq, tq, dh), lambda h, j: (h, 0, 0, 0))
    row_q = pl.BlockSpec((1, nq, 1, tq), lambda h, j: (h, 0, 0, 0))
    blk_k = pl.BlockSpec((1, 1, tk, dh), lambda h, j: (h, j, 0, 0))
    col_k = pl.BlockSpec((1, 1, tk, 1), lambda h, j: (h, j, 0, 0))
    return pl.pallas_call(
        body, name=name, grid=(nh, nk),
        in_specs=[full_q, full_q, blk_k, blk_k, col_k, row_q, row_q, row_q],
        out_specs=[full_q, blk_k, blk_k, col_k, row_q],
        out_shape=[jax.ShapeDtypeStruct((nh, nq, tq, dh), F32), jax.ShapeDtypeStruct((nh, nk, tk, dh), F32),
                   jax.ShapeDtypeStruct((nh, nk, tk, dh), F32), jax.ShapeDtypeStruct((nh, nk, tk, 1), F32),
                   jax.ShapeDtypeStruct((nh, nq, 1, tq), F32)],
        compiler_params=_params("parallel", "arbitrary"),
    )(q4, do4, k4, v4, ckc4, cqr4, lse4, dl4)


def _gelu(z):
    return 0.5 * z * (1.0 + lax.erf(z * 0.7071067811865476))


def _gelu_grad(z):
    return 0.5 * (1.0 + lax.erf(z * 0.7071067811865476)) + z * (0.3989422804014327 * jnp.exp(-0.5 * z * z))


def _sgu_mask():
    i = lax.broadcasted_iota(jnp.int32, (SGU_CHUNK, SGU_CHUNK), 0) // CAUSAL_CHUNK
    j = lax.broadcasted_iota(jnp.int32, (SGU_CHUNK, SGU_CHUNK), 1) // CAUSAL_CHUNK
    return (j <= i).astype(F32)


def _layernorm_stats(x):
    mu = jnp.mean(x, axis=-1, keepdims=True)
    xc = x - mu
    rstd = lax.rsqrt(jnp.mean(xc * xc, axis=-1, keepdims=True) + EPS)
    return xc * rstd, rstd


def _first_group_lanes():
    return lax.broadcasted_iota(jnp.int32, (SGU_CHUNK, LANES), 1) < 64


def _sgu_fwd(h, ln_g, ln_b, w_s, bias_tile, *, tr, name):
    s = h.shape[0]
    zu_blk, zv_blk = 1536 // D_HALF, 2048 // D_HALF

    def body(zu_ref, zv_ref, lng_ref, lnb_ref, ws_ref, bias_ref, o_ref):
        gzu = _gelu(zu_ref[...])
        xh, _ = _layernorm_stats(_gelu(zv_ref[...]))
        zb = (xh * lng_ref[...] + lnb_ref[...]).astype(BF16)
        mask = _sgu_mask()
        first = _first_group_lanes()
        for pair in range(4):
            cols = slice(pair * LANES, (pair + 1) * LANES)
            w0 = (ws_ref[2 * pair] * mask).astype(BF16)
            w1 = (ws_ref[2 * pair + 1] * mask).astype(BF16)
            for ch in range(tr // SGU_CHUNK):
                rows = slice(ch * SGU_CHUNK, (ch + 1) * SGU_CHUNK)
                zp = zb[rows, cols]
                mixed = jnp.where(first, jnp.dot(w0, zp, preferred_element_type=F32),
                                  jnp.dot(w1, zp, preferred_element_type=F32)) + bias_ref[:, cols]
                o_ref[rows, cols] = gzu[rows, cols] * mixed

    return pl.pallas_call(
        body, name=name, grid=(s // tr,),
        in_specs=[pl.BlockSpec((tr, D_HALF), lambda i: (i, zu_blk)), pl.BlockSpec((tr, D_HALF), lambda i: (i, zv_blk)),
                  pl.BlockSpec((1, D_HALF), lambda i: (0, 0)), pl.BlockSpec((1, D_HALF), lambda i: (0, 0)),
                  pl.BlockSpec((N_HEADS, SGU_CHUNK, SGU_CHUNK), lambda i: (0, 0, 0)),
                  pl.BlockSpec((SGU_CHUNK, D_HALF), lambda i: (0, 0))],
        out_specs=pl.BlockSpec((tr, D_HALF), lambda i: (i, 0)),
        out_shape=jax.ShapeDtypeStruct((s, D_HALF), F32),
        compiler_params=_params("parallel"),
    )(h, h, ln_g, ln_b, w_s, bias_tile)


def _sgu_bwd(dsgu, h, ln_g, ln_b, w_s, bias_tile, *, tr, name):
    s = h.shape[0]
    n = s // tr
    zu_blk, zv_blk = 1536 // D_HALF, 2048 // D_HALF

    def body(ds_ref, zu_ref, zv_ref, lng_ref, lnb_ref, ws_ref, bias_ref,
             dzu_ref, dzv_ref, dws_ref, dlng_ref, dlnb_ref, dbs_ref, dgzu_sc, dzvn_sc, dbias_sc):
        step = pl.program_id(0)

        @pl.when(step == 0)
        def _():
            dws_ref[...] = jnp.zeros_like(dws_ref)
            dlng_ref[...] = jnp.zeros_like(dlng_ref)
            dlnb_ref[...] = jnp.zeros_like(dlnb_ref)
            dbias_sc[...] = jnp.zeros_like(dbias_sc)

        zu = zu_ref[...]
        zv = zv_ref[...]
        gzu = _gelu(zu)
        xh, rstd = _layernorm_stats(_gelu(zv))
        zb = (xh * lng_ref[...] + lnb_ref[...]).astype(BF16)
        ds = ds_ref[...]
        mask = _sgu_mask()
        first = _first_group_lanes()
        tn_dims = (((0,), (0,)), ((), ()))
        nt_dims = (((1,), (1,)), ((), ()))
        for pair in range(4):
            cols = slice(pair * LANES, (pair + 1) * LANES)
            w0 = (ws_ref[2 * pair] * mask).astype(BF16)
            w1 = (ws_ref[2 * pair + 1] * mask).astype(BF16)
            for ch in range(tr // SGU_CHUNK):
                rows = slice(ch * SGU_CHUNK, (ch + 1) * SGU_CHUNK)
                zp = zb[rows, cols]
                mixed = jnp.where(first, jnp.dot(w0, zp, preferred_element_type=F32),
                                  jnp.dot(w1, zp, preferred_element_type=F32)) + bias_ref[:, cols]
                dsp = ds[rows, cols]
                dgzu_sc[rows, cols] = dsp * mixed
                dm = dsp * gzu[rows, cols]
                dbias_sc[:, cols] += dm
                dmb = dm.astype(BF16)
                dm0 = jnp.where(first, dmb, jnp.zeros_like(dmb))
                dm1 = jnp.where(first, jnp.zeros_like(dmb), dmb)
                dws_ref[2 * pair] += lax.dot_general(dm0, zp, nt_dims, preferred_element_type=F32)
                dws_ref[2 * pair + 1] += lax.dot_general(dm1, zp, nt_dims, preferred_element_type=F32)
                dzvn_sc[rows, cols] = jnp.where(first, lax.dot_general(w0, dmb, tn_dims, preferred_element_type=F32),
                                                lax.dot_general(w1, dmb, tn_dims, preferred_element_type=F32))
        dzvn = dzvn_sc[...]
        dlng_ref[...] += jnp.sum(dzvn * xh, axis=0, keepdims=True)
        dlnb_ref[...] += jnp.sum(dzvn, axis=0, keepdims=True)
        dxh = dzvn * lng_ref[...]
        dgzv = rstd * (dxh - jnp.mean(dxh, axis=-1, keepdims=True) - xh * jnp.mean(dxh * xh, axis=-1, keepdims=True))
        dzv_ref[...] = (dgzv * _gelu_grad(zv)).astype(dzv_ref.dtype)
        dzu_ref[...] = (dgzu_sc[...] * _gelu_grad(zu)).astype(dzu_ref.dtype)

        @pl.when(step == n - 1)
        def _():
            for g in range(N_HEADS):
                dws_ref[g] = dws_ref[g] * mask
            lane = lax.broadcasted_iota(jnp.int32, (D_HALF, LANES), 0) // 64
            grp = lax.broadcasted_iota(jnp.int32, (D_HALF, LANES), 1)
            dbs_ref[...] = jnp.dot(dbias_sc[...], (lane == grp).astype(F32), precision=lax.Precision.HIGHEST,
                                   preferred_element_type=F32)

    const2 = lambda i: (0, 0)
    return pl.pallas_call(
        body, name=name, grid=(n,),
        in_specs=[pl.BlockSpec((tr, D_HALF), lambda i: (i, 0)),
                  pl.BlockSpec((tr, D_HALF), lambda i: (i, zu_blk)), pl.BlockSpec((tr, D_HALF), lambda i: (i, zv_blk)),
                  pl.BlockSpec((1, D_HALF), const2), pl.BlockSpec((1, D_HALF), const2),
                  pl.BlockSpec((N_HEADS, SGU_CHUNK, SGU_CHUNK), lambda i: (0, 0, 0)),
                  pl.BlockSpec((SGU_CHUNK, D_HALF), const2)],
        out_specs=[pl.BlockSpec((tr, D_HALF), lambda i: (i, 0)), pl.BlockSpec((tr, D_HALF), lambda i: (i, 0)),
                   pl.BlockSpec((N_HEADS, SGU_CHUNK, SGU_CHUNK), lambda i: (0, 0, 0)),
                   pl.BlockSpec((1, D_HALF), const2), pl.BlockSpec((1, D_HALF), const2),
                   pl.BlockSpec((SGU_CHUNK, LANES), const2)],
        out_shape=[jax.ShapeDtypeStruct((s, D_HALF), BF16), jax.ShapeDtypeStruct((s, D_HALF), BF16),
                   jax.ShapeDtypeStruct((N_HEADS, SGU_CHUNK, SGU_CHUNK), F32),
                   jax.ShapeDtypeStruct((1, D_HALF), F32), jax.ShapeDtypeStruct((1, D_HALF), F32),
                   jax.ShapeDtypeStruct((SGU_CHUNK, LANES), F32)],
        scratch_shapes=[pltpu.VMEM((tr, D_HALF), F32), pltpu.VMEM((tr, D_HALF), F32), pltpu.VMEM((SGU_CHUNK, D_HALF), F32)],
        compiler_params=_params("arbitrary"),
    )(dsgu, h, h, ln_g, ln_b, w_s, bias_tile)


FF_BLOCK = D_FF // 2


def _swiglu_fwd(gu, *, tr, name):
    s = gu.shape[0]

    def body(g_ref, u_ref, o_ref):
        g = g_ref[...]
        o_ref[...] = (g * jax.nn.sigmoid(g) * u_ref[...]).astype(o_ref.dtype)

    return pl.pallas_call(
        body, name=name, grid=(s // tr, 2),
        in_specs=[pl.BlockSpec((tr, FF_BLOCK), lambda i, j: (i, j)), pl.BlockSpec((tr, FF_BLOCK), lambda i, j: (i, j + 2))],
        out_specs=pl.BlockSpec((tr, FF_BLOCK), lambda i, j: (i, j)),
        out_shape=jax.ShapeDtypeStruct((s, D_FF), BF16),
        compiler_params=_params("parallel", "parallel"),
    )(gu, gu)


def _swiglu_bwd(dact, gu, *, tr, name):
    s = gu.shape[0]

    def body(d_ref, g_ref, u_ref, o_ref):
        g = g_ref[...]
        d = d_ref[...]
        sig = jax.nn.sigmoid(g)
        dgate = d * u_ref[...] * (sig * (1.0 + g * (1.0 - sig)))
        dup = d * (g * sig)
        o_ref[...] = jnp.where(pl.program_id(1) < 2, dgate, dup).astype(o_ref.dtype)

    return pl.pallas_call(
        body, name=name, grid=(s // tr, 4),
        in_specs=[pl.BlockSpec((tr, FF_BLOCK), lambda i, j: (i, j % 2)), pl.BlockSpec((tr, FF_BLOCK), lambda i, j: (i, j % 2)),
                  pl.BlockSpec((tr, FF_BLOCK), lambda i, j: (i, j % 2 + 2))],
        out_specs=pl.BlockSpec((tr, FF_BLOCK), lambda i, j: (i, j)),
        out_shape=jax.ShapeDtypeStruct((s, 2 * D_FF), BF16),
        compiler_params=_params("parallel", "parallel"),
    )(dact, gu, gu)


def _sum_slots(stacked, *, tr, name):
    k, r, _ = stacked.shape

    def body(x_ref, o_ref):
        acc = x_ref[0]
        for idx in range(1, k):
            acc = acc + x_ref[idx]
        o_ref[...] = acc

    return pl.pallas_call(
        body, name=name, grid=(r // tr,),
        in_specs=[pl.BlockSpec((k, tr, LANES), lambda i: (0, i, 0))],
        out_specs=pl.BlockSpec((tr, LANES), lambda i: (i, 0)),
        out_shape=jax.ShapeDtypeStruct((r, LANES), F32),
        compiler_params=_params("parallel"),
    )(stacked)


def _add2(a, b, *, tr, name):
    r = a.shape[0]

    def body(a_ref, b_ref, o_ref):
        o_ref[...] = a_ref[...] + b_ref[...]

    spec = pl.BlockSpec((tr, LANES), lambda i: (i, 0))
    return pl.pallas_call(
        body, name=name, grid=(r // tr,), in_specs=[spec, spec], out_specs=spec,
        out_shape=jax.ShapeDtypeStruct((r, LANES), F32), compiler_params=_params("parallel"),
    )(a, b)


def _adamw(w, g, m, v, *, tr, name):
    r = w.shape[0]

    def body(w_ref, g_ref, m_ref, v_ref, d_ref, m2_ref, v2_ref):
        gv = g_ref[...]
        m2 = ADAM_B1 * m_ref[...] + (1.0 - ADAM_B1) * gv
        v2 = ADAM_B2 * v_ref[...] + (1.0 - ADAM_B2) * jnp.square(gv)
        m_hat = m2 / (1.0 - ADAM_B1 ** ADAM_STEP)
        v_hat = v2 / (1.0 - ADAM_B2 ** ADAM_STEP)
        d_ref[...] = -ADAM_LR * (m_hat / (jnp.sqrt(v_hat) + ADAM_EPS) + ADAM_WD * w_ref[...])
        m2_ref[...] = m2
        v2_ref[...] = v2

    spec = pl.BlockSpec((tr, LANES), lambda i: (i, 0))
    shape = jax.ShapeDtypeStruct((r, LANES), F32)
    return pl.pallas_call(
        body, name=name, grid=(r // tr,), in_specs=[spec] * 4, out_specs=[spec] * 3, out_shape=[shape] * 3,
        compiler_params=_params("parallel"),
    )(w, g, m, v)


def _coords():
    return lax.axis_index("x"), lax.axis_index("y"), lax.axis_index("c")


def _chip_peer(x, y, k):
    px = 1 - x if k & 2 else x
    py = 1 - y if k & 1 else y
    return px, py


def _allgather_chips(shard, *, name):
    r = shard.shape[0]

    def body(src, out, send_sems, recv_sems, local_sem):
        x, y, c = _coords()
        me = 2 * x + y
        mine = pltpu.make_async_copy(src, out.at[me], local_sem)
        mine.start()
        copies = []
        for k in (1, 2, 3):
            px, py = _chip_peer(x, y, k)
            cp = pltpu.make_async_remote_copy(src_ref=src, dst_ref=out.at[me], send_sem=send_sems.at[k - 1],
                                              recv_sem=recv_sems.at[k - 1], device_id=(px, py, c), device_id_type=MESH)
            cp.start()
            copies.append(cp)
        for cp in copies:
            cp.wait()
        mine.wait()

    return pl.pallas_call(
        body, name=name, in_specs=[ANY], out_specs=ANY,
        out_shape=jax.ShapeDtypeStruct((4, r, LANES), shard.dtype),
        scratch_shapes=[pltpu.SemaphoreType.DMA((3,)), pltpu.SemaphoreType.DMA((3,)), pltpu.SemaphoreType.DMA],
    )(shard)


def _pair_split(grads, *, name):
    _, r, _ = grads.shape
    rh = r // 2

    def body(g_ref, mine_ref, theirs_ref, send_sem, recv_sem, local_sem):
        x, y, c = _coords()
        keep = pltpu.make_async_copy(g_ref.at[:, pl.ds(c * rh, rh), :], mine_ref, local_sem)
        keep.start()
        cp = pltpu.make_async_remote_copy(src_ref=g_ref.at[:, pl.ds((1 - c) * rh, rh), :], dst_ref=theirs_ref,
                                          send_sem=send_sem, recv_sem=recv_sem, device_id=(x, y, 1 - c), device_id_type=MESH)
        cp.start()
        cp.wait()
        keep.wait()

    shape = jax.ShapeDtypeStruct((4, rh, LANES), F32)
    return pl.pallas_call(
        body, name=name, in_specs=[ANY], out_specs=[ANY, ANY], out_shape=[shape, shape],
        scratch_shapes=[pltpu.SemaphoreType.DMA, pltpu.SemaphoreType.DMA, pltpu.SemaphoreType.DMA],
    )(grads)


def _scatter_chips(part, *, name):
    _, rh, _ = part.shape

    def body(p_ref, out, send_sems, recv_sems, local_sem):
        x, y, c = _coords()
        me = 2 * x + y
        mine = pltpu.make_async_copy(p_ref.at[me], out.at[me], local_sem)
        mine.start()
        copies = []
        for k in (1, 2, 3):
            px, py = _chip_peer(x, y, k)
            cp = pltpu.make_async_remote_copy(src_ref=p_ref.at[2 * px + py], dst_ref=out.at[me], send_sem=send_sems.at[k - 1],
                                              recv_sem=recv_sems.at[k - 1], device_id=(px, py, c), device_id_type=MESH)
            cp.start()
            copies.append(cp)
        for cp in copies:
            cp.wait()
        mine.wait()

    return pl.pallas_call(
        body, name=name, in_specs=[ANY], out_specs=ANY, out_shape=jax.ShapeDtypeStruct((4, rh, LANES), F32),
        scratch_shapes=[pltpu.SemaphoreType.DMA((3,)), pltpu.SemaphoreType.DMA((3,)), pltpu.SemaphoreType.DMA],
    )(part)


def _pair_join(half, *, name):
    rh = half.shape[0]

    def body(h_ref, out, send_sem, recv_sem, local_sem):
        x, y, c = _coords()
        rows = out.at[pl.ds(c * rh, rh), :]
        mine = pltpu.make_async_copy(h_ref, rows, local_sem)
        mine.start()
        cp = pltpu.make_async_remote_copy(src_ref=h_ref, dst_ref=rows, send_sem=send_sem, recv_sem=recv_sem,
                                          device_id=(x, y, 1 - c), device_id_type=MESH)
        cp.start()
        cp.wait()
        mine.wait()

    return pl.pallas_call(
        body, name=name, in_specs=[ANY], out_specs=ANY, out_shape=jax.ShapeDtypeStruct((2 * rh, LANES), F32),
        scratch_shapes=[pltpu.SemaphoreType.DMA, pltpu.SemaphoreType.DMA, pltpu.SemaphoreType.DMA],
    )(half)


def _allgather_all(block, *, name):
    r = block.shape[0]

    def body(src, out, send_sems, recv_sems, local_sem):
        x, y, c = _coords()
        me = 4 * x + 2 * y + c
        mine = pltpu.make_async_copy(src, out.at[me], local_sem)
        mine.start()
        copies = []
        for k in range(1, 8):
            px, py = _chip_peer(x, y, k >> 1)
            pc = 1 - c if k & 1 else c
            cp = pltpu.make_async_remote_copy(src_ref=src, dst_ref=out.at[me], send_sem=send_sems.at[k - 1],
                                              recv_sem=recv_sems.at[k - 1], device_id=(px, py, pc), device_id_type=MESH)
            cp.start()
            copies.append(cp)
        for cp in copies:
            cp.wait()
        mine.wait()

    return pl.pallas_call(
        body, name=name, in_specs=[ANY], out_specs=ANY, out_shape=jax.ShapeDtypeStruct((8, r, LANES), F32),
        scratch_shapes=[pltpu.SemaphoreType.DMA((7,)), pltpu.SemaphoreType.DMA((7,)), pltpu.SemaphoreType.DMA],
    )(block)


def _flatten(arrays, pad_rows=None):
    flat = jnp.concatenate([a.reshape(-1) for a in arrays])
    if pad_rows is not None:
        flat = jnp.pad(flat, (0, pad_rows * LANES - flat.shape[0]))
    return flat.reshape(-1, LANES)


def _unflatten(flat, shapes):
    flat = flat.reshape(-1)
    out, off = [], 0
    for shp in shapes:
        size = 1
        for dim in shp:
            size *= dim
        out.append(flat[off:off + size].reshape(shp))
        off += size
    return out


def _pad_w_in(w):
    pad = jnp.zeros(w.shape[:-1] + (D_IN_PAD - D_IN,), w.dtype)
    return jnp.concatenate([w[..., :1536], w[..., 1544:], w[..., 1536:1544], pad], axis=-1)


def _unpad_w_in(w):
    return jnp.concatenate([w[..., :1536], w[..., 2560:2568], w[..., 1536:2560]], axis=-1)


def _to_heads(a, t):
    s = a.shape[0]
    return a.reshape(s, N_HEADS, HEAD_DIM).transpose(1, 0, 2).reshape(N_HEADS, s // t, t, HEAD_DIM)


def _from_heads(a4):
    nh, nb, t, dh = a4.shape
    return a4.reshape(nh, nb * t, dh).transpose(1, 0, 2).reshape(nb * t, nh * dh)


def _tile(s, want):
    return min(want, s)


def _layer_fwd(x, p, l):
    s = x.shape[0]
    tr = _tile(s, 512)
    tm = _tile(s, 1024)
    ta = _tile(s, 512)
    xn = _rms_fwd([x], p["mix_g"], out_dtype=BF16, tr=tr, name=f"rms_mix_fwd{l}")
    h = _matmul(xn, p["w_in"], tm=tm, tn=896, out_dtype=F32, name=f"mm_in{l}")
    c = _gates_fwd(h, p["bf_pad"], tr=_tile(s, 256), name=f"gates_fwd{l}")
    q4 = _to_heads((h[:, 0:512] * 0.125).astype(BF16), ta)
    k4 = _to_heads(h[:, 512:1024].astype(BF16), ta)
    v4 = _to_heads(h[:, 1024:1536].astype(BF16), ta)
    ct = c[:, :N_HEADS].T
    o4, lse4 = _attn_fwd(q4, k4, v4, ct.reshape(N_HEADS, s // ta, ta, 1), ct.reshape(N_HEADS, s // ta, 1, ta),
                         name=f"attn_fwd{l}")
    attn = _from_heads(o4)
    sgu = _sgu_fwd(h, p["ln_g"], p["ln_b"], p["w_s"], p["bias_tile"], tr=_tile(s, 256), name=f"sgu_fwd{l}")
    merged = _rms_fwd([attn, sgu], p["out_g"], out_dtype=BF16, tr=tr, name=f"rms_out_fwd{l}")
    x1 = _matmul(merged, p["w_out"], tm=tm, tn=1024, out_dtype=F32, residual=x, name=f"mm_out{l}")
    xn2 = _rms_fwd([x1], p["ffn_g"], out_dtype=BF16, tr=tr, name=f"rms_ffn_fwd{l}")
    gu = _matmul(xn2, p["w_gu"], tm=tm, tn=1408, out_dtype=F32, name=f"mm_gu{l}")
    act = _swiglu_fwd(gu, tr=tr, name=f"swiglu_fwd{l}")
    x2 = _matmul(act, p["w_down"], tm=tm, tn=1024, out_dtype=F32, residual=x1, name=f"mm_down{l}")
    saved = dict(x=x, xn=xn, h=h, q4=q4, k4=k4, v4=v4, ct=ct, o4=o4, lse4=lse4, attn=attn, sgu=sgu, merged=merged,
                 x1=x1, xn2=xn2, gu=gu, act=act)
    return x2, saved


def _layer_bwd(dx2, p, sv, l):
    s = dx2.shape[0]
    tr = _tile(s, 512)
    tm = _tile(s, 1024)
    ts = _tile(s, 512)
    ta = _tile(s, 512)
    nb = s // ta
    g = {}
    g["w_down"] = _matmul_tn(sv["act"], dx2, tm=1408, tn=1024, ts=ts, name=f"mm_down_dw{l}")
    dact = _matmul(dx2, p["w_down"], trans_b=True, tm=tm, tn=1408, out_dtype=F32, name=f"mm_down_dx{l}")
    dgu = _swiglu_bwd(dact, sv["gu"], tr=tr, name=f"swiglu_bwd{l}")
    g["w_gu"] = _matmul_tn(sv["xn2"], dgu, tm=1024, tn=1408, ts=ts, name=f"mm_gu_dw{l}")
    dxn2 = _matmul(dgu, p["w_gu"], trans_b=True, tm=_tile(s, 512), tn=1024, out_dtype=F32, name=f"mm_gu_dx{l}")
    (dx1,), g["ffn_g"] = _rms_bwd(dxn2, [sv["x1"]], p["ffn_g"], residual=dx2, tr=tr, name=f"rms_ffn_bwd{l}")
    g["w_out"] = _matmul_tn(sv["merged"], dx1, tm=1024, tn=1024, ts=ts, name=f"mm_out_dw{l}")
    dmerged = _matmul(dx1, p["w_out"], trans_b=True, tm=tm, tn=1024, out_dtype=F32, name=f"mm_out_dx{l}")
    (dattn, dsgu), g["out_g"] = _rms_bwd(dmerged, [sv["attn"], sv["sgu"]], p["out_g"], tr=tr, name=f"rms_out_bwd{l}")
    dzu, dzv, g["w_s"], g["ln_g"], g["ln_b"], dbs = _sgu_bwd(dsgu, sv["h"], p["ln_g"], p["ln_b"], p["w_s"], p["bias_tile"],
                                                           tr=_tile(s, 256), name=f"sgu_bwd{l}")
    g["b_s"] = dbs[:, :N_HEADS].T
    do4 = _to_heads(dattn, ta)
    dl4 = _attn_delta(do4, sv["o4"], name=f"attn_delta{l}")
    ct = sv["ct"]
    dq4, dk4, dv4, dck4, dcq4 = _attn_bwd(
        sv["q4"], do4.astype(BF16), sv["k4"], sv["v4"], ct.reshape(N_HEADS, nb, ta, 1), ct.reshape(N_HEADS, nb, 1, ta),
        sv["lse4"].reshape(N_HEADS, nb, 1, ta), dl4.reshape(N_HEADS, nb, 1, ta), name=f"attn_bwd{l}")
    dc = jnp.pad((dck4.reshape(N_HEADS, s) + dcq4.reshape(N_HEADS, s)).T, ((0, 0), (0, LANES - N_HEADS)))
    dfl, dbf = _gates_bwd(dc, sv["h"], p["bf_pad"], tr=_tile(s, 256), name=f"gates_bwd{l}")
    g["b_f"] = dbf[0, :N_HEADS]
    dh = jnp.concatenate([_from_heads(dq4).astype(BF16), _from_heads(dk4).astype(BF16), _from_heads(dv4).astype(BF16),
                          dzu, dzv, dfl], axis=1)
    g["w_in"] = _matmul_tn(sv["xn"], dh, tm=1024, tn=896, ts=ts, name=f"mm_in_dw{l}")
    dxn = _matmul(dh, p["w_in"], trans_b=True, tm=tm, tn=1024, out_dtype=F32, name=f"mm_in_dx{l}")
    (dx,), g["mix_g"] = _rms_bwd(dxn, [sv["x"]], p["mix_g"], residual=dx1, tr=tr, name=f"rms_mix_bwd{l}")
    return dx, g


def _layer_params(l, w_in_pad, w_out, w_gu, w_down, mix_norm_g, b_f, sgu_ln_g, sgu_ln_b, w_s, b_s, out_norm_g, ffn_norm_g):
    return dict(
        w_in=w_in_pad[l], w_out=w_out[l], w_gu=w_gu[l], w_down=w_down[l],
        mix_g=mix_norm_g[l][None, :], out_g=out_norm_g[l][None, :], ffn_g=ffn_norm_g[l][None, :],
        bf_pad=jnp.pad(b_f[l], (0, LANES - N_HEADS))[None, :],
        ln_g=sgu_ln_g[l][None, :], ln_b=sgu_ln_b[l][None, :], w_s=w_s[l],
        bias_tile=jnp.repeat(b_s[l].T, 64, axis=1),
    )


def _local_step(x, tgt, w_in_pad, w_out, w_gu, w_down, mix_norm_g, b_f, sgu_ln_g, sgu_ln_b, w_s, b_s, out_norm_g,
                ffn_norm_g, final_norm_g):
    depth = w_in_pad.shape[0]
    s = x.shape[0]
    params = [_layer_params(l, w_in_pad, w_out, w_gu, w_down, mix_norm_g, b_f, sgu_ln_g, sgu_ln_b, w_s, b_s, out_norm_g,
                            ffn_norm_g) for l in range(depth)]
    saved = []
    for l in range(depth):
        x, sv = _layer_fwd(x, params[l], l)
        saved.append(sv)
    loss_tile, dx, dfinal = _loss_head(x, tgt, final_norm_g[None, :], tr=_tile(s, 512), name="loss_head")
    grads = [None] * depth
    for l in reversed(range(depth)):
        dx, grads[l] = _layer_bwd(dx, params[l], saved[l], l)
    return loss_tile[0, 0], dx, grads, dfinal[0]


SMALL_ROWS = 4272


def kernel(x, mix_norm_g, w_in, b_f, sgu_ln_g, sgu_ln_b, w_s, b_s, out_norm_g, w_out, ffn_norm_g, w_gate_up, w_down, final_norm_g, loss_target, m_mix_norm_g, m_w_in, m_b_f, m_sgu_ln_g, m_sgu_ln_b, m_w_s, m_b_s, m_out_norm_g, m_w_out, m_ffn_norm_g, m_w_gate_up, m_w_down, m_final_norm_g, v_mix_norm_g, v_w_in, v_b_f, v_sgu_ln_g, v_sgu_ln_b, v_w_s, v_b_s, v_out_norm_g, v_w_out, v_ffn_norm_g, v_w_gate_up, v_w_down, v_final_norm_g):
    big = [w_in, w_out, w_gate_up, w_down]
    big_shapes = [a.shape for a in big]
    small = [mix_norm_g, b_f, sgu_ln_g, sgu_ln_b, w_s, b_s, out_norm_g, ffn_norm_g, final_norm_g]
    small_shapes = [a.shape for a in small]

    gathered = _allgather_chips(_flatten([a.astype(BF16) for a in big]), name="ag_weights")
    per_chip = [_unflatten(gathered[j], big_shapes) for j in range(4)]
    w_in_full = _pad_w_in(jnp.concatenate([pc[0] for pc in per_chip], axis=2))
    w_out_full = jnp.concatenate([pc[1] for pc in per_chip], axis=1)
    w_gu_full = jnp.concatenate([pc[2] for pc in per_chip], axis=2)
    w_down_full = jnp.concatenate([pc[3] for pc in per_chip], axis=1)

    loss_part, dx, grads, dfinal = _local_step(
        x[0], loss_target[0], w_in_full, w_out_full, w_gu_full, w_down_full, mix_norm_g, b_f, sgu_ln_g, sgu_ln_b, w_s, b_s,
        out_norm_g, ffn_norm_g, final_norm_g)
    stack = lambda key: jnp.stack([g[key] for g in grads])
    g_in = _unpad_w_in(stack("w_in"))
    g_out, g_gu, g_down = stack("w_out"), stack("w_gu"), stack("w_down")

    send = jnp.stack([_flatten([g_in[:, :, 642 * j:642 * (j + 1)], g_out[:, 256 * j:256 * (j + 1), :],
                                g_gu[:, :, 1408 * j:1408 * (j + 1)], g_down[:, 704 * j:704 * (j + 1), :]]) for j in range(4)])
    rows = send.shape[1]
    mine, theirs = _pair_split(send, name="rs_pair_split")
    pair_sum = _add2(mine.reshape(-1, LANES), theirs.reshape(-1, LANES), tr=2240, name="rs_pair_sum")
    from_chips = _scatter_chips(pair_sum.reshape(4, rows // 2, LANES), name="rs_scatter")
    half = _sum_slots(from_chips, tr=1120, name="rs_chip_sum")
    g_big_flat = _pair_join(half, name="rs_pair_join")

    g_small_local = [stack("mix_g")[:, 0], stack("b_f"), stack("ln_g")[:, 0], stack("ln_b")[:, 0], stack("w_s"), stack("b_s"),
                     stack("out_g")[:, 0], stack("ffn_g")[:, 0], dfinal]
    g_small_flat = _sum_slots(_allgather_all(_flatten(g_small_local, SMALL_ROWS), name="ar_small_gather"), tr=1424,
                              name="ar_small_sum")
    loss = lax.psum(loss_part, ("x", "y", "c"))

    d_big, m_big, v_big = _adamw(_flatten(big), g_big_flat, _flatten([m_w_in, m_w_out, m_w_gate_up, m_w_down]),
                                 _flatten([v_w_in, v_w_out, v_w_gate_up, v_w_down]), tr=2240, name="adamw_big")
    m_small = [m_mix_norm_g, m_b_f, m_sgu_ln_g, m_sgu_ln_b, m_w_s, m_b_s, m_out_norm_g, m_ffn_norm_g, m_final_norm_g]
    v_small = [v_mix_norm_g, v_b_f, v_sgu_ln_g, v_sgu_ln_b, v_w_s, v_b_s, v_out_norm_g, v_ffn_norm_g, v_final_norm_g]
    d_small, m_small2, v_small2 = _adamw(_flatten(small, SMALL_ROWS), g_small_flat, _flatten(m_small, SMALL_ROWS),
                                         _flatten(v_small, SMALL_ROWS), tr=1424, name="adamw_small")

    def in_order(big_flat, small_flat):
        b_in, b_out, b_gu, b_down = _unflatten(big_flat, big_shapes)
        s_mix, s_bf, s_lng, s_lnb, s_ws, s_bs, s_outg, s_ffn, s_fin = _unflatten(small_flat, small_shapes)
        return [s_mix, b_in, s_bf, s_lng, s_lnb, s_ws, s_bs, s_outg, b_out, s_ffn, b_gu, b_down, s_fin]

    return (loss, dx[None], *in_order(g_big_flat, g_small_flat), *in_order(d_big, d_small), *in_order(m_big, m_small2),
            *in_order(v_big, v_small2))
```

```python
import jax
import jax.numpy as jnp
from jax import lax
from jax.experimental import pallas as pl
from jax.experimental.pallas import tpu as pltpu

F32 = jnp.float32
BF16 = jnp.bfloat16

D_MODEL = 1024
D_HALF = 512
N_HEADS = 8
HEAD_DIM = 64
SGU_CHUNK = 128
CAUSAL_CHUNK = 64
D_FF = 2816
D_IN = 2568
D_IN_PAD = 2688
F_COL_BLOCK = 2560 // 128
EPS = 1e-6
NEG = -1e30
LANES = 128
VMEM_LIMIT = 56 * 1024 * 1024

ADAM_LR = 0.001
ADAM_B1 = 0.9
ADAM_B2 = 0.999
ADAM_EPS = 1e-08
ADAM_WD = 0.01
ADAM_STEP = 10

MESH = pl.DeviceIdType.MESH
ANY = pl.BlockSpec(memory_space=pl.ANY)


def _params(*sem):
    return pltpu.CompilerParams(dimension_semantics=sem, vmem_limit_bytes=VMEM_LIMIT)


def _matmul(a, b, *, trans_b=False, tm, tn, out_dtype, residual=None, name):
    m, k = a.shape
    n = b.shape[0] if trans_b else b.shape[1]
    dims = (((1,), (1,)), ((), ())) if trans_b else (((1,), (0,)), ((), ()))

    def body(*refs):
        a_ref, b_ref = refs[0], refs[1]
        o_ref = refs[-1]
        acc = lax.dot_general(a_ref[...].astype(BF16), b_ref[...].astype(BF16), dims,
                              preferred_element_type=F32)
        if residual is not None:
            acc = acc + refs[2][...]
        o_ref[...] = acc.astype(out_dtype)

    b_spec = pl.BlockSpec((tn, k), lambda i, j: (j, 0)) if trans_b else pl.BlockSpec((k, tn), lambda i, j: (0, j))
    in_specs = [pl.BlockSpec((tm, k), lambda i, j: (i, 0)), b_spec]
    args = [a, b]
    if residual is not None:
        in_specs.append(pl.BlockSpec((tm, tn), lambda i, j: (i, j)))
        args.append(residual)
    return pl.pallas_call(
        body, name=name, grid=(m // tm, n // tn), in_specs=in_specs,
        out_specs=pl.BlockSpec((tm, tn), lambda i, j: (i, j)),
        out_shape=jax.ShapeDtypeStruct((m, n), out_dtype),
        compiler_params=_params("parallel", "parallel"),
    )(*args)


def _matmul_tn(a, b, *, tm, tn, ts, name):
    s, m = a.shape
    n = b.shape[1]

    def body(a_ref, b_ref, o_ref):
        @pl.when(pl.program_id(2) == 0)
        def _():
            o_ref[...] = jnp.zeros_like(o_ref)

        o_ref[...] += lax.dot_general(a_ref[...].astype(BF16), b_ref[...].astype(BF16),
                                      (((0,), (0,)), ((), ())), preferred_element_type=F32)

    return pl.pallas_call(
        body, name=name, grid=(m // tm, n // tn, s // ts),
        in_specs=[pl.BlockSpec((ts, tm), lambda i, j, t: (t, i)), pl.BlockSpec((ts, tn), lambda i, j, t: (t, j))],
        out_specs=pl.BlockSpec((tm, tn), lambda i, j, t: (i, j)),
        out_shape=jax.ShapeDtypeStruct((m, n), F32),
        compiler_params=_params("parallel", "parallel", "arbitrary"),
    )(a, b)


def _rms_fwd(xs, g, *, out_dtype, tr, name):
    s = xs[0].shape[0]
    widths = [x.shape[1] for x in xs]
    wsum = sum(widths)
    nx = len(xs)

    def body(*refs):
        g_ref, o_ref = refs[nx], refs[nx + 1]
        off = 0
        for x_ref, w in zip(refs[:nx], widths):
            x = x_ref[...]
            r = lax.rsqrt(jnp.mean(x * x, axis=-1, keepdims=True) + EPS)
            o_ref[:, off:off + w] = (x * r * g_ref[:, off:off + w]).astype(out_dtype)
            off += w

    return pl.pallas_call(
        body, name=name, grid=(s // tr,),
        in_specs=[pl.BlockSpec((tr, w), lambda i: (i, 0)) for w in widths] + [pl.BlockSpec((1, wsum), lambda i: (0, 0))],
        out_specs=pl.BlockSpec((tr, wsum), lambda i: (i, 0)),
        out_shape=jax.ShapeDtypeStruct((s, wsum), out_dtype),
        compiler_params=_params("parallel"),
    )(*xs, g)


def _rms_bwd(dy, xs, g, *, residual=None, tr, name):
    s = xs[0].shape[0]
    widths = [x.shape[1] for x in xs]
    wsum = sum(widths)
    nx = len(xs)
    nin = 2 + nx + (residual is not None)

    def body(*refs):
        dy_ref, g_ref = refs[0], refs[1 + nx]
        dx_refs, dg_ref = refs[nin:nin + nx], refs[nin + nx]

        @pl.when(pl.program_id(0) == 0)
        def _():
            dg_ref[...] = jnp.zeros_like(dg_ref)

        off = 0
        for idx, (x_ref, w) in enumerate(zip(refs[1:1 + nx], widths)):
            x = x_ref[...]
            r = lax.rsqrt(jnp.mean(x * x, axis=-1, keepdims=True) + EPS)
            xh = x * r
            dyv = dy_ref[:, off:off + w]
            dxh = dyv * g_ref[:, off:off + w]
            dx = r * (dxh - xh * jnp.mean(dxh * xh, axis=-1, keepdims=True))
            if residual is not None and idx == 0:
                dx = dx + refs[2 + nx][...]
            dx_refs[idx][...] = dx
            dg_ref[:, off:off + w] += jnp.sum(dyv * xh, axis=0, keepdims=True)
            off += w

    in_specs = ([pl.BlockSpec((tr, wsum), lambda i: (i, 0))]
                + [pl.BlockSpec((tr, w), lambda i: (i, 0)) for w in widths]
                + [pl.BlockSpec((1, wsum), lambda i: (0, 0))])
    args = [dy, *xs, g]
    if residual is not None:
        in_specs.append(pl.BlockSpec((tr, widths[0]), lambda i: (i, 0)))
        args.append(residual)
    outs = pl.pallas_call(
        body, name=name, grid=(s // tr,), in_specs=in_specs,
        out_specs=[pl.BlockSpec((tr, w), lambda i: (i, 0)) for w in widths] + [pl.BlockSpec((1, wsum), lambda i: (0, 0))],
        out_shape=[jax.ShapeDtypeStruct((s, w), F32) for w in widths] + [jax.ShapeDtypeStruct((1, wsum), F32)],
        compiler_params=_params("arbitrary"),
    )(*args)
    return outs[:nx], outs[nx]


def _loss_head(x, tgt, g, *, tr, name):
    s, d = x.shape

    def body(x_ref, t_ref, g_ref, loss_ref, dx_ref, dg_ref):
        @pl.when(pl.program_id(0) == 0)
        def _():
            loss_ref[...] = jnp.zeros_like(loss_ref)
            dg_ref[...] = jnp.zeros_like(dg_ref)

        xv = x_ref[...]
        r = lax.rsqrt(jnp.mean(xv * xv, axis=-1, keepdims=True) + EPS)
        xh = xv * r
        err = xh * g_ref[...] - t_ref[...]
        loss_ref[...] += 0.5 * jnp.sum(jnp.mean(err * err, axis=-1, keepdims=True))
        dy = err * (1.0 / d)
        dxh = dy * g_ref[...]
        dx_ref[...] = r * (dxh - xh * jnp.mean(dxh * xh, axis=-1, keepdims=True))
        dg_ref[...] += jnp.sum(dy * xh, axis=0, keepdims=True)

    return pl.pallas_call(
        body, name=name, grid=(s // tr,),
        in_specs=[pl.BlockSpec((tr, d), lambda i: (i, 0)), pl.BlockSpec((tr, d), lambda i: (i, 0)),
                  pl.BlockSpec((1, d), lambda i: (0, 0))],
        out_specs=[pl.BlockSpec((8, LANES), lambda i: (0, 0)), pl.BlockSpec((tr, d), lambda i: (i, 0)),
                   pl.BlockSpec((1, d), lambda i: (0, 0))],
        out_shape=[jax.ShapeDtypeStruct((8, LANES), F32), jax.ShapeDtypeStruct((s, d), F32),
                   jax.ShapeDtypeStruct((1, d), F32)],
        compiler_params=_params("arbitrary"),
    )(x, tgt, g)


def _gates_fwd(h, bf_pad, *, tr, name):
    s = h.shape[0]

    def body(fl_ref, b_ref, c_ref, carry_ref):
        @pl.when(pl.program_id(0) == 0)
        def _():
            carry_ref[...] = jnp.zeros_like(carry_ref)

        lf = jax.nn.log_sigmoid(fl_ref[...] + b_ref[...])
        row = lax.broadcasted_iota(jnp.int32, (tr, tr), 0)
        col = lax.broadcasted_iota(jnp.int32, (tr, tr), 1)
        tri = (col <= row).astype(F32)
        c_ref[...] = jnp.dot(tri, lf, precision=lax.Precision.HIGHEST, preferred_element_type=F32) + carry_ref[...]
        carry_ref[...] += jnp.sum(lf, axis=0, keepdims=True)

    return pl.pallas_call(
        body, name=name, grid=(s // tr,),
        in_specs=[pl.BlockSpec((tr, LANES), lambda i: (i, F_COL_BLOCK)), pl.BlockSpec((1, LANES), lambda i: (0, 0))],
        out_specs=pl.BlockSpec((tr, LANES), lambda i: (i, 0)),
        out_shape=jax.ShapeDtypeStruct((s, LANES), F32),
        scratch_shapes=[pltpu.VMEM((1, LANES), F32)],
        compiler_params=_params("arbitrary"),
    )(h, bf_pad)


def _gates_bwd(dc, h, bf_pad, *, tr, name):
    s = h.shape[0]
    n = s // tr

    def body(dc_ref, fl_ref, b_ref, dfl_ref, db_ref, carry_ref):
        @pl.when(pl.program_id(0) == 0)
        def _():
            carry_ref[...] = jnp.zeros_like(carry_ref)
            db_ref[...] = jnp.zeros_like(db_ref)

        dcv = dc_ref[...]
        row = lax.broadcasted_iota(jnp.int32, (tr, tr), 0)
        col = lax.broadcasted_iota(jnp.int32, (tr, tr), 1)
        triu = (col >= row).astype(F32)
        dlf = jnp.dot(triu, dcv, precision=lax.Precision.HIGHEST, preferred_element_type=F32) + carry_ref[...]
        carry_ref[...] += jnp.sum(dcv, axis=0, keepdims=True)
        dfl = dlf * jax.nn.sigmoid(-(fl_ref[...] + b_ref[...]))
        dfl_ref[...] = dfl.astype(dfl_ref.dtype)
        db_ref[...] += jnp.sum(dfl, axis=0, keepdims=True)

    return pl.pallas_call(
        body, name=name, grid=(n,),
        in_specs=[pl.BlockSpec((tr, LANES), lambda i: (n - 1 - i, 0)),
                  pl.BlockSpec((tr, LANES), lambda i: (n - 1 - i, F_COL_BLOCK)),
                  pl.BlockSpec((1, LANES), lambda i: (0, 0))],
        out_specs=[pl.BlockSpec((tr, LANES), lambda i: (n - 1 - i, 0)), pl.BlockSpec((1, LANES), lambda i: (0, 0))],
        out_shape=[jax.ShapeDtypeStruct((s, LANES), BF16), jax.ShapeDtypeStruct((1, LANES), F32)],
        scratch_shapes=[pltpu.VMEM((1, LANES), F32)],
        compiler_params=_params("arbitrary"),
    )(dc, h, bf_pad)


LOG2E = 1.4426950408889634
V_ROWS = 80


def _attn_fwd(qat, ka, vta, *, name):
    nh, nq, _, tq = qat.shape
    nkb, kb = ka.shape[1], ka.shape[2]
    per_tile = tq // kb

    def body(qa_ref, ka_ref, vt_ref, o_ref, lse_ref):
        i = pl.program_id(1)

        def blk(n, carry, masked):
            m, acc = carry
            s = jnp.dot(ka_ref[0, n], qa_ref[0, 0], preferred_element_type=F32)
            if masked:
                key = n * kb + lax.broadcasted_iota(jnp.int32, (kb, tq), 0)
                qry = i * tq + lax.broadcasted_iota(jnp.int32, (kb, tq), 1)
                s = jnp.where(key <= qry, s, NEG)
            m_new = jnp.maximum(m, jnp.max(s, axis=0, keepdims=True))
            p = jnp.exp2(s - m_new)
            acc = jnp.exp2(m - m_new) * acc + jnp.dot(vt_ref[0, n], p.astype(BF16), preferred_element_type=F32)
            return m_new, acc

        init = (jnp.full((1, tq), NEG, F32), jnp.zeros((V_ROWS, tq), F32))
        carry = lax.fori_loop(0, i * per_tile, lambda n, c: blk(n, c, False), init)
        for d in range(per_tile):
            carry = blk(i * per_tile + d, carry, True)
        m, acc = carry
        l = acc[HEAD_DIM:HEAD_DIM + 1, :]
        o_ref[0, 0] = acc[:HEAD_DIM, :] / l
        lse_ref[0, 0] = m + jnp.log2(l)

    return pl.pallas_call(
        body, name=name, grid=(nh, nq),
        in_specs=[pl.BlockSpec((1, 1, LANES, tq), lambda h, i: (h, i, 0, 0)),
                  pl.BlockSpec((1, nkb, kb, LANES), lambda h, i: (h, 0, 0, 0)),
                  pl.BlockSpec((1, nkb, V_ROWS, kb), lambda h, i: (h, 0, 0, 0))],
        out_specs=[pl.BlockSpec((1, 1, HEAD_DIM, tq), lambda h, i: (h, i, 0, 0)),
                   pl.BlockSpec((1, 1, 1, tq), lambda h, i: (h, i, 0, 0))],
        out_shape=[jax.ShapeDtypeStruct((nh, nq, HEAD_DIM, tq), F32), jax.ShapeDtypeStruct((nh, nq, 1, tq), F32)],
        compiler_params=_params("parallel", "arbitrary"),
    )(qat, ka, vta)


def _attn_delta(dot, ot, *, name):
    nh, nq, dh, tq = ot.shape

    def body(do_ref, o_ref, d_ref):
        d_ref[0, 0] = jnp.sum(do_ref[0, 0] * o_ref[0, 0], axis=0, keepdims=True)

    spec = pl.BlockSpec((1, 1, dh, tq), lambda h, i: (h, i, 0, 0))
    return pl.pallas_call(
        body, name=name, grid=(nh, nq), in_specs=[spec, spec],
        out_specs=pl.BlockSpec((1, 1, 1, tq), lambda h, i: (h, i, 0, 0)),
        out_shape=jax.ShapeDtypeStruct((nh, nq, 1, tq), F32),
        compiler_params=_params("parallel", "parallel"),
    )(dot, ot)


def _attn_bwd(qat, qa, dot, do, dl, ka, v, k8a, *, name):
    nh, nt, _, t = qat.shape

    def body(qat_ref, qa_ref, dot_ref, do_ref, dl_ref, ka_ref, v_ref, k8_ref, dq_ref, dk_ref, dv_ref):
        j = pl.program_id(1)

        @pl.when(j == 0)
        def _():
            dq_ref[...] = jnp.zeros_like(dq_ref)

        ka_j, v_j, k8_j = ka_ref[0, 0], v_ref[0, 0], k8_ref[0, 0]

        def tile(i, carry, masked):
            dk, dv = carry
            st = jnp.dot(ka_j, qat_ref[0, i], preferred_element_type=F32)
            if masked:
                key = lax.broadcasted_iota(jnp.int32, (t, t), 0)
                qry = lax.broadcasted_iota(jnp.int32, (t, t), 1)
                st = jnp.where(key <= qry, st, NEG)
            pt = jnp.exp2(st)
            dpt = jnp.dot(v_j, dot_ref[0, i], preferred_element_type=F32)
            dsb = (pt * (dpt - dl_ref[0, i])).astype(BF16)
            dv = dv + jnp.dot(pt.astype(BF16), do_ref[0, i], preferred_element_type=F32)
            dk = dk + jnp.dot(dsb, qa_ref[0, i], preferred_element_type=F32)
            dq_ref[0, i] += lax.dot_general(dsb, k8_j, (((0,), (0,)), ((), ())), preferred_element_type=F32)
            return dk, dv

        carry = tile(j, (jnp.zeros((t, LANES), F32), jnp.zeros((t, HEAD_DIM), F32)), True)
        dk, dv = lax.fori_loop(j + 1, nt, lambda i, c: tile(i, c, False), carry)
        dk_ref[0, 0] = dk
        dv_ref[0, 0] = dv

    res = lambda *shape: pl.BlockSpec((1, nt) + shape, lambda h, j: (h, 0, 0, 0))
    blk = lambda *shape: pl.BlockSpec((1, 1) + shape, lambda h, j: (h, j, 0, 0))
    return pl.pallas_call(
        body, name=name, grid=(nh, nt),
        in_specs=[res(LANES, t), res(t, LANES), res(HEAD_DIM, t), res(t, HEAD_DIM), res(1, t),
                  blk(t, LANES), blk(t, HEAD_DIM), blk(t, LANES)],
        out_specs=[res(t, LANES), blk(t, LANES), blk(t, HEAD_DIM)],
        out_shape=[jax.ShapeDtypeStruct((nh, nt, t, LANES), F32), jax.ShapeDtypeStruct((nh, nt, t, LANES), F32),
                   jax.ShapeDtypeStruct((nh, nt, t, HEAD_DIM), F32)],
        compiler_params=_params("parallel", "arbitrary"),
    )(qat, qa, dot, do, dl, ka, v, k8a)


def _split3(x):
    rnd = lambda a: lax.reduce_precision(a, exponent_bits=8, mantissa_bits=7)
    x1 = rnd(x)
    r1 = x - x1
    x2 = rnd(r1)
    x3 = rnd(r1 - x2)
    return [x1, x2, x3]


def _heads_t(a, t):
    s, nh, w = a.shape
    return a.transpose(1, 2, 0).reshape(nh, w, s // t, t).transpose(0, 2, 1, 3)


def _heads_n(a, t):
    s, nh, w = a.shape
    return a.transpose(1, 0, 2).reshape(nh, s // t, t, w)


def _aug(core, extra, width):
    s = core.shape[0]
    cols = [core.astype(BF16)] + [c.astype(BF16)[..., None] for c in extra]
    cols.append(jnp.zeros((s, N_HEADS, width - HEAD_DIM - len(extra)), BF16))
    return jnp.concatenate(cols, axis=-1)


def _attention_fwd(q, k, v, c, *, tq, kb, name):
    s = q.shape[0]
    hd = lambda a: a.reshape(s, N_HEADS, HEAD_DIM)
    ones = [jnp.ones((s, N_HEADS), BF16)] * 3
    c2 = c * LOG2E
    q_aug = _aug(hd(q * (0.125 * LOG2E)), ones + _split3(c2), LANES)
    k_aug = _aug(hd(k), _split3(-c2) + ones, LANES)
    v_aug = _aug(hd(v), ones[:1], V_ROWS)
    ot, lse2 = _attn_fwd(_heads_t(q_aug, tq), _heads_n(k_aug, kb), _heads_t(v_aug, kb), name=name)
    attn = ot.transpose(1, 3, 0, 2).reshape(s, N_HEADS * HEAD_DIM)
    return attn, dict(q=q, k=k, v=v, c2=c2, k_aug=k_aug, ot=ot, lse2=lse2)


def _attention_bwd(dattn, res, *, t, name):
    s = dattn.shape[0]
    hd = lambda a: a.reshape(s, N_HEADS, HEAD_DIM)
    do = hd(dattn)
    tq = res["ot"].shape[-1]
    dl = _attn_delta(_heads_t(do, tq), res["ot"], name=name + "_delta").reshape(N_HEADS, s // t, 1, t)
    lse_sh = res["lse2"].reshape(N_HEADS, s).T
    zeros, ones = jnp.zeros((s, N_HEADS), BF16), jnp.ones((s, N_HEADS), BF16)
    qat = _heads_t(_aug(hd(res["q"] * (0.125 * LOG2E)), [ones] * 3 + _split3(res["c2"] - lse_sh), LANES), t)
    qa = _heads_n(_aug(hd(res["q"] * 0.125), [ones] * 3, LANES), t)
    k8a = _heads_n(_aug(hd(res["k"] * 0.125), [zeros] * 3 + [ones], LANES), t)
    dob = do.astype(BF16)
    dq_aug, dk_aug, dv4 = _attn_bwd(qat, qa, _heads_t(dob, t), _heads_n(dob, t), dl, _heads_n(res["k_aug"], t),
                                    _heads_n(hd(res["v"]).astype(BF16), t), k8a, name=name)
    unheads = lambda a4: a4.reshape(N_HEADS, s, -1)[:, :, :HEAD_DIM].transpose(1, 0, 2).reshape(s, N_HEADS * HEAD_DIM)
    dq_aug = dq_aug.reshape(N_HEADS, s, LANES)
    dk_aug = dk_aug.reshape(N_HEADS, s, LANES)
    dc = (dq_aug[:, :, HEAD_DIM + 3] - dk_aug[:, :, HEAD_DIM]).T
    return unheads(dq_aug), unheads(dk_aug), unheads(dv4), dc


def _gelu(z):
    return 0.5 * z * (1.0 + lax.erf(z * 0.7071067811865476))


def _gelu_grad(z):
    return 0.5 * (1.0 + lax.erf(z * 0.7071067811865476)) + z * (0.3989422804014327 * jnp.exp(-0.5 * z * z))


def _sgu_mask():
    i = lax.broadcasted_iota(jnp.int32, (SGU_CHUNK, SGU_CHUNK), 0) // CAUSAL_CHUNK
    j = lax.broadcasted_iota(jnp.int32, (SGU_CHUNK, SGU_CHUNK), 1) // CAUSAL_CHUNK
    return (j <= i).astype(F32)


def _layernorm_stats(x):
    mu = jnp.mean(x, axis=-1, keepdims=True)
    xc = x - mu
    rstd = lax.rsqrt(jnp.mean(xc * xc, axis=-1, keepdims=True) + EPS)
    return xc * rstd, rstd


def _first_group_lanes():
    return lax.broadcasted_iota(jnp.int32, (SGU_CHUNK, LANES), 1) < 64


def _sgu_fwd(h, ln_g, ln_b, w_s, bias_tile, *, tr, name):
    s = h.shape[0]
    zu_blk, zv_blk = 1536 // D_HALF, 2048 // D_HALF

    def body(zu_ref, zv_ref, lng_ref, lnb_ref, ws_ref, bias_ref, o_ref):
        gzu = _gelu(zu_ref[...])
        xh, _ = _layernorm_stats(_gelu(zv_ref[...]))
        zb = (xh * lng_ref[...] + lnb_ref[...]).astype(BF16)
        mask = _sgu_mask()
        first = _first_group_lanes()
        for pair in range(4):
            cols = slice(pair * LANES, (pair + 1) * LANES)
            w0 = (ws_ref[2 * pair] * mask).astype(BF16)
            w1 = (ws_ref[2 * pair + 1] * mask).astype(BF16)
            for ch in range(tr // SGU_CHUNK):
                rows = slice(ch * SGU_CHUNK, (ch + 1) * SGU_CHUNK)
                zp = zb[rows, cols]
                mixed = jnp.where(first, jnp.dot(w0, zp, preferred_element_type=F32),
                                  jnp.dot(w1, zp, preferred_element_type=F32)) + bias_ref[:, cols]
                o_ref[rows, cols] = gzu[rows, cols] * mixed

    return pl.pallas_call(
        body, name=name, grid=(s // tr,),
        in_specs=[pl.BlockSpec((tr, D_HALF), lambda i: (i, zu_blk)), pl.BlockSpec((tr, D_HALF), lambda i: (i, zv_blk)),
                  pl.BlockSpec((1, D_HALF), lambda i: (0, 0)), pl.BlockSpec((1, D_HALF), lambda i: (0, 0)),
                  pl.BlockSpec((N_HEADS, SGU_CHUNK, SGU_CHUNK), lambda i: (0, 0, 0)),
                  pl.BlockSpec((SGU_CHUNK, D_HALF), lambda i: (0, 0))],
        out_specs=pl.BlockSpec((tr, D_HALF), lambda i: (i, 0)),
        out_shape=jax.ShapeDtypeStruct((s, D_HALF), F32),
        compiler_params=_params("parallel"),
    )(h, h, ln_g, ln_b, w_s, bias_tile)


def _sgu_bwd(dsgu, h, ln_g, ln_b, w_s, bias_tile, *, tr, name):
    s = h.shape[0]
    n = s // tr
    zu_blk, zv_blk = 1536 // D_HALF, 2048 // D_HALF

    def body(ds_ref, zu_ref, zv_ref, lng_ref, lnb_ref, ws_ref, bias_ref,
             dzu_ref, dzv_ref, dws_ref, dlng_ref, dlnb_ref, dbs_ref, dgzu_sc, dzvn_sc, dbias_sc):
        step = pl.program_id(0)

        @pl.when(step == 0)
        def _():
            dws_ref[...] = jnp.zeros_like(dws_ref)
            dlng_ref[...] = jnp.zeros_like(dlng_ref)
            dlnb_ref[...] = jnp.zeros_like(dlnb_ref)
            dbias_sc[...] = jnp.zeros_like(dbias_sc)

        zu = zu_ref[...]
        zv = zv_ref[...]
        gzu = _gelu(zu)
        xh, rstd = _layernorm_stats(_gelu(zv))
        zb = (xh * lng_ref[...] + lnb_ref[...]).astype(BF16)
        ds = ds_ref[...]
        mask = _sgu_mask()
        first = _first_group_lanes()
        tn_dims = (((0,), (0,)), ((), ()))
        nt_dims = (((1,), (1,)), ((), ()))
        for pair in range(4):
            cols = slice(pair * LANES, (pair + 1) * LANES)
            w0 = (ws_ref[2 * pair] * mask).astype(BF16)
            w1 = (ws_ref[2 * pair + 1] * mask).astype(BF16)
            for ch in range(tr // SGU_CHUNK):
                rows = slice(ch * SGU_CHUNK, (ch + 1) * SGU_CHUNK)
                zp = zb[rows, cols]
                mixed = jnp.where(first, jnp.dot(w0, zp, preferred_element_type=F32),
                                  jnp.dot(w1, zp, preferred_element_type=F32)) + bias_ref[:, cols]
                dsp = ds[rows, cols]
                dgzu_sc[rows, cols] = dsp * mixed
                dm = dsp * gzu[rows, cols]
                dbias_sc[:, cols] += dm
                dmb = dm.astype(BF16)
                dm0 = jnp.where(first, dmb, jnp.zeros_like(dmb))
                dm1 = jnp.where(first, jnp.zeros_like(dmb), dmb)
                dws_ref[2 * pair] += lax.dot_general(dm0, zp, nt_dims, preferred_element_type=F32)
                dws_ref[2 * pair + 1] += lax.dot_general(dm1, zp, nt_dims, preferred_element_type=F32)
                dzvn_sc[rows, cols] = jnp.where(first, lax.dot_general(w0, dmb, tn_dims, preferred_element_type=F32),
                                                lax.dot_general(w1, dmb, tn_dims, preferred_element_type=F32))
        dzvn = dzvn_sc[...]
        dlng_ref[...] += jnp.sum(dzvn * xh, axis=0, keepdims=True)
        dlnb_ref[...] += jnp.sum(dzvn, axis=0, keepdims=True)
        dxh = dzvn * lng_ref[...]
        dgzv = rstd * (dxh - jnp.mean(dxh, axis=-1, keepdims=True) - xh * jnp.mean(dxh * xh, axis=-1, keepdims=True))
        dzv_ref[...] = (dgzv * _gelu_grad(zv)).astype(dzv_ref.dtype)
        dzu_ref[...] = (dgzu_sc[...] * _gelu_grad(zu)).astype(dzu_ref.dtype)

        @pl.when(step == n - 1)
        def _():
            for g in range(N_HEADS):
                dws_ref[g] = dws_ref[g] * mask
            lane = lax.broadcasted_iota(jnp.int32, (D_HALF, LANES), 0) // 64
            grp = lax.broadcasted_iota(jnp.int32, (D_HALF, LANES), 1)
            dbs_ref[...] = jnp.dot(dbias_sc[...], (lane == grp).astype(F32), precision=lax.Precision.HIGHEST,
                                   preferred_element_type=F32)

    const2 = lambda i: (0, 0)
    return pl.pallas_call(
        body, name=name, grid=(n,),
        in_specs=[pl.BlockSpec((tr, D_HALF), lambda i: (i, 0)),
                  pl.BlockSpec((tr, D_HALF), lambda i: (i, zu_blk)), pl.BlockSpec((tr, D_HALF), lambda i: (i, zv_blk)),
                  pl.BlockSpec((1, D_HALF), const2), pl.BlockSpec((1, D_HALF), const2),
                  pl.BlockSpec((N_HEADS, SGU_CHUNK, SGU_CHUNK), lambda i: (0, 0, 0)),
                  pl.BlockSpec((SGU_CHUNK, D_HALF), const2)],
        out_specs=[pl.BlockSpec((tr, D_HALF), lambda i: (i, 0)), pl.BlockSpec((tr, D_HALF), lambda i: (i, 0)),
                   pl.BlockSpec((N_HEADS, SGU_CHUNK, SGU_CHUNK), lambda i: (0, 0, 0)),
                   pl.BlockSpec((1, D_HALF), const2), pl.BlockSpec((1, D_HALF), const2),
                   pl.BlockSpec((SGU_CHUNK, LANES), const2)],
        out_shape=[jax.ShapeDtypeStruct((s, D_HALF), BF16), jax.ShapeDtypeStruct((s, D_HALF), BF16),
                   jax.ShapeDtypeStruct((N_HEADS, SGU_CHUNK, SGU_CHUNK), F32),
                   jax.ShapeDtypeStruct((1, D_HALF), F32), jax.ShapeDtypeStruct((1, D_HALF), F32),
                   jax.ShapeDtypeStruct((SGU_CHUNK, LANES), F32)],
        scratch_shapes=[pltpu.VMEM((tr, D_HALF), F32), pltpu.VMEM((tr, D_HALF), F32), pltpu.VMEM((SGU_CHUNK, D_HALF), F32)],
        compiler_params=_params("arbitrary"),
    )(dsgu, h, h, ln_g, ln_b, w_s, bias_tile)


FF_BLOCK = D_FF // 2


def _swiglu_fwd(gu, *, tr, name):
    s = gu.shape[0]

    def body(g_ref, u_ref, o_ref):
        g = g_ref[...]
        o_ref[...] = (g * jax.nn.sigmoid(g) * u_ref[...]).astype(o_ref.dtype)

    return pl.pallas_call(
        body, name=name, grid=(s // tr, 2),
        in_specs=[pl.BlockSpec((tr, FF_BLOCK), lambda i, j: (i, j)), pl.BlockSpec((tr, FF_BLOCK), lambda i, j: (i, j + 2))],
        out_specs=pl.BlockSpec((tr, FF_BLOCK), lambda i, j: (i, j)),
        out_shape=jax.ShapeDtypeStruct((s, D_FF), BF16),
        compiler_params=_params("parallel", "parallel"),
    )(gu, gu)


def _swiglu_bwd(dact, gu, *, tr, name):
    s = gu.shape[0]

    def body(d_ref, g_ref, u_ref, o_ref):
        g = g_ref[...]
        d = d_ref[...]
        sig = jax.nn.sigmoid(g)
        dgate = d * u_ref[...] * (sig * (1.0 + g * (1.0 - sig)))
        dup = d * (g * sig)
        o_ref[...] = jnp.where(pl.program_id(1) < 2, dgate, dup).astype(o_ref.dtype)

    return pl.pallas_call(
        body, name=name, grid=(s // tr, 4),
        in_specs=[pl.BlockSpec((tr, FF_BLOCK), lambda i, j: (i, j % 2)), pl.BlockSpec((tr, FF_BLOCK), lambda i, j: (i, j % 2)),
                  pl.BlockSpec((tr, FF_BLOCK), lambda i, j: (i, j % 2 + 2))],
        out_specs=pl.BlockSpec((tr, FF_BLOCK), lambda i, j: (i, j)),
        out_shape=jax.ShapeDtypeStruct((s, 2 * D_FF), BF16),
        compiler_params=_params("parallel", "parallel"),
    )(dact, gu, gu)


def _sum_slots(stacked, *, tr, name):
    k, r, _ = stacked.shape

    def body(x_ref, o_ref):
        acc = x_ref[0]
        for idx in range(1, k):
            acc = acc + x_ref[idx]
        o_ref[...] = acc

    return pl.pallas_call(
        body, name=name, grid=(r // tr,),
        in_specs=[pl.BlockSpec((k, tr, LANES), lambda i: (0, i, 0))],
        out_specs=pl.BlockSpec((tr, LANES), lambda i: (i, 0)),
        out_shape=jax.ShapeDtypeStruct((r, LANES), F32),
        compiler_params=_params("parallel"),
    )(stacked)


def _add2(a, b, *, tr, name):
    r = a.shape[0]

    def body(a_ref, b_ref, o_ref):
        o_ref[...] = a_ref[...] + b_ref[...]

    spec = pl.BlockSpec((tr, LANES), lambda i: (i, 0))
    return pl.pallas_call(
        body, name=name, grid=(r // tr,), in_specs=[spec, spec], out_specs=spec,
        out_shape=jax.ShapeDtypeStruct((r, LANES), F32), compiler_params=_params("parallel"),
    )(a, b)


def _adamw(w, g, m, v, *, tr, name):
    r = w.shape[0]

    def body(w_ref, g_ref, m_ref, v_ref, d_ref, m2_ref, v2_ref):
        gv = g_ref[...]
        m2 = ADAM_B1 * m_ref[...] + (1.0 - ADAM_B1) * gv
        v2 = ADAM_B2 * v_ref[...] + (1.0 - ADAM_B2) * jnp.square(gv)
        m_hat = m2 / (1.0 - ADAM_B1 ** ADAM_STEP)
        v_hat = v2 / (1.0 - ADAM_B2 ** ADAM_STEP)
        d_ref[...] = -ADAM_LR * (m_hat / (jnp.sqrt(v_hat) + ADAM_EPS) + ADAM_WD * w_ref[...])
        m2_ref[...] = m2
        v2_ref[...] = v2

    spec = pl.BlockSpec((tr, LANES), lambda i: (i, 0))
    shape = jax.ShapeDtypeStruct((r, LANES), F32)
    return pl.pallas_call(
        body, name=name, grid=(r // tr,), in_specs=[spec] * 4, out_specs=[spec] * 3, out_shape=[shape] * 3,
        compiler_params=_params("parallel"),
    )(w, g, m, v)


PAIR_CHUNKS = 5


def _coords():
    return lax.axis_index("x"), lax.axis_index("y"), lax.axis_index("c")


def _chip_peer(x, y, k):
    px = 1 - x if k & 2 else x
    py = 1 - y if k & 1 else y
    return px, py


def _allgather_chips(shard, *, name):
    r = shard.shape[0]

    def body(src, out, send_sems, recv_sems, local_sem):
        x, y, c = _coords()
        me = 2 * x + y
        mine = pltpu.make_async_copy(src, out.at[me], local_sem)
        mine.start()
        copies = []
        for k in (1, 2, 3):
            px, py = _chip_peer(x, y, k)
            cp = pltpu.make_async_remote_copy(src_ref=src, dst_ref=out.at[me], send_sem=send_sems.at[k - 1],
                                              recv_sem=recv_sems.at[k - 1], device_id=(px, py, c), device_id_type=MESH)
            cp.start()
            copies.append(cp)
        for cp in copies:
            cp.wait()
        mine.wait()

    return pl.pallas_call(
        body, name=name, in_specs=[ANY], out_specs=ANY,
        out_shape=jax.ShapeDtypeStruct((4, r, LANES), shard.dtype),
        scratch_shapes=[pltpu.SemaphoreType.DMA((3,)), pltpu.SemaphoreType.DMA((3,)), pltpu.SemaphoreType.DMA],
    )(shard)


def _pair_split(grads, *, name):
    _, r, _ = grads.shape
    rh = r // 2
    rc = rh // PAIR_CHUNKS
    nchunk = 4 * PAIR_CHUNKS

    def body(g_ref, mine_ref, theirs_ref, send_sems, recv_sems, local_sems):
        x, y, c = _coords()
        copies = []
        for j in range(4):
            for q in range(PAIR_CHUNKS):
                idx = j * PAIR_CHUNKS + q
                keep = pltpu.make_async_copy(g_ref.at[j, pl.ds(c * rh + q * rc, rc), :],
                                             mine_ref.at[j, pl.ds(q * rc, rc), :], local_sems.at[idx])
                cp = pltpu.make_async_remote_copy(
                    src_ref=g_ref.at[j, pl.ds((1 - c) * rh + q * rc, rc), :], dst_ref=theirs_ref.at[j, pl.ds(q * rc, rc), :],
                    send_sem=send_sems.at[idx], recv_sem=recv_sems.at[idx], device_id=(x, y, 1 - c), device_id_type=MESH)
                keep.start()
                cp.start()
                copies += [keep, cp]
        for cp in copies:
            cp.wait()

    shape = jax.ShapeDtypeStruct((4, rh, LANES), F32)
    return pl.pallas_call(
        body, name=name, in_specs=[ANY], out_specs=[ANY, ANY], out_shape=[shape, shape],
        scratch_shapes=[pltpu.SemaphoreType.DMA((nchunk,)), pltpu.SemaphoreType.DMA((nchunk,)),
                        pltpu.SemaphoreType.DMA((nchunk,))],
    )(grads)


def _scatter_chips(part, *, name):
    _, rh, _ = part.shape

    def body(p_ref, out, send_sems, recv_sems, local_sem):
        x, y, c = _coords()
        me = 2 * x + y
        mine = pltpu.make_async_copy(p_ref.at[me], out.at[me], local_sem)
        mine.start()
        copies = []
        for k in (1, 2, 3):
            px, py = _chip_peer(x, y, k)
            cp = pltpu.make_async_remote_copy(src_ref=p_ref.at[2 * px + py], dst_ref=out.at[me], send_sem=send_sems.at[k - 1],
                                              recv_sem=recv_sems.at[k - 1], device_id=(px, py, c), device_id_type=MESH)
            cp.start()
            copies.append(cp)
        for cp in copies:
            cp.wait()
        mine.wait()

    return pl.pallas_call(
        body, name=name, in_specs=[ANY], out_specs=ANY, out_shape=jax.ShapeDtypeStruct((4, rh, LANES), F32),
        scratch_shapes=[pltpu.SemaphoreType.DMA((3,)), pltpu.SemaphoreType.DMA((3,)), pltpu.SemaphoreType.DMA],
    )(part)


def _pair_join(half, *, name):
    rh = half.shape[0]
    nchunk = 2 * PAIR_CHUNKS
    rc = rh // nchunk

    def body(h_ref, out, send_sems, recv_sems, local_sems):
        x, y, c = _coords()
        copies = []
        for q in range(nchunk):
            src = h_ref.at[pl.ds(q * rc, rc), :]
            rows = out.at[pl.ds(c * rh + q * rc, rc), :]
            mine = pltpu.make_async_copy(src, rows, local_sems.at[q])
            cp = pltpu.make_async_remote_copy(src_ref=src, dst_ref=rows, send_sem=send_sems.at[q], recv_sem=recv_sems.at[q],
                                              device_id=(x, y, 1 - c), device_id_type=MESH)
            mine.start()
            cp.start()
            copies += [mine, cp]
        for cp in copies:
            cp.wait()

    return pl.pallas_call(
        body, name=name, in_specs=[ANY], out_specs=ANY, out_shape=jax.ShapeDtypeStruct((2 * rh, LANES), F32),
        scratch_shapes=[pltpu.SemaphoreType.DMA((nchunk,)), pltpu.SemaphoreType.DMA((nchunk,)),
                        pltpu.SemaphoreType.DMA((nchunk,))],
    )(half)


def _allgather_all(block, *, name):
    r = block.shape[0]

    def body(src, out, send_sems, recv_sems, local_sem):
        x, y, c = _coords()
        me = 4 * x + 2 * y + c
        mine = pltpu.make_async_copy(src, out.at[me], local_sem)
        mine.start()
        copies = []
        for k in range(1, 8):
            px, py = _chip_peer(x, y, k >> 1)
            pc = 1 - c if k & 1 else c
            cp = pltpu.make_async_remote_copy(src_ref=src, dst_ref=out.at[me], send_sem=send_sems.at[k - 1],
                                              recv_sem=recv_sems.at[k - 1], device_id=(px, py, pc), device_id_type=MESH)
            cp.start()
            copies.append(cp)
        for cp in copies:
            cp.wait()
        mine.wait()

    return pl.pallas_call(
        body, name=name, in_specs=[ANY], out_specs=ANY, out_shape=jax.ShapeDtypeStruct((8, r, LANES), F32),
        scratch_shapes=[pltpu.SemaphoreType.DMA((7,)), pltpu.SemaphoreType.DMA((7,)), pltpu.SemaphoreType.DMA],
    )(block)


def _flatten(arrays, pad_rows=None):
    flat = jnp.concatenate([a.reshape(-1) for a in arrays])
    if pad_rows is not None:
        flat = jnp.pad(flat, (0, pad_rows * LANES - flat.shape[0]))
    return flat.reshape(-1, LANES)


def _unflatten(flat, shapes):
    flat = flat.reshape(-1)
    out, off = [], 0
    for shp in shapes:
        size = 1
        for dim in shp:
            size *= dim
        out.append(flat[off:off + size].reshape(shp))
        off += size
    return out


def _pad_w_in(w):
    pad = jnp.zeros(w.shape[:-1] + (D_IN_PAD - D_IN,), w.dtype)
    return jnp.concatenate([w[..., :1536], w[..., 1544:], w[..., 1536:1544], pad], axis=-1)


def _unpad_w_in(w):
    return jnp.concatenate([w[..., :1536], w[..., 2560:2568], w[..., 1536:2560]], axis=-1)


def _tile(s, want):
    return min(want, s)


def _layer_fwd(x, p, l):
    s = x.shape[0]
    tr = _tile(s, 512)
    tm = _tile(s, 1024)
    xn = _rms_fwd([x], p["mix_g"], out_dtype=BF16, tr=tr, name=f"rms_mix_fwd{l}")
    h = _matmul(xn, p["w_in"], tm=tm, tn=896, out_dtype=F32, name=f"mm_in{l}")
    c = _gates_fwd(h, p["bf_pad"], tr=_tile(s, 256), name=f"gates_fwd{l}")
    attn, attn_res = _attention_fwd(h[:, 0:512], h[:, 512:1024], h[:, 1024:1536], c[:, :N_HEADS],
                                    tq=_tile(s, 2048), kb=_tile(s, 512), name=f"attn_fwd{l}")
    sgu = _sgu_fwd(h, p["ln_g"], p["ln_b"], p["w_s"], p["bias_tile"], tr=_tile(s, 256), name=f"sgu_fwd{l}")
    merged = _rms_fwd([attn, sgu], p["out_g"], out_dtype=BF16, tr=tr, name=f"rms_out_fwd{l}")
    x1 = _matmul(merged, p["w_out"], tm=tm, tn=1024, out_dtype=F32, residual=x, name=f"mm_out{l}")
    xn2 = _rms_fwd([x1], p["ffn_g"], out_dtype=BF16, tr=tr, name=f"rms_ffn_fwd{l}")
    gu = _matmul(xn2, p["w_gu"], tm=tm, tn=1408, out_dtype=F32, name=f"mm_gu{l}")
    act = _swiglu_fwd(gu, tr=tr, name=f"swiglu_fwd{l}")
    x2 = _matmul(act, p["w_down"], tm=tm, tn=1024, out_dtype=F32, residual=x1, name=f"mm_down{l}")
    saved = dict(x=x, xn=xn, h=h, attn_res=attn_res, attn=attn, sgu=sgu, merged=merged, x1=x1, xn2=xn2, gu=gu, act=act)
    return x2, saved


def _layer_bwd(dx2, p, sv, l):
    s = dx2.shape[0]
    tr = _tile(s, 512)
    tm = _tile(s, 1024)
    ts = _tile(s, 512)
    g = {}
    g["w_down"] = _matmul_tn(sv["act"], dx2, tm=1408, tn=1024, ts=ts, name=f"mm_down_dw{l}")
    dact = _matmul(dx2, p["w_down"], trans_b=True, tm=tm, tn=1408, out_dtype=F32, name=f"mm_down_dx{l}")
    dgu = _swiglu_bwd(dact, sv["gu"], tr=tr, name=f"swiglu_bwd{l}")
    g["w_gu"] = _matmul_tn(sv["xn2"], dgu, tm=1024, tn=1408, ts=ts, name=f"mm_gu_dw{l}")
    dxn2 = _matmul(dgu, p["w_gu"], trans_b=True, tm=_tile(s, 512), tn=1024, out_dtype=F32, name=f"mm_gu_dx{l}")
    (dx1,), g["ffn_g"] = _rms_bwd(dxn2, [sv["x1"]], p["ffn_g"], residual=dx2, tr=tr, name=f"rms_ffn_bwd{l}")
    g["w_out"] = _matmul_tn(sv["merged"], dx1, tm=1024, tn=1024, ts=ts, name=f"mm_out_dw{l}")
    dmerged = _matmul(dx1, p["w_out"], trans_b=True, tm=tm, tn=1024, out_dtype=F32, name=f"mm_out_dx{l}")
    (dattn, dsgu), g["out_g"] = _rms_bwd(dmerged, [sv["attn"], sv["sgu"]], p["out_g"], tr=tr, name=f"rms_out_bwd{l}")
    dzu, dzv, g["w_s"], g["ln_g"], g["ln_b"], dbs = _sgu_bwd(dsgu, sv["h"], p["ln_g"], p["ln_b"], p["w_s"], p["bias_tile"],
                                                           tr=_tile(s, 256), name=f"sgu_bwd{l}")
    g["b_s"] = dbs[:, :N_HEADS].T
    dq, dk, dv, dc8 = _attention_bwd(dattn, sv["attn_res"], t=_tile(s, 512), name=f"attn_bwd{l}")
    dc = jnp.pad(dc8, ((0, 0), (0, LANES - N_HEADS)))
    dfl, dbf = _gates_bwd(dc, sv["h"], p["bf_pad"], tr=_tile(s, 256), name=f"gates_bwd{l}")
    g["b_f"] = dbf[0, :N_HEADS]
    dh = jnp.concatenate([dq.astype(BF16), dk.astype(BF16), dv.astype(BF16), dzu, dzv, dfl], axis=1)
    g["w_in"] = _matmul_tn(sv["xn"], dh, tm=1024, tn=896, ts=ts, name=f"mm_in_dw{l}")
    dxn = _matmul(dh, p["w_in"], trans_b=True, tm=tm, tn=1024, out_dtype=F32, name=f"mm_in_dx{l}")
    (dx,), g["mix_g"] = _rms_bwd(dxn, [sv["x"]], p["mix_g"], residual=dx1, tr=tr, name=f"rms_mix_bwd{l}")
    return dx, g


def _layer_params(l, w_in_pad, w_out, w_gu, w_down, mix_norm_g, b_f, sgu_ln_g, sgu_ln_b, w_s, b_s, out_norm_g, ffn_norm_g):
    return dict(
        w_in=w_in_pad[l], w_out=w_out[l], w_gu=w_gu[l], w_down=w_down[l],
        mix_g=mix_norm_g[l][None, :], out_g=out_norm_g[l][None, :], ffn_g=ffn_norm_g[l][None, :],
        bf_pad=jnp.pad(b_f[l], (0, LANES - N_HEADS))[None, :],
        ln_g=sgu_ln_g[l][None, :], ln_b=sgu_ln_b[l][None, :], w_s=w_s[l],
        bias_tile=jnp.repeat(b_s[l].T, 64, axis=1),
    )


def _local_step(x, tgt, w_in_pad, w_out, w_gu, w_down, mix_norm_g, b_f, sgu_ln_g, sgu_ln_b, w_s, b_s, out_norm_g,
                ffn_norm_g, final_norm_g):
    depth = w_in_pad.shape[0]
    s = x.shape[0]
    params = [_layer_params(l, w_in_pad, w_out, w_gu, w_down, mix_norm_g, b_f, sgu_ln_g, sgu_ln_b, w_s, b_s, out_norm_g,
                            ffn_norm_g) for l in range(depth)]
    saved = []
    for l in range(depth):
        x, sv = _layer_fwd(x, params[l], l)
        saved.append(sv)
    loss_tile, dx, dfinal = _loss_head(x, tgt, final_norm_g[None, :], tr=_tile(s, 512), name="loss_head")
    grads = [None] * depth
    for l in reversed(range(depth)):
        dx, grads[l] = _layer_bwd(dx, params[l], saved[l], l)
    return loss_tile[0, 0], dx, grads, dfinal[0]


SMALL_ROWS = 4272


def kernel(x, mix_norm_g, w_in, b_f, sgu_ln_g, sgu_ln_b, w_s, b_s, out_norm_g, w_out, ffn_norm_g, w_gate_up, w_down, final_norm_g, loss_target, m_mix_norm_g, m_w_in, m_b_f, m_sgu_ln_g, m_sgu_ln_b, m_w_s, m_b_s, m_out_norm_g, m_w_out, m_ffn_norm_g, m_w_gate_up, m_w_down, m_final_norm_g, v_mix_norm_g, v_w_in, v_b_f, v_sgu_ln_g, v_sgu_ln_b, v_w_s, v_b_s, v_out_norm_g, v_w_out, v_ffn_norm_g, v_w_gate_up, v_w_down, v_final_norm_g):
    big = [w_in, w_out, w_gate_up, w_down]
    big_shapes = [a.shape for a in big]
    small = [mix_norm_g, b_f, sgu_ln_g, sgu_ln_b, w_s, b_s, out_norm_g, ffn_norm_g, final_norm_g]
    small_shapes = [a.shape for a in small]

    gathered = _allgather_chips(_flatten([a.astype(BF16) for a in big]), name="ag_weights")
    per_chip = [_unflatten(gathered[j], big_shapes) for j in range(4)]
    w_in_full = _pad_w_in(jnp.concatenate([pc[0] for pc in per_chip], axis=2))
    w_out_full = jnp.concatenate([pc[1] for pc in per_chip], axis=1)
    w_gu_full = jnp.concatenate([pc[2] for pc in per_chip], axis=2)
    w_down_full = jnp.concatenate([pc[3] for pc in per_chip], axis=1)

    loss_part, dx, grads, dfinal = _local_step(
        x[0], loss_target[0], w_in_full, w_out_full, w_gu_full, w_down_full, mix_norm_g, b_f, sgu_ln_g, sgu_ln_b, w_s, b_s,
        out_norm_g, ffn_norm_g, final_norm_g)
    stack = lambda key: jnp.stack([g[key] for g in grads])
    g_in = _unpad_w_in(stack("w_in"))
    g_out, g_gu, g_down = stack("w_out"), stack("w_gu"), stack("w_down")

    send = jnp.stack([_flatten([g_in[:, :, 642 * j:642 * (j + 1)], g_out[:, 256 * j:256 * (j + 1), :],
                                g_gu[:, :, 1408 * j:1408 * (j + 1)], g_down[:, 704 * j:704 * (j + 1), :]]) for j in range(4)])
    rows = send.shape[1]
    mine, theirs = _pair_split(send, name="rs_pair_split")
    pair_sum = _add2(mine.reshape(-1, LANES), theirs.reshape(-1, LANES), tr=2240, name="rs_pair_sum")
    from_chips = _scatter_chips(pair_sum.reshape(4, rows // 2, LANES), name="rs_scatter")
    half = _sum_slots(from_chips, tr=1120, name="rs_chip_sum")
    g_big_flat = _pair_join(half, name="rs_pair_join")

    g_small_local = [stack("mix_g")[:, 0], stack("b_f"), stack("ln_g")[:, 0], stack("ln_b")[:, 0], stack("w_s"), stack("b_s"),
                     stack("out_g")[:, 0], stack("ffn_g")[:, 0], dfinal]
    g_small_flat = _sum_slots(_allgather_all(_flatten(g_small_local, SMALL_ROWS), name="ar_small_gather"), tr=1424,
                              name="ar_small_sum")
    loss = lax.psum(loss_part, ("x", "y", "c"))

    d_big, m_big, v_big = _adamw(_flatten(big), g_big_flat, _flatten([m_w_in, m_w_out, m_w_gate_up, m_w_down]),
                                 _flatten([v_w_in, v_w_out, v_w_gate_up, v_w_down]), tr=2240, name="adamw_big")
    m_small = [m_mix_norm_g, m_b_f, m_sgu_ln_g, m_sgu_ln_b, m_w_s, m_b_s, m_out_norm_g, m_ffn_norm_g, m_final_norm_g]
    v_small = [v_mix_norm_g, v_b_f, v_sgu_ln_g, v_sgu_ln_b, v_w_s, v_b_s, v_out_norm_g, v_ffn_norm_g, v_final_norm_g]
    d_small, m_small2, v_small2 = _adamw(_flatten(small, SMALL_ROWS), g_small_flat, _flatten(m_small, SMALL_ROWS),
                                         _flatten(v_small, SMALL_ROWS), tr=1424, name="adamw_small")

    def in_order(big_flat, small_flat):
        b_in, b_out, b_gu, b_down = _unflatten(big_flat, big_shapes)
        s_mix, s_bf, s_lng, s_lnb, s_ws, s_bs, s_outg, s_ffn, s_fin = _unflatten(small_flat, small_shapes)
        return [s_mix, b_in, s_bf, s_lng, s_lnb, s_ws, s_bs, s_outg, b_out, s_ffn, b_gu, b_down, s_fin]

    return (loss, dx[None], *in_order(g_big_flat, g_small_flat), *in_order(d_big, d_small), *in_order(m_big, m_small2),
            *in_order(v_big, v_small2))
```

```python
import jax
import jax.numpy as jnp
from jax import lax
from jax.experimental import pallas as pl
from jax.experimental.pallas import tpu as pltpu

F32 = jnp.float32
BF16 = jnp.bfloat16

D_MODEL = 1024
D_HALF = 512
N_HEADS = 8
HEAD_DIM = 64
SGU_CHUNK = 128
CAUSAL_CHUNK = 64
D_FF = 2816
D_IN = 2568
D_IN_PAD = 2688
F_COL_BLOCK = 2560 // 128
EPS = 1e-6
NEG = -1e30
LANES = 128
VMEM_LIMIT = 56 * 1024 * 1024

ADAM_LR = 0.001
ADAM_B1 = 0.9
ADAM_B2 = 0.999
ADAM_EPS = 1e-08
ADAM_WD = 0.01
ADAM_STEP = 10

MESH = pl.DeviceIdType.MESH
ANY = pl.BlockSpec(memory_space=pl.ANY)


def _params(*sem):
    return pltpu.CompilerParams(dimension_semantics=sem, vmem_limit_bytes=VMEM_LIMIT)


def _matmul(a, b, *, trans_b=False, tm, tn, out_dtype, residual=None, name):
    m, k = a.shape
    n = b.shape[0] if trans_b else b.shape[1]
    dims = (((1,), (1,)), ((), ())) if trans_b else (((1,), (0,)), ((), ()))

    def body(*refs):
        a_ref, b_ref = refs[0], refs[1]
        o_ref = refs[-1]
        acc = lax.dot_general(a_ref[...].astype(BF16), b_ref[...].astype(BF16), dims,
                              preferred_element_type=F32)
        if residual is not None:
            acc = acc + refs[2][...]
        o_ref[...] = acc.astype(out_dtype)

    b_spec = pl.BlockSpec((tn, k), lambda i, j: (j, 0)) if trans_b else pl.BlockSpec((k, tn), lambda i, j: (0, j))
    in_specs = [pl.BlockSpec((tm, k), lambda i, j: (i, 0)), b_spec]
    args = [a, b]
    if residual is not None:
        in_specs.append(pl.BlockSpec((tm, tn), lambda i, j: (i, j)))
        args.append(residual)
    return pl.pallas_call(
        body, name=name, grid=(m // tm, n // tn), in_specs=in_specs,
        out_specs=pl.BlockSpec((tm, tn), lambda i, j: (i, j)),
        out_shape=jax.ShapeDtypeStruct((m, n), out_dtype),
        compiler_params=_params("parallel", "parallel"),
    )(*args)


def _matmul_tn(a, b, *, tm, tn, ts, name):
    s, m = a.shape
    n = b.shape[1]

    def body(a_ref, b_ref, o_ref):
        @pl.when(pl.program_id(2) == 0)
        def _():
            o_ref[...] = jnp.zeros_like(o_ref)

        o_ref[...] += lax.dot_general(a_ref[...].astype(BF16), b_ref[...].astype(BF16),
                                      (((0,), (0,)), ((), ())), preferred_element_type=F32)

    return pl.pallas_call(
        body, name=name, grid=(m // tm, n // tn, s // ts),
        in_specs=[pl.BlockSpec((ts, tm), lambda i, j, t: (t, i)), pl.BlockSpec((ts, tn), lambda i, j, t: (t, j))],
        out_specs=pl.BlockSpec((tm, tn), lambda i, j, t: (i, j)),
        out_shape=jax.ShapeDtypeStruct((m, n), F32),
        compiler_params=_params("parallel", "parallel", "arbitrary"),
    )(a, b)


def _rms_fwd(xs, g, *, out_dtype, tr, name):
    s = xs[0].shape[0]
    widths = [x.shape[1] for x in xs]
    wsum = sum(widths)
    nx = len(xs)

    def body(*refs):
        g_ref, o_ref = refs[nx], refs[nx + 1]
        off = 0
        for x_ref, w in zip(refs[:nx], widths):
            x = x_ref[...]
            r = lax.rsqrt(jnp.mean(x * x, axis=-1, keepdims=True) + EPS)
            o_ref[:, off:off + w] = (x * r * g_ref[:, off:off + w]).astype(out_dtype)
            off += w

    return pl.pallas_call(
        body, name=name, grid=(s // tr,),
        in_specs=[pl.BlockSpec((tr, w), lambda i: (i, 0)) for w in widths] + [pl.BlockSpec((1, wsum), lambda i: (0, 0))],
        out_specs=pl.BlockSpec((tr, wsum), lambda i: (i, 0)),
        out_shape=jax.ShapeDtypeStruct((s, wsum), out_dtype),
        compiler_params=_params("parallel"),
    )(*xs, g)


def _rms_bwd(dy, xs, g, *, residual=None, tr, name):
    s = xs[0].shape[0]
    widths = [x.shape[1] for x in xs]
    wsum = sum(widths)
    nx = len(xs)
    nin = 2 + nx + (residual is not None)

    def body(*refs):
        dy_ref, g_ref = refs[0], refs[1 + nx]
        dx_refs, dg_ref = refs[nin:nin + nx], refs[nin + nx]

        @pl.when(pl.program_id(0) == 0)
        def _():
            dg_ref[...] = jnp.zeros_like(dg_ref)

        off = 0
        for idx, (x_ref, w) in enumerate(zip(refs[1:1 + nx], widths)):
            x = x_ref[...]
            r = lax.rsqrt(jnp.mean(x * x, axis=-1, keepdims=True) + EPS)
            xh = x * r
            dyv = dy_ref[:, off:off + w]
            dxh = dyv * g_ref[:, off:off + w]
            dx = r * (dxh - xh * jnp.mean(dxh * xh, axis=-1, keepdims=True))
            if residual is not None and idx == 0:
                dx = dx + refs[2 + nx][...]
            dx_refs[idx][...] = dx
            dg_ref[:, off:off + w] += jnp.sum(dyv * xh, axis=0, keepdims=True)
            off += w

    in_specs = ([pl.BlockSpec((tr, wsum), lambda i: (i, 0))]
                + [pl.BlockSpec((tr, w), lambda i: (i, 0)) for w in widths]
                + [pl.BlockSpec((1, wsum), lambda i: (0, 0))])
    args = [dy, *xs, g]
    if residual is not None:
        in_specs.append(pl.BlockSpec((tr, widths[0]), lambda i: (i, 0)))
        args.append(residual)
    outs = pl.pallas_call(
        body, name=name, grid=(s // tr,), in_specs=in_specs,
        out_specs=[pl.BlockSpec((tr, w), lambda i: (i, 0)) for w in widths] + [pl.BlockSpec((1, wsum), lambda i: (0, 0))],
        out_shape=[jax.ShapeDtypeStruct((s, w), F32) for w in widths] + [jax.ShapeDtypeStruct((1, wsum), F32)],
        compiler_params=_params("arbitrary"),
    )(*args)
    return outs[:nx], outs[nx]


def _loss_head(x, tgt, g, *, tr, name):
    s, d = x.shape

    def body(x_ref, t_ref, g_ref, loss_ref, dx_ref, dg_ref):
        @pl.when(pl.program_id(0) == 0)
        def _():
            loss_ref[...] = jnp.zeros_like(loss_ref)
            dg_ref[...] = jnp.zeros_like(dg_ref)

        xv = x_ref[...]
        r = lax.rsqrt(jnp.mean(xv * xv, axis=-1, keepdims=True) + EPS)
        xh = xv * r
        err = xh * g_ref[...] - t_ref[...]
        loss_ref[...] += 0.5 * jnp.sum(jnp.mean(err * err, axis=-1, keepdims=True))
        dy = err * (1.0 / d)
        dxh = dy * g_ref[...]
        dx_ref[...] = r * (dxh - xh * jnp.mean(dxh * xh, axis=-1, keepdims=True))
        dg_ref[...] += jnp.sum(dy * xh, axis=0, keepdims=True)

    return pl.pallas_call(
        body, name=name, grid=(s // tr,),
        in_specs=[pl.BlockSpec((tr, d), lambda i: (i, 0)), pl.BlockSpec((tr, d), lambda i: (i, 0)),
                  pl.BlockSpec((1, d), lambda i: (0, 0))],
        out_specs=[pl.BlockSpec((8, LANES), lambda i: (0, 0)), pl.BlockSpec((tr, d), lambda i: (i, 0)),
                   pl.BlockSpec((1, d), lambda i: (0, 0))],
        out_shape=[jax.ShapeDtypeStruct((8, LANES), F32), jax.ShapeDtypeStruct((s, d), F32),
                   jax.ShapeDtypeStruct((1, d), F32)],
        compiler_params=_params("arbitrary"),
    )(x, tgt, g)


def _gates_fwd(h, bf_pad, *, tr, name):
    s = h.shape[0]

    def body(fl_ref, b_ref, c_ref, carry_ref):
        @pl.when(pl.program_id(0) == 0)
        def _():
            carry_ref[...] = jnp.zeros_like(carry_ref)

        lf = jax.nn.log_sigmoid(fl_ref[...] + b_ref[...])
        row = lax.broadcasted_iota(jnp.int32, (tr, tr), 0)
        col = lax.broadcasted_iota(jnp.int32, (tr, tr), 1)
        tri = (col <= row).astype(F32)
        c_ref[...] = jnp.dot(tri, lf, precision=lax.Precision.HIGHEST, preferred_element_type=F32) + carry_ref[...]
        carry_ref[...] += jnp.sum(lf, axis=0, keepdims=True)

    return pl.pallas_call(
        body, name=name, grid=(s // tr,),
        in_specs=[pl.BlockSpec((tr, LANES), lambda i: (i, F_COL_BLOCK)), pl.BlockSpec((1, LANES), lambda i: (0, 0))],
        out_specs=pl.BlockSpec((tr, LANES), lambda i: (i, 0)),
        out_shape=jax.ShapeDtypeStruct((s, LANES), F32),
        scratch_shapes=[pltpu.VMEM((1, LANES), F32)],
        compiler_params=_params("arbitrary"),
    )(h, bf_pad)


def _gates_bwd(dc, h, bf_pad, *, tr, name):
    s = h.shape[0]
    n = s // tr

    def body(dc_ref, fl_ref, b_ref, dfl_ref, db_ref, carry_ref):
        @pl.when(pl.program_id(0) == 0)
        def _():
            carry_ref[...] = jnp.zeros_like(carry_ref)
            db_ref[...] = jnp.zeros_like(db_ref)

        dcv = dc_ref[...]
        row = lax.broadcasted_iota(jnp.int32, (tr, tr), 0)
        col = lax.broadcasted_iota(jnp.int32, (tr, tr), 1)
        triu = (col >= row).astype(F32)
        dlf = jnp.dot(triu, dcv, precision=lax.Precision.HIGHEST, preferred_element_type=F32) + carry_ref[...]
        carry_ref[...] += jnp.sum(dcv, axis=0, keepdims=True)
        dfl = dlf * jax.nn.sigmoid(-(fl_ref[...] + b_ref[...]))
        dfl_ref[...] = dfl.astype(dfl_ref.dtype)
        db_ref[...] += jnp.sum(dfl, axis=0, keepdims=True)

    return pl.pallas_call(
        body, name=name, grid=(n,),
        in_specs=[pl.BlockSpec((tr, LANES), lambda i: (n - 1 - i, 0)),
                  pl.BlockSpec((tr, LANES), lambda i: (n - 1 - i, F_COL_BLOCK)),
                  pl.BlockSpec((1, LANES), lambda i: (0, 0))],
        out_specs=[pl.BlockSpec((tr, LANES), lambda i: (n - 1 - i, 0)), pl.BlockSpec((1, LANES), lambda i: (0, 0))],
        out_shape=[jax.ShapeDtypeStruct((s, LANES), BF16), jax.ShapeDtypeStruct((1, LANES), F32)],
        scratch_shapes=[pltpu.VMEM((1, LANES), F32)],
        compiler_params=_params("arbitrary"),
    )(dc, h, bf_pad)


LOG2E = 1.4426950408889634
LN2 = 0.6931471805599453
V_ROWS = 80


def _transpose_bf16(a):
    return a.astype(F32).T.astype(BF16)


def _transpose_heads(a, *, t, name):
    s = a.shape[0]

    def body(a_ref, o_ref):
        o_ref[0, 0] = _transpose_bf16(a_ref[...])

    return pl.pallas_call(
        body, name=name, grid=(N_HEADS, s // t),
        in_specs=[pl.BlockSpec((t, LANES), lambda h, i: (i, h))],
        out_specs=pl.BlockSpec((1, 1, LANES, t), lambda h, i: (h, i, 0, 0)),
        out_shape=jax.ShapeDtypeStruct((N_HEADS, s // t, LANES, t), BF16),
        compiler_params=_params("parallel", "parallel"),
    )(a)


def _attn_fwd(qat, ka, va, *, tq, kb, name):
    s = ka.shape[0]
    nq = s // tq
    per_tile = tq // kb
    tt = qat.shape[3]
    sub = tq // tt

    def body(qat_ref, ka_ref, va_ref, o_ref, lse_ref):
        i = pl.program_id(1)
        qat = jnp.concatenate([qat_ref[0, d] for d in range(sub)], axis=1)

        def blk(n, carry, masked):
            m, acc = carry
            rows = pl.ds(pl.multiple_of(n * kb, kb), kb)
            sc = jnp.dot(ka_ref[rows, :], qat, preferred_element_type=F32)
            if masked:
                key = n * kb + lax.broadcasted_iota(jnp.int32, (kb, tq), 0)
                qry = i * tq + lax.broadcasted_iota(jnp.int32, (kb, tq), 1)
                sc = jnp.where(key <= qry, sc, NEG)
            m_new = jnp.maximum(m, jnp.max(sc, axis=0, keepdims=True))
            p = jnp.exp2(sc - m_new)
            vt = _transpose_bf16(va_ref[rows, :])[:V_ROWS]
            acc = jnp.exp2(m - m_new) * acc + jnp.dot(vt, p.astype(BF16), preferred_element_type=F32)
            return m_new, acc

        init = (jnp.full((1, tq), NEG, F32), jnp.zeros((V_ROWS, tq), F32))
        carry = lax.fori_loop(0, i * per_tile, lambda n, c: blk(n, c, False), init)
        for d in range(per_tile):
            carry = blk(i * per_tile + d, carry, True)
        m, acc = carry
        l = acc[HEAD_DIM:HEAD_DIM + 1, :]
        padded = jnp.concatenate([acc / l, jnp.zeros((LANES - V_ROWS, tq), F32)], axis=0)
        o_ref[0, 0] = padded.T[:, :HEAD_DIM]
        lse_ref[0, 0] = m + jnp.log2(l)

    return pl.pallas_call(
        body, name=name, grid=(N_HEADS, nq),
        in_specs=[pl.BlockSpec((1, sub, LANES, tt), lambda h, i: (h, i, 0, 0)),
                  pl.BlockSpec((s, LANES), lambda h, i: (0, h)),
                  pl.BlockSpec((s, LANES), lambda h, i: (0, h))],
        out_specs=[pl.BlockSpec((1, 1, tq, HEAD_DIM), lambda h, i: (h, i, 0, 0)),
                   pl.BlockSpec((1, 1, 1, tq), lambda h, i: (h, i, 0, 0))],
        out_shape=[jax.ShapeDtypeStruct((N_HEADS, nq, tq, HEAD_DIM), F32), jax.ShapeDtypeStruct((N_HEADS, nq, 1, tq), F32)],
        compiler_params=_params("parallel", "arbitrary"),
    )(qat, ka, va)


def _attn_delta(do4, o4, *, name):
    nh, nq, tq, dh = o4.shape

    def body(do_ref, o_ref, d_ref):
        d_ref[0, 0] = jnp.sum(do_ref[0, 0] * o_ref[0, 0], axis=1, keepdims=True)

    spec = pl.BlockSpec((1, 1, tq, dh), lambda h, i: (h, i, 0, 0))
    return pl.pallas_call(
        body, name=name, grid=(nh, nq), in_specs=[spec, spec],
        out_specs=pl.BlockSpec((1, 1, tq, 1), lambda h, i: (h, i, 0, 0)),
        out_shape=jax.ShapeDtypeStruct((nh, nq, tq, 1), F32),
        compiler_params=_params("parallel", "parallel"),
    )(do4, o4)


def _attn_bwd(qa, qat, doa, dot, lse, dl, ka, va, *, name):
    s = qa.shape[0]
    nt, t = qat.shape[1], qat.shape[3]

    def body(qa_ref, qat_ref, do_ref, dot_ref, lse_ref, dl_ref, ka_ref, va_ref, dq_ref, dk_ref, dv_ref):
        j = pl.program_id(1)

        @pl.when(j == 0)
        def _():
            dq_ref[...] = jnp.zeros_like(dq_ref)

        ka_j, va_j = ka_ref[...], va_ref[...]

        def tile(i, carry, masked):
            dk, dv = carry
            rows = pl.ds(pl.multiple_of(i * t, t), t)
            qa_i, do_i = qa_ref[rows, :], do_ref[rows, :]
            st = jnp.dot(ka_j, qat_ref[0, i], preferred_element_type=F32) - lse_ref[0, i]
            if masked:
                key = lax.broadcasted_iota(jnp.int32, (t, t), 0)
                qry = lax.broadcasted_iota(jnp.int32, (t, t), 1)
                st = jnp.where(key <= qry, st, NEG)
            pt = jnp.exp2(st)
            dpt = jnp.dot(va_j, dot_ref[0, i], preferred_element_type=F32)
            dsb = (pt * (dpt - dl_ref[0, i])).astype(BF16)
            dv = dv + jnp.dot(pt.astype(BF16), do_i, preferred_element_type=F32)
            dk = dk + jnp.dot(dsb, qa_i, preferred_element_type=F32)
            dq_ref[rows, :] += lax.dot_general(dsb, ka_j, (((0,), (0,)), ((), ())), preferred_element_type=F32)
            return dk, dv

        carry = tile(j, (jnp.zeros((t, LANES), F32), jnp.zeros((t, LANES), F32)), True)
        dk, dv = lax.fori_loop(j + 1, nt, lambda i, c: tile(i, c, False), carry)
        dk_ref[...] = dk
        dv_ref[...] = dv

    res = pl.BlockSpec((s, LANES), lambda h, j: (0, h))
    rest = pl.BlockSpec((1, nt, LANES, t), lambda h, j: (h, 0, 0, 0))
    row = pl.BlockSpec((1, nt, 1, t), lambda h, j: (h, 0, 0, 0))
    blk = pl.BlockSpec((t, LANES), lambda h, j: (j, h))
    shape = jax.ShapeDtypeStruct((s, N_HEADS * LANES), F32)
    return pl.pallas_call(
        body, name=name, grid=(N_HEADS, nt),
        in_specs=[res, rest, res, rest, row, row, blk, blk], out_specs=[res, blk, blk], out_shape=[shape, shape, shape],
        compiler_params=_params("parallel", "arbitrary"),
    )(qa, qat, doa, dot, lse, dl, ka, va)


def _split3(x):
    rnd = lambda a: lax.reduce_precision(a, exponent_bits=8, mantissa_bits=7)
    x1 = rnd(x)
    r1 = x - x1
    x2 = rnd(r1)
    x3 = rnd(r1 - x2)
    return [x1, x2, x3]


def _head_blocks(core, extra):
    s = core.shape[0]
    cols = [core.reshape(s, N_HEADS, HEAD_DIM).astype(BF16)] + [c.astype(BF16)[..., None] for c in extra]
    cols.append(jnp.zeros((s, N_HEADS, LANES - HEAD_DIM - len(extra)), BF16))
    return jnp.concatenate(cols, axis=-1).reshape(s, N_HEADS * LANES)


def _heads_n(a, t):
    s = a.shape[0]
    return a.reshape(s, N_HEADS, HEAD_DIM).transpose(1, 0, 2).reshape(N_HEADS, s // t, t, HEAD_DIM)


def _attention_fwd(q, k, v, c, *, tq, kb, t, name):
    s = q.shape[0]
    ones = [jnp.ones((s, N_HEADS), BF16)] * 3
    c2 = c * LOG2E
    qa = _head_blocks(q * 0.125, ones + _split3(c2))
    ka = _head_blocks(k * LOG2E, _split3(-c2) + ones)
    va = _head_blocks(v, ones[:1])
    qat = _transpose_heads(qa, t=t, name=name + "_qt")
    o4, lse2 = _attn_fwd(qat, ka, va, tq=tq, kb=kb, name=name)
    attn = o4.reshape(N_HEADS, s, HEAD_DIM).transpose(1, 0, 2).reshape(s, N_HEADS * HEAD_DIM)
    return attn, dict(qa=qa, qat=qat, ka=ka, va=va, attn=attn, lse2=lse2)


def _attention_bwd(dattn, res, *, name):
    s = dattn.shape[0]
    t = res["qat"].shape[3]
    dl = _attn_delta(_heads_n(dattn, t), _heads_n(res["attn"], t), name=name + "_delta")
    doa = _head_blocks(dattn, [])
    dqa, dka, dva = _attn_bwd(res["qa"], res["qat"], doa, _transpose_heads(doa, t=t, name=name + "_dot"),
                              res["lse2"].reshape(N_HEADS, s // t, 1, t), dl.reshape(N_HEADS, s // t, 1, t),
                              res["ka"], res["va"], name=name)
    dqa, dka, dva = (a.reshape(s, N_HEADS, LANES) for a in (dqa, dka, dva))
    core = lambda a: a[:, :, :HEAD_DIM].reshape(s, N_HEADS * HEAD_DIM)
    dc = dqa[:, :, HEAD_DIM + 3] - dka[:, :, HEAD_DIM]
    return core(dqa) * (LN2 * 0.125), core(dka), core(dva), dc


def _gelu(z):
    return 0.5 * z * (1.0 + lax.erf(z * 0.7071067811865476))


def _gelu_grad(z):
    return 0.5 * (1.0 + lax.erf(z * 0.7071067811865476)) + z * (0.3989422804014327 * jnp.exp(-0.5 * z * z))


def _sgu_mask():
    i = lax.broadcasted_iota(jnp.int32, (SGU_CHUNK, SGU_CHUNK), 0) // CAUSAL_CHUNK
    j = lax.broadcasted_iota(jnp.int32, (SGU_CHUNK, SGU_CHUNK), 1) // CAUSAL_CHUNK
    return (j <= i).astype(F32)


def _layernorm_stats(x):
    mu = jnp.mean(x, axis=-1, keepdims=True)
    xc = x - mu
    rstd = lax.rsqrt(jnp.mean(xc * xc, axis=-1, keepdims=True) + EPS)
    return xc * rstd, rstd


def _first_group_lanes():
    return lax.broadcasted_iota(jnp.int32, (SGU_CHUNK, LANES), 1) < 64


def _sgu_fwd(h, ln_g, ln_b, w_s, bias_tile, *, tr, name):
    s = h.shape[0]
    zu_blk, zv_blk = 1536 // D_HALF, 2048 // D_HALF

    def body(zu_ref, zv_ref, lng_ref, lnb_ref, ws_ref, bias_ref, o_ref):
        gzu = _gelu(zu_ref[...])
        xh, _ = _layernorm_stats(_gelu(zv_ref[...]))
        zb = (xh * lng_ref[...] + lnb_ref[...]).astype(BF16)
        mask = _sgu_mask()
        first = _first_group_lanes()
        for pair in range(4):
            cols = slice(pair * LANES, (pair + 1) * LANES)
            w0 = (ws_ref[2 * pair] * mask).astype(BF16)
            w1 = (ws_ref[2 * pair + 1] * mask).astype(BF16)
            for ch in range(tr // SGU_CHUNK):
                rows = slice(ch * SGU_CHUNK, (ch + 1) * SGU_CHUNK)
                zp = zb[rows, cols]
                mixed = jnp.where(first, jnp.dot(w0, zp, preferred_element_type=F32),
                                  jnp.dot(w1, zp, preferred_element_type=F32)) + bias_ref[:, cols]
                o_ref[rows, cols] = gzu[rows, cols] * mixed

    return pl.pallas_call(
        body, name=name, grid=(s // tr,),
        in_specs=[pl.BlockSpec((tr, D_HALF), lambda i: (i, zu_blk)), pl.BlockSpec((tr, D_HALF), lambda i: (i, zv_blk)),
                  pl.BlockSpec((1, D_HALF), lambda i: (0, 0)), pl.BlockSpec((1, D_HALF), lambda i: (0, 0)),
                  pl.BlockSpec((N_HEADS, SGU_CHUNK, SGU_CHUNK), lambda i: (0, 0, 0)),
                  pl.BlockSpec((SGU_CHUNK, D_HALF), lambda i: (0, 0))],
        out_specs=pl.BlockSpec((tr, D_HALF), lambda i: (i, 0)),
        out_shape=jax.ShapeDtypeStruct((s, D_HALF), F32),
        compiler_params=_params("parallel"),
    )(h, h, ln_g, ln_b, w_s, bias_tile)


def _sgu_bwd(dsgu, h, ln_g, ln_b, w_s, bias_tile, *, tr, name):
    s = h.shape[0]
    n = s // tr
    zu_blk, zv_blk = 1536 // D_HALF, 2048 // D_HALF

    def body(ds_ref, zu_ref, zv_ref, lng_ref, lnb_ref, ws_ref, bias_ref,
             dzu_ref, dzv_ref, dws_ref, dlng_ref, dlnb_ref, dbs_ref, dgzu_sc, dzvn_sc, dbias_sc):
        step = pl.program_id(0)

        @pl.when(step == 0)
        def _():
            dws_ref[...] = jnp.zeros_like(dws_ref)
            dlng_ref[...] = jnp.zeros_like(dlng_ref)
            dlnb_ref[...] = jnp.zeros_like(dlnb_ref)
            dbias_sc[...] = jnp.zeros_like(dbias_sc)

        zu = zu_ref[...]
        zv = zv_ref[...]
        gzu = _gelu(zu)
        xh, rstd = _layernorm_stats(_gelu(zv))
        zb = (xh * lng_ref[...] + lnb_ref[...]).astype(BF16)
        ds = ds_ref[...]
        mask = _sgu_mask()
        first = _first_group_lanes()
        tn_dims = (((0,), (0,)), ((), ()))
        nt_dims = (((1,), (1,)), ((), ()))
        for pair in range(4):
            cols = slice(pair * LANES, (pair + 1) * LANES)
            w0 = (ws_ref[2 * pair] * mask).astype(BF16)
            w1 = (ws_ref[2 * pair + 1] * mask).astype(BF16)
            for ch in range(tr // SGU_CHUNK):
                rows = slice(ch * SGU_CHUNK, (ch + 1) * SGU_CHUNK)
                zp = zb[rows, cols]
                mixed = jnp.where(first, jnp.dot(w0, zp, preferred_element_type=F32),
                                  jnp.dot(w1, zp, preferred_element_type=F32)) + bias_ref[:, cols]
                dsp = ds[rows, cols]
                dgzu_sc[rows, cols] = dsp * mixed
                dm = dsp * gzu[rows, cols]
                dbias_sc[:, cols] += dm
                dmb = dm.astype(BF16)
                dm0 = jnp.where(first, dmb, jnp.zeros_like(dmb))
                dm1 = jnp.where(first, jnp.zeros_like(dmb), dmb)
                dws_ref[2 * pair] += lax.dot_general(dm0, zp, nt_dims, preferred_element_type=F32)
                dws_ref[2 * pair + 1] += lax.dot_general(dm1, zp, nt_dims, preferred_element_type=F32)
                dzvn_sc[rows, cols] = jnp.where(first, lax.dot_general(w0, dmb, tn_dims, preferred_element_type=F32),
                                                lax.dot_general(w1, dmb, tn_dims, preferred_element_type=F32))
        dzvn = dzvn_sc[...]
        dlng_ref[...] += jnp.sum(dzvn * xh, axis=0, keepdims=True)
        dlnb_ref[...] += jnp.sum(dzvn, axis=0, keepdims=True)
        dxh = dzvn * lng_ref[...]
        dgzv = rstd * (dxh - jnp.mean(dxh, axis=-1, keepdims=True) - xh * jnp.mean(dxh * xh, axis=-1, keepdims=True))
        dzv_ref[...] = (dgzv * _gelu_grad(zv)).astype(dzv_ref.dtype)
        dzu_ref[...] = (dgzu_sc[...] * _gelu_grad(zu)).astype(dzu_ref.dtype)

        @pl.when(step == n - 1)
        def _():
            for g in range(N_HEADS):
                dws_ref[g] = dws_ref[g] * mask
            lane = lax.broadcasted_iota(jnp.int32, (D_HALF, LANES), 0) // 64
            grp = lax.broadcasted_iota(jnp.int32, (D_HALF, LANES), 1)
            dbs_ref[...] = jnp.dot(dbias_sc[...], (lane == grp).astype(F32), precision=lax.Precision.HIGHEST,
                                   preferred_element_type=F32)

    const2 = lambda i: (0, 0)
    return pl.pallas_call(
        body, name=name, grid=(n,),
        in_specs=[pl.BlockSpec((tr, D_HALF), lambda i: (i, 0)),
                  pl.BlockSpec((tr, D_HALF), lambda i: (i, zu_blk)), pl.BlockSpec((tr, D_HALF), lambda i: (i, zv_blk)),
                  pl.BlockSpec((1, D_HALF), const2), pl.BlockSpec((1, D_HALF), const2),
                  pl.BlockSpec((N_HEADS, SGU_CHUNK, SGU_CHUNK), lambda i: (0, 0, 0)),
                  pl.BlockSpec((SGU_CHUNK, D_HALF), const2)],
        out_specs=[pl.BlockSpec((tr, D_HALF), lambda i: (i, 0)), pl.BlockSpec((tr, D_HALF), lambda i: (i, 0)),
                   pl.BlockSpec((N_HEADS, SGU_CHUNK, SGU_CHUNK), lambda i: (0, 0, 0)),
                   pl.BlockSpec((1, D_HALF), const2), pl.BlockSpec((1, D_HALF), const2),
                   pl.BlockSpec((SGU_CHUNK, LANES), const2)],
        out_shape=[jax.ShapeDtypeStruct((s, D_HALF), BF16), jax.ShapeDtypeStruct((s, D_HALF), BF16),
                   jax.ShapeDtypeStruct((N_HEADS, SGU_CHUNK, SGU_CHUNK), F32),
                   jax.ShapeDtypeStruct((1, D_HALF), F32), jax.ShapeDtypeStruct((1, D_HALF), F32),
                   jax.ShapeDtypeStruct((SGU_CHUNK, LANES), F32)],
        scratch_shapes=[pltpu.VMEM((tr, D_HALF), F32), pltpu.VMEM((tr, D_HALF), F32), pltpu.VMEM((SGU_CHUNK, D_HALF), F32)],
        compiler_params=_params("arbitrary"),
    )(dsgu, h, h, ln_g, ln_b, w_s, bias_tile)


FF_BLOCK = D_FF // 2


def _swiglu_fwd(gu, *, tr, name):
    s = gu.shape[0]

    def body(g_ref, u_ref, o_ref):
        g = g_ref[...]
        o_ref[...] = (g * jax.nn.sigmoid(g) * u_ref[...]).astype(o_ref.dtype)

    return pl.pallas_call(
        body, name=name, grid=(s // tr, 2),
        in_specs=[pl.BlockSpec((tr, FF_BLOCK), lambda i, j: (i, j)), pl.BlockSpec((tr, FF_BLOCK), lambda i, j: (i, j + 2))],
        out_specs=pl.BlockSpec((tr, FF_BLOCK), lambda i, j: (i, j)),
        out_shape=jax.ShapeDtypeStruct((s, D_FF), BF16),
        compiler_params=_params("parallel", "parallel"),
    )(gu, gu)


def _swiglu_bwd(dact, gu, *, tr, name):
    s = gu.shape[0]

    def body(d_ref, g_ref, u_ref, o_ref):
        g = g_ref[...]
        d = d_ref[...]
        sig = jax.nn.sigmoid(g)
        dgate = d * u_ref[...] * (sig * (1.0 + g * (1.0 - sig)))
        dup = d * (g * sig)
        o_ref[...] = jnp.where(pl.program_id(1) < 2, dgate, dup).astype(o_ref.dtype)

    return pl.pallas_call(
        body, name=name, grid=(s // tr, 4),
        in_specs=[pl.BlockSpec((tr, FF_BLOCK), lambda i, j: (i, j % 2)), pl.BlockSpec((tr, FF_BLOCK), lambda i, j: (i, j % 2)),
                  pl.BlockSpec((tr, FF_BLOCK), lambda i, j: (i, j % 2 + 2))],
        out_specs=pl.BlockSpec((tr, FF_BLOCK), lambda i, j: (i, j)),
        out_shape=jax.ShapeDtypeStruct((s, 2 * D_FF), BF16),
        compiler_params=_params("parallel", "parallel"),
    )(dact, gu, gu)


def _sum_slots(stacked, *, tr, name):
    k, r, _ = stacked.shape

    def body(x_ref, o_ref):
        acc = x_ref[0]
        for idx in range(1, k):
            acc = acc + x_ref[idx]
        o_ref[...] = acc

    return pl.pallas_call(
        body, name=name, grid=(r // tr,),
        in_specs=[pl.BlockSpec((k, tr, LANES), lambda i: (0, i, 0))],
        out_specs=pl.BlockSpec((tr, LANES), lambda i: (i, 0)),
        out_shape=jax.ShapeDtypeStruct((r, LANES), F32),
        compiler_params=_params("parallel"),
    )(stacked)


def _adamw(w, g, m, v, *, tr, name):
    r = w.shape[0]

    def body(w_ref, g_ref, m_ref, v_ref, d_ref, m2_ref, v2_ref):
        gv = g_ref[...]
        m2 = ADAM_B1 * m_ref[...] + (1.0 - ADAM_B1) * gv
        v2 = ADAM_B2 * v_ref[...] + (1.0 - ADAM_B2) * jnp.square(gv)
        m_hat = m2 / (1.0 - ADAM_B1 ** ADAM_STEP)
        v_hat = v2 / (1.0 - ADAM_B2 ** ADAM_STEP)
        d_ref[...] = -ADAM_LR * (m_hat / (jnp.sqrt(v_hat) + ADAM_EPS) + ADAM_WD * w_ref[...])
        m2_ref[...] = m2
        v2_ref[...] = v2

    spec = pl.BlockSpec((tr, LANES), lambda i: (i, 0))
    shape = jax.ShapeDtypeStruct((r, LANES), F32)
    return pl.pallas_call(
        body, name=name, grid=(r // tr,), in_specs=[spec] * 4, out_specs=[spec] * 3, out_shape=[shape] * 3,
        compiler_params=_params("parallel"),
    )(w, g, m, v)


PAIR_CHUNKS = 5


def _coords():
    return lax.axis_index("x"), lax.axis_index("y"), lax.axis_index("c")


def _chip_peer(x, y, k):
    px = 1 - x if k & 2 else x
    py = 1 - y if k & 1 else y
    return px, py


def _allgather_chips(shard, *, name):
    r = shard.shape[0]

    def body(src, out, send_sems, recv_sems, local_sem):
        x, y, c = _coords()
        me = 2 * x + y
        mine = pltpu.make_async_copy(src, out.at[me], local_sem)
        mine.start()
        copies = []
        for k in (1, 2, 3):
            px, py = _chip_peer(x, y, k)
            cp = pltpu.make_async_remote_copy(src_ref=src, dst_ref=out.at[me], send_sem=send_sems.at[k - 1],
                                              recv_sem=recv_sems.at[k - 1], device_id=(px, py, c), device_id_type=MESH)
            cp.start()
            copies.append(cp)
        for cp in copies:
            cp.wait()
        mine.wait()

    return pl.pallas_call(
        body, name=name, in_specs=[ANY], out_specs=ANY,
        out_shape=jax.ShapeDtypeStruct((4, r, LANES), shard.dtype),
        scratch_shapes=[pltpu.SemaphoreType.DMA((3,)), pltpu.SemaphoreType.DMA((3,)), pltpu.SemaphoreType.DMA],
    )(shard)


def _pair_split(grads, *, name):
    _, r, _ = grads.shape
    rh = r // 2
    rc = rh // PAIR_CHUNKS
    nchunk = 4 * PAIR_CHUNKS

    def body(g_ref, theirs_ref, send_sems, recv_sems):
        x, y, c = _coords()
        copies = []
        for j in range(4):
            for q in range(PAIR_CHUNKS):
                idx = j * PAIR_CHUNKS + q
                cp = pltpu.make_async_remote_copy(
                    src_ref=g_ref.at[j, pl.ds((1 - c) * rh + q * rc, rc), :], dst_ref=theirs_ref.at[j, pl.ds(q * rc, rc), :],
                    send_sem=send_sems.at[idx], recv_sem=recv_sems.at[idx], device_id=(x, y, 1 - c), device_id_type=MESH)
                cp.start()
                copies.append(cp)
        for cp in copies:
            cp.wait()

    return pl.pallas_call(
        body, name=name, in_specs=[ANY], out_specs=ANY, out_shape=jax.ShapeDtypeStruct((4, rh, LANES), F32),
        scratch_shapes=[pltpu.SemaphoreType.DMA((nchunk,)), pltpu.SemaphoreType.DMA((nchunk,))],
    )(grads)


def _pair_sum(grads, theirs, half, *, tr, name):
    _, rh, _ = theirs.shape
    nrt = rh // tr

    def body(half_ref, g_ref, t_ref, o_ref):
        o_ref[...] = g_ref[...] + t_ref[...]

    return pl.pallas_call(
        body, name=name,
        grid_spec=pltpu.PrefetchScalarGridSpec(
            num_scalar_prefetch=1, grid=(4, nrt),
            in_specs=[pl.BlockSpec((1, tr, LANES), lambda j, i, half_ref: (j, half_ref[0] * nrt + i, 0)),
                      pl.BlockSpec((1, tr, LANES), lambda j, i, half_ref: (j, i, 0))],
            out_specs=pl.BlockSpec((1, tr, LANES), lambda j, i, half_ref: (j, i, 0))),
        out_shape=jax.ShapeDtypeStruct((4, rh, LANES), F32),
        compiler_params=_params("parallel", "parallel"),
    )(half, grads, theirs)


def _scatter_chips(part, *, name):
    _, rh, _ = part.shape

    def body(p_ref, out, send_sems, recv_sems, local_sem):
        x, y, c = _coords()
        me = 2 * x + y
        mine = pltpu.make_async_copy(p_ref.at[me], out.at[me], local_sem)
        mine.start()
        copies = []
        for k in (1, 2, 3):
            px, py = _chip_peer(x, y, k)
            cp = pltpu.make_async_remote_copy(src_ref=p_ref.at[2 * px + py], dst_ref=out.at[me], send_sem=send_sems.at[k - 1],
                                              recv_sem=recv_sems.at[k - 1], device_id=(px, py, c), device_id_type=MESH)
            cp.start()
            copies.append(cp)
        for cp in copies:
            cp.wait()
        mine.wait()

    return pl.pallas_call(
        body, name=name, in_specs=[ANY], out_specs=ANY, out_shape=jax.ShapeDtypeStruct((4, rh, LANES), F32),
        scratch_shapes=[pltpu.SemaphoreType.DMA((3,)), pltpu.SemaphoreType.DMA((3,)), pltpu.SemaphoreType.DMA],
    )(part)


def _pair_join(half, *, name):
    rh = half.shape[0]
    nchunk = 2 * PAIR_CHUNKS
    rc = rh // nchunk

    def body(h_ref, out, send_sems, recv_sems, local_sems):
        x, y, c = _coords()
        copies = []
        for q in range(nchunk):
            src = h_ref.at[pl.ds(q * rc, rc), :]
            rows = out.at[pl.ds(c * rh + q * rc, rc), :]
            mine = pltpu.make_async_copy(src, rows, local_sems.at[q])
            cp = pltpu.make_async_remote_copy(src_ref=src, dst_ref=rows, send_sem=send_sems.at[q], recv_sem=recv_sems.at[q],
                                              device_id=(x, y, 1 - c), device_id_type=MESH)
            mine.start()
            cp.start()
            copies += [mine, cp]
        for cp in copies:
            cp.wait()

    return pl.pallas_call(
        body, name=name, in_specs=[ANY], out_specs=ANY, out_shape=jax.ShapeDtypeStruct((2 * rh, LANES), F32),
        scratch_shapes=[pltpu.SemaphoreType.DMA((nchunk,)), pltpu.SemaphoreType.DMA((nchunk,)),
                        pltpu.SemaphoreType.DMA((nchunk,))],
    )(half)


def _allgather_all(block, *, name):
    r = block.shape[0]

    def body(src, out, send_sems, recv_sems, local_sem):
        x, y, c = _coords()
        me = 4 * x + 2 * y + c
        mine = pltpu.make_async_copy(src, out.at[me], local_sem)
        mine.start()
        copies = []
        for k in range(1, 8):
            px, py = _chip_peer(x, y, k >> 1)
            pc = 1 - c if k & 1 else c
            cp = pltpu.make_async_remote_copy(src_ref=src, dst_ref=out.at[me], send_sem=send_sems.at[k - 1],
                                              recv_sem=recv_sems.at[k - 1], device_id=(px, py, pc), device_id_type=MESH)
            cp.start()
            copies.append(cp)
        for cp in copies:
            cp.wait()
        mine.wait()

    return pl.pallas_call(
        body, name=name, in_specs=[ANY], out_specs=ANY, out_shape=jax.ShapeDtypeStruct((8, r, LANES), F32),
        scratch_shapes=[pltpu.SemaphoreType.DMA((7,)), pltpu.SemaphoreType.DMA((7,)), pltpu.SemaphoreType.DMA],
    )(block)


def _flatten(arrays, pad_rows=None):
    flat = jnp.concatenate([a.reshape(-1) for a in arrays])
    if pad_rows is not None:
        flat = jnp.pad(flat, (0, pad_rows * LANES - flat.shape[0]))
    return flat.reshape(-1, LANES)


def _unflatten(flat, shapes):
    flat = flat.reshape(-1)
    out, off = [], 0
    for shp in shapes:
        size = 1
        for dim in shp:
            size *= dim
        out.append(flat[off:off + size].reshape(shp))
        off += size
    return out


def _pad_w_in(w):
    pad = jnp.zeros(w.shape[:-1] + (D_IN_PAD - D_IN,), w.dtype)
    return jnp.concatenate([w[..., :1536], w[..., 1544:], w[..., 1536:1544], pad], axis=-1)


def _unpad_w_in(w):
    return jnp.concatenate([w[..., :1536], w[..., 2560:2568], w[..., 1536:2560]], axis=-1)


def _tile(s, want):
    return min(want, s)


def _layer_fwd(x, p, l):
    s = x.shape[0]
    tr = _tile(s, 512)
    tm = _tile(s, 1024)
    xn = _rms_fwd([x], p["mix_g"], out_dtype=BF16, tr=tr, name=f"rms_mix_fwd{l}")
    h = _matmul(xn, p["w_in"], tm=tm, tn=896, out_dtype=F32, name=f"mm_in{l}")
    c = _gates_fwd(h, p["bf_pad"], tr=_tile(s, 256), name=f"gates_fwd{l}")
    attn, attn_res = _attention_fwd(h[:, 0:512], h[:, 512:1024], h[:, 1024:1536], c[:, :N_HEADS],
                                    tq=_tile(s, 2048), kb=_tile(s, 512), t=_tile(s, 512), name=f"attn_fwd{l}")
    sgu = _sgu_fwd(h, p["ln_g"], p["ln_b"], p["w_s"], p["bias_tile"], tr=_tile(s, 256), name=f"sgu_fwd{l}")
    merged = _rms_fwd([attn, sgu], p["out_g"], out_dtype=BF16, tr=tr, name=f"rms_out_fwd{l}")
    x1 = _matmul(merged, p["w_out"], tm=tm, tn=1024, out_dtype=F32, residual=x, name=f"mm_out{l}")
    xn2 = _rms_fwd([x1], p["ffn_g"], out_dtype=BF16, tr=tr, name=f"rms_ffn_fwd{l}")
    gu = _matmul(xn2, p["w_gu"], tm=tm, tn=1408, out_dtype=F32, name=f"mm_gu{l}")
    act = _swiglu_fwd(gu, tr=tr, name=f"swiglu_fwd{l}")
    x2 = _matmul(act, p["w_down"], tm=tm, tn=1024, out_dtype=F32, residual=x1, name=f"mm_down{l}")
    saved = dict(x=x, xn=xn, h=h, attn_res=attn_res, attn=attn, sgu=sgu, merged=merged, x1=x1, xn2=xn2, gu=gu, act=act)
    return x2, saved


def _layer_bwd(dx2, p, sv, l):
    s = dx2.shape[0]
    tr = _tile(s, 512)
    tm = _tile(s, 1024)
    ts = _tile(s, 512)
    g = {}
    g["w_down"] = _matmul_tn(sv["act"], dx2, tm=1408, tn=1024, ts=ts, name=f"mm_down_dw{l}")
    dact = _matmul(dx2, p["w_down"], trans_b=True, tm=tm, tn=1408, out_dtype=F32, name=f"mm_down_dx{l}")
    dgu = _swiglu_bwd(dact, sv["gu"], tr=tr, name=f"swiglu_bwd{l}")
    g["w_gu"] = _matmul_tn(sv["xn2"], dgu, tm=1024, tn=1408, ts=ts, name=f"mm_gu_dw{l}")
    dxn2 = _matmul(dgu, p["w_gu"], trans_b=True, tm=_tile(s, 512), tn=1024, out_dtype=F32, name=f"mm_gu_dx{l}")
    (dx1,), g["ffn_g"] = _rms_bwd(dxn2, [sv["x1"]], p["ffn_g"], residual=dx2, tr=tr, name=f"rms_ffn_bwd{l}")
    g["w_out"] = _matmul_tn(sv["merged"], dx1, tm=1024, tn=1024, ts=ts, name=f"mm_out_dw{l}")
    dmerged = _matmul(dx1, p["w_out"], trans_b=True, tm=tm, tn=1024, out_dtype=F32, name=f"mm_out_dx{l}")
    (dattn, dsgu), g["out_g"] = _rms_bwd(dmerged, [sv["attn"], sv["sgu"]], p["out_g"], tr=tr, name=f"rms_out_bwd{l}")
    dzu, dzv, g["w_s"], g["ln_g"], g["ln_b"], dbs = _sgu_bwd(dsgu, sv["h"], p["ln_g"], p["ln_b"], p["w_s"], p["bias_tile"],
                                                           tr=_tile(s, 256), name=f"sgu_bwd{l}")
    g["b_s"] = dbs[:, :N_HEADS].T
    dq, dk, dv, dc8 = _attention_bwd(dattn, sv["attn_res"], name=f"attn_bwd{l}")
    dc = jnp.pad(dc8, ((0, 0), (0, LANES - N_HEADS)))
    dfl, dbf = _gates_bwd(dc, sv["h"], p["bf_pad"], tr=_tile(s, 256), name=f"gates_bwd{l}")
    g["b_f"] = dbf[0, :N_HEADS]
    dh = jnp.concatenate([dq.astype(BF16), dk.astype(BF16), dv.astype(BF16), dzu, dzv, dfl], axis=1)
    g["w_in"] = _matmul_tn(sv["xn"], dh, tm=1024, tn=896, ts=ts, name=f"mm_in_dw{l}")
    dxn = _matmul(dh, p["w_in"], trans_b=True, tm=tm, tn=1024, out_dtype=F32, name=f"mm_in_dx{l}")
    (dx,), g["mix_g"] = _rms_bwd(dxn, [sv["x"]], p["mix_g"], residual=dx1, tr=tr, name=f"rms_mix_bwd{l}")
    return dx, g


def _layer_params(l, w_in_pad, w_out, w_gu, w_down, mix_norm_g, b_f, sgu_ln_g, sgu_ln_b, w_s, b_s, out_norm_g, ffn_norm_g):
    return dict(
        w_in=w_in_pad[l], w_out=w_out[l], w_gu=w_gu[l], w_down=w_down[l],
        mix_g=mix_norm_g[l][None, :], out_g=out_norm_g[l][None, :], ffn_g=ffn_norm_g[l][None, :],
        bf_pad=jnp.pad(b_f[l], (0, LANES - N_HEADS))[None, :],
        ln_g=sgu_ln_g[l][None, :], ln_b=sgu_ln_b[l][None, :], w_s=w_s[l],
        bias_tile=jnp.repeat(b_s[l].T, 64, axis=1),
    )


def _local_step(x, tgt, w_in_pad, w_out, w_gu, w_down, mix_norm_g, b_f, sgu_ln_g, sgu_ln_b, w_s, b_s, out_norm_g,
                ffn_norm_g, final_norm_g):
    depth = w_in_pad.shape[0]
    s = x.shape[0]
    params = [_layer_params(l, w_in_pad, w_out, w_gu, w_down, mix_norm_g, b_f, sgu_ln_g, sgu_ln_b, w_s, b_s, out_norm_g,
                            ffn_norm_g) for l in range(depth)]
    saved = []
    for l in range(depth):
        x, sv = _layer_fwd(x, params[l], l)
        saved.append(sv)
    loss_tile, dx, dfinal = _loss_head(x, tgt, final_norm_g[None, :], tr=_tile(s, 512), name="loss_head")
    grads = [None] * depth
    for l in reversed(range(depth)):
        dx, grads[l] = _layer_bwd(dx, params[l], saved[l], l)
    return loss_tile[0, 0], dx, grads, dfinal[0]


SMALL_ROWS = 4272


def kernel(x, mix_norm_g, w_in, b_f, sgu_ln_g, sgu_ln_b, w_s, b_s, out_norm_g, w_out, ffn_norm_g, w_gate_up, w_down, final_norm_g, loss_target, m_mix_norm_g, m_w_in, m_b_f, m_sgu_ln_g, m_sgu_ln_b, m_w_s, m_b_s, m_out_norm_g, m_w_out, m_ffn_norm_g, m_w_gate_up, m_w_down, m_final_norm_g, v_mix_norm_g, v_w_in, v_b_f, v_sgu_ln_g, v_sgu_ln_b, v_w_s, v_b_s, v_out_norm_g, v_w_out, v_ffn_norm_g, v_w_gate_up, v_w_down, v_final_norm_g):
    big = [w_in, w_out, w_gate_up, w_down]
    big_shapes = [a.shape for a in big]
    small = [mix_norm_g, b_f, sgu_ln_g, sgu_ln_b, w_s, b_s, out_norm_g, ffn_norm_g, final_norm_g]
    small_shapes = [a.shape for a in small]

    gathered = _allgather_chips(_flatten([a.astype(BF16) for a in big]), name="ag_weights")
    per_chip = [_unflatten(gathered[j], big_shapes) for j in range(4)]
    w_in_full = _pad_w_in(jnp.concatenate([pc[0] for pc in per_chip], axis=2))
    w_out_full = jnp.concatenate([pc[1] for pc in per_chip], axis=1)
    w_gu_full = jnp.concatenate([pc[2] for pc in per_chip], axis=2)
    w_down_full = jnp.concatenate([pc[3] for pc in per_chip], axis=1)

    loss_part, dx, grads, dfinal = _local_step(
        x[0], loss_target[0], w_in_full, w_out_full, w_gu_full, w_down_full, mix_norm_g, b_f, sgu_ln_g, sgu_ln_b, w_s, b_s,
        out_norm_g, ffn_norm_g, final_norm_g)
    stack = lambda key: jnp.stack([g[key] for g in grads])
    g_in = _unpad_w_in(stack("w_in"))
    g_out, g_gu, g_down = stack("w_out"), stack("w_gu"), stack("w_down")

    send = jnp.stack([_flatten([g_in[:, :, 642 * j:642 * (j + 1)], g_out[:, 256 * j:256 * (j + 1), :],
                                g_gu[:, :, 1408 * j:1408 * (j + 1)], g_down[:, 704 * j:704 * (j + 1), :]]) for j in range(4)])
    theirs = _pair_split(send, name="rs_pair_split")
    pair_sum = _pair_sum(send, theirs, lax.axis_index("c").astype(jnp.int32).reshape(1), tr=2408, name="rs_pair_sum")
    from_chips = _scatter_chips(pair_sum, name="rs_scatter")
    half = _sum_slots(from_chips, tr=1120, name="rs_chip_sum")
    g_big_flat = _pair_join(half, name="rs_pair_join")

    g_small_local = [stack("mix_g")[:, 0], stack("b_f"), stack("ln_g")[:, 0], stack("ln_b")[:, 0], stack("w_s"), stack("b_s"),
                     stack("out_g")[:, 0], stack("ffn_g")[:, 0], dfinal]
    g_small_flat = _sum_slots(_allgather_all(_flatten(g_small_local, SMALL_ROWS), name="ar_small_gather"), tr=1424,
                              name="ar_small_sum")
    loss = lax.psum(loss_part, ("x", "y", "c"))

    d_big, m_big, v_big = _adamw(_flatten(big), g_big_flat, _flatten([m_w_in, m_w_out, m_w_gate_up, m_w_down]),
                                 _flatten([v_w_in, v_w_out, v_w_gate_up, v_w_down]), tr=2240, name="adamw_big")
    m_small = [m_mix_norm_g, m_b_f, m_sgu_ln_g, m_sgu_ln_b, m_w_s, m_b_s, m_out_norm_g, m_ffn_norm_g, m_final_norm_g]
    v_small = [v_mix_norm_g, v_b_f, v_sgu_ln_g, v_sgu_ln_b, v_w_s, v_b_s, v_out_norm_g, v_ffn_norm_g, v_final_norm_g]
    d_small, m_small2, v_small2 = _adamw(_flatten(small, SMALL_ROWS), g_small_flat, _flatten(m_small, SMALL_ROWS),
                                         _flatten(v_small, SMALL_ROWS), tr=1424, name="adamw_small")

    def in_order(big_flat, small_flat):
        b_in, b_out, b_gu, b_down = _unflatten(big_flat, big_shapes)
        s_mix, s_bf, s_lng, s_lnb, s_ws, s_bs, s_outg, s_ffn, s_fin = _unflatten(small_flat, small_shapes)
        return [s_mix, b_in, s_bf, s_lng, s_lnb, s_ws, s_bs, s_outg, b_out, s_ffn, b_gu, b_down, s_fin]

    return (loss, dx[None], *in_order(g_big_flat, g_small_flat), *in_order(d_big, d_small), *in_order(m_big, m_small2),
            *in_order(v_big, v_small2))
```

```python
import jax
import jax.numpy as jnp
from jax import lax
from jax.experimental import pallas as pl
from jax.experimental.pallas import tpu as pltpu

F32 = jnp.float32
BF16 = jnp.bfloat16

D_MODEL = 1024
D_HALF = 512
N_HEADS = 8
HEAD_DIM = 64
SGU_CHUNK = 128
CAUSAL_CHUNK = 64
D_FF = 2816
D_IN = 2568
D_IN_PAD = 2688
F_COL_BLOCK = 2560 // 128
EPS = 1e-6
NEG = -1e30
LANES = 128
VMEM_LIMIT = 56 * 1024 * 1024

ADAM_LR = 0.001
ADAM_B1 = 0.9
ADAM_B2 = 0.999
ADAM_EPS = 1e-08
ADAM_WD = 0.01
ADAM_STEP = 10

MESH = pl.DeviceIdType.MESH
ANY = pl.BlockSpec(memory_space=pl.ANY)


def _params(*sem):
    return pltpu.CompilerParams(dimension_semantics=sem, vmem_limit_bytes=VMEM_LIMIT)


def _matmul(a, b, *, trans_b=False, tm, tn, out_dtype, residual=None, name):
    m, k = a.shape
    n = b.shape[0] if trans_b else b.shape[1]
    dims = (((1,), (1,)), ((), ())) if trans_b else (((1,), (0,)), ((), ()))

    def body(*refs):
        a_ref, b_ref = refs[0], refs[1]
        o_ref = refs[-1]
        acc = lax.dot_general(a_ref[...].astype(BF16), b_ref[...].astype(BF16), dims,
                              preferred_element_type=F32)
        if residual is not None:
            acc = acc + refs[2][...]
        o_ref[...] = acc.astype(out_dtype)

    b_spec = pl.BlockSpec((tn, k), lambda i, j: (j, 0)) if trans_b else pl.BlockSpec((k, tn), lambda i, j: (0, j))
    in_specs = [pl.BlockSpec((tm, k), lambda i, j: (i, 0)), b_spec]
    args = [a, b]
    if residual is not None:
        in_specs.append(pl.BlockSpec((tm, tn), lambda i, j: (i, j)))
        args.append(residual)
    return pl.pallas_call(
        body, name=name, grid=(m // tm, n // tn), in_specs=in_specs,
        out_specs=pl.BlockSpec((tm, tn), lambda i, j: (i, j)),
        out_shape=jax.ShapeDtypeStruct((m, n), out_dtype),
        compiler_params=_params("parallel", "parallel"),
    )(*args)


def _matmul_tn(a, b, *, tm, tn, ts, name):
    s, m = a.shape
    n = b.shape[1]

    def body(a_ref, b_ref, o_ref):
        @pl.when(pl.program_id(2) == 0)
        def _():
            o_ref[...] = jnp.zeros_like(o_ref)

        o_ref[...] += lax.dot_general(a_ref[...].astype(BF16), b_ref[...].astype(BF16),
                                      (((0,), (0,)), ((), ())), preferred_element_type=F32)

    return pl.pallas_call(
        body, name=name, grid=(m // tm, n // tn, s // ts),
        in_specs=[pl.BlockSpec((ts, tm), lambda i, j, t: (t, i)), pl.BlockSpec((ts, tn), lambda i, j, t: (t, j))],
        out_specs=pl.BlockSpec((tm, tn), lambda i, j, t: (i, j)),
        out_shape=jax.ShapeDtypeStruct((m, n), F32),
        compiler_params=_params("parallel", "parallel", "arbitrary"),
    )(a, b)


def _rms_fwd(xs, g, *, out_dtype, tr, name):
    s = xs[0].shape[0]
    widths = [x.shape[1] for x in xs]
    wsum = sum(widths)
    nx = len(xs)

    def body(*refs):
        g_ref, o_ref = refs[nx], refs[nx + 1]
        off = 0
        for x_ref, w in zip(refs[:nx], widths):
            x = x_ref[...]
            r = lax.rsqrt(jnp.mean(x * x, axis=-1, keepdims=True) + EPS)
            o_ref[:, off:off + w] = (x * r * g_ref[:, off:off + w]).astype(out_dtype)
            off += w

    return pl.pallas_call(
        body, name=name, grid=(s // tr,),
        in_specs=[pl.BlockSpec((tr, w), lambda i: (i, 0)) for w in widths] + [pl.BlockSpec((1, wsum), lambda i: (0, 0))],
        out_specs=pl.BlockSpec((tr, wsum), lambda i: (i, 0)),
        out_shape=jax.ShapeDtypeStruct((s, wsum), out_dtype),
        compiler_params=_params("parallel"),
    )(*xs, g)


def _rms_bwd(dy, xs, g, *, residual=None, head_dots=False, tr, name):
    s = xs[0].shape[0]
    widths = [x.shape[1] for x in xs]
    wsum = sum(widths)
    nx = len(xs)
    nin = 2 + nx + (residual is not None)

    def body(*refs):
        dy_ref, g_ref = refs[0], refs[1 + nx]
        dx_refs, dg_ref = refs[nin:nin + nx], refs[nin + nx]

        @pl.when(pl.program_id(0) == 0)
        def _():
            dg_ref[...] = jnp.zeros_like(dg_ref)

        off = 0
        for idx, (x_ref, w) in enumerate(zip(refs[1:1 + nx], widths)):
            x = x_ref[...]
            r = lax.rsqrt(jnp.mean(x * x, axis=-1, keepdims=True) + EPS)
            xh = x * r
            dyv = dy_ref[:, off:off + w]
            dxh = dyv * g_ref[:, off:off + w]
            dx = r * (dxh - xh * jnp.mean(dxh * xh, axis=-1, keepdims=True))
            if residual is not None and idx == 0:
                dx = dx + refs[2 + nx][...]
            dx_refs[idx][...] = dx
            dg_ref[:, off:off + w] += jnp.sum(dyv * xh, axis=0, keepdims=True)
            if head_dots and idx == 0:
                col = lax.broadcasted_iota(jnp.int32, (w, LANES), 0) // HEAD_DIM
                head = lax.broadcasted_iota(jnp.int32, (w, LANES), 1)
                refs[nin + nx + 1][...] = jnp.dot(dx * x, (col == head).astype(F32), precision=lax.Precision.HIGHEST,
                                                  preferred_element_type=F32)
            off += w

    in_specs = ([pl.BlockSpec((tr, wsum), lambda i: (i, 0))]
                + [pl.BlockSpec((tr, w), lambda i: (i, 0)) for w in widths]
                + [pl.BlockSpec((1, wsum), lambda i: (0, 0))])
    args = [dy, *xs, g]
    if residual is not None:
        in_specs.append(pl.BlockSpec((tr, widths[0]), lambda i: (i, 0)))
        args.append(residual)
    out_specs = [pl.BlockSpec((tr, w), lambda i: (i, 0)) for w in widths] + [pl.BlockSpec((1, wsum), lambda i: (0, 0))]
    out_shape = [jax.ShapeDtypeStruct((s, w), F32) for w in widths] + [jax.ShapeDtypeStruct((1, wsum), F32)]
    if head_dots:
        out_specs.append(pl.BlockSpec((tr, LANES), lambda i: (i, 0)))
        out_shape.append(jax.ShapeDtypeStruct((s, LANES), F32))
    outs = pl.pallas_call(
        body, name=name, grid=(s // tr,), in_specs=in_specs, out_specs=out_specs, out_shape=out_shape,
        compiler_params=_params("arbitrary"),
    )(*args)
    if head_dots:
        return outs[:nx], outs[nx], outs[nx + 1]
    return outs[:nx], outs[nx]


def _loss_head(x, tgt, g, *, tr, name):
    s, d = x.shape

    def body(x_ref, t_ref, g_ref, loss_ref, dx_ref, dg_ref):
        @pl.when(pl.program_id(0) == 0)
        def _():
            loss_ref[...] = jnp.zeros_like(loss_ref)
            dg_ref[...] = jnp.zeros_like(dg_ref)

        xv = x_ref[...]
        r = lax.rsqrt(jnp.mean(xv * xv, axis=-1, keepdims=True) + EPS)
        xh = xv * r
        err = xh * g_ref[...] - t_ref[...]
        loss_ref[...] += 0.5 * jnp.sum(jnp.mean(err * err, axis=-1, keepdims=True))
        dy = err * (1.0 / d)
        dxh = dy * g_ref[...]
        dx_ref[...] = r * (dxh - xh * jnp.mean(dxh * xh, axis=-1, keepdims=True))
        dg_ref[...] += jnp.sum(dy * xh, axis=0, keepdims=True)

    return pl.pallas_call(
        body, name=name, grid=(s // tr,),
        in_specs=[pl.BlockSpec((tr, d), lambda i: (i, 0)), pl.BlockSpec((tr, d), lambda i: (i, 0)),
                  pl.BlockSpec((1, d), lambda i: (0, 0))],
        out_specs=[pl.BlockSpec((8, LANES), lambda i: (0, 0)), pl.BlockSpec((tr, d), lambda i: (i, 0)),
                   pl.BlockSpec((1, d), lambda i: (0, 0))],
        out_shape=[jax.ShapeDtypeStruct((8, LANES), F32), jax.ShapeDtypeStruct((s, d), F32),
                   jax.ShapeDtypeStruct((1, d), F32)],
        compiler_params=_params("arbitrary"),
    )(x, tgt, g)


def _gates_fwd(h, bf_pad, *, tr, name):
    s = h.shape[0]

    def body(fl_ref, b_ref, c_ref, carry_ref):
        @pl.when(pl.program_id(0) == 0)
        def _():
            carry_ref[...] = jnp.zeros_like(carry_ref)

        lf = jax.nn.log_sigmoid(fl_ref[...] + b_ref[...])
        row = lax.broadcasted_iota(jnp.int32, (tr, tr), 0)
        col = lax.broadcasted_iota(jnp.int32, (tr, tr), 1)
        tri = (col <= row).astype(F32)
        c_ref[...] = jnp.dot(tri, lf, precision=lax.Precision.HIGHEST, preferred_element_type=F32) + carry_ref[...]
        carry_ref[...] += jnp.sum(lf, axis=0, keepdims=True)

    return pl.pallas_call(
        body, name=name, grid=(s // tr,),
        in_specs=[pl.BlockSpec((tr, LANES), lambda i: (i, F_COL_BLOCK)), pl.BlockSpec((1, LANES), lambda i: (0, 0))],
        out_specs=pl.BlockSpec((tr, LANES), lambda i: (i, 0)),
        out_shape=jax.ShapeDtypeStruct((s, LANES), F32),
        scratch_shapes=[pltpu.VMEM((1, LANES), F32)],
        compiler_params=_params("arbitrary"),
    )(h, bf_pad)


def _gates_bwd(dc, h, bf_pad, *, tr, name):
    s = h.shape[0]
    n = s // tr

    def body(dc_ref, fl_ref, b_ref, dfl_ref, db_ref, carry_ref):
        @pl.when(pl.program_id(0) == 0)
        def _():
            carry_ref[...] = jnp.zeros_like(carry_ref)
            db_ref[...] = jnp.zeros_like(db_ref)

        dcv = dc_ref[...]
        row = lax.broadcasted_iota(jnp.int32, (tr, tr), 0)
        col = lax.broadcasted_iota(jnp.int32, (tr, tr), 1)
        triu = (col >= row).astype(F32)
        dlf = jnp.dot(triu, dcv, precision=lax.Precision.HIGHEST, preferred_element_type=F32) + carry_ref[...]
        carry_ref[...] += jnp.sum(dcv, axis=0, keepdims=True)
        dfl = dlf * jax.nn.sigmoid(-(fl_ref[...] + b_ref[...]))
        dfl_ref[...] = dfl.astype(dfl_ref.dtype)
        db_ref[...] += jnp.sum(dfl, axis=0, keepdims=True)

    return pl.pallas_call(
        body, name=name, grid=(n,),
        in_specs=[pl.BlockSpec((tr, LANES), lambda i: (n - 1 - i, 0)),
                  pl.BlockSpec((tr, LANES), lambda i: (n - 1 - i, F_COL_BLOCK)),
                  pl.BlockSpec((1, LANES), lambda i: (0, 0))],
        out_specs=[pl.BlockSpec((tr, LANES), lambda i: (n - 1 - i, 0)), pl.BlockSpec((1, LANES), lambda i: (0, 0))],
        out_shape=[jax.ShapeDtypeStruct((s, LANES), BF16), jax.ShapeDtypeStruct((1, LANES), F32)],
        scratch_shapes=[pltpu.VMEM((1, LANES), F32)],
        compiler_params=_params("arbitrary"),
    )(dc, h, bf_pad)


LOG2E = 1.4426950408889634
LN2 = 0.6931471805599453
V_ROWS = 80


def _transpose_bf16(a):
    return a.astype(F32).T.astype(BF16)


def _transpose_heads(a, *, t, name):
    s = a.shape[0]

    def body(a_ref, o_ref):
        o_ref[0, 0] = _transpose_bf16(a_ref[...])

    return pl.pallas_call(
        body, name=name, grid=(N_HEADS, s // t),
        in_specs=[pl.BlockSpec((t, LANES), lambda h, i: (i, h))],
        out_specs=pl.BlockSpec((1, 1, LANES, t), lambda h, i: (h, i, 0, 0)),
        out_shape=jax.ShapeDtypeStruct((N_HEADS, s // t, LANES, t), BF16),
        compiler_params=_params("parallel", "parallel"),
    )(a)


def _attn_fwd(qat, ka, va, *, tq, kb, name):
    s = ka.shape[0]
    nq = s // tq
    per_tile = tq // kb
    tt = qat.shape[3]
    sub = tq // tt

    def body(qat_ref, ka_ref, va_ref, o_ref, lse_ref):
        i = pl.program_id(1)
        qat = jnp.concatenate([qat_ref[0, d] for d in range(sub)], axis=1)

        def blk(n, carry, masked):
            m, acc = carry
            rows = pl.ds(pl.multiple_of(n * kb, kb), kb)
            sc = jnp.dot(ka_ref[rows, :], qat, preferred_element_type=F32)
            if masked:
                key = n * kb + lax.broadcasted_iota(jnp.int32, (kb, tq), 0)
                qry = i * tq + lax.broadcasted_iota(jnp.int32, (kb, tq), 1)
                sc = jnp.where(key <= qry, sc, NEG)
            m_new = jnp.maximum(m, jnp.max(sc, axis=0, keepdims=True))
            p = jnp.exp2(sc - m_new)
            vt = _transpose_bf16(va_ref[rows, :])[:V_ROWS]
            acc = jnp.exp2(m - m_new) * acc + jnp.dot(vt, p.astype(BF16), preferred_element_type=F32)
            return m_new, acc

        init = (jnp.full((1, tq), NEG, F32), jnp.zeros((V_ROWS, tq), F32))
        carry = lax.fori_loop(0, i * per_tile, lambda n, c: blk(n, c, False), init)
        for d in range(per_tile):
            carry = blk(i * per_tile + d, carry, True)
        m, acc = carry
        l = acc[HEAD_DIM:HEAD_DIM + 1, :]
        padded = jnp.concatenate([acc / l, jnp.zeros((LANES - V_ROWS, tq), F32)], axis=0)
        o_ref[0, 0] = padded.T[:, :HEAD_DIM]
        lse_ref[0, 0] = m + jnp.log2(l)

    return pl.pallas_call(
        body, name=name, grid=(N_HEADS, nq),
        in_specs=[pl.BlockSpec((1, sub, LANES, tt), lambda h, i: (h, i, 0, 0)),
                  pl.BlockSpec((s, LANES), lambda h, i: (0, h)),
                  pl.BlockSpec((s, LANES), lambda h, i: (0, h))],
        out_specs=[pl.BlockSpec((1, 1, tq, HEAD_DIM), lambda h, i: (h, i, 0, 0)),
                   pl.BlockSpec((1, 1, 1, tq), lambda h, i: (h, i, 0, 0))],
        out_shape=[jax.ShapeDtypeStruct((N_HEADS, nq, tq, HEAD_DIM), F32), jax.ShapeDtypeStruct((N_HEADS, nq, 1, tq), F32)],
        compiler_params=_params("parallel", "arbitrary"),
    )(qat, ka, va)


def _attn_bwd(qa, qat, doa, dot, lse, dl, ka, va, *, name):
    s = qa.shape[0]
    nt, t = qat.shape[1], qat.shape[3]

    def body(qa_ref, qat_ref, do_ref, dot_ref, lse_ref, dl_ref, ka_ref, va_ref, dq_ref, dk_ref, dv_ref):
        j = pl.program_id(1)

        @pl.when(j == 0)
        def _():
            dq_ref[...] = jnp.zeros_like(dq_ref)

        ka_j, va_j = ka_ref[...], va_ref[...]

        def tile(i, carry, masked):
            dk, dv = carry
            rows = pl.ds(pl.multiple_of(i * t, t), t)
            qa_i, do_i = qa_ref[rows, :], do_ref[rows, :]
            st = jnp.dot(ka_j, qat_ref[0, i], preferred_element_type=F32) - lse_ref[0, i]
            if masked:
                key = lax.broadcasted_iota(jnp.int32, (t, t), 0)
                qry = lax.broadcasted_iota(jnp.int32, (t, t), 1)
                st = jnp.where(key <= qry, st, NEG)
            pt = jnp.exp2(st)
            dpt = jnp.dot(va_j, dot_ref[0, i], preferred_element_type=F32)
            dsb = (pt * (dpt - dl_ref[0, i])).astype(BF16)
            dv = dv + jnp.dot(pt.astype(BF16), do_i, preferred_element_type=F32)
            dk = dk + jnp.dot(dsb, qa_i, preferred_element_type=F32)
            dq_ref[rows, :] += lax.dot_general(dsb, ka_j, (((0,), (0,)), ((), ())), preferred_element_type=F32)
            return dk, dv

        carry = tile(j, (jnp.zeros((t, LANES), F32), jnp.zeros((t, LANES), F32)), True)
        dk, dv = lax.fori_loop(j + 1, nt, lambda i, c: tile(i, c, False), carry)
        dk_ref[...] = dk
        dv_ref[...] = dv

    res = pl.BlockSpec((s, LANES), lambda h, j: (0, h))
    rest = pl.BlockSpec((1, nt, LANES, t), lambda h, j: (h, 0, 0, 0))
    row = pl.BlockSpec((1, nt, 1, t), lambda h, j: (h, 0, 0, 0))
    blk = pl.BlockSpec((t, LANES), lambda h, j: (j, h))
    shape = jax.ShapeDtypeStruct((s, N_HEADS * LANES), F32)
    return pl.pallas_call(
        body, name=name, grid=(N_HEADS, nt),
        in_specs=[res, rest, res, rest, row, row, blk, blk], out_specs=[res, blk, blk], out_shape=[shape, shape, shape],
        compiler_params=_params("parallel", "arbitrary"),
    )(qa, qat, doa, dot, lse, dl, ka, va)


def _split3(x):
    rnd = lambda a: lax.reduce_precision(a, exponent_bits=8, mantissa_bits=7)
    x1 = rnd(x)
    r1 = x - x1
    x2 = rnd(r1)
    x3 = rnd(r1 - x2)
    return [x1, x2, x3]


def _head_blocks(core, extra):
    s = core.shape[0]
    cols = [core.reshape(s, N_HEADS, HEAD_DIM).astype(BF16)] + [c.astype(BF16)[..., None] for c in extra]
    cols.append(jnp.zeros((s, N_HEADS, LANES - HEAD_DIM - len(extra)), BF16))
    return jnp.concatenate(cols, axis=-1).reshape(s, N_HEADS * LANES)


def _attention_fwd(q, k, v, c, *, tq, kb, t, name):
    s = q.shape[0]
    ones = [jnp.ones((s, N_HEADS), BF16)] * 3
    c2 = c * LOG2E
    qa = _head_blocks(q * 0.125, ones + _split3(c2))
    ka = _head_blocks(k * LOG2E, _split3(-c2) + ones)
    va = _head_blocks(v, ones[:1])
    qat = _transpose_heads(qa, t=t, name=name + "_qt")
    o4, lse2 = _attn_fwd(qat, ka, va, tq=tq, kb=kb, name=name)
    attn = o4.reshape(N_HEADS, s, HEAD_DIM).transpose(1, 0, 2).reshape(s, N_HEADS * HEAD_DIM)
    return attn, dict(qa=qa, qat=qat, ka=ka, va=va, lse2=lse2)


def _attention_bwd(dattn, delta, res, *, name):
    s = dattn.shape[0]
    t = res["qat"].shape[3]
    doa = _head_blocks(dattn, [])
    dqa, dka, dva = _attn_bwd(res["qa"], res["qat"], doa, _transpose_heads(doa, t=t, name=name + "_dot"),
                              res["lse2"].reshape(N_HEADS, s // t, 1, t), delta.T.reshape(N_HEADS, s // t, 1, t),
                              res["ka"], res["va"], name=name)
    dqa, dka, dva = (a.reshape(s, N_HEADS, LANES) for a in (dqa, dka, dva))
    core = lambda a: a[:, :, :HEAD_DIM].reshape(s, N_HEADS * HEAD_DIM)
    dc = dqa[:, :, HEAD_DIM + 3] - dka[:, :, HEAD_DIM]
    return core(dqa) * (LN2 * 0.125), core(dka), core(dva), dc


def _gelu(z):
    return 0.5 * z * (1.0 + lax.erf(z * 0.7071067811865476))


def _gelu_grad(z):
    return 0.5 * (1.0 + lax.erf(z * 0.7071067811865476)) + z * (0.3989422804014327 * jnp.exp(-0.5 * z * z))


def _sgu_mask():
    i = lax.broadcasted_iota(jnp.int32, (SGU_CHUNK, SGU_CHUNK), 0) // CAUSAL_CHUNK
    j = lax.broadcasted_iota(jnp.int32, (SGU_CHUNK, SGU_CHUNK), 1) // CAUSAL_CHUNK
    return (j <= i).astype(F32)


def _layernorm_stats(x):
    mu = jnp.mean(x, axis=-1, keepdims=True)
    xc = x - mu
    rstd = lax.rsqrt(jnp.mean(xc * xc, axis=-1, keepdims=True) + EPS)
    return xc * rstd, rstd


def _first_group_lanes():
    return lax.broadcasted_iota(jnp.int32, (SGU_CHUNK, LANES), 1) < 64


def _sgu_fwd(h, ln_g, ln_b, w_s, bias_tile, *, tr, name):
    s = h.shape[0]
    zu_blk, zv_blk = 1536 // D_HALF, 2048 // D_HALF

    def body(zu_ref, zv_ref, lng_ref, lnb_ref, ws_ref, bias_ref, o_ref):
        gzu = _gelu(zu_ref[...])
        xh, _ = _layernorm_stats(_gelu(zv_ref[...]))
        zb = (xh * lng_ref[...] + lnb_ref[...]).astype(BF16)
        mask = _sgu_mask()
        first = _first_group_lanes()
        for pair in range(4):
            cols = slice(pair * LANES, (pair + 1) * LANES)
            w0 = (ws_ref[2 * pair] * mask).astype(BF16)
            w1 = (ws_ref[2 * pair + 1] * mask).astype(BF16)
            for ch in range(tr // SGU_CHUNK):
                rows = slice(ch * SGU_CHUNK, (ch + 1) * SGU_CHUNK)
                zp = zb[rows, cols]
                mixed = jnp.where(first, jnp.dot(w0, zp, preferred_element_type=F32),
                                  jnp.dot(w1, zp, preferred_element_type=F32)) + bias_ref[:, cols]
                o_ref[rows, cols] = gzu[rows, cols] * mixed

    return pl.pallas_call(
        body, name=name, grid=(s // tr,),
        in_specs=[pl.BlockSpec((tr, D_HALF), lambda i: (i, zu_blk)), pl.BlockSpec((tr, D_HALF), lambda i: (i, zv_blk)),
                  pl.BlockSpec((1, D_HALF), lambda i: (0, 0)), pl.BlockSpec((1, D_HALF), lambda i: (0, 0)),
                  pl.BlockSpec((N_HEADS, SGU_CHUNK, SGU_CHUNK), lambda i: (0, 0, 0)),
                  pl.BlockSpec((SGU_CHUNK, D_HALF), lambda i: (0, 0))],
        out_specs=pl.BlockSpec((tr, D_HALF), lambda i: (i, 0)),
        out_shape=jax.ShapeDtypeStruct((s, D_HALF), F32),
        compiler_params=_params("parallel"),
    )(h, h, ln_g, ln_b, w_s, bias_tile)


def _sgu_bwd(dsgu, h, ln_g, ln_b, w_s, bias_tile, *, tr, name):
    s = h.shape[0]
    n = s // tr
    zu_blk, zv_blk = 1536 // D_HALF, 2048 // D_HALF

    def body(ds_ref, zu_ref, zv_ref, lng_ref, lnb_ref, ws_ref, bias_ref,
             dzu_ref, dzv_ref, dws_ref, dlng_ref, dlnb_ref, dbs_ref, dgzu_sc, dzvn_sc, dbias_sc):
        step = pl.program_id(0)

        @pl.when(step == 0)
        def _():
            dws_ref[...] = jnp.zeros_like(dws_ref)
            dlng_ref[...] = jnp.zeros_like(dlng_ref)
            dlnb_ref[...] = jnp.zeros_like(dlnb_ref)
            dbias_sc[...] = jnp.zeros_like(dbias_sc)

        zu = zu_ref[...]
        zv = zv_ref[...]
        gzu = _gelu(zu)
        xh, rstd = _layernorm_stats(_gelu(zv))
        zb = (xh * lng_ref[...] + lnb_ref[...]).astype(BF16)
        ds = ds_ref[...]
        mask = _sgu_mask()
        first = _first_group_lanes()
        tn_dims = (((0,), (0,)), ((), ()))
        nt_dims = (((1,), (1,)), ((), ()))
        for pair in range(4):
            cols = slice(pair * LANES, (pair + 1) * LANES)
            w0 = (ws_ref[2 * pair] * mask).astype(BF16)
            w1 = (ws_ref[2 * pair + 1] * mask).astype(BF16)
            for ch in range(tr // SGU_CHUNK):
                rows = slice(ch * SGU_CHUNK, (ch + 1) * SGU_CHUNK)
                zp = zb[rows, cols]
                mixed = jnp.where(first, jnp.dot(w0, zp, preferred_element_type=F32),
                                  jnp.dot(w1, zp, preferred_element_type=F32)) + bias_ref[:, cols]
                dsp = ds[rows, cols]
                dgzu_sc[rows, cols] = dsp * mixed
                dm = dsp * gzu[rows, cols]
                dbias_sc[:, cols] += dm
                dmb = dm.astype(BF16)
                dm0 = jnp.where(first, dmb, jnp.zeros_like(dmb))
                dm1 = jnp.where(first, jnp.zeros_like(dmb), dmb)
                dws_ref[2 * pair] += lax.dot_general(dm0, zp, nt_dims, preferred_element_type=F32)
                dws_ref[2 * pair + 1] += lax.dot_general(dm1, zp, nt_dims, preferred_element_type=F32)
                dzvn_sc[rows, cols] = jnp.where(first, lax.dot_general(w0, dmb, tn_dims, preferred_element_type=F32),
                                                lax.dot_general(w1, dmb, tn_dims, preferred_element_type=F32))
        dzvn = dzvn_sc[...]
        dlng_ref[...] += jnp.sum(dzvn * xh, axis=0, keepdims=True)
        dlnb_ref[...] += jnp.sum(dzvn, axis=0, keepdims=True)
        dxh = dzvn * lng_ref[...]
        dgzv = rstd * (dxh - jnp.mean(dxh, axis=-1, keepdims=True) - xh * jnp.mean(dxh * xh, axis=-1, keepdims=True))
        dzv_ref[...] = (dgzv * _gelu_grad(zv)).astype(dzv_ref.dtype)
        dzu_ref[...] = (dgzu_sc[...] * _gelu_grad(zu)).astype(dzu_ref.dtype)

        @pl.when(step == n - 1)
        def _():
            for g in range(N_HEADS):
                dws_ref[g] = dws_ref[g] * mask
            lane = lax.broadcasted_iota(jnp.int32, (D_HALF, LANES), 0) // 64
            grp = lax.broadcasted_iota(jnp.int32, (D_HALF, LANES), 1)
            dbs_ref[...] = jnp.dot(dbias_sc[...], (lane == grp).astype(F32), precision=lax.Precision.HIGHEST,
                                   preferred_element_type=F32)

    const2 = lambda i: (0, 0)
    return pl.pallas_call(
        body, name=name, grid=(n,),
        in_specs=[pl.BlockSpec((tr, D_HALF), lambda i: (i, 0)),
                  pl.BlockSpec((tr, D_HALF), lambda i: (i, zu_blk)), pl.BlockSpec((tr, D_HALF), lambda i: (i, zv_blk)),
                  pl.BlockSpec((1, D_HALF), const2), pl.BlockSpec((1, D_HALF), const2),
                  pl.BlockSpec((N_HEADS, SGU_CHUNK, SGU_CHUNK), lambda i: (0, 0, 0)),
                  pl.BlockSpec((SGU_CHUNK, D_HALF), const2)],
        out_specs=[pl.BlockSpec((tr, D_HALF), lambda i: (i, 0)), pl.BlockSpec((tr, D_HALF), lambda i: (i, 0)),
                   pl.BlockSpec((N_HEADS, SGU_CHUNK, SGU_CHUNK), lambda i: (0, 0, 0)),
                   pl.BlockSpec((1, D_HALF), const2), pl.BlockSpec((1, D_HALF), const2),
                   pl.BlockSpec((SGU_CHUNK, LANES), const2)],
        out_shape=[jax.ShapeDtypeStruct((s, D_HALF), BF16), jax.ShapeDtypeStruct((s, D_HALF), BF16),
                   jax.ShapeDtypeStruct((N_HEADS, SGU_CHUNK, SGU_CHUNK), F32),
                   jax.ShapeDtypeStruct((1, D_HALF), F32), jax.ShapeDtypeStruct((1, D_HALF), F32),
                   jax.ShapeDtypeStruct((SGU_CHUNK, LANES), F32)],
        scratch_shapes=[pltpu.VMEM((tr, D_HALF), F32), pltpu.VMEM((tr, D_HALF), F32), pltpu.VMEM((SGU_CHUNK, D_HALF), F32)],
        compiler_params=_params("arbitrary"),
    )(dsgu, h, h, ln_g, ln_b, w_s, bias_tile)


FF_BLOCK = D_FF // 2


def _swiglu_fwd(gu, *, tr, name):
    s = gu.shape[0]

    def body(g_ref, u_ref, o_ref):
        g = g_ref[...]
        o_ref[...] = (g * jax.nn.sigmoid(g) * u_ref[...]).astype(o_ref.dtype)

    return pl.pallas_call(
        body, name=name, grid=(s // tr, 2),
        in_specs=[pl.BlockSpec((tr, FF_BLOCK), lambda i, j: (i, j)), pl.BlockSpec((tr, FF_BLOCK), lambda i, j: (i, j + 2))],
        out_specs=pl.BlockSpec((tr, FF_BLOCK), lambda i, j: (i, j)),
        out_shape=jax.ShapeDtypeStruct((s, D_FF), BF16),
        compiler_params=_params("parallel", "parallel"),
    )(gu, gu)


def _swiglu_bwd(dact, gu, *, tr, name):
    s = gu.shape[0]

    def body(d_ref, gu_ref, o_ref):
        g = gu_ref[:, :D_FF]
        d = d_ref[...]
        sig = jax.nn.sigmoid(g)
        o_ref[:, :D_FF] = (d * gu_ref[:, D_FF:] * (sig * (1.0 + g * (1.0 - sig)))).astype(o_ref.dtype)
        o_ref[:, D_FF:] = (d * (g * sig)).astype(o_ref.dtype)

    return pl.pallas_call(
        body, name=name, grid=(s // tr,),
        in_specs=[pl.BlockSpec((tr, D_FF), lambda i: (i, 0)), pl.BlockSpec((tr, 2 * D_FF), lambda i: (i, 0))],
        out_specs=pl.BlockSpec((tr, 2 * D_FF), lambda i: (i, 0)),
        out_shape=jax.ShapeDtypeStruct((s, 2 * D_FF), BF16),
        compiler_params=_params("parallel"),
    )(dact, gu)


def _sum_slots(stacked, *, tr, name):
    k, r, _ = stacked.shape

    def body(x_ref, o_ref):
        acc = x_ref[0]
        for idx in range(1, k):
            acc = acc + x_ref[idx]
        o_ref[...] = acc

    return pl.pallas_call(
        body, name=name, grid=(r // tr,),
        in_specs=[pl.BlockSpec((k, tr, LANES), lambda i: (0, i, 0))],
        out_specs=pl.BlockSpec((tr, LANES), lambda i: (i, 0)),
        out_shape=jax.ShapeDtypeStruct((r, LANES), F32),
        compiler_params=_params("parallel"),
    )(stacked)


def _adamw(w, g, m, v, *, tr, name):
    r = w.shape[0]

    def body(w_ref, g_ref, m_ref, v_ref, d_ref, m2_ref, v2_ref):
        gv = g_ref[...]
        m2 = ADAM_B1 * m_ref[...] + (1.0 - ADAM_B1) * gv
        v2 = ADAM_B2 * v_ref[...] + (1.0 - ADAM_B2) * jnp.square(gv)
        m_hat = m2 / (1.0 - ADAM_B1 ** ADAM_STEP)
        v_hat = v2 / (1.0 - ADAM_B2 ** ADAM_STEP)
        d_ref[...] = -ADAM_LR * (m_hat / (jnp.sqrt(v_hat) + ADAM_EPS) + ADAM_WD * w_ref[...])
        m2_ref[...] = m2
        v2_ref[...] = v2

    spec = pl.BlockSpec((tr, LANES), lambda i: (i, 0))
    shape = jax.ShapeDtypeStruct((r, LANES), F32)
    return pl.pallas_call(
        body, name=name, grid=(r // tr,), in_specs=[spec] * 4, out_specs=[spec] * 3, out_shape=[shape] * 3,
        compiler_params=_params("parallel"),
    )(w, g, m, v)


PAIR_CHUNKS = 5


def _coords():
    return lax.axis_index("x"), lax.axis_index("y"), lax.axis_index("c")


def _my_chip():
    return 2 * lax.axis_index("x") + lax.axis_index("y")


def _chip_peer(x, y, k):
    px = 1 - x if k & 2 else x
    py = 1 - y if k & 1 else y
    return px, py


def _allgather_chips(shard, *, name):
    r = shard.shape[0]

    def body(src, out, send_sems, recv_sems):
        x, y, c = _coords()
        me = 2 * x + y
        copies = []
        for k in (1, 2, 3):
            px, py = _chip_peer(x, y, k)
            cp = pltpu.make_async_remote_copy(src_ref=src, dst_ref=out.at[me], send_sem=send_sems.at[k - 1],
                                              recv_sem=recv_sems.at[k - 1], device_id=(px, py, c), device_id_type=MESH)
            cp.start()
            copies.append(cp)
        for cp in copies:
            cp.wait()

    gathered = pl.pallas_call(
        body, name=name, in_specs=[ANY], out_specs=ANY,
        out_shape=jax.ShapeDtypeStruct((4, r, LANES), shard.dtype),
        scratch_shapes=[pltpu.SemaphoreType.DMA((3,)), pltpu.SemaphoreType.DMA((3,))],
    )(shard)
    return lax.dynamic_update_slice(gathered, shard[None], (_my_chip(), 0, 0))


def _pair_split(grads, *, name):
    _, r, _ = grads.shape
    rh = r // 2
    rc = rh // PAIR_CHUNKS
    nchunk = 4 * PAIR_CHUNKS

    def body(g_ref, theirs_ref, send_sems, recv_sems):
        x, y, c = _coords()
        copies = []
        for j in range(4):
            for q in range(PAIR_CHUNKS):
                idx = j * PAIR_CHUNKS + q
                cp = pltpu.make_async_remote_copy(
                    src_ref=g_ref.at[j, pl.ds((1 - c) * rh + q * rc, rc), :], dst_ref=theirs_ref.at[j, pl.ds(q * rc, rc), :],
                    send_sem=send_sems.at[idx], recv_sem=recv_sems.at[idx], device_id=(x, y, 1 - c), device_id_type=MESH)
                cp.start()
                copies.append(cp)
        for cp in copies:
            cp.wait()

    return pl.pallas_call(
        body, name=name, in_specs=[ANY], out_specs=ANY, out_shape=jax.ShapeDtypeStruct((4, rh, LANES), F32),
        scratch_shapes=[pltpu.SemaphoreType.DMA((nchunk,)), pltpu.SemaphoreType.DMA((nchunk,))],
    )(grads)


def _pair_sum(grads, theirs, half, *, tr, name):
    _, rh, _ = theirs.shape
    nrt = rh // tr

    def body(half_ref, g_ref, t_ref, o_ref):
        o_ref[...] = g_ref[...] + t_ref[...]

    return pl.pallas_call(
        body, name=name,
        grid_spec=pltpu.PrefetchScalarGridSpec(
            num_scalar_prefetch=1, grid=(4, nrt),
            in_specs=[pl.BlockSpec((1, tr, LANES), lambda j, i, half_ref: (j, half_ref[0] * nrt + i, 0)),
                      pl.BlockSpec((1, tr, LANES), lambda j, i, half_ref: (j, i, 0))],
            out_specs=pl.BlockSpec((1, tr, LANES), lambda j, i, half_ref: (j, i, 0))),
        out_shape=jax.ShapeDtypeStruct((4, rh, LANES), F32),
        compiler_params=_params("parallel", "parallel"),
    )(half, grads, theirs)


def _scatter_chips(part, *, name):
    _, rh, _ = part.shape

    def body(p_ref, out, send_sems, recv_sems):
        x, y, c = _coords()
        me = 2 * x + y
        copies = []
        for k in (1, 2, 3):
            px, py = _chip_peer(x, y, k)
            cp = pltpu.make_async_remote_copy(src_ref=p_ref.at[2 * px + py], dst_ref=out.at[me], send_sem=send_sems.at[k - 1],
                                              recv_sem=recv_sems.at[k - 1], device_id=(px, py, c), device_id_type=MESH)
            cp.start()
            copies.append(cp)
        for cp in copies:
            cp.wait()

    from_chips = pl.pallas_call(
        body, name=name, in_specs=[ANY], out_specs=ANY, out_shape=jax.ShapeDtypeStruct((4, rh, LANES), F32),
        scratch_shapes=[pltpu.SemaphoreType.DMA((3,)), pltpu.SemaphoreType.DMA((3,))],
    )(part)
    own = lax.dynamic_index_in_dim(part, _my_chip(), axis=0, keepdims=True)
    return lax.dynamic_update_slice(from_chips, own, (_my_chip(), 0, 0))


def _pair_join(half, *, name):
    rh = half.shape[0]
    nchunk = 2 * PAIR_CHUNKS
    rc = rh // nchunk

    def body(h_ref, out, send_sems, recv_sems):
        x, y, c = _coords()
        copies = []
        for q in range(nchunk):
            src = h_ref.at[pl.ds(q * rc, rc), :]
            rows = out.at[pl.ds(c * rh + q * rc, rc), :]
            cp = pltpu.make_async_remote_copy(src_ref=src, dst_ref=rows, send_sem=send_sems.at[q], recv_sem=recv_sems.at[q],
                                              device_id=(x, y, 1 - c), device_id_type=MESH)
            cp.start()
            copies.append(cp)
        for cp in copies:
            cp.wait()

    joined = pl.pallas_call(
        body, name=name, in_specs=[ANY], out_specs=ANY, out_shape=jax.ShapeDtypeStruct((2 * rh, LANES), F32),
        scratch_shapes=[pltpu.SemaphoreType.DMA((nchunk,)), pltpu.SemaphoreType.DMA((nchunk,))],
    )(half)
    return lax.dynamic_update_slice(joined, half, (lax.axis_index("c") * rh, 0))


def _allgather_all(block, *, name):
    r = block.shape[0]

    def body(src, out, send_sems, recv_sems):
        x, y, c = _coords()
        me = 4 * x + 2 * y + c
        copies = []
        for k in range(1, 8):
            px, py = _chip_peer(x, y, k >> 1)
            pc = 1 - c if k & 1 else c
            cp = pltpu.make_async_remote_copy(src_ref=src, dst_ref=out.at[me], send_sem=send_sems.at[k - 1],
                                              recv_sem=recv_sems.at[k - 1], device_id=(px, py, pc), device_id_type=MESH)
            cp.start()
            copies.append(cp)
        for cp in copies:
            cp.wait()

    gathered = pl.pallas_call(
        body, name=name, in_specs=[ANY], out_specs=ANY, out_shape=jax.ShapeDtypeStruct((8, r, LANES), F32),
        scratch_shapes=[pltpu.SemaphoreType.DMA((7,)), pltpu.SemaphoreType.DMA((7,))],
    )(block)
    return lax.dynamic_update_slice(gathered, block[None], (2 * _my_chip() + lax.axis_index("c"), 0, 0))


def _flatten(arrays, pad_rows=None):
    flat = jnp.concatenate([a.reshape(-1) for a in arrays])
    if pad_rows is not None:
        flat = jnp.pad(flat, (0, pad_rows * LANES - flat.shape[0]))
    return flat.reshape(-1, LANES)


def _unflatten(flat, shapes):
    flat = flat.reshape(-1)
    out, off = [], 0
    for shp in shapes:
        size = 1
        for dim in shp:
            size *= dim
        out.append(flat[off:off + size].reshape(shp))
        off += size
    return out


def _pad_w_in(w):
    pad = jnp.zeros(w.shape[:-1] + (D_IN_PAD - D_IN,), w.dtype)
    return jnp.concatenate([w[..., :1536], w[..., 1544:], w[..., 1536:1544], pad], axis=-1)


def _unpad_w_in(w):
    return jnp.concatenate([w[..., :1536], w[..., 2560:2568], w[..., 1536:2560]], axis=-1)


def _tile(s, want):
    return min(want, s)


def _layer_fwd(x, p, l):
    s = x.shape[0]
    tr = _tile(s, 512)
    tm = _tile(s, 1024)
    xn = _rms_fwd([x], p["mix_g"], out_dtype=BF16, tr=tr, name=f"rms_mix_fwd{l}")
    h = _matmul(xn, p["w_in"], tm=tm, tn=896, out_dtype=F32, name=f"mm_in{l}")
    c = _gates_fwd(h, p["bf_pad"], tr=_tile(s, 256), name=f"gates_fwd{l}")
    attn, attn_res = _attention_fwd(h[:, 0:512], h[:, 512:1024], h[:, 1024:1536], c[:, :N_HEADS],
                                    tq=_tile(s, 2048), kb=_tile(s, 512), t=_tile(s, 512), name=f"attn_fwd{l}")
    sgu = _sgu_fwd(h, p["ln_g"], p["ln_b"], p["w_s"], p["bias_tile"], tr=_tile(s, 256), name=f"sgu_fwd{l}")
    merged = _rms_fwd([attn, sgu], p["out_g"], out_dtype=BF16, tr=tr, name=f"rms_out_fwd{l}")
    x1 = _matmul(merged, p["w_out"], tm=tm, tn=1024, out_dtype=F32, residual=x, name=f"mm_out{l}")
    xn2 = _rms_fwd([x1], p["ffn_g"], out_dtype=BF16, tr=tr, name=f"rms_ffn_fwd{l}")
    gu = _matmul(xn2, p["w_gu"], tm=tm, tn=1408, out_dtype=F32, name=f"mm_gu{l}")
    act = _swiglu_fwd(gu, tr=tr, name=f"swiglu_fwd{l}")
    x2 = _matmul(act, p["w_down"], tm=tm, tn=1024, out_dtype=F32, residual=x1, name=f"mm_down{l}")
    saved = dict(x=x, xn=xn, h=h, attn_res=attn_res, attn=attn, sgu=sgu, merged=merged, x1=x1, xn2=xn2, gu=gu, act=act)
    return x2, saved


def _layer_bwd(dx2, p, sv, l):
    s = dx2.shape[0]
    tr = _tile(s, 512)
    tm = _tile(s, 1024)
    ts = _tile(s, 512)
    g = {}
    g["w_down"] = _matmul_tn(sv["act"], dx2, tm=1408, tn=1024, ts=ts, name=f"mm_down_dw{l}")
    dact = _matmul(dx2, p["w_down"], trans_b=True, tm=tm, tn=1408, out_dtype=F32, name=f"mm_down_dx{l}")
    dgu = _swiglu_bwd(dact, sv["gu"], tr=_tile(s, 256), name=f"swiglu_bwd{l}")
    g["w_gu"] = _matmul_tn(sv["xn2"], dgu, tm=1024, tn=1408, ts=ts, name=f"mm_gu_dw{l}")
    dxn2 = _matmul(dgu, p["w_gu"], trans_b=True, tm=_tile(s, 512), tn=1024, out_dtype=F32, name=f"mm_gu_dx{l}")
    (dx1,), g["ffn_g"] = _rms_bwd(dxn2, [sv["x1"]], p["ffn_g"], residual=dx2, tr=tr, name=f"rms_ffn_bwd{l}")
    g["w_out"] = _matmul_tn(sv["merged"], dx1, tm=1024, tn=1024, ts=ts, name=f"mm_out_dw{l}")
    dmerged = _matmul(dx1, p["w_out"], trans_b=True, tm=tm, tn=1024, out_dtype=F32, name=f"mm_out_dx{l}")
    (dattn, dsgu), g["out_g"], delta = _rms_bwd(dmerged, [sv["attn"], sv["sgu"]], p["out_g"], head_dots=True, tr=tr,
                                                name=f"rms_out_bwd{l}")
    dzu, dzv, g["w_s"], g["ln_g"], g["ln_b"], dbs = _sgu_bwd(dsgu, sv["h"], p["ln_g"], p["ln_b"], p["w_s"], p["bias_tile"],
                                                           tr=_tile(s, 256), name=f"sgu_bwd{l}")
    g["b_s"] = dbs[:, :N_HEADS].T
    dq, dk, dv, dc8 = _attention_bwd(dattn, delta[:, :N_HEADS], sv["attn_res"], name=f"attn_bwd{l}")
    dc = jnp.pad(dc8, ((0, 0), (0, LANES - N_HEADS)))
    dfl, dbf = _gates_bwd(dc, sv["h"], p["bf_pad"], tr=_tile(s, 256), name=f"gates_bwd{l}")
    g["b_f"] = dbf[0, :N_HEADS]
    dh = jnp.concatenate([dq.astype(BF16), dk.astype(BF16), dv.astype(BF16), dzu, dzv, dfl], axis=1)
    g["w_in"] = _matmul_tn(sv["xn"], dh, tm=1024, tn=896, ts=ts, name=f"mm_in_dw{l}")
    dxn = _matmul(dh, p["w_in"], trans_b=True, tm=tm, tn=1024, out_dtype=F32, name=f"mm_in_dx{l}")
    (dx,), g["mix_g"] = _rms_bwd(dxn, [sv["x"]], p["mix_g"], residual=dx1, tr=tr, name=f"rms_mix_bwd{l}")
    return dx, g


def _layer_params(l, w_in_pad, w_out, w_gu, w_down, mix_norm_g, b_f, sgu_ln_g, sgu_ln_b, w_s, b_s, out_norm_g, ffn_norm_g):
    return dict(
        w_in=w_in_pad[l], w_out=w_out[l], w_gu=w_gu[l], w_down=w_down[l],
        mix_g=mix_norm_g[l][None, :], out_g=out_norm_g[l][None, :], ffn_g=ffn_norm_g[l][None, :],
        bf_pad=jnp.pad(b_f[l], (0, LANES - N_HEADS))[None, :],
        ln_g=sgu_ln_g[l][None, :], ln_b=sgu_ln_b[l][None, :], w_s=w_s[l],
        bias_tile=jnp.repeat(b_s[l].T, 64, axis=1),
    )


def _local_step(x, tgt, w_in_pad, w_out, w_gu, w_down, mix_norm_g, b_f, sgu_ln_g, sgu_ln_b, w_s, b_s, out_norm_g,
                ffn_norm_g, final_norm_g):
    depth = w_in_pad.shape[0]
    s = x.shape[0]
    params = [_layer_params(l, w_in_pad, w_out, w_gu, w_down, mix_norm_g, b_f, sgu_ln_g, sgu_ln_b, w_s, b_s, out_norm_g,
                            ffn_norm_g) for l in range(depth)]
    saved = []
    for l in range(depth):
        x, sv = _layer_fwd(x, params[l], l)
        saved.append(sv)
    loss_tile, dx, dfinal = _loss_head(x, tgt, final_norm_g[None, :], tr=_tile(s, 512), name="loss_head")
    grads = [None] * depth
    for l in reversed(range(depth)):
        dx, grads[l] = _layer_bwd(dx, params[l], saved[l], l)
    return loss_tile[0, 0], dx, grads, dfinal[0]


SMALL_ROWS = 4272


def kernel(x, mix_norm_g, w_in, b_f, sgu_ln_g, sgu_ln_b, w_s, b_s, out_norm_g, w_out, ffn_norm_g, w_gate_up, w_down, final_norm_g, loss_target, m_mix_norm_g, m_w_in, m_b_f, m_sgu_ln_g, m_sgu_ln_b, m_w_s, m_b_s, m_out_norm_g, m_w_out, m_ffn_norm_g, m_w_gate_up, m_w_down, m_final_norm_g, v_mix_norm_g, v_w_in, v_b_f, v_sgu_ln_g, v_sgu_ln_b, v_w_s, v_b_s, v_out_norm_g, v_w_out, v_ffn_norm_g, v_w_gate_up, v_w_down, v_final_norm_g):
    big = [w_in, w_out, w_gate_up, w_down]
    big_shapes = [a.shape for a in big]
    small = [mix_norm_g, b_f, sgu_ln_g, sgu_ln_b, w_s, b_s, out_norm_g, ffn_norm_g, final_norm_g]
    small_shapes = [a.shape for a in small]

    gathered = _allgather_chips(_flatten([a.astype(BF16) for a in big]), name="ag_weights")
    per_chip = [_unflatten(gathered[j], big_shapes) for j in range(4)]
    w_in_full = _pad_w_in(jnp.concatenate([pc[0] for pc in per_chip], axis=2))
    w_out_full = jnp.concatenate([pc[1] for pc in per_chip], axis=1)
    w_gu_full = jnp.concatenate([pc[2] for pc in per_chip], axis=2)
    w_down_full = jnp.concatenate([pc[3] for pc in per_chip], axis=1)

    loss_part, dx, grads, dfinal = _local_step(
        x[0], loss_target[0], w_in_full, w_out_full, w_gu_full, w_down_full, mix_norm_g, b_f, sgu_ln_g, sgu_ln_b, w_s, b_s,
        out_norm_g, ffn_norm_g, final_norm_g)
    stack = lambda key: jnp.stack([g[key] for g in grads])
    g_in = _unpad_w_in(stack("w_in"))
    g_out, g_gu, g_down = stack("w_out"), stack("w_gu"), stack("w_down")

    send = jnp.stack([_flatten([g_in[:, :, 642 * j:642 * (j + 1)], g_out[:, 256 * j:256 * (j + 1), :],
                                g_gu[:, :, 1408 * j:1408 * (j + 1)], g_down[:, 704 * j:704 * (j + 1), :]]) for j in range(4)])
    theirs = _pair_split(send, name="rs_pair_split")
    pair_sum = _pair_sum(send, theirs, lax.axis_index("c").astype(jnp.int32).reshape(1), tr=2408, name="rs_pair_sum")
    from_chips = _scatter_chips(pair_sum, name="rs_scatter")
    half = _sum_slots(from_chips, tr=1120, name="rs_chip_sum")
    g_big_flat = _pair_join(half, name="rs_pair_join")

    g_small_local = [stack("mix_g")[:, 0], stack("b_f"), stack("ln_g")[:, 0], stack("ln_b")[:, 0], stack("w_s"), stack("b_s"),
                     stack("out_g")[:, 0], stack("ffn_g")[:, 0], dfinal]
    g_small_flat = _sum_slots(_allgather_all(_flatten(g_small_local, SMALL_ROWS), name="ar_small_gather"), tr=1424,
                              name="ar_small_sum")
    loss = lax.psum(loss_part, ("x", "y", "c"))

    d_big, m_big, v_big = _adamw(_flatten(big), g_big_flat, _flatten([m_w_in, m_w_out, m_w_gate_up, m_w_down]),
                                 _flatten([v_w_in, v_w_out, v_w_gate_up, v_w_down]), tr=2240, name="adamw_big")
    m_small = [m_mix_norm_g, m_b_f, m_sgu_ln_g, m_sgu_ln_b, m_w_s, m_b_s, m_out_norm_g, m_ffn_norm_g, m_final_norm_g]
    v_small = [v_mix_norm_g, v_b_f, v_sgu_ln_g, v_sgu_ln_b, v_w_s, v_b_s, v_out_norm_g, v_ffn_norm_g, v_final_norm_g]
    d_small, m_small2, v_small2 = _adamw(_flatten(small, SMALL_ROWS), g_small_flat, _flatten(m_small, SMALL_ROWS),
                                         _flatten(v_small, SMALL_ROWS), tr=1424, name="adamw_small")

    def in_order(big_flat, small_flat):
        b_in, b_out, b_gu, b_down = _unflatten(big_flat, big_shapes)
        s_mix, s_bf, s_lng, s_lnb, s_ws, s_bs, s_outg, s_ffn, s_fin = _unflatten(small_flat, small_shapes)
        return [s_mix, b_in, s_bf, s_lng, s_lnb, s_ws, s_bs, s_outg, b_out, s_ffn, b_gu, b_down, s_fin]

    return (loss, dx[None], *in_order(g_big_flat, g_small_flat), *in_order(d_big, d_small), *in_order(m_big, m_small2),
            *in_order(v_big, v_small2))
```

```python
import jax
import jax.numpy as jnp
from jax import lax
from jax.experimental import pallas as pl
from jax.experimental.pallas import tpu as pltpu

F32 = jnp.float32
BF16 = jnp.bfloat16

D_MODEL = 1024
D_HALF = 512
N_HEADS = 8
HEAD_DIM = 64
SGU_CHUNK = 128
CAUSAL_CHUNK = 64
D_FF = 2816
D_IN = 2568
D_IN_PAD = 2688
F_COL_BLOCK = 2560 // 128
EPS = 1e-6
NEG = -1e30
LANES = 128
VMEM_LIMIT = 56 * 1024 * 1024

ADAM_LR = 0.001
ADAM_B1 = 0.9
ADAM_B2 = 0.999
ADAM_EPS = 1e-08
ADAM_WD = 0.01
ADAM_STEP = 10

MESH = pl.DeviceIdType.MESH
ANY = pl.BlockSpec(memory_space=pl.ANY)


def _params(*sem):
    return pltpu.CompilerParams(dimension_semantics=sem, vmem_limit_bytes=VMEM_LIMIT)


def _matmul(a, b, *, trans_b=False, tm, tn, out_dtype, residual=None, name):
    m, k = a.shape
    n = b.shape[0] if trans_b else b.shape[1]
    dims = (((1,), (1,)), ((), ())) if trans_b else (((1,), (0,)), ((), ()))

    def body(*refs):
        a_ref, b_ref = refs[0], refs[1]
        o_ref = refs[-1]
        acc = lax.dot_general(a_ref[...].astype(BF16), b_ref[...].astype(BF16), dims,
                              preferred_element_type=F32)
        if residual is not None:
            acc = acc + refs[2][...]
        o_ref[...] = acc.astype(out_dtype)

    b_spec = pl.BlockSpec((tn, k), lambda i, j: (j, 0)) if trans_b else pl.BlockSpec((k, tn), lambda i, j: (0, j))
    in_specs = [pl.BlockSpec((tm, k), lambda i, j: (i, 0)), b_spec]
    args = [a, b]
    if residual is not None:
        in_specs.append(pl.BlockSpec((tm, tn), lambda i, j: (i, j)))
        args.append(residual)
    return pl.pallas_call(
        body, name=name, grid=(m // tm, n // tn), in_specs=in_specs,
        out_specs=pl.BlockSpec((tm, tn), lambda i, j: (i, j)),
        out_shape=jax.ShapeDtypeStruct((m, n), out_dtype),
        compiler_params=_params("parallel", "parallel"),
    )(*args)


def _matmul_tn(a, b, *, tm, tn, ts, name):
    s, m = a.shape
    n = b.shape[1]

    def body(a_ref, b_ref, o_ref):
        @pl.when(pl.program_id(2) == 0)
        def _():
            o_ref[...] = jnp.zeros_like(o_ref)

        o_ref[...] += lax.dot_general(a_ref[...].astype(BF16), b_ref[...].astype(BF16),
                                      (((0,), (0,)), ((), ())), preferred_element_type=F32)

    return pl.pallas_call(
        body, name=name, grid=(m // tm, n // tn, s // ts),
        in_specs=[pl.BlockSpec((ts, tm), lambda i, j, t: (t, i)), pl.BlockSpec((ts, tn), lambda i, j, t: (t, j))],
        out_specs=pl.BlockSpec((tm, tn), lambda i, j, t: (i, j)),
        out_shape=jax.ShapeDtypeStruct((m, n), F32),
        compiler_params=_params("parallel", "parallel", "arbitrary"),
    )(a, b)


def _rms_fwd(xs, g, *, out_dtype, tr, name):
    s = xs[0].shape[0]
    widths = [x.shape[1] for x in xs]
    wsum = sum(widths)
    nx = len(xs)

    def body(*refs):
        g_ref, o_ref = refs[nx], refs[nx + 1]
        off = 0
        for x_ref, w in zip(refs[:nx], widths):
            x = x_ref[...]
            r = lax.rsqrt(jnp.mean(x * x, axis=-1, keepdims=True) + EPS)
            o_ref[:, off:off + w] = (x * r * g_ref[:, off:off + w]).astype(out_dtype)
            off += w

    return pl.pallas_call(
        body, name=name, grid=(s // tr,),
        in_specs=[pl.BlockSpec((tr, w), lambda i: (i, 0)) for w in widths] + [pl.BlockSpec((1, wsum), lambda i: (0, 0))],
        out_specs=pl.BlockSpec((tr, wsum), lambda i: (i, 0)),
        out_shape=jax.ShapeDtypeStruct((s, wsum), out_dtype),
        compiler_params=_params("parallel"),
    )(*xs, g)


def _rms_bwd(dy, xs, g, *, residual=None, head_dots=False, tr, name):
    s = xs[0].shape[0]
    widths = [x.shape[1] for x in xs]
    wsum = sum(widths)
    nx = len(xs)
    nin = 2 + nx + (residual is not None)

    def body(*refs):
        dy_ref, g_ref = refs[0], refs[1 + nx]
        dx_refs, dg_ref = refs[nin:nin + nx], refs[nin + nx]

        @pl.when(pl.program_id(0) == 0)
        def _():
            dg_ref[...] = jnp.zeros_like(dg_ref)

        off = 0
        for idx, (x_ref, w) in enumerate(zip(refs[1:1 + nx], widths)):
            x = x_ref[...]
            r = lax.rsqrt(jnp.mean(x * x, axis=-1, keepdims=True) + EPS)
            xh = x * r
            dyv = dy_ref[:, off:off + w]
            dxh = dyv * g_ref[:, off:off + w]
            dx = r * (dxh - xh * jnp.mean(dxh * xh, axis=-1, keepdims=True))
            if residual is not None and idx == 0:
                dx = dx + refs[2 + nx][...]
            dx_refs[idx][...] = dx
            dg_ref[:, off:off + w] += jnp.sum(dyv * xh, axis=0, keepdims=True)
            if head_dots and idx == 0:
                col = lax.broadcasted_iota(jnp.int32, (w, LANES), 0) // HEAD_DIM
                head = lax.broadcasted_iota(jnp.int32, (w, LANES), 1)
                refs[nin + nx + 1][...] = jnp.dot(dx * x, (col == head).astype(F32), precision=lax.Precision.HIGHEST,
                                                  preferred_element_type=F32)
            off += w

    in_specs = ([pl.BlockSpec((tr, wsum), lambda i: (i, 0))]
                + [pl.BlockSpec((tr, w), lambda i: (i, 0)) for w in widths]
                + [pl.BlockSpec((1, wsum), lambda i: (0, 0))])
    args = [dy, *xs, g]
    if residual is not None:
        in_specs.append(pl.BlockSpec((tr, widths[0]), lambda i: (i, 0)))
        args.append(residual)
    out_specs = [pl.BlockSpec((tr, w), lambda i: (i, 0)) for w in widths] + [pl.BlockSpec((1, wsum), lambda i: (0, 0))]
    out_shape = [jax.ShapeDtypeStruct((s, w), F32) for w in widths] + [jax.ShapeDtypeStruct((1, wsum), F32)]
    if head_dots:
        out_specs.append(pl.BlockSpec((tr, LANES), lambda i: (i, 0)))
        out_shape.append(jax.ShapeDtypeStruct((s, LANES), F32))
    outs = pl.pallas_call(
        body, name=name, grid=(s // tr,), in_specs=in_specs, out_specs=out_specs, out_shape=out_shape,
        compiler_params=_params("arbitrary"),
    )(*args)
    if head_dots:
        return outs[:nx], outs[nx], outs[nx + 1]
    return outs[:nx], outs[nx]


def _loss_head(x, tgt, g, *, tr, name):
    s, d = x.shape

    def body(x_ref, t_ref, g_ref, loss_ref, dx_ref, dg_ref):
        @pl.when(pl.program_id(0) == 0)
        def _():
            loss_ref[...] = jnp.zeros_like(loss_ref)
            dg_ref[...] = jnp.zeros_like(dg_ref)

        xv = x_ref[...]
        r = lax.rsqrt(jnp.mean(xv * xv, axis=-1, keepdims=True) + EPS)
        xh = xv * r
        err = xh * g_ref[...] - t_ref[...]
        loss_ref[...] += 0.5 * jnp.sum(jnp.mean(err * err, axis=-1, keepdims=True))
        dy = err * (1.0 / d)
        dxh = dy * g_ref[...]
        dx_ref[...] = r * (dxh - xh * jnp.mean(dxh * xh, axis=-1, keepdims=True))
        dg_ref[...] += jnp.sum(dy * xh, axis=0, keepdims=True)

    return pl.pallas_call(
        body, name=name, grid=(s // tr,),
        in_specs=[pl.BlockSpec((tr, d), lambda i: (i, 0)), pl.BlockSpec((tr, d), lambda i: (i, 0)),
                  pl.BlockSpec((1, d), lambda i: (0, 0))],
        out_specs=[pl.BlockSpec((8, LANES), lambda i: (0, 0)), pl.BlockSpec((tr, d), lambda i: (i, 0)),
                   pl.BlockSpec((1, d), lambda i: (0, 0))],
        out_shape=[jax.ShapeDtypeStruct((8, LANES), F32), jax.ShapeDtypeStruct((s, d), F32),
                   jax.ShapeDtypeStruct((1, d), F32)],
        compiler_params=_params("arbitrary"),
    )(x, tgt, g)


def _gates_fwd(h, bf_pad, *, tr, name):
    s = h.shape[0]

    def body(fl_ref, b_ref, c_ref, carry_ref):
        @pl.when(pl.program_id(0) == 0)
        def _():
            carry_ref[...] = jnp.zeros_like(carry_ref)

        lf = jax.nn.log_sigmoid(fl_ref[...] + b_ref[...])
        row = lax.broadcasted_iota(jnp.int32, (tr, tr), 0)
        col = lax.broadcasted_iota(jnp.int32, (tr, tr), 1)
        tri = (col <= row).astype(F32)
        c_ref[...] = jnp.dot(tri, lf, precision=lax.Precision.HIGHEST, preferred_element_type=F32) + carry_ref[...]
        carry_ref[...] += jnp.sum(lf, axis=0, keepdims=True)

    return pl.pallas_call(
        body, name=name, grid=(s // tr,),
        in_specs=[pl.BlockSpec((tr, LANES), lambda i: (i, F_COL_BLOCK)), pl.BlockSpec((1, LANES), lambda i: (0, 0))],
        out_specs=pl.BlockSpec((tr, LANES), lambda i: (i, 0)),
        out_shape=jax.ShapeDtypeStruct((s, LANES), F32),
        scratch_shapes=[pltpu.VMEM((1, LANES), F32)],
        compiler_params=_params("arbitrary"),
    )(h, bf_pad)


def _gates_bwd(dc, h, bf_pad, *, tr, name):
    s = h.shape[0]
    n = s // tr

    def body(dc_ref, fl_ref, b_ref, dfl_ref, db_ref, carry_ref):
        @pl.when(pl.program_id(0) == 0)
        def _():
            carry_ref[...] = jnp.zeros_like(carry_ref)
            db_ref[...] = jnp.zeros_like(db_ref)

        dcv = dc_ref[...]
        row = lax.broadcasted_iota(jnp.int32, (tr, tr), 0)
        col = lax.broadcasted_iota(jnp.int32, (tr, tr), 1)
        triu = (col >= row).astype(F32)
        dlf = jnp.dot(triu, dcv, precision=lax.Precision.HIGHEST, preferred_element_type=F32) + carry_ref[...]
        carry_ref[...] += jnp.sum(dcv, axis=0, keepdims=True)
        dfl = dlf * jax.nn.sigmoid(-(fl_ref[...] + b_ref[...]))
        dfl_ref[...] = dfl.astype(dfl_ref.dtype)
        db_ref[...] += jnp.sum(dfl, axis=0, keepdims=True)

    return pl.pallas_call(
        body, name=name, grid=(n,),
        in_specs=[pl.BlockSpec((tr, LANES), lambda i: (n - 1 - i, 0)),
                  pl.BlockSpec((tr, LANES), lambda i: (n - 1 - i, F_COL_BLOCK)),
                  pl.BlockSpec((1, LANES), lambda i: (0, 0))],
        out_specs=[pl.BlockSpec((tr, LANES), lambda i: (n - 1 - i, 0)), pl.BlockSpec((1, LANES), lambda i: (0, 0))],
        out_shape=[jax.ShapeDtypeStruct((s, LANES), BF16), jax.ShapeDtypeStruct((1, LANES), F32)],
        scratch_shapes=[pltpu.VMEM((1, LANES), F32)],
        compiler_params=_params("arbitrary"),
    )(dc, h, bf_pad)


LOG2E = 1.4426950408889634
LN2 = 0.6931471805599453
V_ROWS = 80


def _transpose_bf16(a):
    return a.astype(F32).T.astype(BF16)


def _transpose_heads(a, *, t, name):
    s = a.shape[0]

    def body(a_ref, o_ref):
        o_ref[0, 0] = _transpose_bf16(a_ref[...])

    return pl.pallas_call(
        body, name=name, grid=(N_HEADS, s // t),
        in_specs=[pl.BlockSpec((t, LANES), lambda h, i: (i, h))],
        out_specs=pl.BlockSpec((1, 1, LANES, t), lambda h, i: (h, i, 0, 0)),
        out_shape=jax.ShapeDtypeStruct((N_HEADS, s // t, LANES, t), BF16),
        compiler_params=_params("parallel", "parallel"),
    )(a)


SKIP_MARGIN = 160.0
SMEM = pl.BlockSpec(memory_space=pltpu.SMEM)


def _attn_fwd(qat, ka, va, gnorm, cmax_q, cmin_k, *, tq, kb, name):
    s = ka.shape[0]
    nq = s // tq
    per_tile = tq // kb
    tt = qat.shape[3]
    sub = tq // tt

    def body(gn_ref, cq_ref, ck_ref, qat_ref, ka_ref, va_ref, o_ref, lse_ref):
        h, i = pl.program_id(0), pl.program_id(1)
        qat = jnp.concatenate([qat_ref[0, d] for d in range(sub)], axis=1)

        def blk(n, carry, masked):
            m, acc = carry
            rows = pl.ds(pl.multiple_of(n * kb, kb), kb)
            sc = jnp.dot(ka_ref[rows, :], qat, preferred_element_type=F32)
            if masked:
                key = n * kb + lax.broadcasted_iota(jnp.int32, (kb, tq), 0)
                qry = i * tq + lax.broadcasted_iota(jnp.int32, (kb, tq), 1)
                sc = jnp.where(key <= qry, sc, NEG)
            m_new = jnp.maximum(m, jnp.max(sc, axis=0, keepdims=True))
            p = jnp.exp2(sc - m_new)
            vt = _transpose_bf16(va_ref[rows, :])[:V_ROWS]
            acc = jnp.exp2(m - m_new) * acc + jnp.dot(vt, p.astype(BF16), preferred_element_type=F32)
            return m_new, acc

        carry = (jnp.full((1, tq), NEG, F32), jnp.zeros((V_ROWS, tq), F32))
        for d in reversed(range(per_tile)):
            carry = blk(i * per_tile + d, carry, True)
        top = gn_ref[h] + cq_ref[h, i]

        def live(state):
            n, m_min = state[0], state[1]
            return jnp.logical_and(n >= 0, top - ck_ref[h, jnp.maximum(n, 0)] >= m_min - SKIP_MARGIN)

        def step(state):
            n, _, m, acc = state
            m, acc = blk(n, (m, acc), False)
            return n - 1, jnp.min(m), m, acc

        _, _, m, acc = lax.while_loop(live, step, (i * per_tile - 1, jnp.min(carry[0]), *carry))
        l = acc[HEAD_DIM:HEAD_DIM + 1, :]
        padded = jnp.concatenate([acc / l, jnp.zeros((LANES - V_ROWS, tq), F32)], axis=0)
        o_ref[0, 0] = padded.T[:, :HEAD_DIM]
        lse_ref[0, 0] = m + jnp.log2(l)

    return pl.pallas_call(
        body, name=name, grid=(N_HEADS, nq),
        in_specs=[SMEM, SMEM, SMEM, pl.BlockSpec((1, sub, LANES, tt), lambda h, i: (h, i, 0, 0)),
                  pl.BlockSpec((s, LANES), lambda h, i: (0, h)),
                  pl.BlockSpec((s, LANES), lambda h, i: (0, h))],
        out_specs=[pl.BlockSpec((1, 1, tq, HEAD_DIM), lambda h, i: (h, i, 0, 0)),
                   pl.BlockSpec((1, 1, 1, tq), lambda h, i: (h, i, 0, 0))],
        out_shape=[jax.ShapeDtypeStruct((N_HEADS, nq, tq, HEAD_DIM), F32), jax.ShapeDtypeStruct((N_HEADS, nq, 1, tq), F32)],
        compiler_params=_params("parallel", "arbitrary"),
    )(gnorm, cmax_q, cmin_k, qat, ka, va)


def _attn_bwd(qa, qat, doa, dot, lse, dl, ka, va, gnorm, cmin_k, reach, *, name):
    s = qa.shape[0]
    nt, t = qat.shape[1], qat.shape[3]

    def body(gn_ref, ck_ref, reach_ref, qa_ref, qat_ref, do_ref, dot_ref, lse_ref, dl_ref, ka_ref, va_ref,
             dq_ref, dk_ref, dv_ref):
        h, j = pl.program_id(0), pl.program_id(1)

        @pl.when(j == 0)
        def _():
            dq_ref[...] = jnp.zeros_like(dq_ref)

        ka_j, va_j = ka_ref[...], va_ref[...]

        def tile(i, carry, masked):
            dk, dv = carry
            rows = pl.ds(pl.multiple_of(i * t, t), t)
            qa_i, do_i = qa_ref[rows, :], do_ref[rows, :]
            st = jnp.dot(ka_j, qat_ref[0, i], preferred_element_type=F32) - lse_ref[0, i]
            if masked:
                key = lax.broadcasted_iota(jnp.int32, (t, t), 0)
                qry = lax.broadcasted_iota(jnp.int32, (t, t), 1)
                st = jnp.where(key <= qry, st, NEG)
            pt = jnp.exp2(st)
            dpt = jnp.dot(va_j, dot_ref[0, i], preferred_element_type=F32)
            dsb = (pt * (dpt - dl_ref[0, i])).astype(BF16)
            dv = dv + jnp.dot(pt.astype(BF16), do_i, preferred_element_type=F32)
            dk = dk + jnp.dot(dsb, qa_i, preferred_element_type=F32)
            dq_ref[rows, :] += lax.dot_general(dsb, ka_j, (((0,), (0,)), ((), ())), preferred_element_type=F32)
            return dk, dv

        carry = tile(j, (jnp.zeros((t, LANES), F32), jnp.zeros((t, LANES), F32)), True)
        base = gn_ref[h] - ck_ref[h, j]

        def live(state):
            i = state[0]
            return jnp.logical_and(i < nt, base + reach_ref[h, jnp.minimum(i, nt - 1)] >= -SKIP_MARGIN)

        def step(state):
            i, dk, dv = state
            dk, dv = tile(i, (dk, dv), False)
            return i + 1, dk, dv

        _, dk, dv = lax.while_loop(live, step, (j + 1, *carry))
        dk_ref[...] = dk
        dv_ref[...] = dv

    res = pl.BlockSpec((s, LANES), lambda h, j: (0, h))
    rest = pl.BlockSpec((1, nt, LANES, t), lambda h, j: (h, 0, 0, 0))
    row = pl.BlockSpec((1, nt, 1, t), lambda h, j: (h, 0, 0, 0))
    blk = pl.BlockSpec((t, LANES), lambda h, j: (j, h))
    shape = jax.ShapeDtypeStruct((s, N_HEADS * LANES), F32)
    return pl.pallas_call(
        body, name=name, grid=(N_HEADS, nt),
        in_specs=[SMEM, SMEM, SMEM, res, rest, res, rest, row, row, blk, blk], out_specs=[res, blk, blk],
        out_shape=[shape, shape, shape], compiler_params=_params("parallel", "arbitrary"),
    )(gnorm, cmin_k, reach, qa, qat, doa, dot, lse, dl, ka, va)


def _split3(x):
    rnd = lambda a: lax.reduce_precision(a, exponent_bits=8, mantissa_bits=7)
    x1 = rnd(x)
    r1 = x - x1
    x2 = rnd(r1)
    x3 = rnd(r1 - x2)
    return [x1, x2, x3]


def _head_blocks(core, extra):
    s = core.shape[0]
    cols = [core.reshape(s, N_HEADS, HEAD_DIM).astype(BF16)] + [c.astype(BF16)[..., None] for c in extra]
    cols.append(jnp.zeros((s, N_HEADS, LANES - HEAD_DIM - len(extra)), BF16))
    return jnp.concatenate(cols, axis=-1).reshape(s, N_HEADS * LANES)


def _attention_fwd(q, k, v, c, *, tq, kb, t, name):
    s = q.shape[0]
    ones = [jnp.ones((s, N_HEADS), BF16)] * 3
    c2 = c * LOG2E
    qa = _head_blocks(q * 0.125, ones + _split3(c2))
    ka = _head_blocks(k * LOG2E, _split3(-c2) + ones)
    va = _head_blocks(v, ones[:1])
    qat = _transpose_heads(qa, t=t, name=name + "_qt")
    head_norm = lambda a: jnp.sqrt(jnp.max(jnp.sum(jnp.square(a.reshape(s, N_HEADS, HEAD_DIM)), axis=-1), axis=0))
    gnorm = head_norm(q * 0.125) * head_norm(k * LOG2E) * 1.01 + 1.0
    cmax_q = jnp.max(c2.reshape(s // tq, tq, N_HEADS), axis=1).T
    cmin_k = lax.cummin(jnp.min(c2.reshape(s // kb, kb, N_HEADS), axis=1), axis=0).T
    o4, lse2 = _attn_fwd(qat, ka, va, gnorm, cmax_q, cmin_k, tq=tq, kb=kb, name=name)
    attn = o4.reshape(N_HEADS, s, HEAD_DIM).transpose(1, 0, 2).reshape(s, N_HEADS * HEAD_DIM)
    return attn, dict(qa=qa, qat=qat, ka=ka, va=va, lse2=lse2, c2=c2, gnorm=gnorm)


def _attention_bwd(dattn, delta, res, *, name):
    s = dattn.shape[0]
    t = res["qat"].shape[3]
    doa = _head_blocks(dattn, [])
    c2t = res["c2"].reshape(s // t, t, N_HEADS)
    lse_t = res["lse2"].reshape(N_HEADS, s // t, t)
    reach = lax.cummax(jnp.max(c2t.transpose(2, 0, 1) - lse_t, axis=2), axis=1, reverse=True)
    dqa, dka, dva = _attn_bwd(res["qa"], res["qat"], doa, _transpose_heads(doa, t=t, name=name + "_dot"),
                              lse_t.reshape(N_HEADS, s // t, 1, t), delta.T.reshape(N_HEADS, s // t, 1, t),
                              res["ka"], res["va"], res["gnorm"], jnp.min(c2t, axis=1).T, reach, name=name)
    dqa, dka, dva = (a.reshape(s, N_HEADS, LANES) for a in (dqa, dka, dva))
    core = lambda a: a[:, :, :HEAD_DIM].reshape(s, N_HEADS * HEAD_DIM)
    dc = dqa[:, :, HEAD_DIM + 3] - dka[:, :, HEAD_DIM]
    return core(dqa) * (LN2 * 0.125), core(dka), core(dva), dc


def _gelu(z):
    return 0.5 * z * (1.0 + lax.erf(z * 0.7071067811865476))


def _gelu_grad(z):
    return 0.5 * (1.0 + lax.erf(z * 0.7071067811865476)) + z * (0.3989422804014327 * jnp.exp(-0.5 * z * z))


def _sgu_mask():
    i = lax.broadcasted_iota(jnp.int32, (SGU_CHUNK, SGU_CHUNK), 0) // CAUSAL_CHUNK
    j = lax.broadcasted_iota(jnp.int32, (SGU_CHUNK, SGU_CHUNK), 1) // CAUSAL_CHUNK
    return (j <= i).astype(F32)


def _layernorm_stats(x):
    mu = jnp.mean(x, axis=-1, keepdims=True)
    xc = x - mu
    rstd = lax.rsqrt(jnp.mean(xc * xc, axis=-1, keepdims=True) + EPS)
    return xc * rstd, rstd


def _first_group_lanes():
    return lax.broadcasted_iota(jnp.int32, (SGU_CHUNK, LANES), 1) < 64


def _sgu_fwd(h, ln_g, ln_b, w_s, bias_tile, *, tr, name):
    s = h.shape[0]
    zu_blk, zv_blk = 1536 // D_HALF, 2048 // D_HALF

    def body(zu_ref, zv_ref, lng_ref, lnb_ref, ws_ref, bias_ref, o_ref):
        gzu = _gelu(zu_ref[...])
        xh, _ = _layernorm_stats(_gelu(zv_ref[...]))
        zb = (xh * lng_ref[...] + lnb_ref[...]).astype(BF16)
        mask = _sgu_mask()
        first = _first_group_lanes()
        for pair in range(4):
            cols = slice(pair * LANES, (pair + 1) * LANES)
            w0 = (ws_ref[2 * pair] * mask).astype(BF16)
            w1 = (ws_ref[2 * pair + 1] * mask).astype(BF16)
            for ch in range(tr // SGU_CHUNK):
                rows = slice(ch * SGU_CHUNK, (ch + 1) * SGU_CHUNK)
                zp = zb[rows, cols]
                mixed = jnp.where(first, jnp.dot(w0, zp, preferred_element_type=F32),
                                  jnp.dot(w1, zp, preferred_element_type=F32)) + bias_ref[:, cols]
                o_ref[rows, cols] = gzu[rows, cols] * mixed

    return pl.pallas_call(
        body, name=name, grid=(s // tr,),
        in_specs=[pl.BlockSpec((tr, D_HALF), lambda i: (i, zu_blk)), pl.BlockSpec((tr, D_HALF), lambda i: (i, zv_blk)),
                  pl.BlockSpec((1, D_HALF), lambda i: (0, 0)), pl.BlockSpec((1, D_HALF), lambda i: (0, 0)),
                  pl.BlockSpec((N_HEADS, SGU_CHUNK, SGU_CHUNK), lambda i: (0, 0, 0)),
                  pl.BlockSpec((SGU_CHUNK, D_HALF), lambda i: (0, 0))],
        out_specs=pl.BlockSpec((tr, D_HALF), lambda i: (i, 0)),
        out_shape=jax.ShapeDtypeStruct((s, D_HALF), F32),
        compiler_params=_params("parallel"),
    )(h, h, ln_g, ln_b, w_s, bias_tile)


def _sgu_bwd(dsgu, h, ln_g, ln_b, w_s, bias_tile, *, tr, name):
    s = h.shape[0]
    n = s // tr
    zu_blk, zv_blk = 1536 // D_HALF, 2048 // D_HALF

    def body(ds_ref, zu_ref, zv_ref, lng_ref, lnb_ref, ws_ref, bias_ref,
             dzu_ref, dzv_ref, dws_ref, dlng_ref, dlnb_ref, dbs_ref, dgzu_sc, dzvn_sc, dbias_sc):
        step = pl.program_id(0)

        @pl.when(step == 0)
        def _():
            dws_ref[...] = jnp.zeros_like(dws_ref)
            dlng_ref[...] = jnp.zeros_like(dlng_ref)
            dlnb_ref[...] = jnp.zeros_like(dlnb_ref)
            dbias_sc[...] = jnp.zeros_like(dbias_sc)

        zu = zu_ref[...]
        zv = zv_ref[...]
        gzu = _gelu(zu)
        xh, rstd = _layernorm_stats(_gelu(zv))
        zb = (xh * lng_ref[...] + lnb_ref[...]).astype(BF16)
        ds = ds_ref[...]
        mask = _sgu_mask()
        first = _first_group_lanes()
        tn_dims = (((0,), (0,)), ((), ()))
        nt_dims = (((1,), (1,)), ((), ()))
        for pair in range(4):
            cols = slice(pair * LANES, (pair + 1) * LANES)
            w0 = (ws_ref[2 * pair] * mask).astype(BF16)
            w1 = (ws_ref[2 * pair + 1] * mask).astype(BF16)
            for ch in range(tr // SGU_CHUNK):
                rows = slice(ch * SGU_CHUNK, (ch + 1) * SGU_CHUNK)
                zp = zb[rows, cols]
                mixed = jnp.where(first, jnp.dot(w0, zp, preferred_element_type=F32),
                                  jnp.dot(w1, zp, preferred_element_type=F32)) + bias_ref[:, cols]
                dsp = ds[rows, cols]
                dgzu_sc[rows, cols] = dsp * mixed
                dm = dsp * gzu[rows, cols]
                dbias_sc[:, cols] += dm
                dmb = dm.astype(BF16)
                dm0 = jnp.where(first, dmb, jnp.zeros_like(dmb))
                dm1 = jnp.where(first, jnp.zeros_like(dmb), dmb)
                dws_ref[2 * pair] += lax.dot_general(dm0, zp, nt_dims, preferred_element_type=F32)
                dws_ref[2 * pair + 1] += lax.dot_general(dm1, zp, nt_dims, preferred_element_type=F32)
                dzvn_sc[rows, cols] = jnp.where(first, lax.dot_general(w0, dmb, tn_dims, preferred_element_type=F32),
                                                lax.dot_general(w1, dmb, tn_dims, preferred_element_type=F32))
        dzvn = dzvn_sc[...]
        dlng_ref[...] += jnp.sum(dzvn * xh, axis=0, keepdims=True)
        dlnb_ref[...] += jnp.sum(dzvn, axis=0, keepdims=True)
        dxh = dzvn * lng_ref[...]
        dgzv = rstd * (dxh - jnp.mean(dxh, axis=-1, keepdims=True) - xh * jnp.mean(dxh * xh, axis=-1, keepdims=True))
        dzv_ref[...] = (dgzv * _gelu_grad(zv)).astype(dzv_ref.dtype)
        dzu_ref[...] = (dgzu_sc[...] * _gelu_grad(zu)).astype(dzu_ref.dtype)

        @pl.when(step == n - 1)
        def _():
            for g in range(N_HEADS):
                dws_ref[g] = dws_ref[g] * mask
            lane = lax.broadcasted_iota(jnp.int32, (D_HALF, LANES), 0) // 64
            grp = lax.broadcasted_iota(jnp.int32, (D_HALF, LANES), 1)
            dbs_ref[...] = jnp.dot(dbias_sc[...], (lane == grp).astype(F32), precision=lax.Precision.HIGHEST,
                                   preferred_element_type=F32)

    const2 = lambda i: (0, 0)
    return pl.pallas_call(
        body, name=name, grid=(n,),
        in_specs=[pl.BlockSpec((tr, D_HALF), lambda i: (i, 0)),
                  pl.BlockSpec((tr, D_HALF), lambda i: (i, zu_blk)), pl.BlockSpec((tr, D_HALF), lambda i: (i, zv_blk)),
                  pl.BlockSpec((1, D_HALF), const2), pl.BlockSpec((1, D_HALF), const2),
                  pl.BlockSpec((N_HEADS, SGU_CHUNK, SGU_CHUNK), lambda i: (0, 0, 0)),
                  pl.BlockSpec((SGU_CHUNK, D_HALF), const2)],
        out_specs=[pl.BlockSpec((tr, D_HALF), lambda i: (i, 0)), pl.BlockSpec((tr, D_HALF), lambda i: (i, 0)),
                   pl.BlockSpec((N_HEADS, SGU_CHUNK, SGU_CHUNK), lambda i: (0, 0, 0)),
                   pl.BlockSpec((1, D_HALF), const2), pl.BlockSpec((1, D_HALF), const2),
                   pl.BlockSpec((SGU_CHUNK, LANES), const2)],
        out_shape=[jax.ShapeDtypeStruct((s, D_HALF), BF16), jax.ShapeDtypeStruct((s, D_HALF), BF16),
                   jax.ShapeDtypeStruct((N_HEADS, SGU_CHUNK, SGU_CHUNK), F32),
                   jax.ShapeDtypeStruct((1, D_HALF), F32), jax.ShapeDtypeStruct((1, D_HALF), F32),
                   jax.ShapeDtypeStruct((SGU_CHUNK, LANES), F32)],
        scratch_shapes=[pltpu.VMEM((tr, D_HALF), F32), pltpu.VMEM((tr, D_HALF), F32), pltpu.VMEM((SGU_CHUNK, D_HALF), F32)],
        compiler_params=_params("arbitrary"),
    )(dsgu, h, h, ln_g, ln_b, w_s, bias_tile)


FF_BLOCK = D_FF // 2


def _swiglu_fwd(gu, *, tr, name):
    s = gu.shape[0]

    def body(g_ref, u_ref, o_ref):
        g = g_ref[...]
        o_ref[...] = (g * jax.nn.sigmoid(g) * u_ref[...]).astype(o_ref.dtype)

    return pl.pallas_call(
        body, name=name, grid=(s // tr, 2),
        in_specs=[pl.BlockSpec((tr, FF_BLOCK), lambda i, j: (i, j)), pl.BlockSpec((tr, FF_BLOCK), lambda i, j: (i, j + 2))],
        out_specs=pl.BlockSpec((tr, FF_BLOCK), lambda i, j: (i, j)),
        out_shape=jax.ShapeDtypeStruct((s, D_FF), BF16),
        compiler_params=_params("parallel", "parallel"),
    )(gu, gu)


def _swiglu_bwd(dact, gu, *, tr, name):
    s = gu.shape[0]

    def body(d_ref, gu_ref, o_ref):
        g = gu_ref[:, :D_FF]
        d = d_ref[...]
        sig = jax.nn.sigmoid(g)
        o_ref[:, :D_FF] = (d * gu_ref[:, D_FF:] * (sig * (1.0 + g * (1.0 - sig)))).astype(o_ref.dtype)
        o_ref[:, D_FF:] = (d * (g * sig)).astype(o_ref.dtype)

    return pl.pallas_call(
        body, name=name, grid=(s // tr,),
        in_specs=[pl.BlockSpec((tr, D_FF), lambda i: (i, 0)), pl.BlockSpec((tr, 2 * D_FF), lambda i: (i, 0))],
        out_specs=pl.BlockSpec((tr, 2 * D_FF), lambda i: (i, 0)),
        out_shape=jax.ShapeDtypeStruct((s, 2 * D_FF), BF16),
        compiler_params=_params("parallel"),
    )(dact, gu)


def _sum_slots(stacked, *, tr, name):
    k, r, _ = stacked.shape

    def body(x_ref, o_ref):
        acc = x_ref[0]
        for idx in range(1, k):
            acc = acc + x_ref[idx]
        o_ref[...] = acc

    return pl.pallas_call(
        body, name=name, grid=(r // tr,),
        in_specs=[pl.BlockSpec((k, tr, LANES), lambda i: (0, i, 0))],
        out_specs=pl.BlockSpec((tr, LANES), lambda i: (i, 0)),
        out_shape=jax.ShapeDtypeStruct((r, LANES), F32),
        compiler_params=_params("parallel"),
    )(stacked)


def _adamw(w, g, m, v, *, tr, name):
    r = w.shape[0]

    def body(w_ref, g_ref, m_ref, v_ref, d_ref, m2_ref, v2_ref):
        gv = g_ref[...]
        m2 = ADAM_B1 * m_ref[...] + (1.0 - ADAM_B1) * gv
        v2 = ADAM_B2 * v_ref[...] + (1.0 - ADAM_B2) * jnp.square(gv)
        m_hat = m2 / (1.0 - ADAM_B1 ** ADAM_STEP)
        v_hat = v2 / (1.0 - ADAM_B2 ** ADAM_STEP)
        d_ref[...] = -ADAM_LR * (m_hat / (jnp.sqrt(v_hat) + ADAM_EPS) + ADAM_WD * w_ref[...])
        m2_ref[...] = m2
        v2_ref[...] = v2

    spec = pl.BlockSpec((tr, LANES), lambda i: (i, 0))
    shape = jax.ShapeDtypeStruct((r, LANES), F32)
    return pl.pallas_call(
        body, name=name, grid=(r // tr,), in_specs=[spec] * 4, out_specs=[spec] * 3, out_shape=[shape] * 3,
        compiler_params=_params("parallel"),
    )(w, g, m, v)


PAIR_CHUNKS = 5


def _coords():
    return lax.axis_index("x"), lax.axis_index("y"), lax.axis_index("c")


def _my_chip():
    return 2 * lax.axis_index("x") + lax.axis_index("y")


def _chip_peer(x, y, k):
    px = 1 - x if k & 2 else x
    py = 1 - y if k & 1 else y
    return px, py


def _allgather_chips(shard, *, name):
    r = shard.shape[0]

    def body(src, out, send_sems, recv_sems):
        x, y, c = _coords()
        me = 2 * x + y
        copies = []
        for k in (1, 2, 3):
            px, py = _chip_peer(x, y, k)
            cp = pltpu.make_async_remote_copy(src_ref=src, dst_ref=out.at[me], send_sem=send_sems.at[k - 1],
                                              recv_sem=recv_sems.at[k - 1], device_id=(px, py, c), device_id_type=MESH)
            cp.start()
            copies.append(cp)
        for cp in copies:
            cp.wait()

    gathered = pl.pallas_call(
        body, name=name, in_specs=[ANY], out_specs=ANY,
        out_shape=jax.ShapeDtypeStruct((4, r, LANES), shard.dtype),
        scratch_shapes=[pltpu.SemaphoreType.DMA((3,)), pltpu.SemaphoreType.DMA((3,))],
    )(shard)
    return lax.dynamic_update_slice(gathered, shard[None], (_my_chip(), 0, 0))


def _pair_split(grads, *, name):
    _, r, _ = grads.shape
    rh = r // 2
    rc = rh // PAIR_CHUNKS
    nchunk = 4 * PAIR_CHUNKS

    def body(g_ref, theirs_ref, send_sems, recv_sems):
        x, y, c = _coords()
        copies = []
        for j in range(4):
            for q in range(PAIR_CHUNKS):
                idx = j * PAIR_CHUNKS + q
                cp = pltpu.make_async_remote_copy(
                    src_ref=g_ref.at[j, pl.ds((1 - c) * rh + q * rc, rc), :], dst_ref=theirs_ref.at[j, pl.ds(q * rc, rc), :],
                    send_sem=send_sems.at[idx], recv_sem=recv_sems.at[idx], device_id=(x, y, 1 - c), device_id_type=MESH)
                cp.start()
                copies.append(cp)
        for cp in copies:
            cp.wait()

    return pl.pallas_call(
        body, name=name, in_specs=[ANY], out_specs=ANY, out_shape=jax.ShapeDtypeStruct((4, rh, LANES), F32),
        scratch_shapes=[pltpu.SemaphoreType.DMA((nchunk,)), pltpu.SemaphoreType.DMA((nchunk,))],
    )(grads)


def _pair_sum(grads, theirs, half, *, tr, name):
    _, rh, _ = theirs.shape
    nrt = rh // tr

    def body(half_ref, g_ref, t_ref, o_ref):
        o_ref[...] = g_ref[...] + t_ref[...]

    return pl.pallas_call(
        body, name=name,
        grid_spec=pltpu.PrefetchScalarGridSpec(
            num_scalar_prefetch=1, grid=(4, nrt),
            in_specs=[pl.BlockSpec((1, tr, LANES), lambda j, i, half_ref: (j, half_ref[0] * nrt + i, 0)),
                      pl.BlockSpec((1, tr, LANES), lambda j, i, half_ref: (j, i, 0))],
            out_specs=pl.BlockSpec((1, tr, LANES), lambda j, i, half_ref: (j, i, 0))),
        out_shape=jax.ShapeDtypeStruct((4, rh, LANES), F32),
        compiler_params=_params("parallel", "parallel"),
    )(half, grads, theirs)


def _scatter_chips(part, *, name):
    _, rh, _ = part.shape

    def body(p_ref, out, send_sems, recv_sems):
        x, y, c = _coords()
        me = 2 * x + y
        copies = []
        for k in (1, 2, 3):
            px, py = _chip_peer(x, y, k)
            cp = pltpu.make_async_remote_copy(src_ref=p_ref.at[2 * px + py], dst_ref=out.at[me], send_sem=send_sems.at[k - 1],
                                              recv_sem=recv_sems.at[k - 1], device_id=(px, py, c), device_id_type=MESH)
            cp.start()
            copies.append(cp)
        for cp in copies:
            cp.wait()

    from_chips = pl.pallas_call(
        body, name=name, in_specs=[ANY], out_specs=ANY, out_shape=jax.ShapeDtypeStruct((4, rh, LANES), F32),
        scratch_shapes=[pltpu.SemaphoreType.DMA((3,)), pltpu.SemaphoreType.DMA((3,))],
    )(part)
    own = lax.dynamic_index_in_dim(part, _my_chip(), axis=0, keepdims=True)
    return lax.dynamic_update_slice(from_chips, own, (_my_chip(), 0, 0))


def _pair_join(half, *, name):
    rh = half.shape[0]
    nchunk = 2 * PAIR_CHUNKS
    rc = rh // nchunk

    def body(h_ref, out, send_sems, recv_sems):
        x, y, c = _coords()
        copies = []
        for q in range(nchunk):
            src = h_ref.at[pl.ds(q * rc, rc), :]
            rows = out.at[pl.ds(c * rh + q * rc, rc), :]
            cp = pltpu.make_async_remote_copy(src_ref=src, dst_ref=rows, send_sem=send_sems.at[q], recv_sem=recv_sems.at[q],
                                              device_id=(x, y, 1 - c), device_id_type=MESH)
            cp.start()
            copies.append(cp)
        for cp in copies:
            cp.wait()

    joined = pl.pallas_call(
        body, name=name, in_specs=[ANY], out_specs=ANY, out_shape=jax.ShapeDtypeStruct((2 * rh, LANES), F32),
        scratch_shapes=[pltpu.SemaphoreType.DMA((nchunk,)), pltpu.SemaphoreType.DMA((nchunk,))],
    )(half)
    return lax.dynamic_update_slice(joined, half, (lax.axis_index("c") * rh, 0))


def _allgather_all(block, *, name):
    r = block.shape[0]

    def body(src, out, send_sems, recv_sems):
        x, y, c = _coords()
        me = 4 * x + 2 * y + c
        copies = []
        for k in range(1, 8):
            px, py = _chip_peer(x, y, k >> 1)
            pc = 1 - c if k & 1 else c
            cp = pltpu.make_async_remote_copy(src_ref=src, dst_ref=out.at[me], send_sem=send_sems.at[k - 1],
                                              recv_sem=recv_sems.at[k - 1], device_id=(px, py, pc), device_id_type=MESH)
            cp.start()
            copies.append(cp)
        for cp in copies:
            cp.wait()

    gathered = pl.pallas_call(
        body, name=name, in_specs=[ANY], out_specs=ANY, out_shape=jax.ShapeDtypeStruct((8, r, LANES), F32),
        scratch_shapes=[pltpu.SemaphoreType.DMA((7,)), pltpu.SemaphoreType.DMA((7,))],
    )(block)
    return lax.dynamic_update_slice(gathered, block[None], (2 * _my_chip() + lax.axis_index("c"), 0, 0))


def _flatten(arrays, pad_rows=None):
    flat = jnp.concatenate([a.reshape(-1) for a in arrays])
    if pad_rows is not None:
        flat = jnp.pad(flat, (0, pad_rows * LANES - flat.shape[0]))
    return flat.reshape(-1, LANES)


def _unflatten(flat, shapes):
    flat = flat.reshape(-1)
    out, off = [], 0
    for shp in shapes:
        size = 1
        for dim in shp:
            size *= dim
        out.append(flat[off:off + size].reshape(shp))
        off += size
    return out


def _pad_w_in(w):
    pad = jnp.zeros(w.shape[:-1] + (D_IN_PAD - D_IN,), w.dtype)
    return jnp.concatenate([w[..., :1536], w[..., 1544:], w[..., 1536:1544], pad], axis=-1)


def _unpad_w_in(w):
    return jnp.concatenate([w[..., :1536], w[..., 2560:2568], w[..., 1536:2560]], axis=-1)


def _tile(s, want):
    return min(want, s)


def _layer_fwd(x, p, l):
    s = x.shape[0]
    tr = _tile(s, 512)
    tm = _tile(s, 1024)
    xn = _rms_fwd([x], p["mix_g"], out_dtype=BF16, tr=tr, name=f"rms_mix_fwd{l}")
    h = _matmul(xn, p["w_in"], tm=tm, tn=896, out_dtype=F32, name=f"mm_in{l}")
    c = _gates_fwd(h, p["bf_pad"], tr=_tile(s, 256), name=f"gates_fwd{l}")
    attn, attn_res = _attention_fwd(h[:, 0:512], h[:, 512:1024], h[:, 1024:1536], c[:, :N_HEADS],
                                    tq=_tile(s, 2048), kb=_tile(s, 512), t=_tile(s, 512), name=f"attn_fwd{l}")
    sgu = _sgu_fwd(h, p["ln_g"], p["ln_b"], p["w_s"], p["bias_tile"], tr=_tile(s, 256), name=f"sgu_fwd{l}")
    merged = _rms_fwd([attn, sgu], p["out_g"], out_dtype=BF16, tr=tr, name=f"rms_out_fwd{l}")
    x1 = _matmul(merged, p["w_out"], tm=tm, tn=1024, out_dtype=F32, residual=x, name=f"mm_out{l}")
    xn2 = _rms_fwd([x1], p["ffn_g"], out_dtype=BF16, tr=tr, name=f"rms_ffn_fwd{l}")
    gu = _matmul(xn2, p["w_gu"], tm=tm, tn=1408, out_dtype=F32, name=f"mm_gu{l}")
    act = _swiglu_fwd(gu, tr=tr, name=f"swiglu_fwd{l}")
    x2 = _matmul(act, p["w_down"], tm=tm, tn=1024, out_dtype=F32, residual=x1, name=f"mm_down{l}")
    saved = dict(x=x, xn=xn, h=h, attn_res=attn_res, attn=attn, sgu=sgu, merged=merged, x1=x1, xn2=xn2, gu=gu, act=act)
    return x2, saved


def _layer_bwd(dx2, p, sv, l):
    s = dx2.shape[0]
    tr = _tile(s, 512)
    tm = _tile(s, 1024)
    ts = _tile(s, 512)
    g = {}
    g["w_down"] = _matmul_tn(sv["act"], dx2, tm=1408, tn=1024, ts=ts, name=f"mm_down_dw{l}")
    dact = _matmul(dx2, p["w_down"], trans_b=True, tm=tm, tn=1408, out_dtype=F32, name=f"mm_down_dx{l}")
    dgu = _swiglu_bwd(dact, sv["gu"], tr=_tile(s, 256), name=f"swiglu_bwd{l}")
    g["w_gu"] = _matmul_tn(sv["xn2"], dgu, tm=1024, tn=1408, ts=ts, name=f"mm_gu_dw{l}")
    dxn2 = _matmul(dgu, p["w_gu"], trans_b=True, tm=_tile(s, 512), tn=1024, out_dtype=F32, name=f"mm_gu_dx{l}")
    (dx1,), g["ffn_g"] = _rms_bwd(dxn2, [sv["x1"]], p["ffn_g"], residual=dx2, tr=tr, name=f"rms_ffn_bwd{l}")
    g["w_out"] = _matmul_tn(sv["merged"], dx1, tm=1024, tn=1024, ts=ts, name=f"mm_out_dw{l}")
    dmerged = _matmul(dx1, p["w_out"], trans_b=True, tm=tm, tn=1024, out_dtype=F32, name=f"mm_out_dx{l}")
    (dattn, dsgu), g["out_g"], delta = _rms_bwd(dmerged, [sv["attn"], sv["sgu"]], p["out_g"], head_dots=True, tr=tr,
                                                name=f"rms_out_bwd{l}")
    dzu, dzv, g["w_s"], g["ln_g"], g["ln_b"], dbs = _sgu_bwd(dsgu, sv["h"], p["ln_g"], p["ln_b"], p["w_s"], p["bias_tile"],
                                                           tr=_tile(s, 256), name=f"sgu_bwd{l}")
    g["b_s"] = dbs[:, :N_HEADS].T
    dq, dk, dv, dc8 = _attention_bwd(dattn, delta[:, :N_HEADS], sv["attn_res"], name=f"attn_bwd{l}")
    dc = jnp.pad(dc8, ((0, 0), (0, LANES - N_HEADS)))
    dfl, dbf = _gates_bwd(dc, sv["h"], p["bf_pad"], tr=_tile(s, 256), name=f"gates_bwd{l}")
    g["b_f"] = dbf[0, :N_HEADS]
    dh = jnp.concatenate([dq.astype(BF16), dk.astype(BF16), dv.astype(BF16), dzu, dzv, dfl], axis=1)
    g["w_in"] = _matmul_tn(sv["xn"], dh, tm=1024, tn=896, ts=ts, name=f"mm_in_dw{l}")
    dxn = _matmul(dh, p["w_in"], trans_b=True, tm=tm, tn=1024, out_dtype=F32, name=f"mm_in_dx{l}")
    (dx,), g["mix_g"] = _rms_bwd(dxn, [sv["x"]], p["mix_g"], residual=dx1, tr=tr, name=f"rms_mix_bwd{l}")
    return dx, g


def _layer_params(l, w_in_pad, w_out, w_gu, w_down, mix_norm_g, b_f, sgu_ln_g, sgu_ln_b, w_s, b_s, out_norm_g, ffn_norm_g):
    return dict(
        w_in=w_in_pad[l], w_out=w_out[l], w_gu=w_gu[l], w_down=w_down[l],
        mix_g=mix_norm_g[l][None, :], out_g=out_norm_g[l][None, :], ffn_g=ffn_norm_g[l][None, :],
        bf_pad=jnp.pad(b_f[l], (0, LANES - N_HEADS))[None, :],
        ln_g=sgu_ln_g[l][None, :], ln_b=sgu_ln_b[l][None, :], w_s=w_s[l],
        bias_tile=jnp.repeat(b_s[l].T, 64, axis=1),
    )


def _local_step(x, tgt, w_in_pad, w_out, w_gu, w_down, mix_norm_g, b_f, sgu_ln_g, sgu_ln_b, w_s, b_s, out_norm_g,
                ffn_norm_g, final_norm_g):
    depth = w_in_pad.shape[0]
    s = x.shape[0]
    params = [_layer_params(l, w_in_pad, w_out, w_gu, w_down, mix_norm_g, b_f, sgu_ln_g, sgu_ln_b, w_s, b_s, out_norm_g,
                            ffn_norm_g) for l in range(depth)]
    saved = []
    for l in range(depth):
        x, sv = _layer_fwd(x, params[l], l)
        saved.append(sv)
    loss_tile, dx, dfinal = _loss_head(x, tgt, final_norm_g[None, :], tr=_tile(s, 512), name="loss_head")
    grads = [None] * depth
    for l in reversed(range(depth)):
        dx, grads[l] = _layer_bwd(dx, params[l], saved[l], l)
    return loss_tile[0, 0], dx, grads, dfinal[0]


SMALL_ROWS = 4272


def kernel(x, mix_norm_g, w_in, b_f, sgu_ln_g, sgu_ln_b, w_s, b_s, out_norm_g, w_out, ffn_norm_g, w_gate_up, w_down, final_norm_g, loss_target, m_mix_norm_g, m_w_in, m_b_f, m_sgu_ln_g, m_sgu_ln_b, m_w_s, m_b_s, m_out_norm_g, m_w_out, m_ffn_norm_g, m_w_gate_up, m_w_down, m_final_norm_g, v_mix_norm_g, v_w_in, v_b_f, v_sgu_ln_g, v_sgu_ln_b, v_w_s, v_b_s, v_out_norm_g, v_w_out, v_ffn_norm_g, v_w_gate_up, v_w_down, v_final_norm_g):
    big = [w_in, w_out, w_gate_up, w_down]
    big_shapes = [a.shape for a in big]
    small = [mix_norm_g, b_f, sgu_ln_g, sgu_ln_b, w_s, b_s, out_norm_g, ffn_norm_g, final_norm_g]
    small_shapes = [a.shape for a in small]

    gathered = _allgather_chips(_flatten([a.astype(BF16) for a in big]), name="ag_weights")
    per_chip = [_unflatten(gathered[j], big_shapes) for j in range(4)]
    w_in_full = _pad_w_in(jnp.concatenate([pc[0] for pc in per_chip], axis=2))
    w_out_full = jnp.concatenate([pc[1] for pc in per_chip], axis=1)
    w_gu_full = jnp.concatenate([pc[2] for pc in per_chip], axis=2)
    w_down_full = jnp.concatenate([pc[3] for pc in per_chip], axis=1)

    loss_part, dx, grads, dfinal = _local_step(
        x[0], loss_target[0], w_in_full, w_out_full, w_gu_full, w_down_full, mix_norm_g, b_f, sgu_ln_g, sgu_ln_b, w_s, b_s,
        out_norm_g, ffn_norm_g, final_norm_g)
    stack = lambda key: jnp.stack([g[key] for g in grads])
    g_in = _unpad_w_in(stack("w_in"))
    g_out, g_gu, g_down = stack("w_out"), stack("w_gu"), stack("w_down")

    send = jnp.stack([_flatten([g_in[:, :, 642 * j:642 * (j + 1)], g_out[:, 256 * j:256 * (j + 1), :],
                                g_gu[:, :, 1408 * j:1408 * (j + 1)], g_down[:, 704 * j:704 * (j + 1), :]]) for j in range(4)])
    theirs = _pair_split(send, name="rs_pair_split")
    pair_sum = _pair_sum(send, theirs, lax.axis_index("c").astype(jnp.int32).reshape(1), tr=2408, name="rs_pair_sum")
    from_chips = _scatter_chips(pair_sum, name="rs_scatter")
    half = _sum_slots(from_chips, tr=1120, name="rs_chip_sum")
    g_big_flat = _pair_join(half, name="rs_pair_join")

    g_small_local = [stack("mix_g")[:, 0], stack("b_f"), stack("ln_g")[:, 0], stack("ln_b")[:, 0], stack("w_s"), stack("b_s"),
                     stack("out_g")[:, 0], stack("ffn_g")[:, 0], dfinal]
    g_small_flat = _sum_slots(_allgather_all(_flatten(g_small_local, SMALL_ROWS), name="ar_small_gather"), tr=1424,
                              name="ar_small_sum")
    loss = lax.psum(loss_part, ("x", "y", "c"))

    d_big, m_big, v_big = _adamw(_flatten(big), g_big_flat, _flatten([m_w_in, m_w_out, m_w_gate_up, m_w_down]),
                                 _flatten([v_w_in, v_w_out, v_w_gate_up, v_w_down]), tr=2240, name="adamw_big")
    m_small = [m_mix_norm_g, m_b_f, m_sgu_ln_g, m_sgu_ln_b, m_w_s, m_b_s, m_out_norm_g, m_ffn_norm_g, m_final_norm_g]
    v_small = [v_mix_norm_g, v_b_f, v_sgu_ln_g, v_sgu_ln_b, v_w_s, v_b_s, v_out_norm_g, v_ffn_norm_g, v_final_norm_g]
    d_small, m_small2, v_small2 = _adamw(_flatten(small, SMALL_ROWS), g_small_flat, _flatten(m_small, SMALL_ROWS),
                                         _flatten(v_small, SMALL_ROWS), tr=1424, name="adamw_small")

    def in_order(big_flat, small_flat):
        b_in, b_out, b_gu, b_down = _unflatten(big_flat, big_shapes)
        s_mix, s_bf, s_lng, s_lnb, s_ws, s_bs, s_outg, s_ffn, s_fin = _unflatten(small_flat, small_shapes)
        return [s_mix, b_in, s_bf, s_lng, s_lnb, s_ws, s_bs, s_outg, b_out, s_ffn, b_gu, b_down, s_fin]

    return (loss, dx[None], *in_order(g_big_flat, g_small_flat), *in_order(d_big, d_small), *in_order(m_big, m_small2),
            *in_order(v_big, v_small2))
```

```python
import jax
import jax.numpy as jnp
from jax import lax
from jax.experimental import pallas as pl
from jax.experimental.pallas import tpu as pltpu

F32 = jnp.float32
BF16 = jnp.bfloat16

D_MODEL = 1024
D_HALF = 512
N_HEADS = 8
HEAD_DIM = 64
SGU_CHUNK = 128
CAUSAL_CHUNK = 64
D_FF = 2816
D_IN = 2568
D_IN_PAD = 2688
F_COL_BLOCK = 2560 // 128
EPS = 1e-6
NEG = -1e30
LANES = 128
VMEM_LIMIT = 56 * 1024 * 1024

ADAM_LR = 0.001
ADAM_B1 = 0.9
ADAM_B2 = 0.999
ADAM_EPS = 1e-08
ADAM_WD = 0.01
ADAM_STEP = 10

MESH = pl.DeviceIdType.MESH
ANY = pl.BlockSpec(memory_space=pl.ANY)


def _params(*sem):
    return pltpu.CompilerParams(dimension_semantics=sem, vmem_limit_bytes=VMEM_LIMIT)


def _matmul(a, b, *, trans_b=False, tm, tn, out_dtype, residual=None, name):
    m, k = a.shape
    n = b.shape[0] if trans_b else b.shape[1]
    dims = (((1,), (1,)), ((), ())) if trans_b else (((1,), (0,)), ((), ()))

    def body(*refs):
        a_ref, b_ref = refs[0], refs[1]
        o_ref = refs[-1]
        acc = lax.dot_general(a_ref[...].astype(BF16), b_ref[...].astype(BF16), dims,
                              preferred_element_type=F32)
        if residual is not None:
            acc = acc + refs[2][...]
        o_ref[...] = acc.astype(out_dtype)

    b_spec = pl.BlockSpec((tn, k), lambda i, j: (j, 0)) if trans_b else pl.BlockSpec((k, tn), lambda i, j: (0, j))
    in_specs = [pl.BlockSpec((tm, k), lambda i, j: (i, 0)), b_spec]
    args = [a, b]
    if residual is not None:
        in_specs.append(pl.BlockSpec((tm, tn), lambda i, j: (i, j)))
        args.append(residual)
    return pl.pallas_call(
        body, name=name, grid=(m // tm, n // tn), in_specs=in_specs,
        out_specs=pl.BlockSpec((tm, tn), lambda i, j: (i, j)),
        out_shape=jax.ShapeDtypeStruct((m, n), out_dtype),
        compiler_params=_params("parallel", "parallel"),
    )(*args)


def _matmul_tn(a, b, *, tm, tn, ts, name):
    s, m = a.shape
    n = b.shape[1]

    def body(a_ref, b_ref, o_ref):
        @pl.when(pl.program_id(2) == 0)
        def _():
            o_ref[...] = jnp.zeros_like(o_ref)

        o_ref[...] += lax.dot_general(a_ref[...].astype(BF16), b_ref[...].astype(BF16),
                                      (((0,), (0,)), ((), ())), preferred_element_type=F32)

    return pl.pallas_call(
        body, name=name, grid=(m // tm, n // tn, s // ts),
        in_specs=[pl.BlockSpec((ts, tm), lambda i, j, t: (t, i)), pl.BlockSpec((ts, tn), lambda i, j, t: (t, j))],
        out_specs=pl.BlockSpec((tm, tn), lambda i, j, t: (i, j)),
        out_shape=jax.ShapeDtypeStruct((m, n), F32),
        compiler_params=_params("parallel", "parallel", "arbitrary"),
    )(a, b)


def _rms_fwd(xs, g, *, out_dtype, tr, name):
    s = xs[0].shape[0]
    widths = [x.shape[1] for x in xs]
    wsum = sum(widths)
    nx = len(xs)

    def body(*refs):
        g_ref, o_ref = refs[nx], refs[nx + 1]
        off = 0
        for x_ref, w in zip(refs[:nx], widths):
            x = x_ref[...]
            r = lax.rsqrt(jnp.mean(x * x, axis=-1, keepdims=True) + EPS)
            o_ref[:, off:off + w] = (x * r * g_ref[:, off:off + w]).astype(out_dtype)
            off += w

    return pl.pallas_call(
        body, name=name, grid=(s // tr,),
        in_specs=[pl.BlockSpec((tr, w), lambda i: (i, 0)) for w in widths] + [pl.BlockSpec((1, wsum), lambda i: (0, 0))],
        out_specs=pl.BlockSpec((tr, wsum), lambda i: (i, 0)),
        out_shape=jax.ShapeDtypeStruct((s, wsum), out_dtype),
        compiler_params=_params("parallel"),
    )(*xs, g)


def _rms_bwd(dy, xs, g, *, residual=None, head_dots=False, tr, name):
    s = xs[0].shape[0]
    widths = [x.shape[1] for x in xs]
    wsum = sum(widths)
    nx = len(xs)
    nin = 2 + nx + (residual is not None)

    def body(*refs):
        dy_ref, g_ref = refs[0], refs[1 + nx]
        dx_refs, dg_ref = refs[nin:nin + nx], refs[nin + nx]

        @pl.when(pl.program_id(0) == 0)
        def _():
            dg_ref[...] = jnp.zeros_like(dg_ref)

        off = 0
        for idx, (x_ref, w) in enumerate(zip(refs[1:1 + nx], widths)):
            x = x_ref[...]
            r = lax.rsqrt(jnp.mean(x * x, axis=-1, keepdims=True) + EPS)
            xh = x * r
            dyv = dy_ref[:, off:off + w]
            dxh = dyv * g_ref[:, off:off + w]
            dx = r * (dxh - xh * jnp.mean(dxh * xh, axis=-1, keepdims=True))
            if residual is not None and idx == 0:
                dx = dx + refs[2 + nx][...]
            dx_refs[idx][...] = dx
            dg_ref[:, off:off + w] += jnp.sum(dyv * xh, axis=0, keepdims=True)
            if head_dots and idx == 0:
                col = lax.broadcasted_iota(jnp.int32, (w, LANES), 0) // HEAD_DIM
                head = lax.broadcasted_iota(jnp.int32, (w, LANES), 1)
                refs[nin + nx + 1][...] = jnp.dot(dx * x, (col == head).astype(F32), precision=lax.Precision.HIGHEST,
                                                  preferred_element_type=F32)
            off += w

    in_specs = ([pl.BlockSpec((tr, wsum), lambda i: (i, 0))]
                + [pl.BlockSpec((tr, w), lambda i: (i, 0)) for w in widths]
                + [pl.BlockSpec((1, wsum), lambda i: (0, 0))])
    args = [dy, *xs, g]
    if residual is not None:
        in_specs.append(pl.BlockSpec((tr, widths[0]), lambda i: (i, 0)))
        args.append(residual)
    out_specs = [pl.BlockSpec((tr, w), lambda i: (i, 0)) for w in widths] + [pl.BlockSpec((1, wsum), lambda i: (0, 0))]
    out_shape = [jax.ShapeDtypeStruct((s, w), F32) for w in widths] + [jax.ShapeDtypeStruct((1, wsum), F32)]
    if head_dots:
        out_specs.append(pl.BlockSpec((tr, LANES), lambda i: (i, 0)))
        out_shape.append(jax.ShapeDtypeStruct((s, LANES), F32))
    outs = pl.pallas_call(
        body, name=name, grid=(s // tr,), in_specs=in_specs, out_specs=out_specs, out_shape=out_shape,
        compiler_params=_params("arbitrary"),
    )(*args)
    if head_dots:
        return outs[:nx], outs[nx], outs[nx + 1]
    return outs[:nx], outs[nx]


def _loss_head(x, tgt, g, *, tr, name):
    s, d = x.shape

    def body(x_ref, t_ref, g_ref, loss_ref, dx_ref, dg_ref):
        @pl.when(pl.program_id(0) == 0)
        def _():
            loss_ref[...] = jnp.zeros_like(loss_ref)
            dg_ref[...] = jnp.zeros_like(dg_ref)

        xv = x_ref[...]
        r = lax.rsqrt(jnp.mean(xv * xv, axis=-1, keepdims=True) + EPS)
        xh = xv * r
        err = xh * g_ref[...] - t_ref[...]
        loss_ref[...] += 0.5 * jnp.sum(jnp.mean(err * err, axis=-1, keepdims=True))
        dy = err * (1.0 / d)
        dxh = dy * g_ref[...]
        dx_ref[...] = r * (dxh - xh * jnp.mean(dxh * xh, axis=-1, keepdims=True))
        dg_ref[...] += jnp.sum(dy * xh, axis=0, keepdims=True)

    return pl.pallas_call(
        body, name=name, grid=(s // tr,),
        in_specs=[pl.BlockSpec((tr, d), lambda i: (i, 0)), pl.BlockSpec((tr, d), lambda i: (i, 0)),
                  pl.BlockSpec((1, d), lambda i: (0, 0))],
        out_specs=[pl.BlockSpec((8, LANES), lambda i: (0, 0)), pl.BlockSpec((tr, d), lambda i: (i, 0)),
                   pl.BlockSpec((1, d), lambda i: (0, 0))],
        out_shape=[jax.ShapeDtypeStruct((8, LANES), F32), jax.ShapeDtypeStruct((s, d), F32),
                   jax.ShapeDtypeStruct((1, d), F32)],
        compiler_params=_params("arbitrary"),
    )(x, tgt, g)


def _gates_fwd(h, bf_pad, *, tr, name):
    s = h.shape[0]

    def body(fl_ref, b_ref, c_ref, carry_ref):
        @pl.when(pl.program_id(0) == 0)
        def _():
            carry_ref[...] = jnp.zeros_like(carry_ref)

        lf = jax.nn.log_sigmoid(fl_ref[...] + b_ref[...])
        row = lax.broadcasted_iota(jnp.int32, (tr, tr), 0)
        col = lax.broadcasted_iota(jnp.int32, (tr, tr), 1)
        tri = (col <= row).astype(F32)
        c_ref[...] = jnp.dot(tri, lf, precision=lax.Precision.HIGHEST, preferred_element_type=F32) + carry_ref[...]
        carry_ref[...] += jnp.sum(lf, axis=0, keepdims=True)

    return pl.pallas_call(
        body, name=name, grid=(s // tr,),
        in_specs=[pl.BlockSpec((tr, LANES), lambda i: (i, F_COL_BLOCK)), pl.BlockSpec((1, LANES), lambda i: (0, 0))],
        out_specs=pl.BlockSpec((tr, LANES), lambda i: (i, 0)),
        out_shape=jax.ShapeDtypeStruct((s, LANES), F32),
        scratch_shapes=[pltpu.VMEM((1, LANES), F32)],
        compiler_params=_params("arbitrary"),
    )(h, bf_pad)


def _gates_bwd(dc, h, bf_pad, *, tr, name):
    s = h.shape[0]
    n = s // tr

    def body(dc_ref, fl_ref, b_ref, dfl_ref, db_ref, carry_ref):
        @pl.when(pl.program_id(0) == 0)
        def _():
            carry_ref[...] = jnp.zeros_like(carry_ref)
            db_ref[...] = jnp.zeros_like(db_ref)

        dcv = dc_ref[...]
        row = lax.broadcasted_iota(jnp.int32, (tr, tr), 0)
        col = lax.broadcasted_iota(jnp.int32, (tr, tr), 1)
        triu = (col >= row).astype(F32)
        dlf = jnp.dot(triu, dcv, precision=lax.Precision.HIGHEST, preferred_element_type=F32) + carry_ref[...]
        carry_ref[...] += jnp.sum(dcv, axis=0, keepdims=True)
        dfl = dlf * jax.nn.sigmoid(-(fl_ref[...] + b_ref[...]))
        dfl_ref[...] = dfl.astype(dfl_ref.dtype)
        db_ref[...] += jnp.sum(dfl, axis=0, keepdims=True)

    return pl.pallas_call(
        body, name=name, grid=(n,),
        in_specs=[pl.BlockSpec((tr, LANES), lambda i: (n - 1 - i, 0)),
                  pl.BlockSpec((tr, LANES), lambda i: (n - 1 - i, F_COL_BLOCK)),
                  pl.BlockSpec((1, LANES), lambda i: (0, 0))],
        out_specs=[pl.BlockSpec((tr, LANES), lambda i: (n - 1 - i, 0)), pl.BlockSpec((1, LANES), lambda i: (0, 0))],
        out_shape=[jax.ShapeDtypeStruct((s, LANES), BF16), jax.ShapeDtypeStruct((1, LANES), F32)],
        scratch_shapes=[pltpu.VMEM((1, LANES), F32)],
        compiler_params=_params("arbitrary"),
    )(dc, h, bf_pad)


LOG2E = 1.4426950408889634
LN2 = 0.6931471805599453
V_ROWS = 80


def _transpose_bf16(a):
    return a.astype(F32).T.astype(BF16)


SKIP_MARGIN = 160.0
SMEM = pl.BlockSpec(memory_space=pltpu.SMEM)


def _attn_fwd(qat, ka, va, gnorm, cmax_q, cmin_k, *, tq, kb, name):
    s = ka.shape[0]
    nq = s // tq
    per_tile = tq // kb
    tt = qat.shape[3]
    sub = tq // tt

    def body(gn_ref, cq_ref, ck_ref, qat_ref, ka_ref, va_ref, o_ref, lse_ref):
        h, i = pl.program_id(0), pl.program_id(1)
        qat = jnp.concatenate([qat_ref[0, d] for d in range(sub)], axis=1)

        def blk(n, carry, masked):
            m, acc = carry
            rows = pl.ds(pl.multiple_of(n * kb, kb), kb)
            sc = jnp.dot(ka_ref[rows, :], qat, preferred_element_type=F32)
            if masked:
                key = n * kb + lax.broadcasted_iota(jnp.int32, (kb, tq), 0)
                qry = i * tq + lax.broadcasted_iota(jnp.int32, (kb, tq), 1)
                sc = jnp.where(key <= qry, sc, NEG)
            m_new = jnp.maximum(m, jnp.max(sc, axis=0, keepdims=True))
            p = jnp.exp2(sc - m_new)
            vt = _transpose_bf16(va_ref[rows, :])[:V_ROWS]
            acc = jnp.exp2(m - m_new) * acc + jnp.dot(vt, p.astype(BF16), preferred_element_type=F32)
            return m_new, acc

        carry = (jnp.full((1, tq), NEG, F32), jnp.zeros((V_ROWS, tq), F32))
        for d in reversed(range(per_tile)):
            carry = blk(i * per_tile + d, carry, True)
        top = gn_ref[h] + cq_ref[h, i]

        def live(state):
            n, m_min = state[0], state[1]
            return jnp.logical_and(n >= 0, top - ck_ref[h, jnp.maximum(n, 0)] >= m_min - SKIP_MARGIN)

        def step(state):
            n, _, m, acc = state
            m, acc = blk(n, (m, acc), False)
            return n - 1, jnp.min(m), m, acc

        _, _, m, acc = lax.while_loop(live, step, (i * per_tile - 1, jnp.min(carry[0]), *carry))
        l = acc[HEAD_DIM:HEAD_DIM + 1, :]
        padded = jnp.concatenate([acc / l, jnp.zeros((LANES - V_ROWS, tq), F32)], axis=0)
        o_ref[0, 0] = padded.T[:, :HEAD_DIM]
        lse_ref[0, 0] = m + jnp.log2(l)

    return pl.pallas_call(
        body, name=name, grid=(N_HEADS, nq),
        in_specs=[SMEM, SMEM, SMEM, pl.BlockSpec((1, sub, LANES, tt), lambda h, i: (h, i, 0, 0)),
                  pl.BlockSpec((s, LANES), lambda h, i: (0, h)),
                  pl.BlockSpec((s, LANES), lambda h, i: (0, h))],
        out_specs=[pl.BlockSpec((1, 1, tq, HEAD_DIM), lambda h, i: (h, i, 0, 0)),
                   pl.BlockSpec((1, 1, 1, tq), lambda h, i: (h, i, 0, 0))],
        out_shape=[jax.ShapeDtypeStruct((N_HEADS, nq, tq, HEAD_DIM), F32), jax.ShapeDtypeStruct((N_HEADS, nq, 1, tq), F32)],
        compiler_params=_params("parallel", "arbitrary"),
    )(gnorm, cmax_q, cmin_k, qat, ka, va)


def _attn_bwd(qa, qat, doa, dot, lse, dl, ka, va, gnorm, cmin_k, reach, *, name):
    s = qa.shape[0]
    nt, t = qat.shape[1], qat.shape[3]

    def body(gn_ref, ck_ref, reach_ref, qa_ref, qat_ref, do_ref, dot_ref, lse_ref, dl_ref, ka_ref, va_ref,
             dq_ref, dk_ref, dv_ref):
        h, j = pl.program_id(0), pl.program_id(1)

        @pl.when(j == 0)
        def _():
            dq_ref[...] = jnp.zeros_like(dq_ref)

        ka_j, va_j = ka_ref[...], va_ref[...]

        def tile(i, carry, masked):
            dk, dv = carry
            rows = pl.ds(pl.multiple_of(i * t, t), t)
            qa_i, do_i = qa_ref[rows, :], do_ref[rows, :]
            st = jnp.dot(ka_j, qat_ref[0, i], preferred_element_type=F32) - lse_ref[0, i]
            if masked:
                key = lax.broadcasted_iota(jnp.int32, (t, t), 0)
                qry = lax.broadcasted_iota(jnp.int32, (t, t), 1)
                st = jnp.where(key <= qry, st, NEG)
            pt = jnp.exp2(st)
            dpt = jnp.dot(va_j, dot_ref[0, i], preferred_element_type=F32)
            dsb = (pt * (dpt - dl_ref[0, i])).astype(BF16)
            dv = dv + jnp.dot(pt.astype(BF16), do_i, preferred_element_type=F32)
            dk = dk + jnp.dot(dsb, qa_i, preferred_element_type=F32)
            dq_ref[rows, :] += lax.dot_general(dsb, ka_j, (((0,), (0,)), ((), ())), preferred_element_type=F32)
            return dk, dv

        carry = tile(j, (jnp.zeros((t, LANES), F32), jnp.zeros((t, LANES), F32)), True)
        base = gn_ref[h] - ck_ref[h, j]

        def live(state):
            i = state[0]
            return jnp.logical_and(i < nt, base + reach_ref[h, jnp.minimum(i, nt - 1)] >= -SKIP_MARGIN)

        def step(state):
            i, dk, dv = state
            dk, dv = tile(i, (dk, dv), False)
            return i + 1, dk, dv

        _, dk, dv = lax.while_loop(live, step, (j + 1, *carry))
        dk_ref[...] = dk
        dv_ref[...] = dv

    res = pl.BlockSpec((s, LANES), lambda h, j: (0, h))
    rest = pl.BlockSpec((1, nt, LANES, t), lambda h, j: (h, 0, 0, 0))
    row = pl.BlockSpec((1, nt, 1, t), lambda h, j: (h, 0, 0, 0))
    blk = pl.BlockSpec((t, LANES), lambda h, j: (j, h))
    shape = jax.ShapeDtypeStruct((s, N_HEADS * LANES), F32)
    return pl.pallas_call(
        body, name=name, grid=(N_HEADS, nt),
        in_specs=[SMEM, SMEM, SMEM, res, rest, res, rest, row, row, blk, blk], out_specs=[res, blk, blk],
        out_shape=[shape, shape, shape], compiler_params=_params("parallel", "arbitrary"),
    )(gnorm, cmin_k, reach, qa, qat, doa, dot, lse, dl, ka, va)


def _spread_matrix():
    r = lax.broadcasted_iota(jnp.int32, (D_HALF, N_HEADS * LANES), 0)
    c = lax.broadcasted_iota(jnp.int32, (D_HALF, N_HEADS * LANES), 1)
    return jnp.logical_and(c // LANES == r // HEAD_DIM, c % LANES == r % HEAD_DIM).astype(BF16)


def _piece_matrix(base):
    r = lax.broadcasted_iota(jnp.int32, (3 * LANES, N_HEADS * LANES), 0)
    c = lax.broadcasted_iota(jnp.int32, (3 * LANES, N_HEADS * LANES), 1)
    return jnp.logical_and(r % LANES < N_HEADS, c == LANES * (r % LANES) + base + r // LANES).astype(BF16)


def _ones_columns(first, count):
    c = lax.broadcasted_iota(jnp.int32, (1, N_HEADS * LANES), 1) % LANES
    return jnp.logical_and(c >= first, c < first + count).astype(F32)


def _round_bf16(x):
    return x.astype(BF16).astype(F32)


def _write_transposed(a, out_ref):
    for h in range(N_HEADS):
        out_ref[h, 0] = _transpose_bf16(a[:, h * LANES:(h + 1) * LANES])


def _pack_qkv(h, c, *, tr, name):
    s = h.shape[0]
    wide = N_HEADS * LANES

    def body(q_ref, k_ref, v_ref, c_ref, qa_ref, ka_ref, va_ref, qat_ref):
        spread = _spread_matrix()
        c2 = c_ref[...] * LOG2E
        p1 = _round_bf16(c2)
        p2 = _round_bf16(c2 - p1)
        p3 = _round_bf16(c2 - p1 - p2)
        pieces = jnp.concatenate([p1, p2, p3], axis=1).astype(BF16)
        dot = lambda a, b: jnp.dot(a, b, preferred_element_type=F32)
        qa = (dot((q_ref[...] * 0.125).astype(BF16), spread) + dot(pieces, _piece_matrix(HEAD_DIM + 3))
              + _ones_columns(HEAD_DIM, 3)).astype(BF16)
        ka = (dot((k_ref[...] * LOG2E).astype(BF16), spread) - dot(pieces, _piece_matrix(HEAD_DIM))
              + _ones_columns(HEAD_DIM + 3, 3)).astype(BF16)
        qa_ref[...] = qa
        ka_ref[...] = ka
        va_ref[...] = (dot(v_ref[...].astype(BF16), spread) + _ones_columns(HEAD_DIM, 1)).astype(BF16)
        _write_transposed(qa, qat_ref)

    shape = jax.ShapeDtypeStruct((s, wide), BF16)
    return pl.pallas_call(
        body, name=name, grid=(s // tr,),
        in_specs=[pl.BlockSpec((tr, D_HALF), lambda i: (i, 0)), pl.BlockSpec((tr, D_HALF), lambda i: (i, 1)),
                  pl.BlockSpec((tr, D_HALF), lambda i: (i, 2)), pl.BlockSpec((tr, LANES), lambda i: (i, 0))],
        out_specs=[pl.BlockSpec((tr, wide), lambda i: (i, 0))] * 3 + [pl.BlockSpec((N_HEADS, 1, LANES, tr), lambda i: (0, i, 0, 0))],
        out_shape=[shape, shape, shape, jax.ShapeDtypeStruct((N_HEADS, s // tr, LANES, tr), BF16)],
        compiler_params=_params("parallel"),
    )(h, h, h, c)


def _pack_do(dattn, *, tr, name):
    s = dattn.shape[0]
    wide = N_HEADS * LANES

    def body(d_ref, doa_ref, dot_ref):
        doa = jnp.dot(d_ref[...].astype(BF16), _spread_matrix(), preferred_element_type=F32).astype(BF16)
        doa_ref[...] = doa
        _write_transposed(doa, dot_ref)

    return pl.pallas_call(
        body, name=name, grid=(s // tr,),
        in_specs=[pl.BlockSpec((tr, D_HALF), lambda i: (i, 0))],
        out_specs=[pl.BlockSpec((tr, wide), lambda i: (i, 0)), pl.BlockSpec((N_HEADS, 1, LANES, tr), lambda i: (0, i, 0, 0))],
        out_shape=[jax.ShapeDtypeStruct((s, wide), BF16), jax.ShapeDtypeStruct((N_HEADS, s // tr, LANES, tr), BF16)],
        compiler_params=_params("parallel"),
    )(dattn)


def _unpack_grads(dqa, dka, dva, *, tr, name):
    s = dqa.shape[0]
    wide = N_HEADS * LANES

    def body(dq_ref, dk_ref, dv_ref, o_ref, dc_ref):
        gather = _spread_matrix()
        nt_dims = (((1,), (1,)), ((), ()))
        pick = lambda a: lax.dot_general(a.astype(BF16), gather, nt_dims, preferred_element_type=F32).astype(BF16)
        dq, dk = dq_ref[...], dk_ref[...]
        o_ref[:, 0:D_HALF] = pick(dq * (LN2 * 0.125))
        o_ref[:, D_HALF:2 * D_HALF] = pick(dk)
        o_ref[:, 2 * D_HALF:3 * D_HALF] = pick(dv_ref[...])
        r = lax.broadcasted_iota(jnp.int32, (wide, LANES), 0)
        head = lax.broadcasted_iota(jnp.int32, (wide, LANES), 1)
        sel_q = (r == LANES * head + HEAD_DIM + 3).astype(F32)
        sel_k = (r == LANES * head + HEAD_DIM).astype(F32)
        hi = lax.Precision.HIGHEST
        dc_ref[...] = (jnp.dot(dq, sel_q, precision=hi, preferred_element_type=F32)
                       - jnp.dot(dk, sel_k, precision=hi, preferred_element_type=F32))

    spec = pl.BlockSpec((tr, wide), lambda i: (i, 0))
    return pl.pallas_call(
        body, name=name, grid=(s // tr,), in_specs=[spec, spec, spec],
        out_specs=[pl.BlockSpec((tr, 3 * D_HALF), lambda i: (i, 0)), pl.BlockSpec((tr, LANES), lambda i: (i, 0))],
        out_shape=[jax.ShapeDtypeStruct((s, 3 * D_HALF), BF16), jax.ShapeDtypeStruct((s, LANES), F32)],
        compiler_params=_params("parallel"),
    )(dqa, dka, dva)


def _attention_fwd(h, c, *, tq, kb, t, name):
    s = h.shape[0]
    qa, ka, va, qat = _pack_qkv(h, c, tr=t, name=name + "_pack")
    c2 = c[:, :N_HEADS] * LOG2E
    head_norm = lambda a: jnp.sqrt(jnp.max(jnp.sum(jnp.square(a.reshape(s, N_HEADS, HEAD_DIM)), axis=-1), axis=0))
    gnorm = head_norm(h[:, 0:D_HALF] * 0.125) * head_norm(h[:, D_HALF:2 * D_HALF] * LOG2E) * 1.01 + 1.0
    cmax_q = jnp.max(c2.reshape(s // tq, tq, N_HEADS), axis=1).T
    cmin_k = lax.cummin(jnp.min(c2.reshape(s // kb, kb, N_HEADS), axis=1), axis=0).T
    o4, lse2 = _attn_fwd(qat, ka, va, gnorm, cmax_q, cmin_k, tq=tq, kb=kb, name=name)
    attn = o4.reshape(N_HEADS, s, HEAD_DIM).transpose(1, 0, 2).reshape(s, N_HEADS * HEAD_DIM)
    return attn, dict(qa=qa, qat=qat, ka=ka, va=va, lse2=lse2, c2=c2, gnorm=gnorm)


def _attention_bwd(dattn, delta, res, *, name):
    s = dattn.shape[0]
    t = res["qat"].shape[3]
    doa, dot = _pack_do(dattn, tr=t, name=name + "_pack")
    c2t = res["c2"].reshape(s // t, t, N_HEADS)
    lse_t = res["lse2"].reshape(N_HEADS, s // t, t)
    reach = lax.cummax(jnp.max(c2t.transpose(2, 0, 1) - lse_t, axis=2), axis=1, reverse=True)
    dqa, dka, dva = _attn_bwd(res["qa"], res["qat"], doa, dot, lse_t.reshape(N_HEADS, s // t, 1, t),
                              delta.T.reshape(N_HEADS, s // t, 1, t), res["ka"], res["va"], res["gnorm"],
                              jnp.min(c2t, axis=1).T, reach, name=name)
    return _unpack_grads(dqa, dka, dva, tr=t, name=name + "_unpack")


def _gelu(z):
    return 0.5 * z * (1.0 + lax.erf(z * 0.7071067811865476))


def _gelu_grad(z):
    return 0.5 * (1.0 + lax.erf(z * 0.7071067811865476)) + z * (0.3989422804014327 * jnp.exp(-0.5 * z * z))


def _sgu_mask():
    i = lax.broadcasted_iota(jnp.int32, (SGU_CHUNK, SGU_CHUNK), 0) // CAUSAL_CHUNK
    j = lax.broadcasted_iota(jnp.int32, (SGU_CHUNK, SGU_CHUNK), 1) // CAUSAL_CHUNK
    return (j <= i).astype(F32)


def _layernorm_stats(x):
    mu = jnp.mean(x, axis=-1, keepdims=True)
    xc = x - mu
    rstd = lax.rsqrt(jnp.mean(xc * xc, axis=-1, keepdims=True) + EPS)
    return xc * rstd, rstd


def _first_group_lanes():
    return lax.broadcasted_iota(jnp.int32, (SGU_CHUNK, LANES), 1) < 64


def _sgu_fwd(h, ln_g, ln_b, w_s, bias_tile, *, tr, name):
    s = h.shape[0]
    zu_blk, zv_blk = 1536 // D_HALF, 2048 // D_HALF

    def body(zu_ref, zv_ref, lng_ref, lnb_ref, ws_ref, bias_ref, o_ref):
        gzu = _gelu(zu_ref[...])
        xh, _ = _layernorm_stats(_gelu(zv_ref[...]))
        zb = (xh * lng_ref[...] + lnb_ref[...]).astype(BF16)
        mask = _sgu_mask()
        first = _first_group_lanes()
        for pair in range(4):
            cols = slice(pair * LANES, (pair + 1) * LANES)
            w0 = (ws_ref[2 * pair] * mask).astype(BF16)
            w1 = (ws_ref[2 * pair + 1] * mask).astype(BF16)
            for ch in range(tr // SGU_CHUNK):
                rows = slice(ch * SGU_CHUNK, (ch + 1) * SGU_CHUNK)
                zp = zb[rows, cols]
                mixed = jnp.where(first, jnp.dot(w0, zp, preferred_element_type=F32),
                                  jnp.dot(w1, zp, preferred_element_type=F32)) + bias_ref[:, cols]
                o_ref[rows, cols] = gzu[rows, cols] * mixed

    return pl.pallas_call(
        body, name=name, grid=(s // tr,),
        in_specs=[pl.BlockSpec((tr, D_HALF), lambda i: (i, zu_blk)), pl.BlockSpec((tr, D_HALF), lambda i: (i, zv_blk)),
                  pl.BlockSpec((1, D_HALF), lambda i: (0, 0)), pl.BlockSpec((1, D_HALF), lambda i: (0, 0)),
                  pl.BlockSpec((N_HEADS, SGU_CHUNK, SGU_CHUNK), lambda i: (0, 0, 0)),
                  pl.BlockSpec((SGU_CHUNK, D_HALF), lambda i: (0, 0))],
        out_specs=pl.BlockSpec((tr, D_HALF), lambda i: (i, 0)),
        out_shape=jax.ShapeDtypeStruct((s, D_HALF), F32),
        compiler_params=_params("parallel"),
    )(h, h, ln_g, ln_b, w_s, bias_tile)


def _sgu_bwd(dsgu, h, ln_g, ln_b, w_s, bias_tile, *, tr, name):
    s = h.shape[0]
    n = s // tr
    zu_blk, zv_blk = 1536 // D_HALF, 2048 // D_HALF

    def body(ds_ref, zu_ref, zv_ref, lng_ref, lnb_ref, ws_ref, bias_ref,
             dzu_ref, dzv_ref, dws_ref, dlng_ref, dlnb_ref, dbs_ref, dgzu_sc, dzvn_sc, dbias_sc):
        step = pl.program_id(0)

        @pl.when(step == 0)
        def _():
            dws_ref[...] = jnp.zeros_like(dws_ref)
            dlng_ref[...] = jnp.zeros_like(dlng_ref)
            dlnb_ref[...] = jnp.zeros_like(dlnb_ref)
            dbias_sc[...] = jnp.zeros_like(dbias_sc)

        zu = zu_ref[...]
        zv = zv_ref[...]
        gzu = _gelu(zu)
        xh, rstd = _layernorm_stats(_gelu(zv))
        zb = (xh * lng_ref[...] + lnb_ref[...]).astype(BF16)
        ds = ds_ref[...]
        mask = _sgu_mask()
        first = _first_group_lanes()
        tn_dims = (((0,), (0,)), ((), ()))
        nt_dims = (((1,), (1,)), ((), ()))
        for pair in range(4):
            cols = slice(pair * LANES, (pair + 1) * LANES)
            w0 = (ws_ref[2 * pair] * mask).astype(BF16)
            w1 = (ws_ref[2 * pair + 1] * mask).astype(BF16)
            for ch in range(tr // SGU_CHUNK):
                rows = slice(ch * SGU_CHUNK, (ch + 1) * SGU_CHUNK)
                zp = zb[rows, cols]
                mixed = jnp.where(first, jnp.dot(w0, zp, preferred_element_type=F32),
                                  jnp.dot(w1, zp, preferred_element_type=F32)) + bias_ref[:, cols]
                dsp = ds[rows, cols]
                dgzu_sc[rows, cols] = dsp * mixed
                dm = dsp * gzu[rows, cols]
                dbias_sc[:, cols] += dm
                dmb = dm.astype(BF16)
                dm0 = jnp.where(first, dmb, jnp.zeros_like(dmb))
                dm1 = jnp.where(first, jnp.zeros_like(dmb), dmb)
                dws_ref[2 * pair] += lax.dot_general(dm0, zp, nt_dims, preferred_element_type=F32)
                dws_ref[2 * pair + 1] += lax.dot_general(dm1, zp, nt_dims, preferred_element_type=F32)
                dzvn_sc[rows, cols] = jnp.where(first, lax.dot_general(w0, dmb, tn_dims, preferred_element_type=F32),
                                                lax.dot_general(w1, dmb, tn_dims, preferred_element_type=F32))
        dzvn = dzvn_sc[...]
        dlng_ref[...] += jnp.sum(dzvn * xh, axis=0, keepdims=True)
        dlnb_ref[...] += jnp.sum(dzvn, axis=0, keepdims=True)
        dxh = dzvn * lng_ref[...]
        dgzv = rstd * (dxh - jnp.mean(dxh, axis=-1, keepdims=True) - xh * jnp.mean(dxh * xh, axis=-1, keepdims=True))
        dzv_ref[...] = (dgzv * _gelu_grad(zv)).astype(dzv_ref.dtype)
        dzu_ref[...] = (dgzu_sc[...] * _gelu_grad(zu)).astype(dzu_ref.dtype)

        @pl.when(step == n - 1)
        def _():
            for g in range(N_HEADS):
                dws_ref[g] = dws_ref[g] * mask
            lane = lax.broadcasted_iota(jnp.int32, (D_HALF, LANES), 0) // 64
            grp = lax.broadcasted_iota(jnp.int32, (D_HALF, LANES), 1)
            dbs_ref[...] = jnp.dot(dbias_sc[...], (lane == grp).astype(F32), precision=lax.Precision.HIGHEST,
                                   preferred_element_type=F32)

    const2 = lambda i: (0, 0)
    return pl.pallas_call(
        body, name=name, grid=(n,),
        in_specs=[pl.BlockSpec((tr, D_HALF), lambda i: (i, 0)),
                  pl.BlockSpec((tr, D_HALF), lambda i: (i, zu_blk)), pl.BlockSpec((tr, D_HALF), lambda i: (i, zv_blk)),
                  pl.BlockSpec((1, D_HALF), const2), pl.BlockSpec((1, D_HALF), const2),
                  pl.BlockSpec((N_HEADS, SGU_CHUNK, SGU_CHUNK), lambda i: (0, 0, 0)),
                  pl.BlockSpec((SGU_CHUNK, D_HALF), const2)],
        out_specs=[pl.BlockSpec((tr, D_HALF), lambda i: (i, 0)), pl.BlockSpec((tr, D_HALF), lambda i: (i, 0)),
                   pl.BlockSpec((N_HEADS, SGU_CHUNK, SGU_CHUNK), lambda i: (0, 0, 0)),
                   pl.BlockSpec((1, D_HALF), const2), pl.BlockSpec((1, D_HALF), const2),
                   pl.BlockSpec((SGU_CHUNK, LANES), const2)],
        out_shape=[jax.ShapeDtypeStruct((s, D_HALF), BF16), jax.ShapeDtypeStruct((s, D_HALF), BF16),
                   jax.ShapeDtypeStruct((N_HEADS, SGU_CHUNK, SGU_CHUNK), F32),
                   jax.ShapeDtypeStruct((1, D_HALF), F32), jax.ShapeDtypeStruct((1, D_HALF), F32),
                   jax.ShapeDtypeStruct((SGU_CHUNK, LANES), F32)],
        scratch_shapes=[pltpu.VMEM((tr, D_HALF), F32), pltpu.VMEM((tr, D_HALF), F32), pltpu.VMEM((SGU_CHUNK, D_HALF), F32)],
        compiler_params=_params("arbitrary"),
    )(dsgu, h, h, ln_g, ln_b, w_s, bias_tile)


FF_BLOCK = D_FF // 2


def _matmul_swiglu(a, w, *, tm, name):
    s, k = a.shape

    def body(a_ref, wg_ref, wu_ref, g_ref, u_ref, act_ref):
        av = a_ref[...].astype(BF16)
        g = jnp.dot(av, wg_ref[...].astype(BF16), preferred_element_type=F32)
        u = jnp.dot(av, wu_ref[...].astype(BF16), preferred_element_type=F32)
        g_ref[...] = g
        u_ref[...] = u
        act_ref[...] = (g * jax.nn.sigmoid(g) * u).astype(act_ref.dtype)

    out = pl.BlockSpec((tm, FF_BLOCK), lambda i, j: (i, j))
    return pl.pallas_call(
        body, name=name, grid=(s // tm, 2),
        in_specs=[pl.BlockSpec((tm, k), lambda i, j: (i, 0)), pl.BlockSpec((k, FF_BLOCK), lambda i, j: (0, j)),
                  pl.BlockSpec((k, FF_BLOCK), lambda i, j: (0, j + 2))],
        out_specs=[out, out, out],
        out_shape=[jax.ShapeDtypeStruct((s, D_FF), F32), jax.ShapeDtypeStruct((s, D_FF), F32),
                   jax.ShapeDtypeStruct((s, D_FF), BF16)],
        compiler_params=_params("parallel", "parallel"),
    )(a, w, w)


def _swiglu_bwd(dact, gate, up, *, tr, name):
    s = gate.shape[0]

    def body(d_ref, g_ref, u_ref, o_ref):
        g = g_ref[...]
        d = d_ref[...]
        sig = jax.nn.sigmoid(g)
        o_ref[:, :D_FF] = (d * u_ref[...] * (sig * (1.0 + g * (1.0 - sig)))).astype(o_ref.dtype)
        o_ref[:, D_FF:] = (d * (g * sig)).astype(o_ref.dtype)

    spec = pl.BlockSpec((tr, D_FF), lambda i: (i, 0))
    return pl.pallas_call(
        body, name=name, grid=(s // tr,), in_specs=[spec, spec, spec],
        out_specs=pl.BlockSpec((tr, 2 * D_FF), lambda i: (i, 0)),
        out_shape=jax.ShapeDtypeStruct((s, 2 * D_FF), BF16),
        compiler_params=_params("parallel"),
    )(dact, gate, up)


def _sum_slots(stacked, *, tr, name):
    k, r, _ = stacked.shape

    def body(x_ref, o_ref):
        acc = x_ref[0].astype(F32)
        for idx in range(1, k):
            acc = acc + x_ref[idx].astype(F32)
        o_ref[...] = acc

    return pl.pallas_call(
        body, name=name, grid=(r // tr,),
        in_specs=[pl.BlockSpec((k, tr, LANES), lambda i: (0, i, 0))],
        out_specs=pl.BlockSpec((tr, LANES), lambda i: (i, 0)),
        out_shape=jax.ShapeDtypeStruct((r, LANES), F32),
        compiler_params=_params("parallel"),
    )(stacked)


def _adamw(w, g, m, v, *, tr, name):
    r = w.shape[0]

    def body(w_ref, g_ref, m_ref, v_ref, d_ref, m2_ref, v2_ref):
        gv = g_ref[...]
        m2 = ADAM_B1 * m_ref[...] + (1.0 - ADAM_B1) * gv
        v2 = ADAM_B2 * v_ref[...] + (1.0 - ADAM_B2) * jnp.square(gv)
        m_hat = m2 / (1.0 - ADAM_B1 ** ADAM_STEP)
        v_hat = v2 / (1.0 - ADAM_B2 ** ADAM_STEP)
        d_ref[...] = -ADAM_LR * (m_hat / (jnp.sqrt(v_hat) + ADAM_EPS) + ADAM_WD * w_ref[...])
        m2_ref[...] = m2
        v2_ref[...] = v2

    spec = pl.BlockSpec((tr, LANES), lambda i: (i, 0))
    shape = jax.ShapeDtypeStruct((r, LANES), F32)
    return pl.pallas_call(
        body, name=name, grid=(r // tr,), in_specs=[spec] * 4, out_specs=[spec] * 3, out_shape=[shape] * 3,
        compiler_params=_params("parallel"),
    )(w, g, m, v)


PAIR_CHUNKS = 5


def _coords():
    return lax.axis_index("x"), lax.axis_index("y"), lax.axis_index("c")


def _my_chip():
    return 2 * lax.axis_index("x") + lax.axis_index("y")


def _chip_peer(x, y, k):
    px = 1 - x if k & 2 else x
    py = 1 - y if k & 1 else y
    return px, py


def _allgather_chips(shard, *, name):
    r = shard.shape[0]

    def body(src, out, send_sems, recv_sems):
        x, y, c = _coords()
        me = 2 * x + y
        copies = []
        for k in (1, 2, 3):
            px, py = _chip_peer(x, y, k)
            cp = pltpu.make_async_remote_copy(src_ref=src, dst_ref=out.at[me], send_sem=send_sems.at[k - 1],
                                              recv_sem=recv_sems.at[k - 1], device_id=(px, py, c), device_id_type=MESH)
            cp.start()
            copies.append(cp)
        for cp in copies:
            cp.wait()

    gathered = pl.pallas_call(
        body, name=name, in_specs=[ANY], out_specs=ANY,
        out_shape=jax.ShapeDtypeStruct((4, r, LANES), shard.dtype),
        scratch_shapes=[pltpu.SemaphoreType.DMA((3,)), pltpu.SemaphoreType.DMA((3,))],
    )(shard)
    return lax.dynamic_update_slice(gathered, shard[None], (_my_chip(), 0, 0))


def _pair_split(grads, *, name):
    _, r, _ = grads.shape
    rh = r // 2
    rc = rh // PAIR_CHUNKS
    nchunk = 4 * PAIR_CHUNKS

    def body(g_ref, theirs_ref, send_sems, recv_sems):
        x, y, c = _coords()
        copies = []
        for j in range(4):
            for q in range(PAIR_CHUNKS):
                idx = j * PAIR_CHUNKS + q
                cp = pltpu.make_async_remote_copy(
                    src_ref=g_ref.at[j, pl.ds((1 - c) * rh + q * rc, rc), :], dst_ref=theirs_ref.at[j, pl.ds(q * rc, rc), :],
                    send_sem=send_sems.at[idx], recv_sem=recv_sems.at[idx], device_id=(x, y, 1 - c), device_id_type=MESH)
                cp.start()
                copies.append(cp)
        for cp in copies:
            cp.wait()

    return pl.pallas_call(
        body, name=name, in_specs=[ANY], out_specs=ANY, out_shape=jax.ShapeDtypeStruct((4, rh, LANES), F32),
        scratch_shapes=[pltpu.SemaphoreType.DMA((nchunk,)), pltpu.SemaphoreType.DMA((nchunk,))],
    )(grads)


def _pair_sum(grads, theirs, half, *, tr, name):
    _, rh, _ = theirs.shape
    nrt = rh // tr

    def body(half_ref, g_ref, t_ref, o_ref):
        o_ref[...] = (g_ref[...] + t_ref[...]).astype(o_ref.dtype)

    return pl.pallas_call(
        body, name=name,
        grid_spec=pltpu.PrefetchScalarGridSpec(
            num_scalar_prefetch=1, grid=(4, nrt),
            in_specs=[pl.BlockSpec((1, tr, LANES), lambda j, i, half_ref: (j, half_ref[0] * nrt + i, 0)),
                      pl.BlockSpec((1, tr, LANES), lambda j, i, half_ref: (j, i, 0))],
            out_specs=pl.BlockSpec((1, tr, LANES), lambda j, i, half_ref: (j, i, 0))),
        out_shape=jax.ShapeDtypeStruct((4, rh, LANES), BF16),
        compiler_params=_params("parallel", "parallel"),
    )(half, grads, theirs)


def _scatter_chips(part, *, name):
    _, rh, _ = part.shape

    def body(p_ref, out, send_sems, recv_sems):
        x, y, c = _coords()
        me = 2 * x + y
        copies = []
        for k in (1, 2, 3):
            px, py = _chip_peer(x, y, k)
            cp = pltpu.make_async_remote_copy(src_ref=p_ref.at[2 * px + py], dst_ref=out.at[me], send_sem=send_sems.at[k - 1],
                                              recv_sem=recv_sems.at[k - 1], device_id=(px, py, c), device_id_type=MESH)
            cp.start()
            copies.append(cp)
        for cp in copies:
            cp.wait()

    from_chips = pl.pallas_call(
        body, name=name, in_specs=[ANY], out_specs=ANY, out_shape=jax.ShapeDtypeStruct((4, rh, LANES), part.dtype),
        scratch_shapes=[pltpu.SemaphoreType.DMA((3,)), pltpu.SemaphoreType.DMA((3,))],
    )(part)
    own = lax.dynamic_index_in_dim(part, _my_chip(), axis=0, keepdims=True)
    return lax.dynamic_update_slice(from_chips, own, (_my_chip(), 0, 0))


def _pair_join(half, *, name):
    rh = half.shape[0]
    nchunk = 2 * PAIR_CHUNKS
    rc = rh // nchunk

    def body(h_ref, out, send_sems, recv_sems):
        x, y, c = _coords()
        copies = []
        for q in range(nchunk):
            src = h_ref.at[pl.ds(q * rc, rc), :]
            rows = out.at[pl.ds(c * rh + q * rc, rc), :]
            cp = pltpu.make_async_remote_copy(src_ref=src, dst_ref=rows, send_sem=send_sems.at[q], recv_sem=recv_sems.at[q],
                                              device_id=(x, y, 1 - c), device_id_type=MESH)
            cp.start()
            copies.append(cp)
        for cp in copies:
            cp.wait()

    joined = pl.pallas_call(
        body, name=name, in_specs=[ANY], out_specs=ANY, out_shape=jax.ShapeDtypeStruct((2 * rh, LANES), F32),
        scratch_shapes=[pltpu.SemaphoreType.DMA((nchunk,)), pltpu.SemaphoreType.DMA((nchunk,))],
    )(half)
    return lax.dynamic_update_slice(joined, half, (lax.axis_index("c") * rh, 0))


def _allgather_all(block, *, name):
    r = block.shape[0]

    def body(src, out, send_sems, recv_sems):
        x, y, c = _coords()
        me = 4 * x + 2 * y + c
        copies = []
        for k in range(1, 8):
            px, py = _chip_peer(x, y, k >> 1)
            pc = 1 - c if k & 1 else c
            cp = pltpu.make_async_remote_copy(src_ref=src, dst_ref=out.at[me], send_sem=send_sems.at[k - 1],
                                              recv_sem=recv_sems.at[k - 1], device_id=(px, py, pc), device_id_type=MESH)
            cp.start()
            copies.append(cp)
        for cp in copies:
            cp.wait()

    gathered = pl.pallas_call(
        body, name=name, in_specs=[ANY], out_specs=ANY, out_shape=jax.ShapeDtypeStruct((8, r, LANES), F32),
        scratch_shapes=[pltpu.SemaphoreType.DMA((7,)), pltpu.SemaphoreType.DMA((7,))],
    )(block)
    return lax.dynamic_update_slice(gathered, block[None], (2 * _my_chip() + lax.axis_index("c"), 0, 0))


def _flatten(arrays, pad_rows=None):
    flat = jnp.concatenate([a.reshape(-1) for a in arrays])
    if pad_rows is not None:
        flat = jnp.pad(flat, (0, pad_rows * LANES - flat.shape[0]))
    return flat.reshape(-1, LANES)


def _unflatten(flat, shapes):
    flat = flat.reshape(-1)
    out, off = [], 0
    for shp in shapes:
        size = 1
        for dim in shp:
            size *= dim
        out.append(flat[off:off + size].reshape(shp))
        off += size
    return out


def _pad_w_in(w):
    pad = jnp.zeros(w.shape[:-1] + (D_IN_PAD - D_IN,), w.dtype)
    return jnp.concatenate([w[..., :1536], w[..., 1544:], w[..., 1536:1544], pad], axis=-1)


def _unpad_w_in(w):
    return jnp.concatenate([w[..., :1536], w[..., 2560:2568], w[..., 1536:2560]], axis=-1)


def _tile(s, want):
    return min(want, s)


def _layer_fwd(x, p, l):
    s = x.shape[0]
    tr = _tile(s, 512)
    tm = _tile(s, 1024)
    xn = _rms_fwd([x], p["mix_g"], out_dtype=BF16, tr=tr, name=f"rms_mix_fwd{l}")
    h = _matmul(xn, p["w_in"], tm=tm, tn=896, out_dtype=F32, name=f"mm_in{l}")
    c = _gates_fwd(h, p["bf_pad"], tr=_tile(s, 256), name=f"gates_fwd{l}")
    attn, attn_res = _attention_fwd(h, c, tq=_tile(s, 2048), kb=_tile(s, 512), t=_tile(s, 512), name=f"attn_fwd{l}")
    sgu = _sgu_fwd(h, p["ln_g"], p["ln_b"], p["w_s"], p["bias_tile"], tr=_tile(s, 256), name=f"sgu_fwd{l}")
    merged = _rms_fwd([attn, sgu], p["out_g"], out_dtype=BF16, tr=tr, name=f"rms_out_fwd{l}")
    x1 = _matmul(merged, p["w_out"], tm=tm, tn=1024, out_dtype=F32, residual=x, name=f"mm_out{l}")
    xn2 = _rms_fwd([x1], p["ffn_g"], out_dtype=BF16, tr=tr, name=f"rms_ffn_fwd{l}")
    gate, up, act = _matmul_swiglu(xn2, p["w_gu"], tm=_tile(s, 512), name=f"mm_gu{l}")
    x2 = _matmul(act, p["w_down"], tm=tm, tn=1024, out_dtype=F32, residual=x1, name=f"mm_down{l}")
    saved = dict(x=x, xn=xn, h=h, attn_res=attn_res, attn=attn, sgu=sgu, merged=merged, x1=x1, xn2=xn2, gate=gate, up=up, act=act)
    return x2, saved


def _layer_bwd(dx2, p, sv, l):
    s = dx2.shape[0]
    tr = _tile(s, 512)
    tm = _tile(s, 1024)
    ts = _tile(s, 512)
    g = {}
    g["w_down"] = _matmul_tn(sv["act"], dx2, tm=1408, tn=1024, ts=ts, name=f"mm_down_dw{l}")
    dact = _matmul(dx2, p["w_down"], trans_b=True, tm=tm, tn=1408, out_dtype=F32, name=f"mm_down_dx{l}")
    dgu = _swiglu_bwd(dact, sv["gate"], sv["up"], tr=_tile(s, 256), name=f"swiglu_bwd{l}")
    g["w_gu"] = _matmul_tn(sv["xn2"], dgu, tm=1024, tn=1408, ts=ts, name=f"mm_gu_dw{l}")
    dxn2 = _matmul(dgu, p["w_gu"], trans_b=True, tm=_tile(s, 512), tn=1024, out_dtype=F32, name=f"mm_gu_dx{l}")
    (dx1,), g["ffn_g"] = _rms_bwd(dxn2, [sv["x1"]], p["ffn_g"], residual=dx2, tr=tr, name=f"rms_ffn_bwd{l}")
    g["w_out"] = _matmul_tn(sv["merged"], dx1, tm=1024, tn=1024, ts=ts, name=f"mm_out_dw{l}")
    dmerged = _matmul(dx1, p["w_out"], trans_b=True, tm=tm, tn=1024, out_dtype=F32, name=f"mm_out_dx{l}")
    (dattn, dsgu), g["out_g"], delta = _rms_bwd(dmerged, [sv["attn"], sv["sgu"]], p["out_g"], head_dots=True, tr=tr,
                                                name=f"rms_out_bwd{l}")
    dzu, dzv, g["w_s"], g["ln_g"], g["ln_b"], dbs = _sgu_bwd(dsgu, sv["h"], p["ln_g"], p["ln_b"], p["w_s"], p["bias_tile"],
                                                           tr=_tile(s, 256), name=f"sgu_bwd{l}")
    g["b_s"] = dbs[:, :N_HEADS].T
    dqkv, dc = _attention_bwd(dattn, delta[:, :N_HEADS], sv["attn_res"], name=f"attn_bwd{l}")
    dfl, dbf = _gates_bwd(dc, sv["h"], p["bf_pad"], tr=_tile(s, 256), name=f"gates_bwd{l}")
    g["b_f"] = dbf[0, :N_HEADS]
    dh = jnp.concatenate([dqkv, dzu, dzv, dfl], axis=1)
    g["w_in"] = _matmul_tn(sv["xn"], dh, tm=1024, tn=896, ts=ts, name=f"mm_in_dw{l}")
    dxn = _matmul(dh, p["w_in"], trans_b=True, tm=tm, tn=1024, out_dtype=F32, name=f"mm_in_dx{l}")
    (dx,), g["mix_g"] = _rms_bwd(dxn, [sv["x"]], p["mix_g"], residual=dx1, tr=tr, name=f"rms_mix_bwd{l}")
    return dx, g


def _layer_params(l, w_in_pad, w_out, w_gu, w_down, mix_norm_g, b_f, sgu_ln_g, sgu_ln_b, w_s, b_s, out_norm_g, ffn_norm_g):
    return dict(
        w_in=w_in_pad[l], w_out=w_out[l], w_gu=w_gu[l], w_down=w_down[l],
        mix_g=mix_norm_g[l][None, :], out_g=out_norm_g[l][None, :], ffn_g=ffn_norm_g[l][None, :],
        bf_pad=jnp.pad(b_f[l], (0, LANES - N_HEADS))[None, :],
        ln_g=sgu_ln_g[l][None, :], ln_b=sgu_ln_b[l][None, :], w_s=w_s[l],
        bias_tile=jnp.repeat(b_s[l].T, 64, axis=1),
    )


def _local_step(x, tgt, w_in_pad, w_out, w_gu, w_down, mix_norm_g, b_f, sgu_ln_g, sgu_ln_b, w_s, b_s, out_norm_g,
                ffn_norm_g, final_norm_g):
    depth = w_in_pad.shape[0]
    s = x.shape[0]
    params = [_layer_params(l, w_in_pad, w_out, w_gu, w_down, mix_norm_g, b_f, sgu_ln_g, sgu_ln_b, w_s, b_s, out_norm_g,
                            ffn_norm_g) for l in range(depth)]
    saved = []
    for l in range(depth):
        x, sv = _layer_fwd(x, params[l], l)
        saved.append(sv)
    loss_tile, dx, dfinal = _loss_head(x, tgt, final_norm_g[None, :], tr=_tile(s, 512), name="loss_head")
    grads = [None] * depth
    for l in reversed(range(depth)):
        dx, grads[l] = _layer_bwd(dx, params[l], saved[l], l)
    return loss_tile[0, 0], dx, grads, dfinal[0]


SMALL_ROWS = 4272


def kernel(x, mix_norm_g, w_in, b_f, sgu_ln_g, sgu_ln_b, w_s, b_s, out_norm_g, w_out, ffn_norm_g, w_gate_up, w_down, final_norm_g, loss_target, m_mix_norm_g, m_w_in, m_b_f, m_sgu_ln_g, m_sgu_ln_b, m_w_s, m_b_s, m_out_norm_g, m_w_out, m_ffn_norm_g, m_w_gate_up, m_w_down, m_final_norm_g, v_mix_norm_g, v_w_in, v_b_f, v_sgu_ln_g, v_sgu_ln_b, v_w_s, v_b_s, v_out_norm_g, v_w_out, v_ffn_norm_g, v_w_gate_up, v_w_down, v_final_norm_g):
    big = [w_in, w_out, w_gate_up, w_down]
    big_shapes = [a.shape for a in big]
    small = [mix_norm_g, b_f, sgu_ln_g, sgu_ln_b, w_s, b_s, out_norm_g, ffn_norm_g, final_norm_g]
    small_shapes = [a.shape for a in small]

    gathered = _allgather_chips(_flatten([a.astype(BF16) for a in big]), name="ag_weights")
    per_chip = [_unflatten(gathered[j], big_shapes) for j in range(4)]
    w_in_full = _pad_w_in(jnp.concatenate([pc[0] for pc in per_chip], axis=2))
    w_out_full = jnp.concatenate([pc[1] for pc in per_chip], axis=1)
    w_gu_full = jnp.concatenate([pc[2] for pc in per_chip], axis=2)
    w_down_full = jnp.concatenate([pc[3] for pc in per_chip], axis=1)

    loss_part, dx, grads, dfinal = _local_step(
        x[0], loss_target[0], w_in_full, w_out_full, w_gu_full, w_down_full, mix_norm_g, b_f, sgu_ln_g, sgu_ln_b, w_s, b_s,
        out_norm_g, ffn_norm_g, final_norm_g)
    stack = lambda key: jnp.stack([g[key] for g in grads])
    g_in = _unpad_w_in(stack("w_in"))
    g_out, g_gu, g_down = stack("w_out"), stack("w_gu"), stack("w_down")

    send = jnp.stack([_flatten([g_in[:, :, 642 * j:642 * (j + 1)], g_out[:, 256 * j:256 * (j + 1), :],
                                g_gu[:, :, 1408 * j:1408 * (j + 1)], g_down[:, 704 * j:704 * (j + 1), :]]) for j in range(4)])
    theirs = _pair_split(send, name="rs_pair_split")
    pair_sum = _pair_sum(send, theirs, lax.axis_index("c").astype(jnp.int32).reshape(1), tr=3440, name="rs_pair_sum")
    from_chips = _scatter_chips(pair_sum, name="rs_scatter")
    half = _sum_slots(from_chips, tr=1120, name="rs_chip_sum")
    g_big_flat = _pair_join(half, name="rs_pair_join")

    g_small_local = [stack("mix_g")[:, 0], stack("b_f"), stack("ln_g")[:, 0], stack("ln_b")[:, 0], stack("w_s"), stack("b_s"),
                     stack("out_g")[:, 0], stack("ffn_g")[:, 0], dfinal]
    g_small_flat = _sum_slots(_allgather_all(_flatten(g_small_local, SMALL_ROWS), name="ar_small_gather"), tr=1424,
                              name="ar_small_sum")
    loss = lax.psum(loss_part, ("x", "y", "c"))

    d_big, m_big, v_big = _adamw(_flatten(big), g_big_flat, _flatten([m_w_in, m_w_out, m_w_gate_up, m_w_down]),
                                 _flatten([v_w_in, v_w_out, v_w_gate_up, v_w_down]), tr=2240, name="adamw_big")
    m_small = [m_mix_norm_g, m_b_f, m_sgu_ln_g, m_sgu_ln_b, m_w_s, m_b_s, m_out_norm_g, m_ffn_norm_g, m_final_norm_g]
    v_small = [v_mix_norm_g, v_b_f, v_sgu_ln_g, v_sgu_ln_b, v_w_s, v_b_s, v_out_norm_g, v_ffn_norm_g, v_final_norm_g]
    d_small, m_small2, v_small2 = _adamw(_flatten(small, SMALL_ROWS), g_small_flat, _flatten(m_small, SMALL_ROWS),
                                         _flatten(v_small, SMALL_ROWS), tr=1424, name="adamw_small")

    def in_order(big_flat, small_flat):
        b_in, b_out, b_gu, b_down = _unflatten(big_flat, big_shapes)
        s_mix, s_bf, s_lng, s_lnb, s_ws, s_bs, s_outg, s_ffn, s_fin = _unflatten(small_flat, small_shapes)
        return [s_mix, b_in, s_bf, s_lng, s_lnb, s_ws, s_bs, s_outg, b_out, s_ffn, b_gu, b_down, s_fin]

    return (loss, dx[None], *in_order(g_big_flat, g_small_flat), *in_order(d_big, d_small), *in_order(m_big, m_small2),
            *in_order(v_big, v_small2))
```

```python
import jax
import jax.numpy as jnp
from jax import lax
from jax.experimental import pallas as pl
from jax.experimental.pallas import tpu as pltpu

F32 = jnp.float32
BF16 = jnp.bfloat16

D_MODEL = 1024
D_HALF = 512
N_HEADS = 8
HEAD_DIM = 64
SGU_CHUNK = 128
CAUSAL_CHUNK = 64
D_FF = 2816
D_IN = 2568
D_IN_PAD = 2688
F_COL_BLOCK = 2560 // 128
EPS = 1e-6
NEG = -1e30
LANES = 128
VMEM_LIMIT = 56 * 1024 * 1024

ADAM_LR = 0.001
ADAM_B1 = 0.9
ADAM_B2 = 0.999
ADAM_EPS = 1e-08
ADAM_WD = 0.01
ADAM_STEP = 10

MESH = pl.DeviceIdType.MESH
ANY = pl.BlockSpec(memory_space=pl.ANY)


def _params(*sem):
    return pltpu.CompilerParams(dimension_semantics=sem, vmem_limit_bytes=VMEM_LIMIT)


def _matmul(a, b, *, trans_b=False, tm, tn, out_dtype, residual=None, name):
    m, k = a.shape
    n = b.shape[0] if trans_b else b.shape[1]
    dims = (((1,), (1,)), ((), ())) if trans_b else (((1,), (0,)), ((), ()))

    def body(*refs):
        a_ref, b_ref = refs[0], refs[1]
        o_ref = refs[-1]
        acc = lax.dot_general(a_ref[...].astype(BF16), b_ref[...].astype(BF16), dims,
                              preferred_element_type=F32)
        if residual is not None:
            acc = acc + refs[2][...]
        o_ref[...] = acc.astype(out_dtype)

    b_spec = pl.BlockSpec((tn, k), lambda i, j: (j, 0)) if trans_b else pl.BlockSpec((k, tn), lambda i, j: (0, j))
    in_specs = [pl.BlockSpec((tm, k), lambda i, j: (i, 0)), b_spec]
    args = [a, b]
    if residual is not None:
        in_specs.append(pl.BlockSpec((tm, tn), lambda i, j: (i, j)))
        args.append(residual)
    return pl.pallas_call(
        body, name=name, grid=(m // tm, n // tn), in_specs=in_specs,
        out_specs=pl.BlockSpec((tm, tn), lambda i, j: (i, j)),
        out_shape=jax.ShapeDtypeStruct((m, n), out_dtype),
        compiler_params=_params("parallel", "parallel"),
    )(*args)


def _matmul_tn(a, b, *, tm, tn, ts, name):
    s, m = a.shape
    n = b.shape[1]

    def body(a_ref, b_ref, o_ref):
        @pl.when(pl.program_id(2) == 0)
        def _():
            o_ref[...] = jnp.zeros_like(o_ref)

        o_ref[...] += lax.dot_general(a_ref[...].astype(BF16), b_ref[...].astype(BF16),
                                      (((0,), (0,)), ((), ())), preferred_element_type=F32)

    return pl.pallas_call(
        body, name=name, grid=(m // tm, n // tn, s // ts),
        in_specs=[pl.BlockSpec((ts, tm), lambda i, j, t: (t, i)), pl.BlockSpec((ts, tn), lambda i, j, t: (t, j))],
        out_specs=pl.BlockSpec((tm, tn), lambda i, j, t: (i, j)),
        out_shape=jax.ShapeDtypeStruct((m, n), F32),
        compiler_params=_params("parallel", "parallel", "arbitrary"),
    )(a, b)


def _rms_fwd(xs, g, *, out_dtype, tr, name):
    s = xs[0].shape[0]
    widths = [x.shape[1] for x in xs]
    wsum = sum(widths)
    nx = len(xs)

    def body(*refs):
        g_ref, o_ref = refs[nx], refs[nx + 1]
        off = 0
        for x_ref, w in zip(refs[:nx], widths):
            x = x_ref[...]
            r = lax.rsqrt(jnp.mean(x * x, axis=-1, keepdims=True) + EPS)
            o_ref[:, off:off + w] = (x * r * g_ref[:, off:off + w]).astype(out_dtype)
            off += w

    return pl.pallas_call(
        body, name=name, grid=(s // tr,),
        in_specs=[pl.BlockSpec((tr, w), lambda i: (i, 0)) for w in widths] + [pl.BlockSpec((1, wsum), lambda i: (0, 0))],
        out_specs=pl.BlockSpec((tr, wsum), lambda i: (i, 0)),
        out_shape=jax.ShapeDtypeStruct((s, wsum), out_dtype),
        compiler_params=_params("parallel"),
    )(*xs, g)


def _rms_bwd(dy, xs, g, *, residual=None, head_dots=False, tr, name):
    s = xs[0].shape[0]
    widths = [x.shape[1] for x in xs]
    wsum = sum(widths)
    nx = len(xs)
    nin = 2 + nx + (residual is not None)

    def body(*refs):
        dy_ref, g_ref = refs[0], refs[1 + nx]
        dx_refs, dg_ref = refs[nin:nin + nx], refs[nin + nx]

        @pl.when(pl.program_id(0) == 0)
        def _():
            dg_ref[...] = jnp.zeros_like(dg_ref)

        off = 0
        for idx, (x_ref, w) in enumerate(zip(refs[1:1 + nx], widths)):
            x = x_ref[...]
            r = lax.rsqrt(jnp.mean(x * x, axis=-1, keepdims=True) + EPS)
            xh = x * r
            dyv = dy_ref[:, off:off + w]
            dxh = dyv * g_ref[:, off:off + w]
            dx = r * (dxh - xh * jnp.mean(dxh * xh, axis=-1, keepdims=True))
            if residual is not None and idx == 0:
                dx = dx + refs[2 + nx][...]
            dx_refs[idx][...] = dx
            dg_ref[:, off:off + w] += jnp.sum(dyv * xh, axis=0, keepdims=True)
            if head_dots and idx == 0:
                col = lax.broadcasted_iota(jnp.int32, (w, LANES), 0) // HEAD_DIM
                head = lax.broadcasted_iota(jnp.int32, (w, LANES), 1)
                refs[nin + nx + 1][...] = jnp.dot(dx * x, (col == head).astype(F32), precision=lax.Precision.HIGHEST,
                                                  preferred_element_type=F32)
            off += w

    in_specs = ([pl.BlockSpec((tr, wsum), lambda i: (i, 0))]
                + [pl.BlockSpec((tr, w), lambda i: (i, 0)) for w in widths]
                + [pl.BlockSpec((1, wsum), lambda i: (0, 0))])
    args = [dy, *xs, g]
    if residual is not None:
        in_specs.append(pl.BlockSpec((tr, widths[0]), lambda i: (i, 0)))
        args.append(residual)
    out_specs = [pl.BlockSpec((tr, w), lambda i: (i, 0)) for w in widths] + [pl.BlockSpec((1, wsum), lambda i: (0, 0))]
    out_shape = [jax.ShapeDtypeStruct((s, w), F32) for w in widths] + [jax.ShapeDtypeStruct((1, wsum), F32)]
    if head_dots:
        out_specs.append(pl.BlockSpec((tr, LANES), lambda i: (i, 0)))
        out_shape.append(jax.ShapeDtypeStruct((s, LANES), F32))
    outs = pl.pallas_call(
        body, name=name, grid=(s // tr,), in_specs=in_specs, out_specs=out_specs, out_shape=out_shape,
        compiler_params=_params("arbitrary"),
    )(*args)
    if head_dots:
        return outs[:nx], outs[nx], outs[nx + 1]
    return outs[:nx], outs[nx]


def _loss_head(x, tgt, g, *, tr, name):
    s, d = x.shape

    def body(x_ref, t_ref, g_ref, loss_ref, dx_ref, dg_ref):
        @pl.when(pl.program_id(0) == 0)
        def _():
            loss_ref[...] = jnp.zeros_like(loss_ref)
            dg_ref[...] = jnp.zeros_like(dg_ref)

        xv = x_ref[...]
        r = lax.rsqrt(jnp.mean(xv * xv, axis=-1, keepdims=True) + EPS)
        xh = xv * r
        err = xh * g_ref[...] - t_ref[...]
        loss_ref[...] += 0.5 * jnp.sum(jnp.mean(err * err, axis=-1, keepdims=True))
        dy = err * (1.0 / d)
        dxh = dy * g_ref[...]
        dx_ref[...] = r * (dxh - xh * jnp.mean(dxh * xh, axis=-1, keepdims=True))
        dg_ref[...] += jnp.sum(dy * xh, axis=0, keepdims=True)

    return pl.pallas_call(
        body, name=name, grid=(s // tr,),
        in_specs=[pl.BlockSpec((tr, d), lambda i: (i, 0)), pl.BlockSpec((tr, d), lambda i: (i, 0)),
                  pl.BlockSpec((1, d), lambda i: (0, 0))],
        out_specs=[pl.BlockSpec((8, LANES), lambda i: (0, 0)), pl.BlockSpec((tr, d), lambda i: (i, 0)),
                   pl.BlockSpec((1, d), lambda i: (0, 0))],
        out_shape=[jax.ShapeDtypeStruct((8, LANES), F32), jax.ShapeDtypeStruct((s, d), F32),
                   jax.ShapeDtypeStruct((1, d), F32)],
        compiler_params=_params("arbitrary"),
    )(x, tgt, g)


def _gates_fwd(h, bf_pad, *, tr, name):
    s = h.shape[0]

    def body(fl_ref, b_ref, c_ref, carry_ref):
        @pl.when(pl.program_id(0) == 0)
        def _():
            carry_ref[...] = jnp.zeros_like(carry_ref)

        lf = jax.nn.log_sigmoid(fl_ref[...] + b_ref[...])
        row = lax.broadcasted_iota(jnp.int32, (tr, tr), 0)
        col = lax.broadcasted_iota(jnp.int32, (tr, tr), 1)
        tri = (col <= row).astype(F32)
        c_ref[...] = jnp.dot(tri, lf, precision=lax.Precision.HIGHEST, preferred_element_type=F32) + carry_ref[...]
        carry_ref[...] += jnp.sum(lf, axis=0, keepdims=True)

    return pl.pallas_call(
        body, name=name, grid=(s // tr,),
        in_specs=[pl.BlockSpec((tr, LANES), lambda i: (i, F_COL_BLOCK)), pl.BlockSpec((1, LANES), lambda i: (0, 0))],
        out_specs=pl.BlockSpec((tr, LANES), lambda i: (i, 0)),
        out_shape=jax.ShapeDtypeStruct((s, LANES), F32),
        scratch_shapes=[pltpu.VMEM((1, LANES), F32)],
        compiler_params=_params("arbitrary"),
    )(h, bf_pad)


def _gates_bwd(dc, h, bf_pad, *, tr, name):
    s = h.shape[0]
    n = s // tr

    def body(dc_ref, fl_ref, b_ref, dfl_ref, db_ref, carry_ref):
        @pl.when(pl.program_id(0) == 0)
        def _():
            carry_ref[...] = jnp.zeros_like(carry_ref)
            db_ref[...] = jnp.zeros_like(db_ref)

        dcv = dc_ref[...]
        row = lax.broadcasted_iota(jnp.int32, (tr, tr), 0)
        col = lax.broadcasted_iota(jnp.int32, (tr, tr), 1)
        triu = (col >= row).astype(F32)
        dlf = jnp.dot(triu, dcv, precision=lax.Precision.HIGHEST, preferred_element_type=F32) + carry_ref[...]
        carry_ref[...] += jnp.sum(dcv, axis=0, keepdims=True)
        dfl = dlf * jax.nn.sigmoid(-(fl_ref[...] + b_ref[...]))
        dfl_ref[...] = dfl.astype(dfl_ref.dtype)
        db_ref[...] += jnp.sum(dfl, axis=0, keepdims=True)

    return pl.pallas_call(
        body, name=name, grid=(n,),
        in_specs=[pl.BlockSpec((tr, LANES), lambda i: (n - 1 - i, 0)),
                  pl.BlockSpec((tr, LANES), lambda i: (n - 1 - i, F_COL_BLOCK)),
                  pl.BlockSpec((1, LANES), lambda i: (0, 0))],
        out_specs=[pl.BlockSpec((tr, LANES), lambda i: (n - 1 - i, 0)), pl.BlockSpec((1, LANES), lambda i: (0, 0))],
        out_shape=[jax.ShapeDtypeStruct((s, LANES), BF16), jax.ShapeDtypeStruct((1, LANES), F32)],
        scratch_shapes=[pltpu.VMEM((1, LANES), F32)],
        compiler_params=_params("arbitrary"),
    )(dc, h, bf_pad)


LOG2E = 1.4426950408889634
LN2 = 0.6931471805599453
V_ROWS = 80


def _transpose_bf16(a):
    return a.astype(F32).T.astype(BF16)


SKIP_MARGIN = 160.0
SMEM = pl.BlockSpec(memory_space=pltpu.SMEM)


def _attn_fwd(qat, ka, va, gnorm, cmax_q, cmin_k, *, tq, kb, name):
    s = ka.shape[0]
    nq = s // tq
    per_tile = tq // kb
    tt = qat.shape[3]
    sub = tq // tt

    def body(gn_ref, cq_ref, ck_ref, qat_ref, ka_ref, va_ref, o_ref, lse_ref):
        h, i = pl.program_id(0), pl.program_id(1)
        qat = jnp.concatenate([qat_ref[0, d] for d in range(sub)], axis=1)

        def blk(n, carry, masked):
            m, acc = carry
            rows = pl.ds(pl.multiple_of(n * kb, kb), kb)
            sc = jnp.dot(ka_ref[rows, :], qat, preferred_element_type=F32)
            if masked:
                key = n * kb + lax.broadcasted_iota(jnp.int32, (kb, tq), 0)
                qry = i * tq + lax.broadcasted_iota(jnp.int32, (kb, tq), 1)
                sc = jnp.where(key <= qry, sc, NEG)
            m_new = jnp.maximum(m, jnp.max(sc, axis=0, keepdims=True))
            p = jnp.exp2(sc - m_new)
            vt = _transpose_bf16(va_ref[rows, :])[:V_ROWS]
            acc = jnp.exp2(m - m_new) * acc + jnp.dot(vt, p.astype(BF16), preferred_element_type=F32)
            return m_new, acc

        carry = (jnp.full((1, tq), NEG, F32), jnp.zeros((V_ROWS, tq), F32))
        for d in reversed(range(per_tile)):
            carry = blk(i * per_tile + d, carry, True)
        top = gn_ref[h] + cq_ref[h, i]

        def live(state):
            n, m_min = state[0], state[1]
            return jnp.logical_and(n >= 0, top - ck_ref[h, jnp.maximum(n, 0)] >= m_min - SKIP_MARGIN)

        def step(state):
            n, _, m, acc = state
            m, acc = blk(n, (m, acc), False)
            return n - 1, jnp.min(m), m, acc

        _, _, m, acc = lax.while_loop(live, step, (i * per_tile - 1, jnp.min(carry[0]), *carry))
        l = acc[HEAD_DIM:HEAD_DIM + 1, :]
        padded = jnp.concatenate([acc / l, jnp.zeros((LANES - V_ROWS, tq), F32)], axis=0)
        o_ref[0, 0] = padded.T[:, :HEAD_DIM]
        lse_ref[0, 0] = m + jnp.log2(l)

    return pl.pallas_call(
        body, name=name, grid=(N_HEADS, nq),
        in_specs=[SMEM, SMEM, SMEM, pl.BlockSpec((1, sub, LANES, tt), lambda h, i: (h, i, 0, 0)),
                  pl.BlockSpec((s, LANES), lambda h, i: (0, h)),
                  pl.BlockSpec((s, LANES), lambda h, i: (0, h))],
        out_specs=[pl.BlockSpec((1, 1, tq, HEAD_DIM), lambda h, i: (h, i, 0, 0)),
                   pl.BlockSpec((1, 1, 1, tq), lambda h, i: (h, i, 0, 0))],
        out_shape=[jax.ShapeDtypeStruct((N_HEADS, nq, tq, HEAD_DIM), F32), jax.ShapeDtypeStruct((N_HEADS, nq, 1, tq), F32)],
        compiler_params=_params("parallel", "arbitrary"),
    )(gnorm, cmax_q, cmin_k, qat, ka, va)


def _attn_bwd(qa, qat, doa, dot, lse, dl, ka, va, gnorm, cmin_k, reach, *, name):
    s = qa.shape[0]
    nt, t = qat.shape[1], qat.shape[3]

    def body(gn_ref, ck_ref, reach_ref, qa_ref, qat_ref, do_ref, dot_ref, lse_ref, dl_ref, ka_ref, va_ref,
             dq_ref, dk_ref, dv_ref):
        h, j = pl.program_id(0), pl.program_id(1)

        @pl.when(j == 0)
        def _():
            dq_ref[...] = jnp.zeros_like(dq_ref)

        ka_j, va_j = ka_ref[...], va_ref[...]

        def tile(i, carry, masked):
            dk, dv = carry
            rows = pl.ds(pl.multiple_of(i * t, t), t)
            qa_i, do_i = qa_ref[rows, :], do_ref[rows, :]
            st = jnp.dot(ka_j, qat_ref[0, i], preferred_element_type=F32) - lse_ref[0, i]
            if masked:
                key = lax.broadcasted_iota(jnp.int32, (t, t), 0)
                qry = lax.broadcasted_iota(jnp.int32, (t, t), 1)
                st = jnp.where(key <= qry, st, NEG)
            pt = jnp.exp2(st)
            dpt = jnp.dot(va_j, dot_ref[0, i], preferred_element_type=F32)
            dsb = (pt * (dpt - dl_ref[0, i])).astype(BF16)
            dv = dv + jnp.dot(pt.astype(BF16), do_i, preferred_element_type=F32)
            dk = dk + jnp.dot(dsb, qa_i, preferred_element_type=F32)
            dq_ref[rows, :] += lax.dot_general(dsb, ka_j, (((0,), (0,)), ((), ())), preferred_element_type=F32)
            return dk, dv

        carry = tile(j, (jnp.zeros((t, LANES), F32), jnp.zeros((t, LANES), F32)), True)
        base = gn_ref[h] - ck_ref[h, j]

        def live(state):
            i = state[0]
            return jnp.logical_and(i < nt, base + reach_ref[h, jnp.minimum(i, nt - 1)] >= -SKIP_MARGIN)

        def step(state):
            i, dk, dv = state
            dk, dv = tile(i, (dk, dv), False)
            return i + 1, dk, dv

        _, dk, dv = lax.while_loop(live, step, (j + 1, *carry))
        dk_ref[...] = dk
        dv_ref[...] = dv

    res = pl.BlockSpec((s, LANES), lambda h, j: (0, h))
    rest = pl.BlockSpec((1, nt, LANES, t), lambda h, j: (h, 0, 0, 0))
    row = pl.BlockSpec((1, nt, 1, t), lambda h, j: (h, 0, 0, 0))
    blk = pl.BlockSpec((t, LANES), lambda h, j: (j, h))
    shape = jax.ShapeDtypeStruct((s, N_HEADS * LANES), F32)
    return pl.pallas_call(
        body, name=name, grid=(N_HEADS, nt),
        in_specs=[SMEM, SMEM, SMEM, res, rest, res, rest, row, row, blk, blk], out_specs=[res, blk, blk],
        out_shape=[shape, shape, shape], compiler_params=_params("parallel", "arbitrary"),
    )(gnorm, cmin_k, reach, qa, qat, doa, dot, lse, dl, ka, va)


def _spread_matrix():
    r = lax.broadcasted_iota(jnp.int32, (D_HALF, N_HEADS * LANES), 0)
    c = lax.broadcasted_iota(jnp.int32, (D_HALF, N_HEADS * LANES), 1)
    return jnp.logical_and(c // LANES == r // HEAD_DIM, c % LANES == r % HEAD_DIM).astype(BF16)


def _piece_matrix(base):
    r = lax.broadcasted_iota(jnp.int32, (3 * LANES, N_HEADS * LANES), 0)
    c = lax.broadcasted_iota(jnp.int32, (3 * LANES, N_HEADS * LANES), 1)
    return jnp.logical_and(r % LANES < N_HEADS, c == LANES * (r % LANES) + base + r // LANES).astype(BF16)


def _ones_columns(first, count):
    c = lax.broadcasted_iota(jnp.int32, (1, N_HEADS * LANES), 1) % LANES
    return jnp.logical_and(c >= first, c < first + count).astype(F32)


def _round_bf16(x):
    return x.astype(BF16).astype(F32)


def _write_transposed(a, out_ref):
    for h in range(N_HEADS):
        out_ref[h, 0] = _transpose_bf16(a[:, h * LANES:(h + 1) * LANES])


def _pack_qkv(h, c, *, tr, name):
    s = h.shape[0]
    wide = N_HEADS * LANES

    def body(q_ref, k_ref, v_ref, c_ref, qa_ref, ka_ref, va_ref, qat_ref):
        spread = _spread_matrix()
        c2 = c_ref[...] * LOG2E
        p1 = _round_bf16(c2)
        p2 = _round_bf16(c2 - p1)
        p3 = _round_bf16(c2 - p1 - p2)
        pieces = jnp.concatenate([p1, p2, p3], axis=1).astype(BF16)
        dot = lambda a, b: jnp.dot(a, b, preferred_element_type=F32)
        qa = (dot((q_ref[...] * 0.125).astype(BF16), spread) + dot(pieces, _piece_matrix(HEAD_DIM + 3))
              + _ones_columns(HEAD_DIM, 3)).astype(BF16)
        ka = (dot((k_ref[...] * LOG2E).astype(BF16), spread) - dot(pieces, _piece_matrix(HEAD_DIM))
              + _ones_columns(HEAD_DIM + 3, 3)).astype(BF16)
        qa_ref[...] = qa
        ka_ref[...] = ka
        va_ref[...] = (dot(v_ref[...].astype(BF16), spread) + _ones_columns(HEAD_DIM, 1)).astype(BF16)
        _write_transposed(qa, qat_ref)

    shape = jax.ShapeDtypeStruct((s, wide), BF16)
    return pl.pallas_call(
        body, name=name, grid=(s // tr,),
        in_specs=[pl.BlockSpec((tr, D_HALF), lambda i: (i, 0)), pl.BlockSpec((tr, D_HALF), lambda i: (i, 1)),
                  pl.BlockSpec((tr, D_HALF), lambda i: (i, 2)), pl.BlockSpec((tr, LANES), lambda i: (i, 0))],
        out_specs=[pl.BlockSpec((tr, wide), lambda i: (i, 0))] * 3 + [pl.BlockSpec((N_HEADS, 1, LANES, tr), lambda i: (0, i, 0, 0))],
        out_shape=[shape, shape, shape, jax.ShapeDtypeStruct((N_HEADS, s // tr, LANES, tr), BF16)],
        compiler_params=_params("parallel"),
    )(h, h, h, c)


def _pack_do(dattn, *, tr, name):
    s = dattn.shape[0]
    wide = N_HEADS * LANES

    def body(d_ref, doa_ref, dot_ref):
        doa = jnp.dot(d_ref[...].astype(BF16), _spread_matrix(), preferred_element_type=F32).astype(BF16)
        doa_ref[...] = doa
        _write_transposed(doa, dot_ref)

    return pl.pallas_call(
        body, name=name, grid=(s // tr,),
        in_specs=[pl.BlockSpec((tr, D_HALF), lambda i: (i, 0))],
        out_specs=[pl.BlockSpec((tr, wide), lambda i: (i, 0)), pl.BlockSpec((N_HEADS, 1, LANES, tr), lambda i: (0, i, 0, 0))],
        out_shape=[jax.ShapeDtypeStruct((s, wide), BF16), jax.ShapeDtypeStruct((N_HEADS, s // tr, LANES, tr), BF16)],
        compiler_params=_params("parallel"),
    )(dattn)


def _unpack_grads(dqa, dka, dva, *, tr, name):
    s = dqa.shape[0]
    wide = N_HEADS * LANES

    def body(dq_ref, dk_ref, dv_ref, o_ref, dc_ref):
        gather = _spread_matrix()
        nt_dims = (((1,), (1,)), ((), ()))
        pick = lambda a: lax.dot_general(a.astype(BF16), gather, nt_dims, preferred_element_type=F32).astype(BF16)
        dq, dk = dq_ref[...], dk_ref[...]
        o_ref[:, 0:D_HALF] = pick(dq * (LN2 * 0.125))
        o_ref[:, D_HALF:2 * D_HALF] = pick(dk)
        o_ref[:, 2 * D_HALF:3 * D_HALF] = pick(dv_ref[...])
        r = lax.broadcasted_iota(jnp.int32, (wide, LANES), 0)
        head = lax.broadcasted_iota(jnp.int32, (wide, LANES), 1)
        sel_q = (r == LANES * head + HEAD_DIM + 3).astype(F32)
        sel_k = (r == LANES * head + HEAD_DIM).astype(F32)
        hi = lax.Precision.HIGHEST
        dc_ref[...] = (jnp.dot(dq, sel_q, precision=hi, preferred_element_type=F32)
                       - jnp.dot(dk, sel_k, precision=hi, preferred_element_type=F32))

    spec = pl.BlockSpec((tr, wide), lambda i: (i, 0))
    return pl.pallas_call(
        body, name=name, grid=(s // tr,), in_specs=[spec, spec, spec],
        out_specs=[pl.BlockSpec((tr, 3 * D_HALF), lambda i: (i, 0)), pl.BlockSpec((tr, LANES), lambda i: (i, 0))],
        out_shape=[jax.ShapeDtypeStruct((s, 3 * D_HALF), BF16), jax.ShapeDtypeStruct((s, LANES), F32)],
        compiler_params=_params("parallel"),
    )(dqa, dka, dva)


def _attention_fwd(h, c, *, tq, kb, t, name):
    s = h.shape[0]
    qa, ka, va, qat = _pack_qkv(h, c, tr=t, name=name + "_pack")
    c2 = c[:, :N_HEADS] * LOG2E
    head_norm = lambda a: jnp.sqrt(jnp.max(jnp.sum(jnp.square(a.reshape(s, N_HEADS, HEAD_DIM)), axis=-1), axis=0))
    gnorm = head_norm(h[:, 0:D_HALF] * 0.125) * head_norm(h[:, D_HALF:2 * D_HALF] * LOG2E) * 1.01 + 1.0
    cmax_q = jnp.max(c2.reshape(s // tq, tq, N_HEADS), axis=1).T
    cmin_k = lax.cummin(jnp.min(c2.reshape(s // kb, kb, N_HEADS), axis=1), axis=0).T
    o4, lse2 = _attn_fwd(qat, ka, va, gnorm, cmax_q, cmin_k, tq=tq, kb=kb, name=name)
    attn = o4.reshape(N_HEADS, s, HEAD_DIM).transpose(1, 0, 2).reshape(s, N_HEADS * HEAD_DIM)
    return attn, dict(qa=qa, qat=qat, ka=ka, va=va, lse2=lse2, c2=c2, gnorm=gnorm)


def _attention_bwd(dattn, delta, res, *, name):
    s = dattn.shape[0]
    t = res["qat"].shape[3]
    doa, dot = _pack_do(dattn, tr=t, name=name + "_pack")
    c2t = res["c2"].reshape(s // t, t, N_HEADS)
    lse_t = res["lse2"].reshape(N_HEADS, s // t, t)
    reach = lax.cummax(jnp.max(c2t.transpose(2, 0, 1) - lse_t, axis=2), axis=1, reverse=True)
    dqa, dka, dva = _attn_bwd(res["qa"], res["qat"], doa, dot, lse_t.reshape(N_HEADS, s // t, 1, t),
                              delta.T.reshape(N_HEADS, s // t, 1, t), res["ka"], res["va"], res["gnorm"],
                              jnp.min(c2t, axis=1).T, reach, name=name)
    return _unpack_grads(dqa, dka, dva, tr=t, name=name + "_unpack")


def _gelu(z):
    return 0.5 * z * (1.0 + lax.erf(z * 0.7071067811865476))


def _gelu_grad(z):
    return 0.5 * (1.0 + lax.erf(z * 0.7071067811865476)) + z * (0.3989422804014327 * jnp.exp(-0.5 * z * z))


def _sgu_mask():
    i = lax.broadcasted_iota(jnp.int32, (SGU_CHUNK, SGU_CHUNK), 0) // CAUSAL_CHUNK
    j = lax.broadcasted_iota(jnp.int32, (SGU_CHUNK, SGU_CHUNK), 1) // CAUSAL_CHUNK
    return (j <= i).astype(F32)


def _layernorm_stats(x):
    mu = jnp.mean(x, axis=-1, keepdims=True)
    xc = x - mu
    rstd = lax.rsqrt(jnp.mean(xc * xc, axis=-1, keepdims=True) + EPS)
    return xc * rstd, rstd


def _first_group_lanes():
    return lax.broadcasted_iota(jnp.int32, (SGU_CHUNK, LANES), 1) < 64


def _sgu_fwd(h, ln_g, ln_b, w_s, bias_tile, *, tr, name):
    s = h.shape[0]
    zu_blk, zv_blk = 1536 // D_HALF, 2048 // D_HALF

    def body(zu_ref, zv_ref, lng_ref, lnb_ref, ws_ref, bias_ref, o_ref):
        gzu = _gelu(zu_ref[...])
        xh, _ = _layernorm_stats(_gelu(zv_ref[...]))
        zb = (xh * lng_ref[...] + lnb_ref[...]).astype(BF16)
        mask = _sgu_mask()
        first = _first_group_lanes()
        for pair in range(4):
            cols = slice(pair * LANES, (pair + 1) * LANES)
            w0 = (ws_ref[2 * pair] * mask).astype(BF16)
            w1 = (ws_ref[2 * pair + 1] * mask).astype(BF16)
            for ch in range(tr // SGU_CHUNK):
                rows = slice(ch * SGU_CHUNK, (ch + 1) * SGU_CHUNK)
                zp = zb[rows, cols]
                mixed = jnp.where(first, jnp.dot(w0, zp, preferred_element_type=F32),
                                  jnp.dot(w1, zp, preferred_element_type=F32)) + bias_ref[:, cols]
                o_ref[rows, cols] = gzu[rows, cols] * mixed

    return pl.pallas_call(
        body, name=name, grid=(s // tr,),
        in_specs=[pl.BlockSpec((tr, D_HALF), lambda i: (i, zu_blk)), pl.BlockSpec((tr, D_HALF), lambda i: (i, zv_blk)),
                  pl.BlockSpec((1, D_HALF), lambda i: (0, 0)), pl.BlockSpec((1, D_HALF), lambda i: (0, 0)),
                  pl.BlockSpec((N_HEADS, SGU_CHUNK, SGU_CHUNK), lambda i: (0, 0, 0)),
                  pl.BlockSpec((SGU_CHUNK, D_HALF), lambda i: (0, 0))],
        out_specs=pl.BlockSpec((tr, D_HALF), lambda i: (i, 0)),
        out_shape=jax.ShapeDtypeStruct((s, D_HALF), F32),
        compiler_params=_params("parallel"),
    )(h, h, ln_g, ln_b, w_s, bias_tile)


def _sgu_bwd(dsgu, h, ln_g, ln_b, w_s, bias_tile, *, tr, name):
    s = h.shape[0]
    n = s // tr
    zu_blk, zv_blk = 1536 // D_HALF, 2048 // D_HALF

    def body(ds_ref, zu_ref, zv_ref, lng_ref, lnb_ref, ws_ref, bias_ref,
             dzu_ref, dzv_ref, dws_ref, dlng_ref, dlnb_ref, dbs_ref, dgzu_sc, dzvn_sc, dbias_sc):
        step = pl.program_id(0)

        @pl.when(step == 0)
        def _():
            dws_ref[...] = jnp.zeros_like(dws_ref)
            dlng_ref[...] = jnp.zeros_like(dlng_ref)
            dlnb_ref[...] = jnp.zeros_like(dlnb_ref)
            dbias_sc[...] = jnp.zeros_like(dbias_sc)

        zu = zu_ref[...]
        zv = zv_ref[...]
        gzu = _gelu(zu)
        xh, rstd = _layernorm_stats(_gelu(zv))
        zb = (xh * lng_ref[...] + lnb_ref[...]).astype(BF16)
        ds = ds_ref[...]
        mask = _sgu_mask()
        first = _first_group_lanes()
        tn_dims = (((0,), (0,)), ((), ()))
        nt_dims = (((1,), (1,)), ((), ()))
        for pair in range(4):
            cols = slice(pair * LANES, (pair + 1) * LANES)
            w0 = (ws_ref[2 * pair] * mask).astype(BF16)
            w1 = (ws_ref[2 * pair + 1] * mask).astype(BF16)
            for ch in range(tr // SGU_CHUNK):
                rows = slice(ch * SGU_CHUNK, (ch + 1) * SGU_CHUNK)
                zp = zb[rows, cols]
                mixed = jnp.where(first, jnp.dot(w0, zp, preferred_element_type=F32),
                                  jnp.dot(w1, zp, preferred_element_type=F32)) + bias_ref[:, cols]
                dsp = ds[rows, cols]
                dgzu_sc[rows, cols] = dsp * mixed
                dm = dsp * gzu[rows, cols]
                dbias_sc[:, cols] += dm
                dmb = dm.astype(BF16)
                dm0 = jnp.where(first, dmb, jnp.zeros_like(dmb))
                dm1 = jnp.where(first, jnp.zeros_like(dmb), dmb)
                dws_ref[2 * pair] += lax.dot_general(dm0, zp, nt_dims, preferred_element_type=F32)
                dws_ref[2 * pair + 1] += lax.dot_general(dm1, zp, nt_dims, preferred_element_type=F32)
                dzvn_sc[rows, cols] = jnp.where(first, lax.dot_general(w0, dmb, tn_dims, preferred_element_type=F32),
                                                lax.dot_general(w1, dmb, tn_dims, preferred_element_type=F32))
        dzvn = dzvn_sc[...]
        dlng_ref[...] += jnp.sum(dzvn * xh, axis=0, keepdims=True)
        dlnb_ref[...] += jnp.sum(dzvn, axis=0, keepdims=True)
        dxh = dzvn * lng_ref[...]
        dgzv = rstd * (dxh - jnp.mean(dxh, axis=-1, keepdims=True) - xh * jnp.mean(dxh * xh, axis=-1, keepdims=True))
        dzv_ref[...] = (dgzv * _gelu_grad(zv)).astype(dzv_ref.dtype)
        dzu_ref[...] = (dgzu_sc[...] * _gelu_grad(zu)).astype(dzu_ref.dtype)

        @pl.when(step == n - 1)
        def _():
            for g in range(N_HEADS):
                dws_ref[g] = dws_ref[g] * mask
            lane = lax.broadcasted_iota(jnp.int32, (D_HALF, LANES), 0) // 64
            grp = lax.broadcasted_iota(jnp.int32, (D_HALF, LANES), 1)
            dbs_ref[...] = jnp.dot(dbias_sc[...], (lane == grp).astype(F32), precision=lax.Precision.HIGHEST,
                                   preferred_element_type=F32)

    const2 = lambda i: (0, 0)
    return pl.pallas_call(
        body, name=name, grid=(n,),
        in_specs=[pl.BlockSpec((tr, D_HALF), lambda i: (i, 0)),
                  pl.BlockSpec((tr, D_HALF), lambda i: (i, zu_blk)), pl.BlockSpec((tr, D_HALF), lambda i: (i, zv_blk)),
                  pl.BlockSpec((1, D_HALF), const2), pl.BlockSpec((1, D_HALF), const2),
                  pl.BlockSpec((N_HEADS, SGU_CHUNK, SGU_CHUNK), lambda i: (0, 0, 0)),
                  pl.BlockSpec((SGU_CHUNK, D_HALF), const2)],
        out_specs=[pl.BlockSpec((tr, D_HALF), lambda i: (i, 0)), pl.BlockSpec((tr, D_HALF), lambda i: (i, 0)),
                   pl.BlockSpec((N_HEADS, SGU_CHUNK, SGU_CHUNK), lambda i: (0, 0, 0)),
                   pl.BlockSpec((1, D_HALF), const2), pl.BlockSpec((1, D_HALF), const2),
                   pl.BlockSpec((SGU_CHUNK, LANES), const2)],
        out_shape=[jax.ShapeDtypeStruct((s, D_HALF), BF16), jax.ShapeDtypeStruct((s, D_HALF), BF16),
                   jax.ShapeDtypeStruct((N_HEADS, SGU_CHUNK, SGU_CHUNK), F32),
                   jax.ShapeDtypeStruct((1, D_HALF), F32), jax.ShapeDtypeStruct((1, D_HALF), F32),
                   jax.ShapeDtypeStruct((SGU_CHUNK, LANES), F32)],
        scratch_shapes=[pltpu.VMEM((tr, D_HALF), F32), pltpu.VMEM((tr, D_HALF), F32), pltpu.VMEM((SGU_CHUNK, D_HALF), F32)],
        compiler_params=_params("arbitrary"),
    )(dsgu, h, h, ln_g, ln_b, w_s, bias_tile)


FF_BLOCK = D_FF // 2


def _matmul_swiglu(a, w, *, tm, name):
    s, k = a.shape

    def body(a_ref, wg_ref, wu_ref, g_ref, u_ref, act_ref):
        av = a_ref[...].astype(BF16)
        g = jnp.dot(av, wg_ref[...].astype(BF16), preferred_element_type=F32)
        u = jnp.dot(av, wu_ref[...].astype(BF16), preferred_element_type=F32)
        g_ref[...] = g.astype(g_ref.dtype)
        u_ref[...] = u.astype(u_ref.dtype)
        act_ref[...] = (g * jax.nn.sigmoid(g) * u).astype(act_ref.dtype)

    out = pl.BlockSpec((tm, FF_BLOCK), lambda i, j: (i, j))
    return pl.pallas_call(
        body, name=name, grid=(s // tm, 2),
        in_specs=[pl.BlockSpec((tm, k), lambda i, j: (i, 0)), pl.BlockSpec((k, FF_BLOCK), lambda i, j: (0, j)),
                  pl.BlockSpec((k, FF_BLOCK), lambda i, j: (0, j + 2))],
        out_specs=[out, out, out],
        out_shape=[jax.ShapeDtypeStruct((s, D_FF), BF16)] * 3,
        compiler_params=_params("parallel", "parallel"),
    )(a, w, w)


def _matmul_dswiglu(dx, w_down, gate, up, *, tm, name):
    s, k = dx.shape

    def body(dx_ref, w_ref, g_ref, u_ref, dg_ref, du_ref):
        d = lax.dot_general(dx_ref[...].astype(BF16), w_ref[...].astype(BF16), (((1,), (1,)), ((), ())),
                            preferred_element_type=F32)
        g = g_ref[...].astype(F32)
        sig = jax.nn.sigmoid(g)
        dg_ref[...] = (d * u_ref[...].astype(F32) * (sig * (1.0 + g * (1.0 - sig)))).astype(dg_ref.dtype)
        du_ref[...] = (d * (g * sig)).astype(du_ref.dtype)

    blk = pl.BlockSpec((tm, FF_BLOCK), lambda i, j: (i, j))
    return pl.pallas_call(
        body, name=name, grid=(s // tm, 2),
        in_specs=[pl.BlockSpec((tm, k), lambda i, j: (i, 0)), pl.BlockSpec((FF_BLOCK, k), lambda i, j: (j, 0)), blk, blk],
        out_specs=[blk, blk], out_shape=[jax.ShapeDtypeStruct((s, D_FF), BF16)] * 2,
        compiler_params=_params("parallel", "parallel"),
    )(dx, w_down, gate, up)


def _matmul_gu_dx(dgate, dup, w_gu, *, tm, name):
    s = dgate.shape[0]
    k = w_gu.shape[0]
    nt_dims = (((1,), (1,)), ((), ()))

    def body(dg_ref, du_ref, wg_ref, wu_ref, o_ref):
        o_ref[...] = (lax.dot_general(dg_ref[...], wg_ref[...], nt_dims, preferred_element_type=F32)
                      + lax.dot_general(du_ref[...], wu_ref[...], nt_dims, preferred_element_type=F32))

    return pl.pallas_call(
        body, name=name, grid=(s // tm,),
        in_specs=[pl.BlockSpec((tm, D_FF), lambda i: (i, 0)), pl.BlockSpec((tm, D_FF), lambda i: (i, 0)),
                  pl.BlockSpec((k, D_FF), lambda i: (0, 0)), pl.BlockSpec((k, D_FF), lambda i: (0, 1))],
        out_specs=pl.BlockSpec((tm, k), lambda i: (i, 0)),
        out_shape=jax.ShapeDtypeStruct((s, k), F32),
        compiler_params=_params("parallel"),
    )(dgate, dup, w_gu, w_gu)


def _sum_slots(stacked, *, tr, name):
    k, r, _ = stacked.shape

    def body(x_ref, o_ref):
        acc = x_ref[0].astype(F32)
        for idx in range(1, k):
            acc = acc + x_ref[idx].astype(F32)
        o_ref[...] = acc

    return pl.pallas_call(
        body, name=name, grid=(r // tr,),
        in_specs=[pl.BlockSpec((k, tr, LANES), lambda i: (0, i, 0))],
        out_specs=pl.BlockSpec((tr, LANES), lambda i: (i, 0)),
        out_shape=jax.ShapeDtypeStruct((r, LANES), F32),
        compiler_params=_params("parallel"),
    )(stacked)


def _adamw(w, g, m, v, *, tr, name):
    r = w.shape[0]

    def body(w_ref, g_ref, m_ref, v_ref, d_ref, m2_ref, v2_ref):
        gv = g_ref[...]
        m2 = ADAM_B1 * m_ref[...] + (1.0 - ADAM_B1) * gv
        v2 = ADAM_B2 * v_ref[...] + (1.0 - ADAM_B2) * jnp.square(gv)
        m_hat = m2 / (1.0 - ADAM_B1 ** ADAM_STEP)
        v_hat = v2 / (1.0 - ADAM_B2 ** ADAM_STEP)
        d_ref[...] = -ADAM_LR * (m_hat / (jnp.sqrt(v_hat) + ADAM_EPS) + ADAM_WD * w_ref[...])
        m2_ref[...] = m2
        v2_ref[...] = v2

    spec = pl.BlockSpec((tr, LANES), lambda i: (i, 0))
    shape = jax.ShapeDtypeStruct((r, LANES), F32)
    return pl.pallas_call(
        body, name=name, grid=(r // tr,), in_specs=[spec] * 4, out_specs=[spec] * 3, out_shape=[shape] * 3,
        compiler_params=_params("parallel"),
    )(w, g, m, v)


PAIR_CHUNKS = 5


def _coords():
    return lax.axis_index("x"), lax.axis_index("y"), lax.axis_index("c")


def _my_chip():
    return 2 * lax.axis_index("x") + lax.axis_index("y")


def _chip_peer(x, y, k):
    px = 1 - x if k & 2 else x
    py = 1 - y if k & 1 else y
    return px, py


def _allgather_chips(shard, *, name):
    r = shard.shape[0]
    rh = r // 2

    def body(src, out, send_sems, recv_sems):
        x, y, c = _coords()
        me = 2 * x + y
        half = pl.ds(c * rh, rh)

        def copy(k, src_ref, slot, to):
            return pltpu.make_async_remote_copy(src_ref=src_ref, dst_ref=out.at[slot, half, :], send_sem=send_sems.at[k],
                                                recv_sem=recv_sems.at[k], device_id=to, device_id_type=MESH)

        first, passed = [], []
        for k in (1, 2, 3):
            px, py = _chip_peer(x, y, k)
            first.append(copy(k - 1, src.at[half, :], me, (px, py, c)))
            first[-1].start()
        for k in (1, 2, 3):
            px, py = _chip_peer(x, y, k)
            slot = 2 * px + py
            first[k - 1].wait_recv()
            passed.append(copy(2 + k, out.at[slot, half, :], slot, (x, y, 1 - c)))
            passed[-1].start()
        for cp in passed:
            cp.wait_recv()
        for cp in first + passed:
            cp.wait_send()

    gathered = pl.pallas_call(
        body, name=name, in_specs=[ANY], out_specs=ANY,
        out_shape=jax.ShapeDtypeStruct((4, r, LANES), shard.dtype),
        scratch_shapes=[pltpu.SemaphoreType.DMA((6,)), pltpu.SemaphoreType.DMA((6,))],
    )(shard)
    return lax.dynamic_update_slice(gathered, shard[None], (_my_chip(), 0, 0))


def _pair_split(grads, *, name):
    _, r, _ = grads.shape
    rh = r // 2
    rc = rh // PAIR_CHUNKS
    nchunk = 4 * PAIR_CHUNKS

    def body(g_ref, theirs_ref, send_sems, recv_sems):
        x, y, c = _coords()
        copies = []
        for j in range(4):
            for q in range(PAIR_CHUNKS):
                idx = j * PAIR_CHUNKS + q
                cp = pltpu.make_async_remote_copy(
                    src_ref=g_ref.at[j, pl.ds((1 - c) * rh + q * rc, rc), :], dst_ref=theirs_ref.at[j, pl.ds(q * rc, rc), :],
                    send_sem=send_sems.at[idx], recv_sem=recv_sems.at[idx], device_id=(x, y, 1 - c), device_id_type=MESH)
                cp.start()
                copies.append(cp)
        for cp in copies:
            cp.wait()

    return pl.pallas_call(
        body, name=name, in_specs=[ANY], out_specs=ANY, out_shape=jax.ShapeDtypeStruct((4, rh, LANES), F32),
        scratch_shapes=[pltpu.SemaphoreType.DMA((nchunk,)), pltpu.SemaphoreType.DMA((nchunk,))],
    )(grads)


def _pair_sum(grads, theirs, half, *, tr, name):
    _, rh, _ = theirs.shape
    nrt = rh // tr

    def body(half_ref, g_ref, t_ref, o_ref):
        o_ref[...] = (g_ref[...] + t_ref[...]).astype(o_ref.dtype)

    return pl.pallas_call(
        body, name=name,
        grid_spec=pltpu.PrefetchScalarGridSpec(
            num_scalar_prefetch=1, grid=(4, nrt),
            in_specs=[pl.BlockSpec((1, tr, LANES), lambda j, i, half_ref: (j, half_ref[0] * nrt + i, 0)),
                      pl.BlockSpec((1, tr, LANES), lambda j, i, half_ref: (j, i, 0))],
            out_specs=pl.BlockSpec((1, tr, LANES), lambda j, i, half_ref: (j, i, 0))),
        out_shape=jax.ShapeDtypeStruct((4, rh, LANES), BF16),
        compiler_params=_params("parallel", "parallel"),
    )(half, grads, theirs)


def _scatter_chips(part, *, name):
    _, rh, _ = part.shape

    def body(p_ref, out, send_sems, recv_sems):
        x, y, c = _coords()
        me = 2 * x + y
        copies = []
        for k in (1, 2, 3):
            px, py = _chip_peer(x, y, k)
            cp = pltpu.make_async_remote_copy(src_ref=p_ref.at[2 * px + py], dst_ref=out.at[me], send_sem=send_sems.at[k - 1],
                                              recv_sem=recv_sems.at[k - 1], device_id=(px, py, c), device_id_type=MESH)
            cp.start()
            copies.append(cp)
        for cp in copies:
            cp.wait()

    from_chips = pl.pallas_call(
        body, name=name, in_specs=[ANY], out_specs=ANY, out_shape=jax.ShapeDtypeStruct((4, rh, LANES), part.dtype),
        scratch_shapes=[pltpu.SemaphoreType.DMA((3,)), pltpu.SemaphoreType.DMA((3,))],
    )(part)
    own = lax.dynamic_index_in_dim(part, _my_chip(), axis=0, keepdims=True)
    return lax.dynamic_update_slice(from_chips, own, (_my_chip(), 0, 0))


def _pair_join(half, *, name):
    rh = half.shape[0]
    nchunk = 2 * PAIR_CHUNKS
    rc = rh // nchunk

    def body(h_ref, out, send_sems, recv_sems):
        x, y, c = _coords()
        copies = []
        for q in range(nchunk):
            src = h_ref.at[pl.ds(q * rc, rc), :]
            rows = out.at[pl.ds(c * rh + q * rc, rc), :]
            cp = pltpu.make_async_remote_copy(src_ref=src, dst_ref=rows, send_sem=send_sems.at[q], recv_sem=recv_sems.at[q],
                                              device_id=(x, y, 1 - c), device_id_type=MESH)
            cp.start()
            copies.append(cp)
        for cp in copies:
            cp.wait()

    joined = pl.pallas_call(
        body, name=name, in_specs=[ANY], out_specs=ANY, out_shape=jax.ShapeDtypeStruct((2 * rh, LANES), F32),
        scratch_shapes=[pltpu.SemaphoreType.DMA((nchunk,)), pltpu.SemaphoreType.DMA((nchunk,))],
    )(half)
    return lax.dynamic_update_slice(joined, half, (lax.axis_index("c") * rh, 0))


def _allgather_all(block, *, name):
    r = block.shape[0]

    def body(src, out, send_sems, recv_sems):
        x, y, c = _coords()
        me = 4 * x + 2 * y + c
        copies = []
        for k in range(1, 8):
            px, py = _chip_peer(x, y, k >> 1)
            pc = 1 - c if k & 1 else c
            cp = pltpu.make_async_remote_copy(src_ref=src, dst_ref=out.at[me], send_sem=send_sems.at[k - 1],
                                              recv_sem=recv_sems.at[k - 1], device_id=(px, py, pc), device_id_type=MESH)
            cp.start()
            copies.append(cp)
        for cp in copies:
            cp.wait()

    gathered = pl.pallas_call(
        body, name=name, in_specs=[ANY], out_specs=ANY, out_shape=jax.ShapeDtypeStruct((8, r, LANES), F32),
        scratch_shapes=[pltpu.SemaphoreType.DMA((7,)), pltpu.SemaphoreType.DMA((7,))],
    )(block)
    return lax.dynamic_update_slice(gathered, block[None], (2 * _my_chip() + lax.axis_index("c"), 0, 0))


def _flatten(arrays, pad_rows=None):
    flat = jnp.concatenate([a.reshape(-1) for a in arrays])
    if pad_rows is not None:
        flat = jnp.pad(flat, (0, pad_rows * LANES - flat.shape[0]))
    return flat.reshape(-1, LANES)


def _unflatten(flat, shapes):
    flat = flat.reshape(-1)
    out, off = [], 0
    for shp in shapes:
        size = 1
        for dim in shp:
            size *= dim
        out.append(flat[off:off + size].reshape(shp))
        off += size
    return out


def _pad_w_in(w):
    pad = jnp.zeros(w.shape[:-1] + (D_IN_PAD - D_IN,), w.dtype)
    return jnp.concatenate([w[..., :1536], w[..., 1544:], w[..., 1536:1544], pad], axis=-1)


def _unpad_w_in(w):
    return jnp.concatenate([w[..., :1536], w[..., 2560:2568], w[..., 1536:2560]], axis=-1)


def _tile(s, want):
    return min(want, s)


def _layer_fwd(x, p, l):
    s = x.shape[0]
    tr = _tile(s, 512)
    tm = _tile(s, 1024)
    xn = _rms_fwd([x], p["mix_g"], out_dtype=BF16, tr=tr, name=f"rms_mix_fwd{l}")
    h = _matmul(xn, p["w_in"], tm=tm, tn=896, out_dtype=F32, name=f"mm_in{l}")
    c = _gates_fwd(h, p["bf_pad"], tr=_tile(s, 256), name=f"gates_fwd{l}")
    attn, attn_res = _attention_fwd(h, c, tq=_tile(s, 2048), kb=_tile(s, 512), t=_tile(s, 512), name=f"attn_fwd{l}")
    sgu = _sgu_fwd(h, p["ln_g"], p["ln_b"], p["w_s"], p["bias_tile"], tr=_tile(s, 256), name=f"sgu_fwd{l}")
    merged = _rms_fwd([attn, sgu], p["out_g"], out_dtype=BF16, tr=tr, name=f"rms_out_fwd{l}")
    x1 = _matmul(merged, p["w_out"], tm=tm, tn=1024, out_dtype=F32, residual=x, name=f"mm_out{l}")
    xn2 = _rms_fwd([x1], p["ffn_g"], out_dtype=BF16, tr=tr, name=f"rms_ffn_fwd{l}")
    gate, up, act = _matmul_swiglu(xn2, p["w_gu"], tm=_tile(s, 512), name=f"mm_gu{l}")
    x2 = _matmul(act, p["w_down"], tm=tm, tn=1024, out_dtype=F32, residual=x1, name=f"mm_down{l}")
    saved = dict(x=x, xn=xn, h=h, attn_res=attn_res, attn=attn, sgu=sgu, merged=merged, x1=x1, xn2=xn2, gate=gate, up=up, act=act)
    return x2, saved


def _layer_bwd(dx2, p, sv, l):
    s = dx2.shape[0]
    tr = _tile(s, 512)
    tm = _tile(s, 1024)
    ts = _tile(s, 1024)
    g = {}
    g["w_down"] = _matmul_tn(sv["act"], dx2, tm=1408, tn=1024, ts=ts, name=f"mm_down_dw{l}")
    dgate, dup = _matmul_dswiglu(dx2, p["w_down"], sv["gate"], sv["up"], tm=_tile(s, 512), name=f"mm_down_dx{l}")
    g["w_gu"] = jnp.concatenate([_matmul_tn(sv["xn2"], dgate, tm=1024, tn=1408, ts=ts, name=f"mm_gate_dw{l}"),
                                 _matmul_tn(sv["xn2"], dup, tm=1024, tn=1408, ts=ts, name=f"mm_up_dw{l}")], axis=1)
    dxn2 = _matmul_gu_dx(dgate, dup, p["w_gu"], tm=_tile(s, 512), name=f"mm_gu_dx{l}")
    (dx1,), g["ffn_g"] = _rms_bwd(dxn2, [sv["x1"]], p["ffn_g"], residual=dx2, tr=tr, name=f"rms_ffn_bwd{l}")
    g["w_out"] = _matmul_tn(sv["merged"], dx1, tm=1024, tn=1024, ts=ts, name=f"mm_out_dw{l}")
    dmerged = _matmul(dx1, p["w_out"], trans_b=True, tm=tm, tn=1024, out_dtype=F32, name=f"mm_out_dx{l}")
    (dattn, dsgu), g["out_g"], delta = _rms_bwd(dmerged, [sv["attn"], sv["sgu"]], p["out_g"], head_dots=True, tr=tr,
                                                name=f"rms_out_bwd{l}")
    dzu, dzv, g["w_s"], g["ln_g"], g["ln_b"], dbs = _sgu_bwd(dsgu, sv["h"], p["ln_g"], p["ln_b"], p["w_s"], p["bias_tile"],
                                                           tr=_tile(s, 256), name=f"sgu_bwd{l}")
    g["b_s"] = dbs[:, :N_HEADS].T
    dqkv, dc = _attention_bwd(dattn, delta[:, :N_HEADS], sv["attn_res"], name=f"attn_bwd{l}")
    dfl, dbf = _gates_bwd(dc, sv["h"], p["bf_pad"], tr=_tile(s, 256), name=f"gates_bwd{l}")
    g["b_f"] = dbf[0, :N_HEADS]
    dh = jnp.concatenate([dqkv, dzu, dzv, dfl], axis=1)
    g["w_in"] = _matmul_tn(sv["xn"], dh, tm=1024, tn=896, ts=ts, name=f"mm_in_dw{l}")
    dxn = _matmul(dh, p["w_in"], trans_b=True, tm=tm, tn=1024, out_dtype=F32, name=f"mm_in_dx{l}")
    (dx,), g["mix_g"] = _rms_bwd(dxn, [sv["x"]], p["mix_g"], residual=dx1, tr=tr, name=f"rms_mix_bwd{l}")
    return dx, g


def _layer_params(l, w_in_pad, w_out, w_gu, w_down, mix_norm_g, b_f, sgu_ln_g, sgu_ln_b, w_s, b_s, out_norm_g, ffn_norm_g):
    return dict(
        w_in=w_in_pad[l], w_out=w_out[l], w_gu=w_gu[l], w_down=w_down[l],
        mix_g=mix_norm_g[l][None, :], out_g=out_norm_g[l][None, :], ffn_g=ffn_norm_g[l][None, :],
        bf_pad=jnp.pad(b_f[l], (0, LANES - N_HEADS))[None, :],
        ln_g=sgu_ln_g[l][None, :], ln_b=sgu_ln_b[l][None, :], w_s=w_s[l],
        bias_tile=jnp.repeat(b_s[l].T, 64, axis=1),
    )


def _local_step(x, tgt, w_in_pad, w_out, w_gu, w_down, mix_norm_g, b_f, sgu_ln_g, sgu_ln_b, w_s, b_s, out_norm_g,
                ffn_norm_g, final_norm_g):
    depth = w_in_pad.shape[0]
    s = x.shape[0]
    params = [_layer_params(l, w_in_pad, w_out, w_gu, w_down, mix_norm_g, b_f, sgu_ln_g, sgu_ln_b, w_s, b_s, out_norm_g,
                            ffn_norm_g) for l in range(depth)]
    saved = []
    for l in range(depth):
        x, sv = _layer_fwd(x, params[l], l)
        saved.append(sv)
    loss_tile, dx, dfinal = _loss_head(x, tgt, final_norm_g[None, :], tr=_tile(s, 512), name="loss_head")
    grads = [None] * depth
    for l in reversed(range(depth)):
        dx, grads[l] = _layer_bwd(dx, params[l], saved[l], l)
    return loss_tile[0, 0], dx, grads, dfinal[0]


SMALL_ROWS = 4272


def kernel(x, mix_norm_g, w_in, b_f, sgu_ln_g, sgu_ln_b, w_s, b_s, out_norm_g, w_out, ffn_norm_g, w_gate_up, w_down, final_norm_g, loss_target, m_mix_norm_g, m_w_in, m_b_f, m_sgu_ln_g, m_sgu_ln_b, m_w_s, m_b_s, m_out_norm_g, m_w_out, m_ffn_norm_g, m_w_gate_up, m_w_down, m_final_norm_g, v_mix_norm_g, v_w_in, v_b_f, v_sgu_ln_g, v_sgu_ln_b, v_w_s, v_b_s, v_out_norm_g, v_w_out, v_ffn_norm_g, v_w_gate_up, v_w_down, v_final_norm_g):
    big = [w_in, w_out, w_gate_up, w_down]
    big_shapes = [a.shape for a in big]
    small = [mix_norm_g, b_f, sgu_ln_g, sgu_ln_b, w_s, b_s, out_norm_g, ffn_norm_g, final_norm_g]
    small_shapes = [a.shape for a in small]

    gathered = _allgather_chips(_flatten([a.astype(BF16) for a in big]), name="ag_weights")
    per_chip = [_unflatten(gathered[j], big_shapes) for j in range(4)]
    w_in_full = _pad_w_in(jnp.concatenate([pc[0] for pc in per_chip], axis=2))
    w_out_full = jnp.concatenate([pc[1] for pc in per_chip], axis=1)
    w_gu_full = jnp.concatenate([pc[2] for pc in per_chip], axis=2)
    w_down_full = jnp.concatenate([pc[3] for pc in per_chip], axis=1)

    loss_part, dx, grads, dfinal = _local_step(
        x[0], loss_target[0], w_in_full, w_out_full, w_gu_full, w_down_full, mix_norm_g, b_f, sgu_ln_g, sgu_ln_b, w_s, b_s,
        out_norm_g, ffn_norm_g, final_norm_g)
    stack = lambda key: jnp.stack([g[key] for g in grads])
    g_in = _unpad_w_in(stack("w_in"))
    g_out, g_gu, g_down = stack("w_out"), stack("w_gu"), stack("w_down")

    send = jnp.stack([_flatten([g_in[:, :, 642 * j:642 * (j + 1)], g_out[:, 256 * j:256 * (j + 1), :],
                                g_gu[:, :, 1408 * j:1408 * (j + 1)], g_down[:, 704 * j:704 * (j + 1), :]]) for j in range(4)])
    theirs = _pair_split(send, name="rs_pair_split")
    pair_sum = _pair_sum(send, theirs, lax.axis_index("c").astype(jnp.int32).reshape(1), tr=3440, name="rs_pair_sum")
    from_chips = _scatter_chips(pair_sum, name="rs_scatter")
    half = _sum_slots(from_chips, tr=1120, name="rs_chip_sum")
    g_big_flat = _pair_join(half, name="rs_pair_join")

    g_small_local = [stack("mix_g")[:, 0], stack("b_f"), stack("ln_g")[:, 0], stack("ln_b")[:, 0], stack("w_s"), stack("b_s"),
                     stack("out_g")[:, 0], stack("ffn_g")[:, 0], dfinal]
    g_small_flat = _sum_slots(_allgather_all(_flatten(g_small_local, SMALL_ROWS), name="ar_small_gather"), tr=1424,
                              name="ar_small_sum")
    loss = lax.psum(loss_part, ("x", "y", "c"))

    d_big, m_big, v_big = _adamw(_flatten(big), g_big_flat, _flatten([m_w_in, m_w_out, m_w_gate_up, m_w_down]),
                                 _flatten([v_w_in, v_w_out, v_w_gate_up, v_w_down]), tr=2240, name="adamw_big")
    m_small = [m_mix_norm_g, m_b_f, m_sgu_ln_g, m_sgu_ln_b, m_w_s, m_b_s, m_out_norm_g, m_ffn_norm_g, m_final_norm_g]
    v_small = [v_mix_norm_g, v_b_f, v_sgu_ln_g, v_sgu_ln_b, v_w_s, v_b_s, v_out_norm_g, v_ffn_norm_g, v_final_norm_g]
    d_small, m_small2, v_small2 = _adamw(_flatten(small, SMALL_ROWS), g_small_flat, _flatten(m_small, SMALL_ROWS),
                                         _flatten(v_small, SMALL_ROWS), tr=1424, name="adamw_small")

    def in_order(big_flat, small_flat):
        b_in, b_out, b_gu, b_down = _unflatten(big_flat, big_shapes)
        s_mix, s_bf, s_lng, s_lnb, s_ws, s_bs, s_outg, s_ffn, s_fin = _unflatten(small_flat, small_shapes)
        return [s_mix, b_in, s_bf, s_lng, s_lnb, s_ws, s_bs, s_outg, b_out, s_ffn, b_gu, b_down, s_fin]

    return (loss, dx[None], *in_order(g_big_flat, g_small_flat), *in_order(d_big, d_small), *in_order(m_big, m_small2),
            *in_order(v_big, v_small2))
```

```python
import jax
import jax.numpy as jnp
from jax import lax
from jax.experimental import pallas as pl
from jax.experimental.pallas import tpu as pltpu

F32 = jnp.float32
BF16 = jnp.bfloat16

D_MODEL = 1024
D_HALF = 512
N_HEADS = 8
HEAD_DIM = 64
SGU_CHUNK = 128
CAUSAL_CHUNK = 64
D_FF = 2816
D_IN = 2568
D_IN_PAD = 2688
F_COL_BLOCK = 2560 // 128
EPS = 1e-6
NEG = -1e30
LANES = 128
VMEM_LIMIT = 56 * 1024 * 1024

ADAM_LR = 0.001
ADAM_B1 = 0.9
ADAM_B2 = 0.999
ADAM_EPS = 1e-08
ADAM_WD = 0.01
ADAM_STEP = 10

MESH = pl.DeviceIdType.MESH
ANY = pl.BlockSpec(memory_space=pl.ANY)


def _params(*sem):
    return pltpu.CompilerParams(dimension_semantics=sem, vmem_limit_bytes=VMEM_LIMIT)


def _matmul(a, b, *, trans_b=False, tm, tn, out_dtype, residual=None, name):
    m, k = a.shape
    n = b.shape[0] if trans_b else b.shape[1]
    dims = (((1,), (1,)), ((), ())) if trans_b else (((1,), (0,)), ((), ()))

    def body(*refs):
        a_ref, b_ref = refs[0], refs[1]
        o_ref = refs[-1]
        acc = lax.dot_general(a_ref[...].astype(BF16), b_ref[...].astype(BF16), dims,
                              preferred_element_type=F32)
        if residual is not None:
            acc = acc + refs[2][...]
        o_ref[...] = acc.astype(out_dtype)

    b_spec = pl.BlockSpec((tn, k), lambda i, j: (j, 0)) if trans_b else pl.BlockSpec((k, tn), lambda i, j: (0, j))
    in_specs = [pl.BlockSpec((tm, k), lambda i, j: (i, 0)), b_spec]
    args = [a, b]
    if residual is not None:
        in_specs.append(pl.BlockSpec((tm, tn), lambda i, j: (i, j)))
        args.append(residual)
    return pl.pallas_call(
        body, name=name, grid=(m // tm, n // tn), in_specs=in_specs,
        out_specs=pl.BlockSpec((tm, tn), lambda i, j: (i, j)),
        out_shape=jax.ShapeDtypeStruct((m, n), out_dtype),
        compiler_params=_params("parallel", "parallel"),
    )(*args)


def _matmul_tn(a, b, *, tm, tn, ts, name):
    s, m = a.shape
    n = b.shape[1]

    def body(a_ref, b_ref, o_ref):
        @pl.when(pl.program_id(2) == 0)
        def _():
            o_ref[...] = jnp.zeros_like(o_ref)

        o_ref[...] += lax.dot_general(a_ref[...].astype(BF16), b_ref[...].astype(BF16),
                                      (((0,), (0,)), ((), ())), preferred_element_type=F32)

    return pl.pallas_call(
        body, name=name, grid=(m // tm, n // tn, s // ts),
        in_specs=[pl.BlockSpec((ts, tm), lambda i, j, t: (t, i)), pl.BlockSpec((ts, tn), lambda i, j, t: (t, j))],
        out_specs=pl.BlockSpec((tm, tn), lambda i, j, t: (i, j)),
        out_shape=jax.ShapeDtypeStruct((m, n), F32),
        compiler_params=_params("parallel", "parallel", "arbitrary"),
    )(a, b)


def _rms_fwd(xs, g, *, out_dtype, tr, name):
    s = xs[0].shape[0]
    widths = [x.shape[1] for x in xs]
    wsum = sum(widths)
    nx = len(xs)

    def body(*refs):
        g_ref, o_ref = refs[nx], refs[nx + 1]
        off = 0
        for x_ref, w in zip(refs[:nx], widths):
            x = x_ref[...]
            r = lax.rsqrt(jnp.mean(x * x, axis=-1, keepdims=True) + EPS)
            o_ref[:, off:off + w] = (x * r * g_ref[:, off:off + w]).astype(out_dtype)
            off += w

    return pl.pallas_call(
        body, name=name, grid=(s // tr,),
        in_specs=[pl.BlockSpec((tr, w), lambda i: (i, 0)) for w in widths] + [pl.BlockSpec((1, wsum), lambda i: (0, 0))],
        out_specs=pl.BlockSpec((tr, wsum), lambda i: (i, 0)),
        out_shape=jax.ShapeDtypeStruct((s, wsum), out_dtype),
        compiler_params=_params("parallel"),
    )(*xs, g)


def _rms_bwd(dy, xs, g, *, residual=None, head_dots=False, tr, name):
    s = xs[0].shape[0]
    widths = [x.shape[1] for x in xs]
    wsum = sum(widths)
    nx = len(xs)
    nin = 2 + nx + (residual is not None)

    def body(*refs):
        dy_ref, g_ref = refs[0], refs[1 + nx]
        dx_refs, dg_ref = refs[nin:nin + nx], refs[nin + nx]

        @pl.when(pl.program_id(0) == 0)
        def _():
            dg_ref[...] = jnp.zeros_like(dg_ref)

        off = 0
        for idx, (x_ref, w) in enumerate(zip(refs[1:1 + nx], widths)):
            x = x_ref[...]
            r = lax.rsqrt(jnp.mean(x * x, axis=-1, keepdims=True) + EPS)
            xh = x * r
            dyv = dy_ref[:, off:off + w]
            dxh = dyv * g_ref[:, off:off + w]
            dx = r * (dxh - xh * jnp.mean(dxh * xh, axis=-1, keepdims=True))
            if residual is not None and idx == 0:
                dx = dx + refs[2 + nx][...]
            dx_refs[idx][...] = dx
            dg_ref[:, off:off + w] += jnp.sum(dyv * xh, axis=0, keepdims=True)
            if head_dots and idx == 0:
                col = lax.broadcasted_iota(jnp.int32, (w, LANES), 0) // HEAD_DIM
                head = lax.broadcasted_iota(jnp.int32, (w, LANES), 1)
                refs[nin + nx + 1][...] = jnp.dot(dx * x, (col == head).astype(F32), precision=lax.Precision.HIGHEST,
                                                  preferred_element_type=F32)
            off += w

    in_specs = ([pl.BlockSpec((tr, wsum), lambda i: (i, 0))]
                + [pl.BlockSpec((tr, w), lambda i: (i, 0)) for w in widths]
                + [pl.BlockSpec((1, wsum), lambda i: (0, 0))])
    args = [dy, *xs, g]
    if residual is not None:
        in_specs.append(pl.BlockSpec((tr, widths[0]), lambda i: (i, 0)))
        args.append(residual)
    out_specs = [pl.BlockSpec((tr, w), lambda i: (i, 0)) for w in widths] + [pl.BlockSpec((1, wsum), lambda i: (0, 0))]
    out_shape = [jax.ShapeDtypeStruct((s, w), F32) for w in widths] + [jax.ShapeDtypeStruct((1, wsum), F32)]
    if head_dots:
        out_specs.append(pl.BlockSpec((tr, LANES), lambda i: (i, 0)))
        out_shape.append(jax.ShapeDtypeStruct((s, LANES), F32))
    outs = pl.pallas_call(
        body, name=name, grid=(s // tr,), in_specs=in_specs, out_specs=out_specs, out_shape=out_shape,
        compiler_params=_params("arbitrary"),
    )(*args)
    if head_dots:
        return outs[:nx], outs[nx], outs[nx + 1]
    return outs[:nx], outs[nx]


def _loss_head(x, tgt, g, *, tr, name):
    s, d = x.shape

    def body(x_ref, t_ref, g_ref, loss_ref, dx_ref, dg_ref):
        @pl.when(pl.program_id(0) == 0)
        def _():
            loss_ref[...] = jnp.zeros_like(loss_ref)
            dg_ref[...] = jnp.zeros_like(dg_ref)

        xv = x_ref[...]
        r = lax.rsqrt(jnp.mean(xv * xv, axis=-1, keepdims=True) + EPS)
        xh = xv * r
        err = xh * g_ref[...] - t_ref[...]
        loss_ref[...] += 0.5 * jnp.sum(jnp.mean(err * err, axis=-1, keepdims=True))
        dy = err * (1.0 / d)
        dxh = dy * g_ref[...]
        dx_ref[...] = r * (dxh - xh * jnp.mean(dxh * xh, axis=-1, keepdims=True))
        dg_ref[...] += jnp.sum(dy * xh, axis=0, keepdims=True)

    return pl.pallas_call(
        body, name=name, grid=(s // tr,),
        in_specs=[pl.BlockSpec((tr, d), lambda i: (i, 0)), pl.BlockSpec((tr, d), lambda i: (i, 0)),
                  pl.BlockSpec((1, d), lambda i: (0, 0))],
        out_specs=[pl.BlockSpec((8, LANES), lambda i: (0, 0)), pl.BlockSpec((tr, d), lambda i: (i, 0)),
                   pl.BlockSpec((1, d), lambda i: (0, 0))],
        out_shape=[jax.ShapeDtypeStruct((8, LANES), F32), jax.ShapeDtypeStruct((s, d), F32),
                   jax.ShapeDtypeStruct((1, d), F32)],
        compiler_params=_params("arbitrary"),
    )(x, tgt, g)


def _gates_fwd(h, bf_pad, *, tr, name):
    s = h.shape[0]

    def body(fl_ref, b_ref, c_ref, carry_ref):
        @pl.when(pl.program_id(0) == 0)
        def _():
            carry_ref[...] = jnp.zeros_like(carry_ref)

        lf = jax.nn.log_sigmoid(fl_ref[...] + b_ref[...])
        row = lax.broadcasted_iota(jnp.int32, (tr, tr), 0)
        col = lax.broadcasted_iota(jnp.int32, (tr, tr), 1)
        tri = (col <= row).astype(F32)
        c_ref[...] = jnp.dot(tri, lf, precision=lax.Precision.HIGHEST, preferred_element_type=F32) + carry_ref[...]
        carry_ref[...] += jnp.sum(lf, axis=0, keepdims=True)

    return pl.pallas_call(
        body, name=name, grid=(s // tr,),
        in_specs=[pl.BlockSpec((tr, LANES), lambda i: (i, F_COL_BLOCK)), pl.BlockSpec((1, LANES), lambda i: (0, 0))],
        out_specs=pl.BlockSpec((tr, LANES), lambda i: (i, 0)),
        out_shape=jax.ShapeDtypeStruct((s, LANES), F32),
        scratch_shapes=[pltpu.VMEM((1, LANES), F32)],
        compiler_params=_params("arbitrary"),
    )(h, bf_pad)


def _gates_bwd(dc, h, bf_pad, *, tr, name):
    s = h.shape[0]
    n = s // tr

    def body(dc_ref, fl_ref, b_ref, dfl_ref, db_ref, carry_ref):
        @pl.when(pl.program_id(0) == 0)
        def _():
            carry_ref[...] = jnp.zeros_like(carry_ref)
            db_ref[...] = jnp.zeros_like(db_ref)

        dcv = dc_ref[...]
        row = lax.broadcasted_iota(jnp.int32, (tr, tr), 0)
        col = lax.broadcasted_iota(jnp.int32, (tr, tr), 1)
        triu = (col >= row).astype(F32)
        dlf = jnp.dot(triu, dcv, precision=lax.Precision.HIGHEST, preferred_element_type=F32) + carry_ref[...]
        carry_ref[...] += jnp.sum(dcv, axis=0, keepdims=True)
        dfl = dlf * jax.nn.sigmoid(-(fl_ref[...] + b_ref[...]))
        dfl_ref[...] = dfl.astype(dfl_ref.dtype)
        db_ref[...] += jnp.sum(dfl, axis=0, keepdims=True)

    return pl.pallas_call(
        body, name=name, grid=(n,),
        in_specs=[pl.BlockSpec((tr, LANES), lambda i: (n - 1 - i, 0)),
                  pl.BlockSpec((tr, LANES), lambda i: (n - 1 - i, F_COL_BLOCK)),
                  pl.BlockSpec((1, LANES), lambda i: (0, 0))],
        out_specs=[pl.BlockSpec((tr, LANES), lambda i: (n - 1 - i, 0)), pl.BlockSpec((1, LANES), lambda i: (0, 0))],
        out_shape=[jax.ShapeDtypeStruct((s, LANES), BF16), jax.ShapeDtypeStruct((1, LANES), F32)],
        scratch_shapes=[pltpu.VMEM((1, LANES), F32)],
        compiler_params=_params("arbitrary"),
    )(dc, h, bf_pad)


LOG2E = 1.4426950408889634
LN2 = 0.6931471805599453
V_ROWS = 80


def _transpose_bf16(a):
    return a.astype(F32).T.astype(BF16)


SKIP_MARGIN = 160.0
SMEM = pl.BlockSpec(memory_space=pltpu.SMEM)


def _attn_fwd(qat, ka, va, gnorm, cmax_q, cmin_k, *, tq, kb, name):
    s = ka.shape[0]
    nq = s // tq
    per_tile = tq // kb
    tt = qat.shape[3]
    sub = tq // tt

    def body(gn_ref, cq_ref, ck_ref, qat_ref, ka_ref, va_ref, o_ref, lse_ref):
        h, i = pl.program_id(0), pl.program_id(1)
        qat = jnp.concatenate([qat_ref[0, d] for d in range(sub)], axis=1)

        def blk(n, carry, masked):
            m, acc = carry
            rows = pl.ds(pl.multiple_of(n * kb, kb), kb)
            sc = jnp.dot(ka_ref[rows, :], qat, preferred_element_type=F32)
            if masked:
                key = n * kb + lax.broadcasted_iota(jnp.int32, (kb, tq), 0)
                qry = i * tq + lax.broadcasted_iota(jnp.int32, (kb, tq), 1)
                sc = jnp.where(key <= qry, sc, NEG)
            m_new = jnp.maximum(m, jnp.max(sc, axis=0, keepdims=True))
            p = jnp.exp2(sc - m_new)
            vt = _transpose_bf16(va_ref[rows, :])[:V_ROWS]
            acc = jnp.exp2(m - m_new) * acc + jnp.dot(vt, p.astype(BF16), preferred_element_type=F32)
            return m_new, acc

        carry = (jnp.full((1, tq), NEG, F32), jnp.zeros((V_ROWS, tq), F32))
        for d in reversed(range(per_tile)):
            carry = blk(i * per_tile + d, carry, True)
        top = gn_ref[h] + cq_ref[h, i]

        def live(state):
            n, m_min = state[0], state[1]
            return jnp.logical_and(n >= 0, top - ck_ref[h, jnp.maximum(n, 0)] >= m_min - SKIP_MARGIN)

        def step(state):
            n, _, m, acc = state
            m, acc = blk(n, (m, acc), False)
            return n - 1, jnp.min(m), m, acc

        _, _, m, acc = lax.while_loop(live, step, (i * per_tile - 1, jnp.min(carry[0]), *carry))
        l = acc[HEAD_DIM:HEAD_DIM + 1, :]
        padded = jnp.concatenate([acc / l, jnp.zeros((LANES - V_ROWS, tq), F32)], axis=0)
        o_ref[0, 0] = padded.T[:, :HEAD_DIM]
        lse_ref[0, 0] = m + jnp.log2(l)

    return pl.pallas_call(
        body, name=name, grid=(N_HEADS, nq),
        in_specs=[SMEM, SMEM, SMEM, pl.BlockSpec((1, sub, LANES, tt), lambda h, i: (h, i, 0, 0)),
                  pl.BlockSpec((s, LANES), lambda h, i: (0, h)),
                  pl.BlockSpec((s, LANES), lambda h, i: (0, h))],
        out_specs=[pl.BlockSpec((1, 1, tq, HEAD_DIM), lambda h, i: (h, i, 0, 0)),
                   pl.BlockSpec((1, 1, 1, tq), lambda h, i: (h, i, 0, 0))],
        out_shape=[jax.ShapeDtypeStruct((N_HEADS, nq, tq, HEAD_DIM), F32), jax.ShapeDtypeStruct((N_HEADS, nq, 1, tq), F32)],
        compiler_params=_params("parallel", "arbitrary"),
    )(gnorm, cmax_q, cmin_k, qat, ka, va)


def _attn_bwd(qa, qat, doa, dot, lse, dl, ka, va, gnorm, cmin_k, reach, *, name):
    s = qa.shape[0]
    nt, t = qat.shape[1], qat.shape[3]

    def body(gn_ref, ck_ref, reach_ref, qa_ref, qat_ref, do_ref, dot_ref, lse_ref, dl_ref, ka_ref, va_ref,
             dq_ref, dk_ref, dv_ref):
        h, j = pl.program_id(0), pl.program_id(1)

        @pl.when(j == 0)
        def _():
            dq_ref[...] = jnp.zeros_like(dq_ref)

        ka_j, va_j = ka_ref[...], va_ref[...]

        def tile(i, carry, masked):
            dk, dv = carry
            rows = pl.ds(pl.multiple_of(i * t, t), t)
            qa_i, do_i = qa_ref[rows, :], do_ref[rows, :]
            st = jnp.dot(ka_j, qat_ref[0, i], preferred_element_type=F32) - lse_ref[0, i]
            if masked:
                key = lax.broadcasted_iota(jnp.int32, (t, t), 0)
                qry = lax.broadcasted_iota(jnp.int32, (t, t), 1)
                st = jnp.where(key <= qry, st, NEG)
            pt = jnp.exp2(st)
            dpt = jnp.dot(va_j, dot_ref[0, i], preferred_element_type=F32)
            dsb = (pt * (dpt - dl_ref[0, i])).astype(BF16)
            dv = dv + jnp.dot(pt.astype(BF16), do_i, preferred_element_type=F32)
            dk = dk + jnp.dot(dsb, qa_i, preferred_element_type=F32)
            dq_ref[rows, :] += lax.dot_general(dsb, ka_j, (((0,), (0,)), ((), ())), preferred_element_type=F32)
            return dk, dv

        carry = tile(j, (jnp.zeros((t, LANES), F32), jnp.zeros((t, LANES), F32)), True)
        base = gn_ref[h] - ck_ref[h, j]

        def live(state):
            i = state[0]
            return jnp.logical_and(i < nt, base + reach_ref[h, jnp.minimum(i, nt - 1)] >= -SKIP_MARGIN)

        def step(state):
            i, dk, dv = state
            dk, dv = tile(i, (dk, dv), False)
            return i + 1, dk, dv

        _, dk, dv = lax.while_loop(live, step, (j + 1, *carry))
        dk_ref[...] = dk
        dv_ref[...] = dv

    res = pl.BlockSpec((s, LANES), lambda h, j: (0, h))
    rest = pl.BlockSpec((1, nt, LANES, t), lambda h, j: (h, 0, 0, 0))
    row = pl.BlockSpec((1, nt, 1, t), lambda h, j: (h, 0, 0, 0))
    blk = pl.BlockSpec((t, LANES), lambda h, j: (j, h))
    shape = jax.ShapeDtypeStruct((s, N_HEADS * LANES), F32)
    return pl.pallas_call(
        body, name=name, grid=(N_HEADS, nt),
        in_specs=[SMEM, SMEM, SMEM, res, rest, res, rest, row, row, blk, blk], out_specs=[res, blk, blk],
        out_shape=[shape, shape, shape], compiler_params=_params("parallel", "arbitrary"),
    )(gnorm, cmin_k, reach, qa, qat, doa, dot, lse, dl, ka, va)


def _spread_matrix():
    r = lax.broadcasted_iota(jnp.int32, (D_HALF, N_HEADS * LANES), 0)
    c = lax.broadcasted_iota(jnp.int32, (D_HALF, N_HEADS * LANES), 1)
    return jnp.logical_and(c // LANES == r // HEAD_DIM, c % LANES == r % HEAD_DIM).astype(BF16)


def _piece_matrix(base):
    r = lax.broadcasted_iota(jnp.int32, (3 * LANES, N_HEADS * LANES), 0)
    c = lax.broadcasted_iota(jnp.int32, (3 * LANES, N_HEADS * LANES), 1)
    return jnp.logical_and(r % LANES < N_HEADS, c == LANES * (r % LANES) + base + r // LANES).astype(BF16)


def _ones_columns(first, count):
    c = lax.broadcasted_iota(jnp.int32, (1, N_HEADS * LANES), 1) % LANES
    return jnp.logical_and(c >= first, c < first + count).astype(F32)


def _round_bf16(x):
    return x.astype(BF16).astype(F32)


def _write_transposed(a, out_ref):
    for h in range(N_HEADS):
        out_ref[h, 0] = _transpose_bf16(a[:, h * LANES:(h + 1) * LANES])


def _pack_qkv(h, c, *, tr, name):
    s = h.shape[0]
    wide = N_HEADS * LANES

    def body(q_ref, k_ref, v_ref, c_ref, qa_ref, ka_ref, va_ref, qat_ref):
        spread = _spread_matrix()
        c2 = c_ref[...] * LOG2E
        p1 = _round_bf16(c2)
        p2 = _round_bf16(c2 - p1)
        p3 = _round_bf16(c2 - p1 - p2)
        pieces = jnp.concatenate([p1, p2, p3], axis=1).astype(BF16)
        dot = lambda a, b: jnp.dot(a, b, preferred_element_type=F32)
        qa = (dot((q_ref[...] * 0.125).astype(BF16), spread) + dot(pieces, _piece_matrix(HEAD_DIM + 3))
              + _ones_columns(HEAD_DIM, 3)).astype(BF16)
        ka = (dot((k_ref[...] * LOG2E).astype(BF16), spread) - dot(pieces, _piece_matrix(HEAD_DIM))
              + _ones_columns(HEAD_DIM + 3, 3)).astype(BF16)
        qa_ref[...] = qa
        ka_ref[...] = ka
        va_ref[...] = (dot(v_ref[...].astype(BF16), spread) + _ones_columns(HEAD_DIM, 1)).astype(BF16)
        _write_transposed(qa, qat_ref)

    shape = jax.ShapeDtypeStruct((s, wide), BF16)
    return pl.pallas_call(
        body, name=name, grid=(s // tr,),
        in_specs=[pl.BlockSpec((tr, D_HALF), lambda i: (i, 0)), pl.BlockSpec((tr, D_HALF), lambda i: (i, 1)),
                  pl.BlockSpec((tr, D_HALF), lambda i: (i, 2)), pl.BlockSpec((tr, LANES), lambda i: (i, 0))],
        out_specs=[pl.BlockSpec((tr, wide), lambda i: (i, 0))] * 3 + [pl.BlockSpec((N_HEADS, 1, LANES, tr), lambda i: (0, i, 0, 0))],
        out_shape=[shape, shape, shape, jax.ShapeDtypeStruct((N_HEADS, s // tr, LANES, tr), BF16)],
        compiler_params=_params("parallel"),
    )(h, h, h, c)


def _pack_do(dattn, *, tr, name):
    s = dattn.shape[0]
    wide = N_HEADS * LANES

    def body(d_ref, doa_ref, dot_ref):
        doa = jnp.dot(d_ref[...].astype(BF16), _spread_matrix(), preferred_element_type=F32).astype(BF16)
        doa_ref[...] = doa
        _write_transposed(doa, dot_ref)

    return pl.pallas_call(
        body, name=name, grid=(s // tr,),
        in_specs=[pl.BlockSpec((tr, D_HALF), lambda i: (i, 0))],
        out_specs=[pl.BlockSpec((tr, wide), lambda i: (i, 0)), pl.BlockSpec((N_HEADS, 1, LANES, tr), lambda i: (0, i, 0, 0))],
        out_shape=[jax.ShapeDtypeStruct((s, wide), BF16), jax.ShapeDtypeStruct((N_HEADS, s // tr, LANES, tr), BF16)],
        compiler_params=_params("parallel"),
    )(dattn)


def _unpack_grads(dqa, dka, dva, *, tr, name):
    s = dqa.shape[0]
    wide = N_HEADS * LANES

    def body(dq_ref, dk_ref, dv_ref, o_ref, dc_ref):
        gather = _spread_matrix()
        nt_dims = (((1,), (1,)), ((), ()))
        pick = lambda a: lax.dot_general(a.astype(BF16), gather, nt_dims, preferred_element_type=F32).astype(BF16)
        dq, dk = dq_ref[...], dk_ref[...]
        o_ref[:, 0:D_HALF] = pick(dq * (LN2 * 0.125))
        o_ref[:, D_HALF:2 * D_HALF] = pick(dk)
        o_ref[:, 2 * D_HALF:3 * D_HALF] = pick(dv_ref[...])
        lane = lax.broadcasted_iota(jnp.int32, (1, wide), 1) % LANES
        both = jnp.where(lane == HEAD_DIM + 3, dq, 0.0) - jnp.where(lane == HEAD_DIM, dk, 0.0)
        r = lax.broadcasted_iota(jnp.int32, (wide, LANES), 0)
        head = lax.broadcasted_iota(jnp.int32, (wide, LANES), 1)
        dc_ref[...] = jnp.dot(both, (r // LANES == head).astype(F32), precision=lax.Precision.HIGHEST,
                              preferred_element_type=F32)

    spec = pl.BlockSpec((tr, wide), lambda i: (i, 0))
    return pl.pallas_call(
        body, name=name, grid=(s // tr,), in_specs=[spec, spec, spec],
        out_specs=[pl.BlockSpec((tr, 3 * D_HALF), lambda i: (i, 0)), pl.BlockSpec((tr, LANES), lambda i: (i, 0))],
        out_shape=[jax.ShapeDtypeStruct((s, 3 * D_HALF), BF16), jax.ShapeDtypeStruct((s, LANES), F32)],
        compiler_params=_params("parallel"),
    )(dqa, dka, dva)


def _attention_fwd(h, c, *, tq, kb, t, name):
    s = h.shape[0]
    qa, ka, va, qat = _pack_qkv(h, c, tr=t, name=name + "_pack")
    c2 = c[:, :N_HEADS] * LOG2E
    head_norm = lambda a: jnp.sqrt(jnp.max(jnp.sum(jnp.square(a.reshape(s, N_HEADS, HEAD_DIM)), axis=-1), axis=0))
    gnorm = head_norm(h[:, 0:D_HALF] * 0.125) * head_norm(h[:, D_HALF:2 * D_HALF] * LOG2E) * 1.01 + 1.0
    cmax_q = jnp.max(c2.reshape(s // tq, tq, N_HEADS), axis=1).T
    cmin_k = lax.cummin(jnp.min(c2.reshape(s // kb, kb, N_HEADS), axis=1), axis=0).T
    o4, lse2 = _attn_fwd(qat, ka, va, gnorm, cmax_q, cmin_k, tq=tq, kb=kb, name=name)
    attn = o4.reshape(N_HEADS, s, HEAD_DIM).transpose(1, 0, 2).reshape(s, N_HEADS * HEAD_DIM)
    return attn, dict(qa=qa, qat=qat, ka=ka, va=va, lse2=lse2, c2=c2, gnorm=gnorm)


def _attention_bwd(dattn, delta, res, *, name):
    s = dattn.shape[0]
    t = res["qat"].shape[3]
    doa, dot = _pack_do(dattn, tr=t, name=name + "_pack")
    c2t = res["c2"].reshape(s // t, t, N_HEADS)
    lse_t = res["lse2"].reshape(N_HEADS, s // t, t)
    reach = lax.cummax(jnp.max(c2t.transpose(2, 0, 1) - lse_t, axis=2), axis=1, reverse=True)
    dqa, dka, dva = _attn_bwd(res["qa"], res["qat"], doa, dot, lse_t.reshape(N_HEADS, s // t, 1, t),
                              delta.T.reshape(N_HEADS, s // t, 1, t), res["ka"], res["va"], res["gnorm"],
                              jnp.min(c2t, axis=1).T, reach, name=name)
    return _unpack_grads(dqa, dka, dva, tr=t, name=name + "_unpack")


def _gelu(z):
    return 0.5 * z * (1.0 + lax.erf(z * 0.7071067811865476))


def _gelu_grad(z):
    return 0.5 * (1.0 + lax.erf(z * 0.7071067811865476)) + z * (0.3989422804014327 * jnp.exp(-0.5 * z * z))


def _sgu_mask():
    i = lax.broadcasted_iota(jnp.int32, (SGU_CHUNK, SGU_CHUNK), 0) // CAUSAL_CHUNK
    j = lax.broadcasted_iota(jnp.int32, (SGU_CHUNK, SGU_CHUNK), 1) // CAUSAL_CHUNK
    return (j <= i).astype(F32)


def _layernorm_stats(x):
    mu = jnp.mean(x, axis=-1, keepdims=True)
    xc = x - mu
    rstd = lax.rsqrt(jnp.mean(xc * xc, axis=-1, keepdims=True) + EPS)
    return xc * rstd, rstd


def _first_group_lanes():
    return lax.broadcasted_iota(jnp.int32, (SGU_CHUNK, LANES), 1) < 64


def _sgu_fwd(h, ln_g, ln_b, w_s, bias_tile, *, tr, name):
    s = h.shape[0]
    zu_blk, zv_blk = 1536 // D_HALF, 2048 // D_HALF

    def body(zu_ref, zv_ref, lng_ref, lnb_ref, ws_ref, bias_ref, o_ref):
        gzu = _gelu(zu_ref[...])
        xh, _ = _layernorm_stats(_gelu(zv_ref[...]))
        zb = (xh * lng_ref[...] + lnb_ref[...]).astype(BF16)
        mask = _sgu_mask()
        first = _first_group_lanes()
        for pair in range(4):
            cols = slice(pair * LANES, (pair + 1) * LANES)
            w0 = (ws_ref[2 * pair] * mask).astype(BF16)
            w1 = (ws_ref[2 * pair + 1] * mask).astype(BF16)
            for ch in range(tr // SGU_CHUNK):
                rows = slice(ch * SGU_CHUNK, (ch + 1) * SGU_CHUNK)
                zp = zb[rows, cols]
                mixed = jnp.where(first, jnp.dot(w0, zp, preferred_element_type=F32),
                                  jnp.dot(w1, zp, preferred_element_type=F32)) + bias_ref[:, cols]
                o_ref[rows, cols] = gzu[rows, cols] * mixed

    return pl.pallas_call(
        body, name=name, grid=(s // tr,),
        in_specs=[pl.BlockSpec((tr, D_HALF), lambda i: (i, zu_blk)), pl.BlockSpec((tr, D_HALF), lambda i: (i, zv_blk)),
                  pl.BlockSpec((1, D_HALF), lambda i: (0, 0)), pl.BlockSpec((1, D_HALF), lambda i: (0, 0)),
                  pl.BlockSpec((N_HEADS, SGU_CHUNK, SGU_CHUNK), lambda i: (0, 0, 0)),
                  pl.BlockSpec((SGU_CHUNK, D_HALF), lambda i: (0, 0))],
        out_specs=pl.BlockSpec((tr, D_HALF), lambda i: (i, 0)),
        out_shape=jax.ShapeDtypeStruct((s, D_HALF), F32),
        compiler_params=_params("parallel"),
    )(h, h, ln_g, ln_b, w_s, bias_tile)


def _sgu_bwd(dsgu, h, ln_g, ln_b, w_s, bias_tile, *, tr, name):
    s = h.shape[0]
    n = s // tr
    zu_blk, zv_blk = 1536 // D_HALF, 2048 // D_HALF

    def body(ds_ref, zu_ref, zv_ref, lng_ref, lnb_ref, ws_ref, bias_ref,
             dzu_ref, dzv_ref, dws_ref, dlng_ref, dlnb_ref, dbs_ref, dgzu_sc, dzvn_sc, dbias_sc):
        step = pl.program_id(0)

        @pl.when(step == 0)
        def _():
            dws_ref[...] = jnp.zeros_like(dws_ref)
            dlng_ref[...] = jnp.zeros_like(dlng_ref)
            dlnb_ref[...] = jnp.zeros_like(dlnb_ref)
            dbias_sc[...] = jnp.zeros_like(dbias_sc)

        zu = zu_ref[...]
        zv = zv_ref[...]
        gzu = _gelu(zu)
        xh, rstd = _layernorm_stats(_gelu(zv))
        zb = (xh * lng_ref[...] + lnb_ref[...]).astype(BF16)
        ds = ds_ref[...]
        mask = _sgu_mask()
        first = _first_group_lanes()
        tn_dims = (((0,), (0,)), ((), ()))
        nt_dims = (((1,), (1,)), ((), ()))
        for pair in range(4):
            cols = slice(pair * LANES, (pair + 1) * LANES)
            w0 = (ws_ref[2 * pair] * mask).astype(BF16)
            w1 = (ws_ref[2 * pair + 1] * mask).astype(BF16)
            for ch in range(tr // SGU_CHUNK):
                rows = slice(ch * SGU_CHUNK, (ch + 1) * SGU_CHUNK)
                zp = zb[rows, cols]
                mixed = jnp.where(first, jnp.dot(w0, zp, preferred_element_type=F32),
                                  jnp.dot(w1, zp, preferred_element_type=F32)) + bias_ref[:, cols]
                dsp = ds[rows, cols]
                dgzu_sc[rows, cols] = dsp * mixed
                dm = dsp * gzu[rows, cols]
                dbias_sc[:, cols] += dm
                dmb = dm.astype(BF16)
                dm0 = jnp.where(first, dmb, jnp.zeros_like(dmb))
                dm1 = jnp.where(first, jnp.zeros_like(dmb), dmb)
                dws_ref[2 * pair] += lax.dot_general(dm0, zp, nt_dims, preferred_element_type=F32)
                dws_ref[2 * pair + 1] += lax.dot_general(dm1, zp, nt_dims, preferred_element_type=F32)
                dzvn_sc[rows, cols] = jnp.where(first, lax.dot_general(w0, dmb, tn_dims, preferred_element_type=F32),
                                                lax.dot_general(w1, dmb, tn_dims, preferred_element_type=F32))
        dzvn = dzvn_sc[...]
        dlng_ref[...] += jnp.sum(dzvn * xh, axis=0, keepdims=True)
        dlnb_ref[...] += jnp.sum(dzvn, axis=0, keepdims=True)
        dxh = dzvn * lng_ref[...]
        dgzv = rstd * (dxh - jnp.mean(dxh, axis=-1, keepdims=True) - xh * jnp.mean(dxh * xh, axis=-1, keepdims=True))
        dzv_ref[...] = (dgzv * _gelu_grad(zv)).astype(dzv_ref.dtype)
        dzu_ref[...] = (dgzu_sc[...] * _gelu_grad(zu)).astype(dzu_ref.dtype)

        @pl.when(step == n - 1)
        def _():
            for g in range(N_HEADS):
                dws_ref[g] = dws_ref[g] * mask
            lane = lax.broadcasted_iota(jnp.int32, (D_HALF, LANES), 0) // 64
            grp = lax.broadcasted_iota(jnp.int32, (D_HALF, LANES), 1)
            dbs_ref[...] = jnp.dot(dbias_sc[...], (lane == grp).astype(F32), precision=lax.Precision.HIGHEST,
                                   preferred_element_type=F32)

    const2 = lambda i: (0, 0)
    return pl.pallas_call(
        body, name=name, grid=(n,),
        in_specs=[pl.BlockSpec((tr, D_HALF), lambda i: (i, 0)),
                  pl.BlockSpec((tr, D_HALF), lambda i: (i, zu_blk)), pl.BlockSpec((tr, D_HALF), lambda i: (i, zv_blk)),
                  pl.BlockSpec((1, D_HALF), const2), pl.BlockSpec((1, D_HALF), const2),
                  pl.BlockSpec((N_HEADS, SGU_CHUNK, SGU_CHUNK), lambda i: (0, 0, 0)),
                  pl.BlockSpec((SGU_CHUNK, D_HALF), const2)],
        out_specs=[pl.BlockSpec((tr, D_HALF), lambda i: (i, 0)), pl.BlockSpec((tr, D_HALF), lambda i: (i, 0)),
                   pl.BlockSpec((N_HEADS, SGU_CHUNK, SGU_CHUNK), lambda i: (0, 0, 0)),
                   pl.BlockSpec((1, D_HALF), const2), pl.BlockSpec((1, D_HALF), const2),
                   pl.BlockSpec((SGU_CHUNK, LANES), const2)],
        out_shape=[jax.ShapeDtypeStruct((s, D_HALF), BF16), jax.ShapeDtypeStruct((s, D_HALF), BF16),
                   jax.ShapeDtypeStruct((N_HEADS, SGU_CHUNK, SGU_CHUNK), F32),
                   jax.ShapeDtypeStruct((1, D_HALF), F32), jax.ShapeDtypeStruct((1, D_HALF), F32),
                   jax.ShapeDtypeStruct((SGU_CHUNK, LANES), F32)],
        scratch_shapes=[pltpu.VMEM((tr, D_HALF), F32), pltpu.VMEM((tr, D_HALF), F32), pltpu.VMEM((SGU_CHUNK, D_HALF), F32)],
        compiler_params=_params("arbitrary"),
    )(dsgu, h, h, ln_g, ln_b, w_s, bias_tile)


FF_BLOCK = D_FF // 2
DSWIGLU_CHUNK = 384


def _matmul_swiglu(a, w, *, tm, name):
    s, k = a.shape

    def body(a_ref, wg_ref, wu_ref, g_ref, u_ref, act_ref):
        av = a_ref[...].astype(BF16)
        g = jnp.dot(av, wg_ref[...].astype(BF16), preferred_element_type=F32)
        u = jnp.dot(av, wu_ref[...].astype(BF16), preferred_element_type=F32)
        g_ref[...] = g.astype(g_ref.dtype)
        u_ref[...] = u.astype(u_ref.dtype)
        act_ref[...] = (g * jax.nn.sigmoid(g) * u).astype(act_ref.dtype)

    out = pl.BlockSpec((tm, FF_BLOCK), lambda i, j: (i, j))
    return pl.pallas_call(
        body, name=name, grid=(s // tm, 2),
        in_specs=[pl.BlockSpec((tm, k), lambda i, j: (i, 0)), pl.BlockSpec((k, FF_BLOCK), lambda i, j: (0, j)),
                  pl.BlockSpec((k, FF_BLOCK), lambda i, j: (0, j + 2))],
        out_specs=[out, out, out],
        out_shape=[jax.ShapeDtypeStruct((s, D_FF), BF16)] * 3,
        compiler_params=_params("parallel", "parallel"),
    )(a, w, w)


def _matmul_dswiglu(dx, w_down, gate, up, *, tm, name):
    s, k = dx.shape

    def body(dx_ref, w_ref, g_ref, u_ref, dg_ref, du_ref):
        dxb = dx_ref[...].astype(BF16)
        for lo in range(0, FF_BLOCK, DSWIGLU_CHUNK):
            cols = slice(lo, min(lo + DSWIGLU_CHUNK, FF_BLOCK))
            d = lax.dot_general(dxb, w_ref[cols, :].astype(BF16), (((1,), (1,)), ((), ())), preferred_element_type=F32)
            g = g_ref[:, cols].astype(F32)
            sig = jax.nn.sigmoid(g)
            dg_ref[:, cols] = (d * u_ref[:, cols].astype(F32) * (sig * (1.0 + g * (1.0 - sig)))).astype(dg_ref.dtype)
            du_ref[:, cols] = (d * (g * sig)).astype(du_ref.dtype)

    blk = pl.BlockSpec((tm, FF_BLOCK), lambda i, j: (i, j))
    return pl.pallas_call(
        body, name=name, grid=(s // tm, 2),
        in_specs=[pl.BlockSpec((tm, k), lambda i, j: (i, 0)), pl.BlockSpec((FF_BLOCK, k), lambda i, j: (j, 0)), blk, blk],
        out_specs=[blk, blk], out_shape=[jax.ShapeDtypeStruct((s, D_FF), BF16)] * 2,
        compiler_params=_params("parallel", "parallel"),
    )(dx, w_down, gate, up)


def _matmul_gu_dx(dgate, dup, w_gu, *, tm, name):
    s = dgate.shape[0]
    k = w_gu.shape[0]
    nt_dims = (((1,), (1,)), ((), ()))

    def body(dg_ref, du_ref, wg_ref, wu_ref, o_ref):
        o_ref[...] = (lax.dot_general(dg_ref[...], wg_ref[...], nt_dims, preferred_element_type=F32)
                      + lax.dot_general(du_ref[...], wu_ref[...], nt_dims, preferred_element_type=F32))

    return pl.pallas_call(
        body, name=name, grid=(s // tm,),
        in_specs=[pl.BlockSpec((tm, D_FF), lambda i: (i, 0)), pl.BlockSpec((tm, D_FF), lambda i: (i, 0)),
                  pl.BlockSpec((k, D_FF), lambda i: (0, 0)), pl.BlockSpec((k, D_FF), lambda i: (0, 1))],
        out_specs=pl.BlockSpec((tm, k), lambda i: (i, 0)),
        out_shape=jax.ShapeDtypeStruct((s, k), F32),
        compiler_params=_params("parallel"),
    )(dgate, dup, w_gu, w_gu)


def _sum_slots(stacked, *, tr, name):
    k, r, _ = stacked.shape

    def body(x_ref, o_ref):
        acc = x_ref[0].astype(F32)
        for idx in range(1, k):
            acc = acc + x_ref[idx].astype(F32)
        o_ref[...] = acc

    return pl.pallas_call(
        body, name=name, grid=(r // tr,),
        in_specs=[pl.BlockSpec((k, tr, LANES), lambda i: (0, i, 0))],
        out_specs=pl.BlockSpec((tr, LANES), lambda i: (i, 0)),
        out_shape=jax.ShapeDtypeStruct((r, LANES), F32),
        compiler_params=_params("parallel"),
    )(stacked)


def _adamw(w, g, m, v, *, tr, name):
    r = w.shape[0]

    def body(w_ref, g_ref, m_ref, v_ref, d_ref, m2_ref, v2_ref):
        gv = g_ref[...]
        m2 = ADAM_B1 * m_ref[...] + (1.0 - ADAM_B1) * gv
        v2 = ADAM_B2 * v_ref[...] + (1.0 - ADAM_B2) * jnp.square(gv)
        m_hat = m2 / (1.0 - ADAM_B1 ** ADAM_STEP)
        v_hat = v2 / (1.0 - ADAM_B2 ** ADAM_STEP)
        d_ref[...] = -ADAM_LR * (m_hat / (jnp.sqrt(v_hat) + ADAM_EPS) + ADAM_WD * w_ref[...])
        m2_ref[...] = m2
        v2_ref[...] = v2

    spec = pl.BlockSpec((tr, LANES), lambda i: (i, 0))
    shape = jax.ShapeDtypeStruct((r, LANES), F32)
    return pl.pallas_call(
        body, name=name, grid=(r // tr,), in_specs=[spec] * 4, out_specs=[spec] * 3, out_shape=[shape] * 3,
        compiler_params=_params("parallel"),
    )(w, g, m, v)


PAIR_CHUNKS = 5


def _coords():
    return lax.axis_index("x"), lax.axis_index("y"), lax.axis_index("c")


def _my_chip():
    return 2 * lax.axis_index("x") + lax.axis_index("y")


def _chip_peer(x, y, k):
    px = 1 - x if k & 2 else x
    py = 1 - y if k & 1 else y
    return px, py


def _allgather_chips(shard, *, name):
    r = shard.shape[0]
    rh = r // 2

    def body(src, out, send_sems, recv_sems):
        x, y, c = _coords()
        me = 2 * x + y
        half = pl.ds(c * rh, rh)

        def copy(k, src_ref, slot, to):
            return pltpu.make_async_remote_copy(src_ref=src_ref, dst_ref=out.at[slot, half, :], send_sem=send_sems.at[k],
                                                recv_sem=recv_sems.at[k], device_id=to, device_id_type=MESH)

        first, passed = [], []
        for k in (1, 2, 3):
            px, py = _chip_peer(x, y, k)
            first.append(copy(k - 1, src.at[half, :], me, (px, py, c)))
            first[-1].start()
        for k in (1, 2, 3):
            px, py = _chip_peer(x, y, k)
            slot = 2 * px + py
            first[k - 1].wait_recv()
            passed.append(copy(2 + k, out.at[slot, half, :], slot, (x, y, 1 - c)))
            passed[-1].start()
        for cp in passed:
            cp.wait_recv()
        for cp in first + passed:
            cp.wait_send()

    gathered = pl.pallas_call(
        body, name=name, in_specs=[ANY], out_specs=ANY,
        out_shape=jax.ShapeDtypeStruct((4, r, LANES), shard.dtype),
        scratch_shapes=[pltpu.SemaphoreType.DMA((6,)), pltpu.SemaphoreType.DMA((6,))],
    )(shard)
    return lax.dynamic_update_slice(gathered, shard[None], (_my_chip(), 0, 0))


def _pair_split(grads, *, name):
    _, r, _ = grads.shape
    rh = r // 2
    rc = rh // PAIR_CHUNKS
    nchunk = 4 * PAIR_CHUNKS

    def body(g_ref, theirs_ref, send_sems, recv_sems):
        x, y, c = _coords()
        copies = []
        for j in range(4):
            for q in range(PAIR_CHUNKS):
                idx = j * PAIR_CHUNKS + q
                cp = pltpu.make_async_remote_copy(
                    src_ref=g_ref.at[j, pl.ds((1 - c) * rh + q * rc, rc), :], dst_ref=theirs_ref.at[j, pl.ds(q * rc, rc), :],
                    send_sem=send_sems.at[idx], recv_sem=recv_sems.at[idx], device_id=(x, y, 1 - c), device_id_type=MESH)
                cp.start()
                copies.append(cp)
        for cp in copies:
            cp.wait()

    return pl.pallas_call(
        body, name=name, in_specs=[ANY], out_specs=ANY, out_shape=jax.ShapeDtypeStruct((4, rh, LANES), F32),
        scratch_shapes=[pltpu.SemaphoreType.DMA((nchunk,)), pltpu.SemaphoreType.DMA((nchunk,))],
    )(grads)


def _pair_sum(grads, theirs, half, *, tr, name):
    _, rh, _ = theirs.shape
    nrt = rh // tr

    def body(half_ref, g_ref, t_ref, o_ref):
        o_ref[...] = (g_ref[...] + t_ref[...]).astype(o_ref.dtype)

    return pl.pallas_call(
        body, name=name,
        grid_spec=pltpu.PrefetchScalarGridSpec(
            num_scalar_prefetch=1, grid=(4, nrt),
            in_specs=[pl.BlockSpec((1, tr, LANES), lambda j, i, half_ref: (j, half_ref[0] * nrt + i, 0)),
                      pl.BlockSpec((1, tr, LANES), lambda j, i, half_ref: (j, i, 0))],
            out_specs=pl.BlockSpec((1, tr, LANES), lambda j, i, half_ref: (j, i, 0))),
        out_shape=jax.ShapeDtypeStruct((4, rh, LANES), BF16),
        compiler_params=_params("parallel", "parallel"),
    )(half, grads, theirs)


def _scatter_chips(part, *, name):
    _, rh, _ = part.shape

    def body(p_ref, out, send_sems, recv_sems):
        x, y, c = _coords()
        me = 2 * x + y
        copies = []
        for k in (1, 2, 3):
            px, py = _chip_peer(x, y, k)
            cp = pltpu.make_async_remote_copy(src_ref=p_ref.at[2 * px + py], dst_ref=out.at[me], send_sem=send_sems.at[k - 1],
                                              recv_sem=recv_sems.at[k - 1], device_id=(px, py, c), device_id_type=MESH)
            cp.start()
            copies.append(cp)
        for cp in copies:
            cp.wait()

    from_chips = pl.pallas_call(
        body, name=name, in_specs=[ANY], out_specs=ANY, out_shape=jax.ShapeDtypeStruct((4, rh, LANES), part.dtype),
        scratch_shapes=[pltpu.SemaphoreType.DMA((3,)), pltpu.SemaphoreType.DMA((3,))],
    )(part)
    own = lax.dynamic_index_in_dim(part, _my_chip(), axis=0, keepdims=True)
    return lax.dynamic_update_slice(from_chips, own, (_my_chip(), 0, 0))


def _pair_join(half, *, name):
    rh = half.shape[0]
    nchunk = 2 * PAIR_CHUNKS
    rc = rh // nchunk

    def body(h_ref, out, send_sems, recv_sems):
        x, y, c = _coords()
        copies = []
        for q in range(nchunk):
            src = h_ref.at[pl.ds(q * rc, rc), :]
            rows = out.at[pl.ds(c * rh + q * rc, rc), :]
            cp = pltpu.make_async_remote_copy(src_ref=src, dst_ref=rows, send_sem=send_sems.at[q], recv_sem=recv_sems.at[q],
                                              device_id=(x, y, 1 - c), device_id_type=MESH)
            cp.start()
            copies.append(cp)
        for cp in copies:
            cp.wait()

    joined = pl.pallas_call(
        body, name=name, in_specs=[ANY], out_specs=ANY, out_shape=jax.ShapeDtypeStruct((2 * rh, LANES), F32),
        scratch_shapes=[pltpu.SemaphoreType.DMA((nchunk,)), pltpu.SemaphoreType.DMA((nchunk,))],
    )(half)
    return lax.dynamic_update_slice(joined, half, (lax.axis_index("c") * rh, 0))


def _allgather_all(block, *, name):
    r = block.shape[0]

    def body(src, out, send_sems, recv_sems):
        x, y, c = _coords()
        me = 4 * x + 2 * y + c
        copies = []
        for k in range(1, 8):
            px, py = _chip_peer(x, y, k >> 1)
            pc = 1 - c if k & 1 else c
            cp = pltpu.make_async_remote_copy(src_ref=src, dst_ref=out.at[me], send_sem=send_sems.at[k - 1],
                                              recv_sem=recv_sems.at[k - 1], device_id=(px, py, pc), device_id_type=MESH)
            cp.start()
            copies.append(cp)
        for cp in copies:
            cp.wait()

    gathered = pl.pallas_call(
        body, name=name, in_specs=[ANY], out_specs=ANY, out_shape=jax.ShapeDtypeStruct((8, r, LANES), F32),
        scratch_shapes=[pltpu.SemaphoreType.DMA((7,)), pltpu.SemaphoreType.DMA((7,))],
    )(block)
    return lax.dynamic_update_slice(gathered, block[None], (2 * _my_chip() + lax.axis_index("c"), 0, 0))


def _flatten(arrays, pad_rows=None):
    flat = jnp.concatenate([a.reshape(-1) for a in arrays])
    if pad_rows is not None:
        flat = jnp.pad(flat, (0, pad_rows * LANES - flat.shape[0]))
    return flat.reshape(-1, LANES)


def _unflatten(flat, shapes):
    flat = flat.reshape(-1)
    out, off = [], 0
    for shp in shapes:
        size = 1
        for dim in shp:
            size *= dim
        out.append(flat[off:off + size].reshape(shp))
        off += size
    return out


def _pad_w_in(w):
    pad = jnp.zeros(w.shape[:-1] + (D_IN_PAD - D_IN,), w.dtype)
    return jnp.concatenate([w[..., :1536], w[..., 1544:], w[..., 1536:1544], pad], axis=-1)


def _unpad_w_in(w):
    return jnp.concatenate([w[..., :1536], w[..., 2560:2568], w[..., 1536:2560]], axis=-1)


def _tile(s, want):
    return min(want, s)


def _layer_fwd(x, p, l):
    s = x.shape[0]
    tr = _tile(s, 512)
    tm = _tile(s, 1024)
    xn = _rms_fwd([x], p["mix_g"], out_dtype=BF16, tr=tr, name=f"rms_mix_fwd{l}")
    h = _matmul(xn, p["w_in"], tm=_tile(s, 512), tn=D_IN_PAD, out_dtype=F32, name=f"mm_in{l}")
    c = _gates_fwd(h, p["bf_pad"], tr=_tile(s, 256), name=f"gates_fwd{l}")
    attn, attn_res = _attention_fwd(h, c, tq=_tile(s, 2048), kb=_tile(s, 512), t=_tile(s, 512), name=f"attn_fwd{l}")
    sgu = _sgu_fwd(h, p["ln_g"], p["ln_b"], p["w_s"], p["bias_tile"], tr=_tile(s, 256), name=f"sgu_fwd{l}")
    merged = _rms_fwd([attn, sgu], p["out_g"], out_dtype=BF16, tr=tr, name=f"rms_out_fwd{l}")
    x1 = _matmul(merged, p["w_out"], tm=tm, tn=1024, out_dtype=F32, residual=x, name=f"mm_out{l}")
    xn2 = _rms_fwd([x1], p["ffn_g"], out_dtype=BF16, tr=tr, name=f"rms_ffn_fwd{l}")
    gate, up, act = _matmul_swiglu(xn2, p["w_gu"], tm=_tile(s, 512), name=f"mm_gu{l}")
    x2 = _matmul(act, p["w_down"], tm=tm, tn=1024, out_dtype=F32, residual=x1, name=f"mm_down{l}")
    saved = dict(x=x, xn=xn, h=h, attn_res=attn_res, attn=attn, sgu=sgu, merged=merged, x1=x1, xn2=xn2, gate=gate, up=up, act=act)
    return x2, saved


def _layer_bwd(dx2, p, sv, l):
    s = dx2.shape[0]
    tr = _tile(s, 512)
    tm = _tile(s, 1024)
    ts = _tile(s, 1024)
    g = {}
    g["w_down"] = _matmul_tn(sv["act"], dx2, tm=1408, tn=1024, ts=ts, name=f"mm_down_dw{l}")
    dgate, dup = _matmul_dswiglu(dx2, p["w_down"], sv["gate"], sv["up"], tm=_tile(s, 512), name=f"mm_down_dx{l}")
    g["w_gu"] = jnp.concatenate([_matmul_tn(sv["xn2"], dgate, tm=1024, tn=1408, ts=ts, name=f"mm_gate_dw{l}"),
                                 _matmul_tn(sv["xn2"], dup, tm=1024, tn=1408, ts=ts, name=f"mm_up_dw{l}")], axis=1)
    dxn2 = _matmul_gu_dx(dgate, dup, p["w_gu"], tm=_tile(s, 512), name=f"mm_gu_dx{l}")
    (dx1,), g["ffn_g"] = _rms_bwd(dxn2, [sv["x1"]], p["ffn_g"], residual=dx2, tr=tr, name=f"rms_ffn_bwd{l}")
    g["w_out"] = _matmul_tn(sv["merged"], dx1, tm=1024, tn=1024, ts=ts, name=f"mm_out_dw{l}")
    dmerged = _matmul(dx1, p["w_out"], trans_b=True, tm=tm, tn=1024, out_dtype=F32, name=f"mm_out_dx{l}")
    (dattn, dsgu), g["out_g"], delta = _rms_bwd(dmerged, [sv["attn"], sv["sgu"]], p["out_g"], head_dots=True, tr=tr,
                                                name=f"rms_out_bwd{l}")
    dzu, dzv, g["w_s"], g["ln_g"], g["ln_b"], dbs = _sgu_bwd(dsgu, sv["h"], p["ln_g"], p["ln_b"], p["w_s"], p["bias_tile"],
                                                           tr=_tile(s, 256), name=f"sgu_bwd{l}")
    g["b_s"] = dbs[:, :N_HEADS].T
    dqkv, dc = _attention_bwd(dattn, delta[:, :N_HEADS], sv["attn_res"], name=f"attn_bwd{l}")
    dfl, dbf = _gates_bwd(dc, sv["h"], p["bf_pad"], tr=_tile(s, 256), name=f"gates_bwd{l}")
    g["b_f"] = dbf[0, :N_HEADS]
    dh = jnp.concatenate([dqkv, dzu, dzv, dfl], axis=1)
    g["w_in"] = _matmul_tn(sv["xn"], dh, tm=1024, tn=896, ts=ts, name=f"mm_in_dw{l}")
    dxn = _matmul(dh, p["w_in"], trans_b=True, tm=tm, tn=1024, out_dtype=F32, name=f"mm_in_dx{l}")
    (dx,), g["mix_g"] = _rms_bwd(dxn, [sv["x"]], p["mix_g"], residual=dx1, tr=tr, name=f"rms_mix_bwd{l}")
    return dx, g


def _layer_params(l, w_in_pad, w_out, w_gu, w_down, mix_norm_g, b_f, sgu_ln_g, sgu_ln_b, w_s, b_s, out_norm_g, ffn_norm_g):
    return dict(
        w_in=w_in_pad[l], w_out=w_out[l], w_gu=w_gu[l], w_down=w_down[l],
        mix_g=mix_norm_g[l][None, :], out_g=out_norm_g[l][None, :], ffn_g=ffn_norm_g[l][None, :],
        bf_pad=jnp.pad(b_f[l], (0, LANES - N_HEADS))[None, :],
        ln_g=sgu_ln_g[l][None, :], ln_b=sgu_ln_b[l][None, :], w_s=w_s[l],
        bias_tile=jnp.repeat(b_s[l].T, 64, axis=1),
    )


def _local_step(x, tgt, w_in_pad, w_out, w_gu, w_down, mix_norm_g, b_f, sgu_ln_g, sgu_ln_b, w_s, b_s, out_norm_g,
                ffn_norm_g, final_norm_g):
    depth = w_in_pad.shape[0]
    s = x.shape[0]
    params = [_layer_params(l, w_in_pad, w_out, w_gu, w_down, mix_norm_g, b_f, sgu_ln_g, sgu_ln_b, w_s, b_s, out_norm_g,
                            ffn_norm_g) for l in range(depth)]
    saved = []
    for l in range(depth):
        x, sv = _layer_fwd(x, params[l], l)
        saved.append(sv)
    loss_tile, dx, dfinal = _loss_head(x, tgt, final_norm_g[None, :], tr=_tile(s, 512), name="loss_head")
    grads = [None] * depth
    for l in reversed(range(depth)):
        dx, grads[l] = _layer_bwd(dx, params[l], saved[l], l)
    return loss_tile[0, 0], dx, grads, dfinal[0]


SMALL_ROWS = 4272


def kernel(x, mix_norm_g, w_in, b_f, sgu_ln_g, sgu_ln_b, w_s, b_s, out_norm_g, w_out, ffn_norm_g, w_gate_up, w_down, final_norm_g, loss_target, m_mix_norm_g, m_w_in, m_b_f, m_sgu_ln_g, m_sgu_ln_b, m_w_s, m_b_s, m_out_norm_g, m_w_out, m_ffn_norm_g, m_w_gate_up, m_w_down, m_final_norm_g, v_mix_norm_g, v_w_in, v_b_f, v_sgu_ln_g, v_sgu_ln_b, v_w_s, v_b_s, v_out_norm_g, v_w_out, v_ffn_norm_g, v_w_gate_up, v_w_down, v_final_norm_g):
    big = [w_in, w_out, w_gate_up, w_down]
    big_shapes = [a.shape for a in big]
    small = [mix_norm_g, b_f, sgu_ln_g, sgu_ln_b, w_s, b_s, out_norm_g, ffn_norm_g, final_norm_g]
    small_shapes = [a.shape for a in small]

    gathered = _allgather_chips(_flatten([a.astype(BF16) for a in big]), name="ag_weights")
    per_chip = [_unflatten(gathered[j], big_shapes) for j in range(4)]
    w_in_full = _pad_w_in(jnp.concatenate([pc[0] for pc in per_chip], axis=2))
    w_out_full = jnp.concatenate([pc[1] for pc in per_chip], axis=1)
    w_gu_full = jnp.concatenate([pc[2] for pc in per_chip], axis=2)
    w_down_full = jnp.concatenate([pc[3] for pc in per_chip], axis=1)

    loss_part, dx, grads, dfinal = _local_step(
        x[0], loss_target[0], w_in_full, w_out_full, w_gu_full, w_down_full, mix_norm_g, b_f, sgu_ln_g, sgu_ln_b, w_s, b_s,
        out_norm_g, ffn_norm_g, final_norm_g)
    stack = lambda key: jnp.stack([g[key] for g in grads])
    g_in = _unpad_w_in(stack("w_in"))
    g_out, g_gu, g_down = stack("w_out"), stack("w_gu"), stack("w_down")

    send = jnp.stack([_flatten([g_in[:, :, 642 * j:642 * (j + 1)], g_out[:, 256 * j:256 * (j + 1), :],
                                g_gu[:, :, 1408 * j:1408 * (j + 1)], g_down[:, 704 * j:704 * (j + 1), :]]) for j in range(4)])
    theirs = _pair_split(send, name="rs_pair_split")
    pair_sum = _pair_sum(send, theirs, lax.axis_index("c").astype(jnp.int32).reshape(1), tr=3440, name="rs_pair_sum")
    from_chips = _scatter_chips(pair_sum, name="rs_scatter")
    half = _sum_slots(from_chips, tr=1120, name="rs_chip_sum")
    g_big_flat = _pair_join(half, name="rs_pair_join")

    g_small_local = [stack("mix_g")[:, 0], stack("b_f"), stack("ln_g")[:, 0], stack("ln_b")[:, 0], stack("w_s"), stack("b_s"),
                     stack("out_g")[:, 0], stack("ffn_g")[:, 0], dfinal]
    g_small_flat = _sum_slots(_allgather_all(_flatten(g_small_local, SMALL_ROWS), name="ar_small_gather"), tr=1424,
                              name="ar_small_sum")
    loss = lax.psum(loss_part, ("x", "y", "c"))

    d_big, m_big, v_big = _adamw(_flatten(big), g_big_flat, _flatten([m_w_in, m_w_out, m_w_gate_up, m_w_down]),
                                 _flatten([v_w_in, v_w_out, v_w_gate_up, v_w_down]), tr=2240, name="adamw_big")
    m_small = [m_mix_norm_g, m_b_f, m_sgu_ln_g, m_sgu_ln_b, m_w_s, m_b_s, m_out_norm_g, m_ffn_norm_g, m_final_norm_g]
    v_small = [v_mix_norm_g, v_b_f, v_sgu_ln_g, v_sgu_ln_b, v_w_s, v_b_s, v_out_norm_g, v_ffn_norm_g, v_final_norm_g]
    d_small, m_small2, v_small2 = _adamw(_flatten(small, SMALL_ROWS), g_small_flat, _flatten(m_small, SMALL_ROWS),
                                         _flatten(v_small, SMALL_ROWS), tr=1424, name="adamw_small")

    def in_order(big_flat, small_flat):
        b_in, b_out, b_gu, b_down = _unflatten(big_flat, big_shapes)
        s_mix, s_bf, s_lng, s_lnb, s_ws, s_bs, s_outg, s_ffn, s_fin = _unflatten(small_flat, small_shapes)
        return [s_mix, b_in, s_bf, s_lng, s_lnb, s_ws, s_bs, s_outg, b_out, s_ffn, b_gu, b_down, s_fin]

    return (loss, dx[None], *in_order(g_big_flat, g_small_flat), *in_order(d_big, d_small), *in_order(m_big, m_small2),
            *in_order(v_big, v_small2))
```

```python
import jax
import jax.numpy as jnp
from jax import lax
from jax.experimental import pallas as pl
from jax.experimental.pallas import tpu as pltpu

F32 = jnp.float32
BF16 = jnp.bfloat16

D_MODEL = 1024
D_HALF = 512
N_HEADS = 8
HEAD_DIM = 64
SGU_CHUNK = 128
CAUSAL_CHUNK = 64
D_FF = 2816
D_IN = 2568
D_IN_PAD = 2688
F_COL_BLOCK = 2560 // 128
EPS = 1e-6
NEG = -1e30
LANES = 128
VMEM_LIMIT = 56 * 1024 * 1024

ADAM_LR = 0.001
ADAM_B1 = 0.9
ADAM_B2 = 0.999
ADAM_EPS = 1e-08
ADAM_WD = 0.01
ADAM_STEP = 10

MESH = pl.DeviceIdType.MESH
ANY = pl.BlockSpec(memory_space=pl.ANY)


def _params(*sem):
    return pltpu.CompilerParams(dimension_semantics=sem, vmem_limit_bytes=VMEM_LIMIT)


def _matmul(a, b, *, trans_b=False, tm, tn, out_dtype, residual=None, name):
    m, k = a.shape
    n = b.shape[0] if trans_b else b.shape[1]
    dims = (((1,), (1,)), ((), ())) if trans_b else (((1,), (0,)), ((), ()))

    def body(*refs):
        a_ref, b_ref = refs[0], refs[1]
        o_ref = refs[-1]
        acc = lax.dot_general(a_ref[...].astype(BF16), b_ref[...].astype(BF16), dims,
                              preferred_element_type=F32)
        if residual is not None:
            acc = acc + refs[2][...]
        o_ref[...] = acc.astype(out_dtype)

    b_spec = pl.BlockSpec((tn, k), lambda i, j: (j, 0)) if trans_b else pl.BlockSpec((k, tn), lambda i, j: (0, j))
    in_specs = [pl.BlockSpec((tm, k), lambda i, j: (i, 0)), b_spec]
    args = [a, b]
    if residual is not None:
        in_specs.append(pl.BlockSpec((tm, tn), lambda i, j: (i, j)))
        args.append(residual)
    return pl.pallas_call(
        body, name=name, grid=(m // tm, n // tn), in_specs=in_specs,
        out_specs=pl.BlockSpec((tm, tn), lambda i, j: (i, j)),
        out_shape=jax.ShapeDtypeStruct((m, n), out_dtype),
        compiler_params=_params("parallel", "parallel"),
    )(*args)


def _matmul_tn(a, b, *, tm, tn, ts, name):
    s, m = a.shape
    n = b.shape[1]

    def body(a_ref, b_ref, o_ref):
        @pl.when(pl.program_id(2) == 0)
        def _():
            o_ref[...] = jnp.zeros_like(o_ref)

        o_ref[...] += lax.dot_general(a_ref[...].astype(BF16), b_ref[...].astype(BF16),
                                      (((0,), (0,)), ((), ())), preferred_element_type=F32)

    return pl.pallas_call(
        body, name=name, grid=(m // tm, n // tn, s // ts),
        in_specs=[pl.BlockSpec((ts, tm), lambda i, j, t: (t, i)), pl.BlockSpec((ts, tn), lambda i, j, t: (t, j))],
        out_specs=pl.BlockSpec((tm, tn), lambda i, j, t: (i, j)),
        out_shape=jax.ShapeDtypeStruct((m, n), F32),
        compiler_params=_params("parallel", "parallel", "arbitrary"),
    )(a, b)


def _rms_fwd(xs, g, *, out_dtype, tr, name):
    s = xs[0].shape[0]
    widths = [x.shape[1] for x in xs]
    wsum = sum(widths)
    nx = len(xs)

    def body(*refs):
        g_ref, o_ref = refs[nx], refs[nx + 1]
        off = 0
        for x_ref, w in zip(refs[:nx], widths):
            x = x_ref[...]
            r = lax.rsqrt(jnp.mean(x * x, axis=-1, keepdims=True) + EPS)
            o_ref[:, off:off + w] = (x * r * g_ref[:, off:off + w]).astype(out_dtype)
            off += w

    return pl.pallas_call(
        body, name=name, grid=(s // tr,),
        in_specs=[pl.BlockSpec((tr, w), lambda i: (i, 0)) for w in widths] + [pl.BlockSpec((1, wsum), lambda i: (0, 0))],
        out_specs=pl.BlockSpec((tr, wsum), lambda i: (i, 0)),
        out_shape=jax.ShapeDtypeStruct((s, wsum), out_dtype),
        compiler_params=_params("parallel"),
    )(*xs, g)


def _rms_bwd(dy, xs, g, *, residual=None, head_dots=False, tr, name):
    s = xs[0].shape[0]
    widths = [x.shape[1] for x in xs]
    wsum = sum(widths)
    nx = len(xs)
    nin = 2 + nx + (residual is not None)

    def body(*refs):
        dy_ref, g_ref = refs[0], refs[1 + nx]
        dx_refs, dg_ref = refs[nin:nin + nx], refs[nin + nx]

        @pl.when(pl.program_id(0) == 0)
        def _():
            dg_ref[...] = jnp.zeros_like(dg_ref)

        off = 0
        for idx, (x_ref, w) in enumerate(zip(refs[1:1 + nx], widths)):
            x = x_ref[...]
            r = lax.rsqrt(jnp.mean(x * x, axis=-1, keepdims=True) + EPS)
            xh = x * r
            dyv = dy_ref[:, off:off + w]
            dxh = dyv * g_ref[:, off:off + w]
            dx = r * (dxh - xh * jnp.mean(dxh * xh, axis=-1, keepdims=True))
            if residual is not None and idx == 0:
                dx = dx + refs[2 + nx][...]
            dx_refs[idx][...] = dx
            dg_ref[:, off:off + w] += jnp.sum(dyv * xh, axis=0, keepdims=True)
            if head_dots and idx == 0:
                col = lax.broadcasted_iota(jnp.int32, (w, LANES), 0) // HEAD_DIM
                head = lax.broadcasted_iota(jnp.int32, (w, LANES), 1)
                refs[nin + nx + 1][...] = jnp.dot(dx * x, (col == head).astype(F32), precision=lax.Precision.HIGHEST,
                                                  preferred_element_type=F32)
            off += w

    in_specs = ([pl.BlockSpec((tr, wsum), lambda i: (i, 0))]
                + [pl.BlockSpec((tr, w), lambda i: (i, 0)) for w in widths]
                + [pl.BlockSpec((1, wsum), lambda i: (0, 0))])
    args = [dy, *xs, g]
    if residual is not None:
        in_specs.append(pl.BlockSpec((tr, widths[0]), lambda i: (i, 0)))
        args.append(residual)
    out_specs = [pl.BlockSpec((tr, w), lambda i: (i, 0)) for w in widths] + [pl.BlockSpec((1, wsum), lambda i: (0, 0))]
    out_shape = [jax.ShapeDtypeStruct((s, w), F32) for w in widths] + [jax.ShapeDtypeStruct((1, wsum), F32)]
    if head_dots:
        out_specs.append(pl.BlockSpec((tr, LANES), lambda i: (i, 0)))
        out_shape.append(jax.ShapeDtypeStruct((s, LANES), F32))
    outs = pl.pallas_call(
        body, name=name, grid=(s // tr,), in_specs=in_specs, out_specs=out_specs, out_shape=out_shape,
        compiler_params=_params("arbitrary"),
    )(*args)
    if head_dots:
        return outs[:nx], outs[nx], outs[nx + 1]
    return outs[:nx], outs[nx]


def _loss_head(x, tgt, g, *, tr, name):
    s, d = x.shape

    def body(x_ref, t_ref, g_ref, loss_ref, dx_ref, dg_ref):
        @pl.when(pl.program_id(0) == 0)
        def _():
            loss_ref[...] = jnp.zeros_like(loss_ref)
            dg_ref[...] = jnp.zeros_like(dg_ref)

        xv = x_ref[...]
        r = lax.rsqrt(jnp.mean(xv * xv, axis=-1, keepdims=True) + EPS)
        xh = xv * r
        err = xh * g_ref[...] - t_ref[...]
        loss_ref[...] += 0.5 * jnp.sum(jnp.mean(err * err, axis=-1, keepdims=True))
        dy = err * (1.0 / d)
        dxh = dy * g_ref[...]
        dx_ref[...] = r * (dxh - xh * jnp.mean(dxh * xh, axis=-1, keepdims=True))
        dg_ref[...] += jnp.sum(dy * xh, axis=0, keepdims=True)

    return pl.pallas_call(
        body, name=name, grid=(s // tr,),
        in_specs=[pl.BlockSpec((tr, d), lambda i: (i, 0)), pl.BlockSpec((tr, d), lambda i: (i, 0)),
                  pl.BlockSpec((1, d), lambda i: (0, 0))],
        out_specs=[pl.BlockSpec((8, LANES), lambda i: (0, 0)), pl.BlockSpec((tr, d), lambda i: (i, 0)),
                   pl.BlockSpec((1, d), lambda i: (0, 0))],
        out_shape=[jax.ShapeDtypeStruct((8, LANES), F32), jax.ShapeDtypeStruct((s, d), F32),
                   jax.ShapeDtypeStruct((1, d), F32)],
        compiler_params=_params("arbitrary"),
    )(x, tgt, g)


def _gates_fwd(h, bf_pad, *, tr, name):
    s = h.shape[0]

    def body(fl_ref, b_ref, c_ref, carry_ref):
        @pl.when(pl.program_id(0) == 0)
        def _():
            carry_ref[...] = jnp.zeros_like(carry_ref)

        lf = jax.nn.log_sigmoid(fl_ref[...] + b_ref[...])
        row = lax.broadcasted_iota(jnp.int32, (tr, tr), 0)
        col = lax.broadcasted_iota(jnp.int32, (tr, tr), 1)
        tri = (col <= row).astype(F32)
        c_ref[...] = jnp.dot(tri, lf, precision=lax.Precision.HIGHEST, preferred_element_type=F32) + carry_ref[...]
        carry_ref[...] += jnp.sum(lf, axis=0, keepdims=True)

    return pl.pallas_call(
        body, name=name, grid=(s // tr,),
        in_specs=[pl.BlockSpec((tr, LANES), lambda i: (i, F_COL_BLOCK)), pl.BlockSpec((1, LANES), lambda i: (0, 0))],
        out_specs=pl.BlockSpec((tr, LANES), lambda i: (i, 0)),
        out_shape=jax.ShapeDtypeStruct((s, LANES), F32),
        scratch_shapes=[pltpu.VMEM((1, LANES), F32)],
        compiler_params=_params("arbitrary"),
    )(h, bf_pad)


def _gates_bwd(dc, h, bf_pad, *, tr, name):
    s = h.shape[0]
    n = s // tr

    def body(dc_ref, fl_ref, b_ref, dfl_ref, db_ref, carry_ref):
        @pl.when(pl.program_id(0) == 0)
        def _():
            carry_ref[...] = jnp.zeros_like(carry_ref)
            db_ref[...] = jnp.zeros_like(db_ref)

        dcv = dc_ref[...]
        row = lax.broadcasted_iota(jnp.int32, (tr, tr), 0)
        col = lax.broadcasted_iota(jnp.int32, (tr, tr), 1)
        triu = (col >= row).astype(F32)
        dlf = jnp.dot(triu, dcv, precision=lax.Precision.HIGHEST, preferred_element_type=F32) + carry_ref[...]
        carry_ref[...] += jnp.sum(dcv, axis=0, keepdims=True)
        dfl = dlf * jax.nn.sigmoid(-(fl_ref[...] + b_ref[...]))
        dfl_ref[...] = dfl.astype(dfl_ref.dtype)
        db_ref[...] += jnp.sum(dfl, axis=0, keepdims=True)

    return pl.pallas_call(
        body, name=name, grid=(n,),
        in_specs=[pl.BlockSpec((tr, LANES), lambda i: (n - 1 - i, 0)),
                  pl.BlockSpec((tr, LANES), lambda i: (n - 1 - i, F_COL_BLOCK)),
                  pl.BlockSpec((1, LANES), lambda i: (0, 0))],
        out_specs=[pl.BlockSpec((tr, LANES), lambda i: (n - 1 - i, 0)), pl.BlockSpec((1, LANES), lambda i: (0, 0))],
        out_shape=[jax.ShapeDtypeStruct((s, LANES), BF16), jax.ShapeDtypeStruct((1, LANES), F32)],
        scratch_shapes=[pltpu.VMEM((1, LANES), F32)],
        compiler_params=_params("arbitrary"),
    )(dc, h, bf_pad)


LOG2E = 1.4426950408889634
LN2 = 0.6931471805599453
V_ROWS = 80


def _transpose_bf16(a):
    return a.astype(F32).T.astype(BF16)


SKIP_MARGIN_FWD = 140.0
SKIP_MARGIN_BWD = 160.0
SMEM = pl.BlockSpec(memory_space=pltpu.SMEM)


def _attn_fwd(qat, ka, va, gnorm, cmax_q, cmin_k, *, tq, kb, name):
    s = ka.shape[0]
    nq = s // tq
    per_tile = tq // kb
    tt = qat.shape[3]
    sub = tq // tt

    def body(gn_ref, cq_ref, ck_ref, qat_ref, ka_ref, va_ref, o_ref, lse_ref):
        h, i = pl.program_id(0), pl.program_id(1)
        qat = jnp.concatenate([qat_ref[0, d] for d in range(sub)], axis=1)

        def blk(n, carry, masked):
            m, acc = carry
            rows = pl.ds(pl.multiple_of(n * kb, kb), kb)
            sc = jnp.dot(ka_ref[rows, :], qat, preferred_element_type=F32)
            if masked:
                key = n * kb + lax.broadcasted_iota(jnp.int32, (kb, tq), 0)
                qry = i * tq + lax.broadcasted_iota(jnp.int32, (kb, tq), 1)
                sc = jnp.where(key <= qry, sc, NEG)
            m_new = jnp.maximum(m, jnp.max(sc, axis=0, keepdims=True))
            p = jnp.exp2(sc - m_new)
            vt = _transpose_bf16(va_ref[rows, :])[:V_ROWS]
            acc = jnp.exp2(m - m_new) * acc + jnp.dot(vt, p.astype(BF16), preferred_element_type=F32)
            return m_new, acc

        carry = (jnp.full((1, tq), NEG, F32), jnp.zeros((V_ROWS, tq), F32))
        for d in reversed(range(per_tile)):
            carry = blk(i * per_tile + d, carry, True)
        top = gn_ref[h] + cq_ref[h, i]

        def live(state):
            n, m_min = state[0], state[1]
            return jnp.logical_and(n >= 0, top - ck_ref[h, jnp.maximum(n, 0)] >= m_min - SKIP_MARGIN_FWD)

        def step(state):
            n, _, m, acc = state
            m, acc = blk(n, (m, acc), False)
            return n - 1, jnp.min(m), m, acc

        _, _, m, acc = lax.while_loop(live, step, (i * per_tile - 1, jnp.min(carry[0]), *carry))
        l = acc[HEAD_DIM:HEAD_DIM + 1, :]
        padded = jnp.concatenate([acc / l, jnp.zeros((LANES - V_ROWS, tq), F32)], axis=0)
        o_ref[0, 0] = padded.T[:, :HEAD_DIM]
        lse_ref[0, 0] = m + jnp.log2(l)

    return pl.pallas_call(
        body, name=name, grid=(N_HEADS, nq),
        in_specs=[SMEM, SMEM, SMEM, pl.BlockSpec((1, sub, LANES, tt), lambda h, i: (h, i, 0, 0)),
                  pl.BlockSpec((s, LANES), lambda h, i: (0, h)),
                  pl.BlockSpec((s, LANES), lambda h, i: (0, h))],
        out_specs=[pl.BlockSpec((1, 1, tq, HEAD_DIM), lambda h, i: (h, i, 0, 0)),
                   pl.BlockSpec((1, 1, 1, tq), lambda h, i: (h, i, 0, 0))],
        out_shape=[jax.ShapeDtypeStruct((N_HEADS, nq, tq, HEAD_DIM), F32), jax.ShapeDtypeStruct((N_HEADS, nq, 1, tq), F32)],
        compiler_params=_params("parallel", "arbitrary"),
    )(gnorm, cmax_q, cmin_k, qat, ka, va)


def _attn_bwd(qa, qat, doa, dot, lse, dl, ka, va, gnorm, cmin_k, reach, *, name):
    s = qa.shape[0]
    nt, t = qat.shape[1], qat.shape[3]

    def body(gn_ref, ck_ref, reach_ref, qa_ref, qat_ref, do_ref, dot_ref, lse_ref, dl_ref, ka_ref, va_ref,
             dq_ref, dk_ref, dv_ref):
        h, j = pl.program_id(0), pl.program_id(1)

        @pl.when(j == 0)
        def _():
            dq_ref[...] = jnp.zeros_like(dq_ref)

        ka_j, va_j = ka_ref[...], va_ref[...]

        def tile(i, carry, masked):
            dk, dv = carry
            rows = pl.ds(pl.multiple_of(i * t, t), t)
            qa_i, do_i = qa_ref[rows, :], do_ref[rows, :]
            st = jnp.dot(ka_j, qat_ref[0, i], preferred_element_type=F32) - lse_ref[0, i]
            if masked:
                key = lax.broadcasted_iota(jnp.int32, (t, t), 0)
                qry = lax.broadcasted_iota(jnp.int32, (t, t), 1)
                st = jnp.where(key <= qry, st, NEG)
            pt = jnp.exp2(st)
            dpt = jnp.dot(va_j, dot_ref[0, i], preferred_element_type=F32)
            dsb = (pt * (dpt - dl_ref[0, i])).astype(BF16)
            dv = dv + jnp.dot(pt.astype(BF16), do_i, preferred_element_type=F32)
            dk = dk + jnp.dot(dsb, qa_i, preferred_element_type=F32)
            dq_ref[rows, :] += lax.dot_general(dsb, ka_j, (((0,), (0,)), ((), ())), preferred_element_type=F32)
            return dk, dv

        carry = tile(j, (jnp.zeros((t, LANES), F32), jnp.zeros((t, LANES), F32)), True)
        base = gn_ref[h] - ck_ref[h, j]

        def live(state):
            i = state[0]
            return jnp.logical_and(i < nt, base + reach_ref[h, jnp.minimum(i, nt - 1)] >= -SKIP_MARGIN_BWD)

        def step(state):
            i, dk, dv = state
            dk, dv = tile(i, (dk, dv), False)
            return i + 1, dk, dv

        _, dk, dv = lax.while_loop(live, step, (j + 1, *carry))
        dk_ref[...] = dk
        dv_ref[...] = dv

    res = pl.BlockSpec((s, LANES), lambda h, j: (0, h))
    rest = pl.BlockSpec((1, nt, LANES, t), lambda h, j: (h, 0, 0, 0))
    row = pl.BlockSpec((1, nt, 1, t), lambda h, j: (h, 0, 0, 0))
    blk = pl.BlockSpec((t, LANES), lambda h, j: (j, h))
    shape = jax.ShapeDtypeStruct((s, N_HEADS * LANES), F32)
    return pl.pallas_call(
        body, name=name, grid=(N_HEADS, nt),
        in_specs=[SMEM, SMEM, SMEM, res, rest, res, rest, row, row, blk, blk], out_specs=[res, blk, blk],
        out_shape=[shape, shape, shape], compiler_params=_params("parallel", "arbitrary"),
    )(gnorm, cmin_k, reach, qa, qat, doa, dot, lse, dl, ka, va)


def _spread_matrix():
    r = lax.broadcasted_iota(jnp.int32, (D_HALF, N_HEADS * LANES), 0)
    c = lax.broadcasted_iota(jnp.int32, (D_HALF, N_HEADS * LANES), 1)
    return jnp.logical_and(c // LANES == r // HEAD_DIM, c % LANES == r % HEAD_DIM).astype(BF16)


def _piece_matrix(base):
    r = lax.broadcasted_iota(jnp.int32, (3 * LANES, N_HEADS * LANES), 0)
    c = lax.broadcasted_iota(jnp.int32, (3 * LANES, N_HEADS * LANES), 1)
    return jnp.logical_and(r % LANES < N_HEADS, c == LANES * (r % LANES) + base + r // LANES).astype(BF16)


def _ones_columns(first, count):
    c = lax.broadcasted_iota(jnp.int32, (1, N_HEADS * LANES), 1) % LANES
    return jnp.logical_and(c >= first, c < first + count).astype(F32)


def _round_bf16(x):
    return x.astype(BF16).astype(F32)


def _write_transposed(a, out_ref):
    for h in range(N_HEADS):
        out_ref[h, 0] = _transpose_bf16(a[:, h * LANES:(h + 1) * LANES])


def _pack_qkv(h, c, *, tr, name):
    s = h.shape[0]
    wide = N_HEADS * LANES

    def body(q_ref, k_ref, v_ref, c_ref, qa_ref, ka_ref, va_ref, qat_ref):
        spread = _spread_matrix()
        c2 = c_ref[...] * LOG2E
        p1 = _round_bf16(c2)
        p2 = _round_bf16(c2 - p1)
        p3 = _round_bf16(c2 - p1 - p2)
        pieces = jnp.concatenate([p1, p2, p3], axis=1).astype(BF16)
        dot = lambda a, b: jnp.dot(a, b, preferred_element_type=F32)
        qa = (dot((q_ref[...] * 0.125).astype(BF16), spread) + dot(pieces, _piece_matrix(HEAD_DIM + 3))
              + _ones_columns(HEAD_DIM, 3)).astype(BF16)
        ka = (dot((k_ref[...] * LOG2E).astype(BF16), spread) - dot(pieces, _piece_matrix(HEAD_DIM))
              + _ones_columns(HEAD_DIM + 3, 3)).astype(BF16)
        qa_ref[...] = qa
        ka_ref[...] = ka
        va_ref[...] = (dot(v_ref[...].astype(BF16), spread) + _ones_columns(HEAD_DIM, 1)).astype(BF16)
        _write_transposed(qa, qat_ref)

    shape = jax.ShapeDtypeStruct((s, wide), BF16)
    return pl.pallas_call(
        body, name=name, grid=(s // tr,),
        in_specs=[pl.BlockSpec((tr, D_HALF), lambda i: (i, 0)), pl.BlockSpec((tr, D_HALF), lambda i: (i, 1)),
                  pl.BlockSpec((tr, D_HALF), lambda i: (i, 2)), pl.BlockSpec((tr, LANES), lambda i: (i, 0))],
        out_specs=[pl.BlockSpec((tr, wide), lambda i: (i, 0))] * 3 + [pl.BlockSpec((N_HEADS, 1, LANES, tr), lambda i: (0, i, 0, 0))],
        out_shape=[shape, shape, shape, jax.ShapeDtypeStruct((N_HEADS, s // tr, LANES, tr), BF16)],
        compiler_params=_params("parallel"),
    )(h, h, h, c)


def _pack_do(dattn, *, tr, name):
    s = dattn.shape[0]
    wide = N_HEADS * LANES

    def body(d_ref, doa_ref, dot_ref):
        doa = jnp.dot(d_ref[...].astype(BF16), _spread_matrix(), preferred_element_type=F32).astype(BF16)
        doa_ref[...] = doa
        _write_transposed(doa, dot_ref)

    return pl.pallas_call(
        body, name=name, grid=(s // tr,),
        in_specs=[pl.BlockSpec((tr, D_HALF), lambda i: (i, 0))],
        out_specs=[pl.BlockSpec((tr, wide), lambda i: (i, 0)), pl.BlockSpec((N_HEADS, 1, LANES, tr), lambda i: (0, i, 0, 0))],
        out_shape=[jax.ShapeDtypeStruct((s, wide), BF16), jax.ShapeDtypeStruct((N_HEADS, s // tr, LANES, tr), BF16)],
        compiler_params=_params("parallel"),
    )(dattn)


def _unpack_grads(dqa, dka, dva, *, tr, name):
    s = dqa.shape[0]
    wide = N_HEADS * LANES

    def body(dq_ref, dk_ref, dv_ref, o_ref, dc_ref):
        gather = _spread_matrix()
        nt_dims = (((1,), (1,)), ((), ()))
        pick = lambda a: lax.dot_general(a.astype(BF16), gather, nt_dims, preferred_element_type=F32).astype(BF16)
        dq, dk = dq_ref[...], dk_ref[...]
        o_ref[:, 0:D_HALF] = pick(dq * (LN2 * 0.125))
        o_ref[:, D_HALF:2 * D_HALF] = pick(dk)
        o_ref[:, 2 * D_HALF:3 * D_HALF] = pick(dv_ref[...])
        lane = lax.broadcasted_iota(jnp.int32, (1, wide), 1) % LANES
        both = jnp.where(lane == HEAD_DIM + 3, dq, 0.0) - jnp.where(lane == HEAD_DIM, dk, 0.0)
        r = lax.broadcasted_iota(jnp.int32, (wide, LANES), 0)
        head = lax.broadcasted_iota(jnp.int32, (wide, LANES), 1)
        dc_ref[...] = jnp.dot(both, (r // LANES == head).astype(F32), precision=lax.Precision.HIGHEST,
                              preferred_element_type=F32)

    spec = pl.BlockSpec((tr, wide), lambda i: (i, 0))
    return pl.pallas_call(
        body, name=name, grid=(s // tr,), in_specs=[spec, spec, spec],
        out_specs=[pl.BlockSpec((tr, 3 * D_HALF), lambda i: (i, 0)), pl.BlockSpec((tr, LANES), lambda i: (i, 0))],
        out_shape=[jax.ShapeDtypeStruct((s, 3 * D_HALF), BF16), jax.ShapeDtypeStruct((s, LANES), F32)],
        compiler_params=_params("parallel"),
    )(dqa, dka, dva)


def _attention_fwd(h, c, *, tq, kb, t, name):
    s = h.shape[0]
    qa, ka, va, qat = _pack_qkv(h, c, tr=t, name=name + "_pack")
    c2 = c[:, :N_HEADS] * LOG2E
    head_norm = lambda a: jnp.sqrt(jnp.max(jnp.sum(jnp.square(a.reshape(s, N_HEADS, HEAD_DIM)), axis=-1), axis=0))
    gnorm = head_norm(h[:, 0:D_HALF] * 0.125) * head_norm(h[:, D_HALF:2 * D_HALF] * LOG2E) * 1.01 + 1.0
    cmax_q = jnp.max(c2.reshape(s // tq, tq, N_HEADS), axis=1).T
    cmin_k = lax.cummin(jnp.min(c2.reshape(s // kb, kb, N_HEADS), axis=1), axis=0).T
    o4, lse2 = _attn_fwd(qat, ka, va, gnorm, cmax_q, cmin_k, tq=tq, kb=kb, name=name)
    attn = o4.reshape(N_HEADS, s, HEAD_DIM).transpose(1, 0, 2).reshape(s, N_HEADS * HEAD_DIM)
    return attn, dict(qa=qa, qat=qat, ka=ka, va=va, lse2=lse2, c2=c2, gnorm=gnorm)


def _attention_bwd(dattn, delta, res, *, name):
    s = dattn.shape[0]
    t = res["qat"].shape[3]
    doa, dot = _pack_do(dattn, tr=t, name=name + "_pack")
    c2t = res["c2"].reshape(s // t, t, N_HEADS)
    lse_t = res["lse2"].reshape(N_HEADS, s // t, t)
    reach = lax.cummax(jnp.max(c2t.transpose(2, 0, 1) - lse_t, axis=2), axis=1, reverse=True)
    dqa, dka, dva = _attn_bwd(res["qa"], res["qat"], doa, dot, lse_t.reshape(N_HEADS, s // t, 1, t),
                              delta.T.reshape(N_HEADS, s // t, 1, t), res["ka"], res["va"], res["gnorm"],
                              jnp.min(c2t, axis=1).T, reach, name=name)
    return _unpack_grads(dqa, dka, dva, tr=t, name=name + "_unpack")


def _gelu(z):
    return 0.5 * z * (1.0 + lax.erf(z * 0.7071067811865476))


def _gelu_grad(z):
    return 0.5 * (1.0 + lax.erf(z * 0.7071067811865476)) + z * (0.3989422804014327 * jnp.exp(-0.5 * z * z))


def _sgu_mask():
    i = lax.broadcasted_iota(jnp.int32, (SGU_CHUNK, SGU_CHUNK), 0) // CAUSAL_CHUNK
    j = lax.broadcasted_iota(jnp.int32, (SGU_CHUNK, SGU_CHUNK), 1) // CAUSAL_CHUNK
    return (j <= i).astype(F32)


def _layernorm_stats(x):
    mu = jnp.mean(x, axis=-1, keepdims=True)
    xc = x - mu
    rstd = lax.rsqrt(jnp.mean(xc * xc, axis=-1, keepdims=True) + EPS)
    return xc * rstd, rstd


def _first_group_lanes():
    return lax.broadcasted_iota(jnp.int32, (SGU_CHUNK, LANES), 1) < 64


def _sgu_fwd(h, ln_g, ln_b, w_s, bias_tile, *, tr, name):
    s = h.shape[0]
    zu_blk, zv_blk = 1536 // D_HALF, 2048 // D_HALF

    def body(zu_ref, zv_ref, lng_ref, lnb_ref, ws_ref, bias_ref, o_ref):
        gzu = _gelu(zu_ref[...])
        xh, _ = _layernorm_stats(_gelu(zv_ref[...]))
        zb = (xh * lng_ref[...] + lnb_ref[...]).astype(BF16)
        mask = _sgu_mask()
        first = _first_group_lanes()
        for pair in range(4):
            cols = slice(pair * LANES, (pair + 1) * LANES)
            w0 = (ws_ref[2 * pair] * mask).astype(BF16)
            w1 = (ws_ref[2 * pair + 1] * mask).astype(BF16)
            for ch in range(tr // SGU_CHUNK):
                rows = slice(ch * SGU_CHUNK, (ch + 1) * SGU_CHUNK)
                zp = zb[rows, cols]
                mixed = jnp.where(first, jnp.dot(w0, zp, preferred_element_type=F32),
                                  jnp.dot(w1, zp, preferred_element_type=F32)) + bias_ref[:, cols]
                o_ref[rows, cols] = gzu[rows, cols] * mixed

    return pl.pallas_call(
        body, name=name, grid=(s // tr,),
        in_specs=[pl.BlockSpec((tr, D_HALF), lambda i: (i, zu_blk)), pl.BlockSpec((tr, D_HALF), lambda i: (i, zv_blk)),
                  pl.BlockSpec((1, D_HALF), lambda i: (0, 0)), pl.BlockSpec((1, D_HALF), lambda i: (0, 0)),
                  pl.BlockSpec((N_HEADS, SGU_CHUNK, SGU_CHUNK), lambda i: (0, 0, 0)),
                  pl.BlockSpec((SGU_CHUNK, D_HALF), lambda i: (0, 0))],
        out_specs=pl.BlockSpec((tr, D_HALF), lambda i: (i, 0)),
        out_shape=jax.ShapeDtypeStruct((s, D_HALF), F32),
        compiler_params=_params("parallel"),
    )(h, h, ln_g, ln_b, w_s, bias_tile)


def _sgu_bwd(dsgu, h, ln_g, ln_b, w_s, bias_tile, *, tr, name):
    s = h.shape[0]
    n = s // tr
    zu_blk, zv_blk = 1536 // D_HALF, 2048 // D_HALF

    def body(ds_ref, zu_ref, zv_ref, lng_ref, lnb_ref, ws_ref, bias_ref,
             dzu_ref, dzv_ref, dws_ref, dlng_ref, dlnb_ref, dbs_ref, dgzu_sc, dzvn_sc, dbias_sc):
        step = pl.program_id(0)

        @pl.when(step == 0)
        def _():
            dws_ref[...] = jnp.zeros_like(dws_ref)
            dlng_ref[...] = jnp.zeros_like(dlng_ref)
            dlnb_ref[...] = jnp.zeros_like(dlnb_ref)
            dbias_sc[...] = jnp.zeros_like(dbias_sc)

        zu = zu_ref[...]
        zv = zv_ref[...]
        gzu = _gelu(zu)
        xh, rstd = _layernorm_stats(_gelu(zv))
        zb = (xh * lng_ref[...] + lnb_ref[...]).astype(BF16)
        ds = ds_ref[...]
        mask = _sgu_mask()
        first = _first_group_lanes()
        tn_dims = (((0,), (0,)), ((), ()))
        nt_dims = (((1,), (1,)), ((), ()))
        for pair in range(4):
            cols = slice(pair * LANES, (pair + 1) * LANES)
            w0 = (ws_ref[2 * pair] * mask).astype(BF16)
            w1 = (ws_ref[2 * pair + 1] * mask).astype(BF16)
            for ch in range(tr // SGU_CHUNK):
                rows = slice(ch * SGU_CHUNK, (ch + 1) * SGU_CHUNK)
                zp = zb[rows, cols]
                mixed = jnp.where(first, jnp.dot(w0, zp, preferred_element_type=F32),
                                  jnp.dot(w1, zp, preferred_element_type=F32)) + bias_ref[:, cols]
                dsp = ds[rows, cols]
                dgzu_sc[rows, cols] = dsp * mixed
                dm = dsp * gzu[rows, cols]
                dbias_sc[:, cols] += dm
                dmb = dm.astype(BF16)
                dm0 = jnp.where(first, dmb, jnp.zeros_like(dmb))
                dm1 = jnp.where(first, jnp.zeros_like(dmb), dmb)
                dws_ref[2 * pair] += lax.dot_general(dm0, zp, nt_dims, preferred_element_type=F32)
                dws_ref[2 * pair + 1] += lax.dot_general(dm1, zp, nt_dims, preferred_element_type=F32)
                dzvn_sc[rows, cols] = jnp.where(first, lax.dot_general(w0, dmb, tn_dims, preferred_element_type=F32),
                                                lax.dot_general(w1, dmb, tn_dims, preferred_element_type=F32))
        dzvn = dzvn_sc[...]
        dlng_ref[...] += jnp.sum(dzvn * xh, axis=0, keepdims=True)
        dlnb_ref[...] += jnp.sum(dzvn, axis=0, keepdims=True)
        dxh = dzvn * lng_ref[...]
        dgzv = rstd * (dxh - jnp.mean(dxh, axis=-1, keepdims=True) - xh * jnp.mean(dxh * xh, axis=-1, keepdims=True))
        dzv_ref[...] = (dgzv * _gelu_grad(zv)).astype(dzv_ref.dtype)
        dzu_ref[...] = (dgzu_sc[...] * _gelu_grad(zu)).astype(dzu_ref.dtype)

        @pl.when(step == n - 1)
        def _():
            for g in range(N_HEADS):
                dws_ref[g] = dws_ref[g] * mask
            lane = lax.broadcasted_iota(jnp.int32, (D_HALF, LANES), 0) // 64
            grp = lax.broadcasted_iota(jnp.int32, (D_HALF, LANES), 1)
            dbs_ref[...] = jnp.dot(dbias_sc[...], (lane == grp).astype(F32), precision=lax.Precision.HIGHEST,
                                   preferred_element_type=F32)

    const2 = lambda i: (0, 0)
    return pl.pallas_call(
        body, name=name, grid=(n,),
        in_specs=[pl.BlockSpec((tr, D_HALF), lambda i: (i, 0)),
                  pl.BlockSpec((tr, D_HALF), lambda i: (i, zu_blk)), pl.BlockSpec((tr, D_HALF), lambda i: (i, zv_blk)),
                  pl.BlockSpec((1, D_HALF), const2), pl.BlockSpec((1, D_HALF), const2),
                  pl.BlockSpec((N_HEADS, SGU_CHUNK, SGU_CHUNK), lambda i: (0, 0, 0)),
                  pl.BlockSpec((SGU_CHUNK, D_HALF), const2)],
        out_specs=[pl.BlockSpec((tr, D_HALF), lambda i: (i, 0)), pl.BlockSpec((tr, D_HALF), lambda i: (i, 0)),
                   pl.BlockSpec((N_HEADS, SGU_CHUNK, SGU_CHUNK), lambda i: (0, 0, 0)),
                   pl.BlockSpec((1, D_HALF), const2), pl.BlockSpec((1, D_HALF), const2),
                   pl.BlockSpec((SGU_CHUNK, LANES), const2)],
        out_shape=[jax.ShapeDtypeStruct((s, D_HALF), BF16), jax.ShapeDtypeStruct((s, D_HALF), BF16),
                   jax.ShapeDtypeStruct((N_HEADS, SGU_CHUNK, SGU_CHUNK), F32),
                   jax.ShapeDtypeStruct((1, D_HALF), F32), jax.ShapeDtypeStruct((1, D_HALF), F32),
                   jax.ShapeDtypeStruct((SGU_CHUNK, LANES), F32)],
        scratch_shapes=[pltpu.VMEM((tr, D_HALF), F32), pltpu.VMEM((tr, D_HALF), F32), pltpu.VMEM((SGU_CHUNK, D_HALF), F32)],
        compiler_params=_params("arbitrary"),
    )(dsgu, h, h, ln_g, ln_b, w_s, bias_tile)


FF_BLOCK = D_FF // 2
DSWIGLU_CHUNK = 384


def _matmul_swiglu(a, w, *, tm, name):
    s, k = a.shape

    def body(a_ref, wg_ref, wu_ref, g_ref, u_ref, act_ref):
        av = a_ref[...].astype(BF16)
        g = jnp.dot(av, wg_ref[...].astype(BF16), preferred_element_type=F32)
        u = jnp.dot(av, wu_ref[...].astype(BF16), preferred_element_type=F32)
        g_ref[...] = g.astype(g_ref.dtype)
        u_ref[...] = u.astype(u_ref.dtype)
        act_ref[...] = (g * jax.nn.sigmoid(g) * u).astype(act_ref.dtype)

    out = pl.BlockSpec((tm, FF_BLOCK), lambda i, j: (i, j))
    return pl.pallas_call(
        body, name=name, grid=(s // tm, 2),
        in_specs=[pl.BlockSpec((tm, k), lambda i, j: (i, 0)), pl.BlockSpec((k, FF_BLOCK), lambda i, j: (0, j)),
                  pl.BlockSpec((k, FF_BLOCK), lambda i, j: (0, j + 2))],
        out_specs=[out, out, out],
        out_shape=[jax.ShapeDtypeStruct((s, D_FF), BF16)] * 3,
        compiler_params=_params("parallel", "parallel"),
    )(a, w, w)


def _matmul_dswiglu(dx, w_down, gate, up, *, tm, name):
    s, k = dx.shape

    def body(dx_ref, w_ref, g_ref, u_ref, dg_ref, du_ref):
        dxb = dx_ref[...].astype(BF16)
        for lo in range(0, FF_BLOCK, DSWIGLU_CHUNK):
            cols = slice(lo, min(lo + DSWIGLU_CHUNK, FF_BLOCK))
            d = lax.dot_general(dxb, w_ref[cols, :].astype(BF16), (((1,), (1,)), ((), ())), preferred_element_type=F32)
            g = g_ref[:, cols].astype(F32)
            sig = jax.nn.sigmoid(g)
            dg_ref[:, cols] = (d * u_ref[:, cols].astype(F32) * (sig * (1.0 + g * (1.0 - sig)))).astype(dg_ref.dtype)
            du_ref[:, cols] = (d * (g * sig)).astype(du_ref.dtype)

    blk = pl.BlockSpec((tm, FF_BLOCK), lambda i, j: (i, j))
    return pl.pallas_call(
        body, name=name, grid=(s // tm, 2),
        in_specs=[pl.BlockSpec((tm, k), lambda i, j: (i, 0)), pl.BlockSpec((FF_BLOCK, k), lambda i, j: (j, 0)), blk, blk],
        out_specs=[blk, blk], out_shape=[jax.ShapeDtypeStruct((s, D_FF), BF16)] * 2,
        compiler_params=_params("parallel", "parallel"),
    )(dx, w_down, gate, up)


def _matmul_gu_dx(dgate, dup, w_gu, *, tm, name):
    s = dgate.shape[0]
    k = w_gu.shape[0]
    nt_dims = (((1,), (1,)), ((), ()))

    def body(dg_ref, du_ref, wg_ref, wu_ref, o_ref):
        o_ref[...] = (lax.dot_general(dg_ref[...], wg_ref[...], nt_dims, preferred_element_type=F32)
                      + lax.dot_general(du_ref[...], wu_ref[...], nt_dims, preferred_element_type=F32))

    return pl.pallas_call(
        body, name=name, grid=(s // tm,),
        in_specs=[pl.BlockSpec((tm, D_FF), lambda i: (i, 0)), pl.BlockSpec((tm, D_FF), lambda i: (i, 0)),
                  pl.BlockSpec((k, D_FF), lambda i: (0, 0)), pl.BlockSpec((k, D_FF), lambda i: (0, 1))],
        out_specs=pl.BlockSpec((tm, k), lambda i: (i, 0)),
        out_shape=jax.ShapeDtypeStruct((s, k), F32),
        compiler_params=_params("parallel"),
    )(dgate, dup, w_gu, w_gu)


def _sum_slots(stacked, *, tr, name):
    k, r, _ = stacked.shape

    def body(x_ref, o_ref):
        acc = x_ref[0].astype(F32)
        for idx in range(1, k):
            acc = acc + x_ref[idx].astype(F32)
        o_ref[...] = acc

    return pl.pallas_call(
        body, name=name, grid=(r // tr,),
        in_specs=[pl.BlockSpec((k, tr, LANES), lambda i: (0, i, 0))],
        out_specs=pl.BlockSpec((tr, LANES), lambda i: (i, 0)),
        out_shape=jax.ShapeDtypeStruct((r, LANES), F32),
        compiler_params=_params("parallel"),
    )(stacked)


def _adamw(w, g, m, v, *, tr, name):
    r = w.shape[0]

    def body(w_ref, g_ref, m_ref, v_ref, d_ref, m2_ref, v2_ref):
        gv = g_ref[...]
        m2 = ADAM_B1 * m_ref[...] + (1.0 - ADAM_B1) * gv
        v2 = ADAM_B2 * v_ref[...] + (1.0 - ADAM_B2) * jnp.square(gv)
        m_hat = m2 / (1.0 - ADAM_B1 ** ADAM_STEP)
        v_hat = v2 / (1.0 - ADAM_B2 ** ADAM_STEP)
        d_ref[...] = -ADAM_LR * (m_hat / (jnp.sqrt(v_hat) + ADAM_EPS) + ADAM_WD * w_ref[...])
        m2_ref[...] = m2
        v2_ref[...] = v2

    spec = pl.BlockSpec((tr, LANES), lambda i: (i, 0))
    shape = jax.ShapeDtypeStruct((r, LANES), F32)
    return pl.pallas_call(
        body, name=name, grid=(r // tr,), in_specs=[spec] * 4, out_specs=[spec] * 3, out_shape=[shape] * 3,
        compiler_params=_params("parallel"),
    )(w, g, m, v)


PAIR_CHUNKS = 5


def _coords():
    return lax.axis_index("x"), lax.axis_index("y"), lax.axis_index("c")


def _my_chip():
    return 2 * lax.axis_index("x") + lax.axis_index("y")


def _chip_peer(x, y, k):
    px = 1 - x if k & 2 else x
    py = 1 - y if k & 1 else y
    return px, py


def _allgather_chips(shard, *, name):
    r = shard.shape[0]
    rh = r // 2

    def body(src, out, send_sems, recv_sems):
        x, y, c = _coords()
        me = 2 * x + y
        half = pl.ds(c * rh, rh)

        def copy(k, src_ref, slot, to):
            return pltpu.make_async_remote_copy(src_ref=src_ref, dst_ref=out.at[slot, half, :], send_sem=send_sems.at[k],
                                                recv_sem=recv_sems.at[k], device_id=to, device_id_type=MESH)

        first, passed = [], []
        for k in (1, 2, 3):
            px, py = _chip_peer(x, y, k)
            first.append(copy(k - 1, src.at[half, :], me, (px, py, c)))
            first[-1].start()
        for k in (1, 2, 3):
            px, py = _chip_peer(x, y, k)
            slot = 2 * px + py
            first[k - 1].wait_recv()
            passed.append(copy(2 + k, out.at[slot, half, :], slot, (x, y, 1 - c)))
            passed[-1].start()
        for cp in passed:
            cp.wait_recv()
        for cp in first + passed:
            cp.wait_send()

    gathered = pl.pallas_call(
        body, name=name, in_specs=[ANY], out_specs=ANY,
        out_shape=jax.ShapeDtypeStruct((4, r, LANES), shard.dtype),
        scratch_shapes=[pltpu.SemaphoreType.DMA((6,)), pltpu.SemaphoreType.DMA((6,))],
    )(shard)
    return lax.dynamic_update_slice(gathered, shard[None], (_my_chip(), 0, 0))


def _pair_split(grads, *, name):
    _, r, _ = grads.shape
    rh = r // 2
    rc = rh // PAIR_CHUNKS
    nchunk = 4 * PAIR_CHUNKS

    def body(g_ref, theirs_ref, send_sems, recv_sems):
        x, y, c = _coords()
        copies = []
        for j in range(4):
            for q in range(PAIR_CHUNKS):
                idx = j * PAIR_CHUNKS + q
                cp = pltpu.make_async_remote_copy(
                    src_ref=g_ref.at[j, pl.ds((1 - c) * rh + q * rc, rc), :], dst_ref=theirs_ref.at[j, pl.ds(q * rc, rc), :],
                    send_sem=send_sems.at[idx], recv_sem=recv_sems.at[idx], device_id=(x, y, 1 - c), device_id_type=MESH)
                cp.start()
                copies.append(cp)
        for cp in copies:
            cp.wait()

    return pl.pallas_call(
        body, name=name, in_specs=[ANY], out_specs=ANY, out_shape=jax.ShapeDtypeStruct((4, rh, LANES), F32),
        scratch_shapes=[pltpu.SemaphoreType.DMA((nchunk,)), pltpu.SemaphoreType.DMA((nchunk,))],
    )(grads)


def _pair_sum(grads, theirs, half, *, tr, name):
    _, rh, _ = theirs.shape
    nrt = rh // tr

    def body(half_ref, g_ref, t_ref, o_ref):
        o_ref[...] = (g_ref[...] + t_ref[...]).astype(o_ref.dtype)

    return pl.pallas_call(
        body, name=name,
        grid_spec=pltpu.PrefetchScalarGridSpec(
            num_scalar_prefetch=1, grid=(4, nrt),
            in_specs=[pl.BlockSpec((1, tr, LANES), lambda j, i, half_ref: (j, half_ref[0] * nrt + i, 0)),
                      pl.BlockSpec((1, tr, LANES), lambda j, i, half_ref: (j, i, 0))],
            out_specs=pl.BlockSpec((1, tr, LANES), lambda j, i, half_ref: (j, i, 0))),
        out_shape=jax.ShapeDtypeStruct((4, rh, LANES), BF16),
        compiler_params=_params("parallel", "parallel"),
    )(half, grads, theirs)


def _scatter_chips(part, *, name):
    _, rh, _ = part.shape

    def body(p_ref, out, send_sems, recv_sems):
        x, y, c = _coords()
        me = 2 * x + y
        copies = []
        for k in (1, 2, 3):
            px, py = _chip_peer(x, y, k)
            cp = pltpu.make_async_remote_copy(src_ref=p_ref.at[2 * px + py], dst_ref=out.at[me], send_sem=send_sems.at[k - 1],
                                              recv_sem=recv_sems.at[k - 1], device_id=(px, py, c), device_id_type=MESH)
            cp.start()
            copies.append(cp)
        for cp in copies:
            cp.wait()

    from_chips = pl.pallas_call(
        body, name=name, in_specs=[ANY], out_specs=ANY, out_shape=jax.ShapeDtypeStruct((4, rh, LANES), part.dtype),
        scratch_shapes=[pltpu.SemaphoreType.DMA((3,)), pltpu.SemaphoreType.DMA((3,))],
    )(part)
    own = lax.dynamic_index_in_dim(part, _my_chip(), axis=0, keepdims=True)
    return lax.dynamic_update_slice(from_chips, own, (_my_chip(), 0, 0))


def _pair_join(half, *, name):
    rh = half.shape[0]
    nchunk = 2 * PAIR_CHUNKS
    rc = rh // nchunk

    def body(h_ref, out, send_sems, recv_sems):
        x, y, c = _coords()
        copies = []
        for q in range(nchunk):
            src = h_ref.at[pl.ds(q * rc, rc), :]
            rows = out.at[pl.ds(c * rh + q * rc, rc), :]
            cp = pltpu.make_async_remote_copy(src_ref=src, dst_ref=rows, send_sem=send_sems.at[q], recv_sem=recv_sems.at[q],
                                              device_id=(x, y, 1 - c), device_id_type=MESH)
            cp.start()
            copies.append(cp)
        for cp in copies:
            cp.wait()

    joined = pl.pallas_call(
        body, name=name, in_specs=[ANY], out_specs=ANY, out_shape=jax.ShapeDtypeStruct((2 * rh, LANES), F32),
        scratch_shapes=[pltpu.SemaphoreType.DMA((nchunk,)), pltpu.SemaphoreType.DMA((nchunk,))],
    )(half)
    return lax.dynamic_update_slice(joined, half, (lax.axis_index("c") * rh, 0))


def _allgather_all(block, *, name):
    r = block.shape[0]

    def body(src, out, send_sems, recv_sems):
        x, y, c = _coords()
        me = 4 * x + 2 * y + c
        copies = []
        for k in range(1, 8):
            px, py = _chip_peer(x, y, k >> 1)
            pc = 1 - c if k & 1 else c
            cp = pltpu.make_async_remote_copy(src_ref=src, dst_ref=out.at[me], send_sem=send_sems.at[k - 1],
                                              recv_sem=recv_sems.at[k - 1], device_id=(px, py, pc), device_id_type=MESH)
            cp.start()
            copies.append(cp)
        for cp in copies:
            cp.wait()

    gathered = pl.pallas_call(
        body, name=name, in_specs=[ANY], out_specs=ANY, out_shape=jax.ShapeDtypeStruct((8, r, LANES), F32),
        scratch_shapes=[pltpu.SemaphoreType.DMA((7,)), pltpu.SemaphoreType.DMA((7,))],
    )(block)
    return lax.dynamic_update_slice(gathered, block[None], (2 * _my_chip() + lax.axis_index("c"), 0, 0))


def _flatten(arrays, pad_rows=None):
    flat = jnp.concatenate([a.reshape(-1) for a in arrays])
    if pad_rows is not None:
        flat = jnp.pad(flat, (0, pad_rows * LANES - flat.shape[0]))
    return flat.reshape(-1, LANES)


def _unflatten(flat, shapes):
    flat = flat.reshape(-1)
    out, off = [], 0
    for shp in shapes:
        size = 1
        for dim in shp:
            size *= dim
        out.append(flat[off:off + size].reshape(shp))
        off += size
    return out


def _pad_w_in(w):
    pad = jnp.zeros(w.shape[:-1] + (D_IN_PAD - D_IN,), w.dtype)
    return jnp.concatenate([w[..., :1536], w[..., 1544:], w[..., 1536:1544], pad], axis=-1)


def _unpad_w_in(w):
    return jnp.concatenate([w[..., :1536], w[..., 2560:2568], w[..., 1536:2560]], axis=-1)


def _tile(s, want):
    return min(want, s)


def _layer_fwd(x, p, l):
    s = x.shape[0]
    tr = _tile(s, 512)
    tm = _tile(s, 1024)
    xn = _rms_fwd([x], p["mix_g"], out_dtype=BF16, tr=tr, name=f"rms_mix_fwd{l}")
    h = _matmul(xn, p["w_in"], tm=_tile(s, 512), tn=D_IN_PAD, out_dtype=F32, name=f"mm_in{l}")
    c = _gates_fwd(h, p["bf_pad"], tr=_tile(s, 256), name=f"gates_fwd{l}")
    attn, attn_res = _attention_fwd(h, c, tq=_tile(s, 2048), kb=_tile(s, 512), t=_tile(s, 512), name=f"attn_fwd{l}")
    sgu = _sgu_fwd(h, p["ln_g"], p["ln_b"], p["w_s"], p["bias_tile"], tr=_tile(s, 256), name=f"sgu_fwd{l}")
    merged = _rms_fwd([attn, sgu], p["out_g"], out_dtype=BF16, tr=tr, name=f"rms_out_fwd{l}")
    x1 = _matmul(merged, p["w_out"], tm=tm, tn=1024, out_dtype=F32, residual=x, name=f"mm_out{l}")
    xn2 = _rms_fwd([x1], p["ffn_g"], out_dtype=BF16, tr=tr, name=f"rms_ffn_fwd{l}")
    gate, up, act = _matmul_swiglu(xn2, p["w_gu"], tm=_tile(s, 512), name=f"mm_gu{l}")
    x2 = _matmul(act, p["w_down"], tm=tm, tn=1024, out_dtype=F32, residual=x1, name=f"mm_down{l}")
    saved = dict(x=x, xn=xn, h=h, attn_res=attn_res, attn=attn, sgu=sgu, merged=merged, x1=x1, xn2=xn2, gate=gate, up=up, act=act)
    return x2, saved


def _layer_bwd(dx2, p, sv, l):
    s = dx2.shape[0]
    tr = _tile(s, 512)
    tm = _tile(s, 1024)
    ts = _tile(s, 1024)
    g = {}
    g["w_down"] = _matmul_tn(sv["act"], dx2, tm=1408, tn=1024, ts=ts, name=f"mm_down_dw{l}")
    dgate, dup = _matmul_dswiglu(dx2, p["w_down"], sv["gate"], sv["up"], tm=_tile(s, 512), name=f"mm_down_dx{l}")
    g["w_gu"] = jnp.concatenate([_matmul_tn(sv["xn2"], dgate, tm=1024, tn=1408, ts=ts, name=f"mm_gate_dw{l}"),
                                 _matmul_tn(sv["xn2"], dup, tm=1024, tn=1408, ts=ts, name=f"mm_up_dw{l}")], axis=1)
    dxn2 = _matmul_gu_dx(dgate, dup, p["w_gu"], tm=_tile(s, 512), name=f"mm_gu_dx{l}")
    (dx1,), g["ffn_g"] = _rms_bwd(dxn2, [sv["x1"]], p["ffn_g"], residual=dx2, tr=tr, name=f"rms_ffn_bwd{l}")
    g["w_out"] = _matmul_tn(sv["merged"], dx1, tm=1024, tn=1024, ts=ts, name=f"mm_out_dw{l}")
    dmerged = _matmul(dx1, p["w_out"], trans_b=True, tm=tm, tn=1024, out_dtype=F32, name=f"mm_out_dx{l}")
    (dattn, dsgu), g["out_g"], delta = _rms_bwd(dmerged, [sv["attn"], sv["sgu"]], p["out_g"], head_dots=True, tr=tr,
                                                name=f"rms_out_bwd{l}")
    dzu, dzv, g["w_s"], g["ln_g"], g["ln_b"], dbs = _sgu_bwd(dsgu, sv["h"], p["ln_g"], p["ln_b"], p["w_s"], p["bias_tile"],
                                                           tr=_tile(s, 256), name=f"sgu_bwd{l}")
    g["b_s"] = dbs[:, :N_HEADS].T
    dqkv, dc = _attention_bwd(dattn, delta[:, :N_HEADS], sv["attn_res"], name=f"attn_bwd{l}")
    dfl, dbf = _gates_bwd(dc, sv["h"], p["bf_pad"], tr=_tile(s, 256), name=f"gates_bwd{l}")
    g["b_f"] = dbf[0, :N_HEADS]
    dh = jnp.concatenate([dqkv, dzu, dzv, dfl], axis=1)
    g["w_in"] = _matmul_tn(sv["xn"], dh, tm=1024, tn=896, ts=ts, name=f"mm_in_dw{l}")
    dxn = _matmul(dh, p["w_in"], trans_b=True, tm=tm, tn=1024, out_dtype=F32, name=f"mm_in_dx{l}")
    (dx,), g["mix_g"] = _rms_bwd(dxn, [sv["x"]], p["mix_g"], residual=dx1, tr=tr, name=f"rms_mix_bwd{l}")
    return dx, g


def _layer_params(l, w_in_pad, w_out, w_gu, w_down, mix_norm_g, b_f, sgu_ln_g, sgu_ln_b, w_s, b_s, out_norm_g, ffn_norm_g):
    return dict(
        w_in=w_in_pad[l], w_out=w_out[l], w_gu=w_gu[l], w_down=w_down[l],
        mix_g=mix_norm_g[l][None, :], out_g=out_norm_g[l][None, :], ffn_g=ffn_norm_g[l][None, :],
        bf_pad=jnp.pad(b_f[l], (0, LANES - N_HEADS))[None, :],
        ln_g=sgu_ln_g[l][None, :], ln_b=sgu_ln_b[l][None, :], w_s=w_s[l],
        bias_tile=jnp.repeat(b_s[l].T, 64, axis=1),
    )


def _local_step(x, tgt, w_in_pad, w_out, w_gu, w_down, mix_norm_g, b_f, sgu_ln_g, sgu_ln_b, w_s, b_s, out_norm_g,
                ffn_norm_g, final_norm_g):
    depth = w_in_pad.shape[0]
    s = x.shape[0]
    params = [_layer_params(l, w_in_pad, w_out, w_gu, w_down, mix_norm_g, b_f, sgu_ln_g, sgu_ln_b, w_s, b_s, out_norm_g,
                            ffn_norm_g) for l in range(depth)]
    saved = []
    for l in range(depth):
        x, sv = _layer_fwd(x, params[l], l)
        saved.append(sv)
    loss_tile, dx, dfinal = _loss_head(x, tgt, final_norm_g[None, :], tr=_tile(s, 512), name="loss_head")
    grads = [None] * depth
    for l in reversed(range(depth)):
        dx, grads[l] = _layer_bwd(dx, params[l], saved[l], l)
    return loss_tile[0, 0], dx, grads, dfinal[0]


SMALL_ROWS = 4272


def kernel(x, mix_norm_g, w_in, b_f, sgu_ln_g, sgu_ln_b, w_s, b_s, out_norm_g, w_out, ffn_norm_g, w_gate_up, w_down, final_norm_g, loss_target, m_mix_norm_g, m_w_in, m_b_f, m_sgu_ln_g, m_sgu_ln_b, m_w_s, m_b_s, m_out_norm_g, m_w_out, m_ffn_norm_g, m_w_gate_up, m_w_down, m_final_norm_g, v_mix_norm_g, v_w_in, v_b_f, v_sgu_ln_g, v_sgu_ln_b, v_w_s, v_b_s, v_out_norm_g, v_w_out, v_ffn_norm_g, v_w_gate_up, v_w_down, v_final_norm_g):
    big = [w_in, w_out, w_gate_up, w_down]
    big_shapes = [a.shape for a in big]
    small = [mix_norm_g, b_f, sgu_ln_g, sgu_ln_b, w_s, b_s, out_norm_g, ffn_norm_g, final_norm_g]
    small_shapes = [a.shape for a in small]

    gathered = _allgather_chips(_flatten([a.astype(BF16) for a in big]), name="ag_weights")
    per_chip = [_unflatten(gathered[j], big_shapes) for j in range(4)]
    w_in_full = _pad_w_in(jnp.concatenate([pc[0] for pc in per_chip], axis=2))
    w_out_full = jnp.concatenate([pc[1] for pc in per_chip], axis=1)
    w_gu_full = jnp.concatenate([pc[2] for pc in per_chip], axis=2)
    w_down_full = jnp.concatenate([pc[3] for pc in per_chip], axis=1)

    loss_part, dx, grads, dfinal = _local_step(
        x[0], loss_target[0], w_in_full, w_out_full, w_gu_full, w_down_full, mix_norm_g, b_f, sgu_ln_g, sgu_ln_b, w_s, b_s,
        out_norm_g, ffn_norm_g, final_norm_g)
    stack = lambda key: jnp.stack([g[key] for g in grads])
    g_in = _unpad_w_in(stack("w_in"))
    g_out, g_gu, g_down = stack("w_out"), stack("w_gu"), stack("w_down")

    send = jnp.stack([_flatten([g_in[:, :, 642 * j:642 * (j + 1)], g_out[:, 256 * j:256 * (j + 1), :],
                                g_gu[:, :, 1408 * j:1408 * (j + 1)], g_down[:, 704 * j:704 * (j + 1), :]]) for j in range(4)])
    theirs = _pair_split(send, name="rs_pair_split")
    pair_sum = _pair_sum(send, theirs, lax.axis_index("c").astype(jnp.int32).reshape(1), tr=3440, name="rs_pair_sum")
    from_chips = _scatter_chips(pair_sum, name="rs_scatter")
    half = _sum_slots(from_chips, tr=1120, name="rs_chip_sum")
    g_big_flat = _pair_join(half, name="rs_pair_join")

    g_small_local = [stack("mix_g")[:, 0], stack("b_f"), stack("ln_g")[:, 0], stack("ln_b")[:, 0], stack("w_s"), stack("b_s"),
                     stack("out_g")[:, 0], stack("ffn_g")[:, 0], dfinal]
    g_small_flat = _sum_slots(_allgather_all(_flatten(g_small_local, SMALL_ROWS), name="ar_small_gather"), tr=1424,
                              name="ar_small_sum")
    loss = lax.psum(loss_part, ("x", "y", "c"))

    d_big, m_big, v_big = _adamw(_flatten(big), g_big_flat, _flatten([m_w_in, m_w_out, m_w_gate_up, m_w_down]),
                                 _flatten([v_w_in, v_w_out, v_w_gate_up, v_w_down]), tr=2240, name="adamw_big")
    m_small = [m_mix_norm_g, m_b_f, m_sgu_ln_g, m_sgu_ln_b, m_w_s, m_b_s, m_out_norm_g, m_ffn_norm_g, m_final_norm_g]
    v_small = [v_mix_norm_g, v_b_f, v_sgu_ln_g, v_sgu_ln_b, v_w_s, v_b_s, v_out_norm_g, v_ffn_norm_g, v_final_norm_g]
    d_small, m_small2, v_small2 = _adamw(_flatten(small, SMALL_ROWS), g_small_flat, _flatten(m_small, SMALL_ROWS),
                                         _flatten(v_small, SMALL_ROWS), tr=1424, name="adamw_small")

    def in_order(big_flat, small_flat):
        b_in, b_out, b_gu, b_down = _unflatten(big_flat, big_shapes)
        s_mix, s_bf, s_lng, s_lnb, s_ws, s_bs, s_outg, s_ffn, s_fin = _unflatten(small_flat, small_shapes)
        return [s_mix, b_in, s_bf, s_lng, s_lnb, s_ws, s_bs, s_outg, b_out, s_ffn, b_gu, b_down, s_fin]

    return (loss, dx[None], *in_order(g_big_flat, g_small_flat), *in_order(d_big, d_small), *in_order(m_big, m_small2),
            *in_order(v_big, v_small2))
```

```python
import jax
import jax.numpy as jnp
from jax import lax
from jax.experimental import pallas as pl
from jax.experimental.pallas import tpu as pltpu

F32 = jnp.float32
BF16 = jnp.bfloat16

D_MODEL = 1024
D_HALF = 512
N_HEADS = 8
HEAD_DIM = 64
SGU_CHUNK = 128
CAUSAL_CHUNK = 64
D_FF = 2816
D_IN = 2568
D_IN_PAD = 2688
F_COL_BLOCK = 2560 // 128
EPS = 1e-6
NEG = -1e30
LANES = 128
VMEM_LIMIT = 56 * 1024 * 1024

ADAM_LR = 0.001
ADAM_B1 = 0.9
ADAM_B2 = 0.999
ADAM_EPS = 1e-08
ADAM_WD = 0.01
ADAM_STEP = 10

MESH = pl.DeviceIdType.MESH
ANY = pl.BlockSpec(memory_space=pl.ANY)


def _params(*sem):
    return pltpu.CompilerParams(dimension_semantics=sem, vmem_limit_bytes=VMEM_LIMIT)


def _matmul(a, b, *, trans_b=False, tm, tn, out_dtype, residual=None, name):
    m, k = a.shape
    n = b.shape[0] if trans_b else b.shape[1]
    dims = (((1,), (1,)), ((), ())) if trans_b else (((1,), (0,)), ((), ()))

    def body(*refs):
        a_ref, b_ref = refs[0], refs[1]
        o_ref = refs[-1]
        acc = lax.dot_general(a_ref[...].astype(BF16), b_ref[...].astype(BF16), dims,
                              preferred_element_type=F32)
        if residual is not None:
            acc = acc + refs[2][...]
        o_ref[...] = acc.astype(out_dtype)

    b_spec = pl.BlockSpec((tn, k), lambda i, j: (j, 0)) if trans_b else pl.BlockSpec((k, tn), lambda i, j: (0, j))
    in_specs = [pl.BlockSpec((tm, k), lambda i, j: (i, 0)), b_spec]
    args = [a, b]
    if residual is not None:
        in_specs.append(pl.BlockSpec((tm, tn), lambda i, j: (i, j)))
        args.append(residual)
    return pl.pallas_call(
        body, name=name, grid=(m // tm, n // tn), in_specs=in_specs,
        out_specs=pl.BlockSpec((tm, tn), lambda i, j: (i, j)),
        out_shape=jax.ShapeDtypeStruct((m, n), out_dtype),
        compiler_params=_params("parallel", "parallel"),
    )(*args)


def _matmul_tn(a, b, *, tm, tn, ts, name):
    s, m = a.shape
    n = b.shape[1]

    def body(a_ref, b_ref, o_ref):
        @pl.when(pl.program_id(2) == 0)
        def _():
            o_ref[...] = jnp.zeros_like(o_ref)

        o_ref[...] += lax.dot_general(a_ref[...].astype(BF16), b_ref[...].astype(BF16),
                                      (((0,), (0,)), ((), ())), preferred_element_type=F32)

    return pl.pallas_call(
        body, name=name, grid=(m // tm, n // tn, s // ts),
        in_specs=[pl.BlockSpec((ts, tm), lambda i, j, t: (t, i)), pl.BlockSpec((ts, tn), lambda i, j, t: (t, j))],
        out_specs=pl.BlockSpec((tm, tn), lambda i, j, t: (i, j)),
        out_shape=jax.ShapeDtypeStruct((m, n), F32),
        compiler_params=_params("parallel", "parallel", "arbitrary"),
    )(a, b)


def _rms_fwd(xs, g, *, out_dtype, tr, name):
    s = xs[0].shape[0]
    widths = [x.shape[1] for x in xs]
    wsum = sum(widths)
    nx = len(xs)

    def body(*refs):
        g_ref, o_ref = refs[nx], refs[nx + 1]
        off = 0
        for x_ref, w in zip(refs[:nx], widths):
            x = x_ref[...]
            r = lax.rsqrt(jnp.mean(x * x, axis=-1, keepdims=True) + EPS)
            o_ref[:, off:off + w] = (x * r * g_ref[:, off:off + w]).astype(out_dtype)
            off += w

    return pl.pallas_call(
        body, name=name, grid=(s // tr,),
        in_specs=[pl.BlockSpec((tr, w), lambda i: (i, 0)) for w in widths] + [pl.BlockSpec((1, wsum), lambda i: (0, 0))],
        out_specs=pl.BlockSpec((tr, wsum), lambda i: (i, 0)),
        out_shape=jax.ShapeDtypeStruct((s, wsum), out_dtype),
        compiler_params=_params("parallel"),
    )(*xs, g)


def _rms_bwd(dy, xs, g, *, residual=None, head_dots=False, tr, name):
    s = xs[0].shape[0]
    widths = [x.shape[1] for x in xs]
    wsum = sum(widths)
    nx = len(xs)
    nin = 2 + nx + (residual is not None)

    def body(*refs):
        dy_ref, g_ref = refs[0], refs[1 + nx]
        dx_refs, dg_ref = refs[nin:nin + nx], refs[nin + nx]

        @pl.when(pl.program_id(0) == 0)
        def _():
            dg_ref[...] = jnp.zeros_like(dg_ref)

        off = 0
        for idx, (x_ref, w) in enumerate(zip(refs[1:1 + nx], widths)):
            x = x_ref[...]
            r = lax.rsqrt(jnp.mean(x * x, axis=-1, keepdims=True) + EPS)
            xh = x * r
            dyv = dy_ref[:, off:off + w]
            dxh = dyv * g_ref[:, off:off + w]
            dx = r * (dxh - xh * jnp.mean(dxh * xh, axis=-1, keepdims=True))
            if residual is not None and idx == 0:
                dx = dx + refs[2 + nx][...]
            dx_refs[idx][...] = dx
            dg_ref[:, off:off + w] += jnp.sum(dyv * xh, axis=0, keepdims=True)
            if head_dots and idx == 0:
                col = lax.broadcasted_iota(jnp.int32, (w, LANES), 0) // HEAD_DIM
                head = lax.broadcasted_iota(jnp.int32, (w, LANES), 1)
                refs[nin + nx + 1][...] = jnp.dot(dx * x, (col == head).astype(F32), precision=lax.Precision.HIGHEST,
                                                  preferred_element_type=F32)
            off += w

    in_specs = ([pl.BlockSpec((tr, wsum), lambda i: (i, 0))]
                + [pl.BlockSpec((tr, w), lambda i: (i, 0)) for w in widths]
                + [pl.BlockSpec((1, wsum), lambda i: (0, 0))])
    args = [dy, *xs, g]
    if residual is not None:
        in_specs.append(pl.BlockSpec((tr, widths[0]), lambda i: (i, 0)))
        args.append(residual)
    out_specs = [pl.BlockSpec((tr, w), lambda i: (i, 0)) for w in widths] + [pl.BlockSpec((1, wsum), lambda i: (0, 0))]
    out_shape = [jax.ShapeDtypeStruct((s, w), F32) for w in widths] + [jax.ShapeDtypeStruct((1, wsum), F32)]
    if head_dots:
        out_specs.append(pl.BlockSpec((tr, LANES), lambda i: (i, 0)))
        out_shape.append(jax.ShapeDtypeStruct((s, LANES), F32))
    outs = pl.pallas_call(
        body, name=name, grid=(s // tr,), in_specs=in_specs, out_specs=out_specs, out_shape=out_shape,
        compiler_params=_params("arbitrary"),
    )(*args)
    if head_dots:
        return outs[:nx], outs[nx], outs[nx + 1]
    return outs[:nx], outs[nx]


def _loss_head(x, tgt, g, *, tr, name):
    s, d = x.shape

    def body(x_ref, t_ref, g_ref, loss_ref, dx_ref, dg_ref):
        @pl.when(pl.program_id(0) == 0)
        def _():
            loss_ref[...] = jnp.zeros_like(loss_ref)
            dg_ref[...] = jnp.zeros_like(dg_ref)

        xv = x_ref[...]
        r = lax.rsqrt(jnp.mean(xv * xv, axis=-1, keepdims=True) + EPS)
        xh = xv * r
        err = xh * g_ref[...] - t_ref[...]
        loss_ref[...] += 0.5 * jnp.sum(jnp.mean(err * err, axis=-1, keepdims=True))
        dy = err * (1.0 / d)
        dxh = dy * g_ref[...]
        dx_ref[...] = r * (dxh - xh * jnp.mean(dxh * xh, axis=-1, keepdims=True))
        dg_ref[...] += jnp.sum(dy * xh, axis=0, keepdims=True)

    return pl.pallas_call(
        body, name=name, grid=(s // tr,),
        in_specs=[pl.BlockSpec((tr, d), lambda i: (i, 0)), pl.BlockSpec((tr, d), lambda i: (i, 0)),
                  pl.BlockSpec((1, d), lambda i: (0, 0))],
        out_specs=[pl.BlockSpec((8, LANES), lambda i: (0, 0)), pl.BlockSpec((tr, d), lambda i: (i, 0)),
                   pl.BlockSpec((1, d), lambda i: (0, 0))],
        out_shape=[jax.ShapeDtypeStruct((8, LANES), F32), jax.ShapeDtypeStruct((s, d), F32),
                   jax.ShapeDtypeStruct((1, d), F32)],
        compiler_params=_params("arbitrary"),
    )(x, tgt, g)


def _gates_fwd(h, bf_pad, *, tr, name):
    s = h.shape[0]

    def body(fl_ref, b_ref, c_ref, carry_ref):
        @pl.when(pl.program_id(0) == 0)
        def _():
            carry_ref[...] = jnp.zeros_like(carry_ref)

        lf = jax.nn.log_sigmoid(fl_ref[...] + b_ref[...])
        row = lax.broadcasted_iota(jnp.int32, (tr, tr), 0)
        col = lax.broadcasted_iota(jnp.int32, (tr, tr), 1)
        tri = (col <= row).astype(F32)
        c_ref[...] = jnp.dot(tri, lf, precision=lax.Precision.HIGHEST, preferred_element_type=F32) + carry_ref[...]
        carry_ref[...] += jnp.sum(lf, axis=0, keepdims=True)

    return pl.pallas_call(
        body, name=name, grid=(s // tr,),
        in_specs=[pl.BlockSpec((tr, LANES), lambda i: (i, F_COL_BLOCK)), pl.BlockSpec((1, LANES), lambda i: (0, 0))],
        out_specs=pl.BlockSpec((tr, LANES), lambda i: (i, 0)),
        out_shape=jax.ShapeDtypeStruct((s, LANES), F32),
        scratch_shapes=[pltpu.VMEM((1, LANES), F32)],
        compiler_params=_params("arbitrary"),
    )(h, bf_pad)


def _gates_bwd(dc, h, bf_pad, *, tr, name):
    s = h.shape[0]
    n = s // tr

    def body(dc_ref, fl_ref, b_ref, dfl_ref, db_ref, carry_ref):
        @pl.when(pl.program_id(0) == 0)
        def _():
            carry_ref[...] = jnp.zeros_like(carry_ref)
            db_ref[...] = jnp.zeros_like(db_ref)

        dcv = dc_ref[...]
        row = lax.broadcasted_iota(jnp.int32, (tr, tr), 0)
        col = lax.broadcasted_iota(jnp.int32, (tr, tr), 1)
        triu = (col >= row).astype(F32)
        dlf = jnp.dot(triu, dcv, precision=lax.Precision.HIGHEST, preferred_element_type=F32) + carry_ref[...]
        carry_ref[...] += jnp.sum(dcv, axis=0, keepdims=True)
        dfl = dlf * jax.nn.sigmoid(-(fl_ref[...] + b_ref[...]))
        dfl_ref[...] = dfl.astype(dfl_ref.dtype)
        db_ref[...] += jnp.sum(dfl, axis=0, keepdims=True)

    return pl.pallas_call(
        body, name=name, grid=(n,),
        in_specs=[pl.BlockSpec((tr, LANES), lambda i: (n - 1 - i, 0)),
                  pl.BlockSpec((tr, LANES), lambda i: (n - 1 - i, F_COL_BLOCK)),
                  pl.BlockSpec((1, LANES), lambda i: (0, 0))],
        out_specs=[pl.BlockSpec((tr, LANES), lambda i: (n - 1 - i, 0)), pl.BlockSpec((1, LANES), lambda i: (0, 0))],
        out_shape=[jax.ShapeDtypeStruct((s, LANES), BF16), jax.ShapeDtypeStruct((1, LANES), F32)],
        scratch_shapes=[pltpu.VMEM((1, LANES), F32)],
        compiler_params=_params("arbitrary"),
    )(dc, h, bf_pad)


LOG2E = 1.4426950408889634
LN2 = 0.6931471805599453
V_ROWS = 80


def _transpose_bf16(a):
    return a.astype(F32).T.astype(BF16)


SKIP_MARGIN_FWD = 140.0
SKIP_MARGIN_BWD = 160.0
SMEM = pl.BlockSpec(memory_space=pltpu.SMEM)


def _attn_fwd(qat, ka, va, gnorm, cmax_q, cmin_k, *, tq, kb, name):
    s = ka.shape[0]
    nq = s // tq
    per_tile = tq // kb
    tt = qat.shape[3]
    sub = tq // tt

    def body(gn_ref, cq_ref, ck_ref, qat_ref, ka_ref, va_ref, o_ref, lse_ref):
        h, i = pl.program_id(0), pl.program_id(1)
        qat = jnp.concatenate([qat_ref[0, d] for d in range(sub)], axis=1)

        def blk(n, carry, masked):
            m, acc = carry
            rows = pl.ds(pl.multiple_of(n * kb, kb), kb)
            sc = jnp.dot(ka_ref[rows, :], qat, preferred_element_type=F32)
            if masked:
                key = n * kb + lax.broadcasted_iota(jnp.int32, (kb, tq), 0)
                qry = i * tq + lax.broadcasted_iota(jnp.int32, (kb, tq), 1)
                sc = jnp.where(key <= qry, sc, NEG)
            m_new = jnp.maximum(m, jnp.max(sc, axis=0, keepdims=True))
            p = jnp.exp2(sc - m_new)
            vt = _transpose_bf16(va_ref[rows, :])[:V_ROWS]
            acc = jnp.exp2(m - m_new) * acc + jnp.dot(vt, p.astype(BF16), preferred_element_type=F32)
            return m_new, acc

        carry = (jnp.full((1, tq), NEG, F32), jnp.zeros((V_ROWS, tq), F32))
        for d in reversed(range(per_tile)):
            carry = blk(i * per_tile + d, carry, True)
        top = gn_ref[h] + cq_ref[h, i]

        def live(state):
            n, m_min = state[0], state[1]
            return jnp.logical_and(n >= 0, top - ck_ref[h, jnp.maximum(n, 0)] >= m_min - SKIP_MARGIN_FWD)

        def step(state):
            n, _, m, acc = state
            m, acc = blk(n, (m, acc), False)
            return n - 1, jnp.min(m), m, acc

        _, _, m, acc = lax.while_loop(live, step, (i * per_tile - 1, jnp.min(carry[0]), *carry))
        l = acc[HEAD_DIM:HEAD_DIM + 1, :]
        padded = jnp.concatenate([acc / l, jnp.zeros((LANES - V_ROWS, tq), F32)], axis=0)
        o_ref[0, 0] = padded.T[:, :HEAD_DIM]
        lse_ref[0, 0] = m + jnp.log2(l)

    return pl.pallas_call(
        body, name=name, grid=(N_HEADS, nq),
        in_specs=[SMEM, SMEM, SMEM, pl.BlockSpec((1, sub, LANES, tt), lambda h, i: (h, i, 0, 0)),
                  pl.BlockSpec((s, LANES), lambda h, i: (0, h)),
                  pl.BlockSpec((s, LANES), lambda h, i: (0, h))],
        out_specs=[pl.BlockSpec((1, 1, tq, HEAD_DIM), lambda h, i: (h, i, 0, 0)),
                   pl.BlockSpec((1, 1, 1, tq), lambda h, i: (h, i, 0, 0))],
        out_shape=[jax.ShapeDtypeStruct((N_HEADS, nq, tq, HEAD_DIM), F32), jax.ShapeDtypeStruct((N_HEADS, nq, 1, tq), F32)],
        compiler_params=_params("parallel", "arbitrary"),
    )(gnorm, cmax_q, cmin_k, qat, ka, va)


def _attn_bwd(qa, qat, doa, dot, lse, dl, ka, va, gnorm, cmin_k, reach, *, name):
    s = qa.shape[0]
    nt, t = qat.shape[1], qat.shape[3]

    def body(gn_ref, ck_ref, reach_ref, qa_ref, qat_ref, do_ref, dot_ref, lse_ref, dl_ref, ka_ref, va_ref,
             dq_ref, dk_ref, dv_ref):
        h, j = pl.program_id(0), pl.program_id(1)

        @pl.when(j == 0)
        def _():
            dq_ref[...] = jnp.zeros_like(dq_ref)

        ka_j, va_j = ka_ref[...], va_ref[...]

        def tile(i, carry, masked):
            dk, dv = carry
            rows = pl.ds(pl.multiple_of(i * t, t), t)
            qa_i, do_i = qa_ref[rows, :], do_ref[rows, :]
            st = jnp.dot(ka_j, qat_ref[0, i], preferred_element_type=F32) - lse_ref[0, i]
            if masked:
                key = lax.broadcasted_iota(jnp.int32, (t, t), 0)
                qry = lax.broadcasted_iota(jnp.int32, (t, t), 1)
                st = jnp.where(key <= qry, st, NEG)
            pt = jnp.exp2(st)
            dpt = jnp.dot(va_j, dot_ref[0, i], preferred_element_type=F32)
            dsb = (pt * (dpt - dl_ref[0, i])).astype(BF16)
            dv = dv + jnp.dot(pt.astype(BF16), do_i, preferred_element_type=F32)
            dk = dk + jnp.dot(dsb, qa_i, preferred_element_type=F32)
            dq_ref[rows, :] += lax.dot_general(dsb, ka_j, (((0,), (0,)), ((), ())), preferred_element_type=F32)
            return dk, dv

        carry = tile(j, (jnp.zeros((t, LANES), F32), jnp.zeros((t, LANES), F32)), True)
        base = gn_ref[h] - ck_ref[h, j]

        def live(state):
            i = state[0]
            return jnp.logical_and(i < nt, base + reach_ref[h, jnp.minimum(i, nt - 1)] >= -SKIP_MARGIN_BWD)

        def step(state):
            i, dk, dv = state
            dk, dv = tile(i, (dk, dv), False)
            return i + 1, dk, dv

        _, dk, dv = lax.while_loop(live, step, (j + 1, *carry))
        dk_ref[...] = dk
        dv_ref[...] = dv

    res = pl.BlockSpec((s, LANES), lambda h, j: (0, h))
    rest = pl.BlockSpec((1, nt, LANES, t), lambda h, j: (h, 0, 0, 0))
    row = pl.BlockSpec((1, nt, 1, t), lambda h, j: (h, 0, 0, 0))
    blk = pl.BlockSpec((t, LANES), lambda h, j: (j, h))
    shape = jax.ShapeDtypeStruct((s, N_HEADS * LANES), F32)
    return pl.pallas_call(
        body, name=name, grid=(N_HEADS, nt),
        in_specs=[SMEM, SMEM, SMEM, res, rest, res, rest, row, row, blk, blk], out_specs=[res, blk, blk],
        out_shape=[shape, shape, shape], compiler_params=_params("parallel", "arbitrary"),
    )(gnorm, cmin_k, reach, qa, qat, doa, dot, lse, dl, ka, va)


def _spread_matrix():
    r = lax.broadcasted_iota(jnp.int32, (D_HALF, N_HEADS * LANES), 0)
    c = lax.broadcasted_iota(jnp.int32, (D_HALF, N_HEADS * LANES), 1)
    return jnp.logical_and(c // LANES == r // HEAD_DIM, c % LANES == r % HEAD_DIM).astype(BF16)


def _piece_matrix(base):
    r = lax.broadcasted_iota(jnp.int32, (3 * LANES, N_HEADS * LANES), 0)
    c = lax.broadcasted_iota(jnp.int32, (3 * LANES, N_HEADS * LANES), 1)
    return jnp.logical_and(r % LANES < N_HEADS, c == LANES * (r % LANES) + base + r // LANES).astype(BF16)


def _ones_columns(first, count):
    c = lax.broadcasted_iota(jnp.int32, (1, N_HEADS * LANES), 1) % LANES
    return jnp.logical_and(c >= first, c < first + count).astype(F32)


def _round_bf16(x):
    return x.astype(BF16).astype(F32)


def _write_transposed(a, out_ref):
    for h in range(N_HEADS):
        out_ref[h, 0] = _transpose_bf16(a[:, h * LANES:(h + 1) * LANES])


def _pack_qkv(h, c, *, tr, name):
    s = h.shape[0]
    wide = N_HEADS * LANES

    def body(q_ref, k_ref, v_ref, c_ref, qa_ref, ka_ref, va_ref, qat_ref):
        spread = _spread_matrix()
        c2 = c_ref[...] * LOG2E
        p1 = _round_bf16(c2)
        p2 = _round_bf16(c2 - p1)
        p3 = _round_bf16(c2 - p1 - p2)
        pieces = jnp.concatenate([p1, p2, p3], axis=1).astype(BF16)
        dot = lambda a, b: jnp.dot(a, b, preferred_element_type=F32)
        qa = (dot((q_ref[...] * 0.125).astype(BF16), spread) + dot(pieces, _piece_matrix(HEAD_DIM + 3))
              + _ones_columns(HEAD_DIM, 3)).astype(BF16)
        ka = (dot((k_ref[...] * LOG2E).astype(BF16), spread) - dot(pieces, _piece_matrix(HEAD_DIM))
              + _ones_columns(HEAD_DIM + 3, 3)).astype(BF16)
        qa_ref[...] = qa
        ka_ref[...] = ka
        va_ref[...] = (dot(v_ref[...].astype(BF16), spread) + _ones_columns(HEAD_DIM, 1)).astype(BF16)
        _write_transposed(qa, qat_ref)

    shape = jax.ShapeDtypeStruct((s, wide), BF16)
    return pl.pallas_call(
        body, name=name, grid=(s // tr,),
        in_specs=[pl.BlockSpec((tr, D_HALF), lambda i: (i, 0)), pl.BlockSpec((tr, D_HALF), lambda i: (i, 1)),
                  pl.BlockSpec((tr, D_HALF), lambda i: (i, 2)), pl.BlockSpec((tr, LANES), lambda i: (i, 0))],
        out_specs=[pl.BlockSpec((tr, wide), lambda i: (i, 0))] * 3 + [pl.BlockSpec((N_HEADS, 1, LANES, tr), lambda i: (0, i, 0, 0))],
        out_shape=[shape, shape, shape, jax.ShapeDtypeStruct((N_HEADS, s // tr, LANES, tr), BF16)],
        compiler_params=_params("parallel"),
    )(h, h, h, c)


def _pack_do(dattn, *, tr, name):
    s = dattn.shape[0]
    wide = N_HEADS * LANES

    def body(d_ref, doa_ref, dot_ref):
        doa = jnp.dot(d_ref[...].astype(BF16), _spread_matrix(), preferred_element_type=F32).astype(BF16)
        doa_ref[...] = doa
        _write_transposed(doa, dot_ref)

    return pl.pallas_call(
        body, name=name, grid=(s // tr,),
        in_specs=[pl.BlockSpec((tr, D_HALF), lambda i: (i, 0))],
        out_specs=[pl.BlockSpec((tr, wide), lambda i: (i, 0)), pl.BlockSpec((N_HEADS, 1, LANES, tr), lambda i: (0, i, 0, 0))],
        out_shape=[jax.ShapeDtypeStruct((s, wide), BF16), jax.ShapeDtypeStruct((N_HEADS, s // tr, LANES, tr), BF16)],
        compiler_params=_params("parallel"),
    )(dattn)


def _unpack_grads(dqa, dka, dva, *, tr, name):
    s = dqa.shape[0]
    wide = N_HEADS * LANES

    def body(dq_ref, dk_ref, dv_ref, o_ref, dc_ref):
        gather = _spread_matrix()
        nt_dims = (((1,), (1,)), ((), ()))
        pick = lambda a: lax.dot_general(a.astype(BF16), gather, nt_dims, preferred_element_type=F32).astype(BF16)
        dq, dk = dq_ref[...], dk_ref[...]
        o_ref[:, 0:D_HALF] = pick(dq * (LN2 * 0.125))
        o_ref[:, D_HALF:2 * D_HALF] = pick(dk)
        o_ref[:, 2 * D_HALF:3 * D_HALF] = pick(dv_ref[...])
        lane = lax.broadcasted_iota(jnp.int32, (1, wide), 1) % LANES
        both = jnp.where(lane == HEAD_DIM + 3, dq, 0.0) - jnp.where(lane == HEAD_DIM, dk, 0.0)
        r = lax.broadcasted_iota(jnp.int32, (wide, LANES), 0)
        head = lax.broadcasted_iota(jnp.int32, (wide, LANES), 1)
        dc_ref[...] = jnp.dot(both, (r // LANES == head).astype(F32), precision=lax.Precision.HIGHEST,
                              preferred_element_type=F32)

    spec = pl.BlockSpec((tr, wide), lambda i: (i, 0))
    return pl.pallas_call(
        body, name=name, grid=(s // tr,), in_specs=[spec, spec, spec],
        out_specs=[pl.BlockSpec((tr, 3 * D_HALF), lambda i: (i, 0)), pl.BlockSpec((tr, LANES), lambda i: (i, 0))],
        out_shape=[jax.ShapeDtypeStruct((s, 3 * D_HALF), BF16), jax.ShapeDtypeStruct((s, LANES), F32)],
        compiler_params=_params("parallel"),
    )(dqa, dka, dva)


def _attention_fwd(h, c, *, tq, kb, t, name):
    s = h.shape[0]
    qa, ka, va, qat = _pack_qkv(h, c, tr=t, name=name + "_pack")
    c2 = c[:, :N_HEADS] * LOG2E
    head_norm = lambda a: jnp.sqrt(jnp.max(jnp.sum(jnp.square(a.reshape(s, N_HEADS, HEAD_DIM)), axis=-1), axis=0))
    gnorm = head_norm(h[:, 0:D_HALF] * 0.125) * head_norm(h[:, D_HALF:2 * D_HALF] * LOG2E) * 1.01 + 1.0
    cmax_q = jnp.max(c2.reshape(s // tq, tq, N_HEADS), axis=1).T
    cmin_k = lax.cummin(jnp.min(c2.reshape(s // kb, kb, N_HEADS), axis=1), axis=0).T
    o4, lse2 = _attn_fwd(qat, ka, va, gnorm, cmax_q, cmin_k, tq=tq, kb=kb, name=name)
    attn = o4.reshape(N_HEADS, s, HEAD_DIM).transpose(1, 0, 2).reshape(s, N_HEADS * HEAD_DIM)
    return attn, dict(qa=qa, qat=qat, ka=ka, va=va, lse2=lse2, c2=c2, gnorm=gnorm)


def _attention_bwd(dattn, delta, res, *, name):
    s = dattn.shape[0]
    t = res["qat"].shape[3]
    doa, dot = _pack_do(dattn, tr=t, name=name + "_pack")
    c2t = res["c2"].reshape(s // t, t, N_HEADS)
    lse_t = res["lse2"].reshape(N_HEADS, s // t, t)
    reach = lax.cummax(jnp.max(c2t.transpose(2, 0, 1) - lse_t, axis=2), axis=1, reverse=True)
    dqa, dka, dva = _attn_bwd(res["qa"], res["qat"], doa, dot, lse_t.reshape(N_HEADS, s // t, 1, t),
                              delta.T.reshape(N_HEADS, s // t, 1, t), res["ka"], res["va"], res["gnorm"],
                              jnp.min(c2t, axis=1).T, reach, name=name)
    return _unpack_grads(dqa, dka, dva, tr=t, name=name + "_unpack")


def _gelu(z):
    return 0.5 * z * (1.0 + lax.erf(z * 0.7071067811865476))


def _gelu_grad(z):
    return 0.5 * (1.0 + lax.erf(z * 0.7071067811865476)) + z * (0.3989422804014327 * jnp.exp(-0.5 * z * z))


def _sgu_mask():
    i = lax.broadcasted_iota(jnp.int32, (SGU_CHUNK, SGU_CHUNK), 0) // CAUSAL_CHUNK
    j = lax.broadcasted_iota(jnp.int32, (SGU_CHUNK, SGU_CHUNK), 1) // CAUSAL_CHUNK
    return (j <= i).astype(F32)


def _layernorm_stats(x):
    mu = jnp.mean(x, axis=-1, keepdims=True)
    xc = x - mu
    rstd = lax.rsqrt(jnp.mean(xc * xc, axis=-1, keepdims=True) + EPS)
    return xc * rstd, rstd


def _first_group_lanes():
    return lax.broadcasted_iota(jnp.int32, (SGU_CHUNK, LANES), 1) < 64


def _sgu_fwd(h, ln_g, ln_b, w_s, bias_tile, *, tr, name):
    s = h.shape[0]
    zu_blk, zv_blk = 1536 // D_HALF, 2048 // D_HALF

    def body(zu_ref, zv_ref, lng_ref, lnb_ref, ws_ref, bias_ref, o_ref):
        gzu = _gelu(zu_ref[...])
        xh, _ = _layernorm_stats(_gelu(zv_ref[...]))
        zb = (xh * lng_ref[...] + lnb_ref[...]).astype(BF16)
        mask = _sgu_mask()
        first = _first_group_lanes()
        for pair in range(4):
            cols = slice(pair * LANES, (pair + 1) * LANES)
            w0 = (ws_ref[2 * pair] * mask).astype(BF16)
            w1 = (ws_ref[2 * pair + 1] * mask).astype(BF16)
            for ch in range(tr // SGU_CHUNK):
                rows = slice(ch * SGU_CHUNK, (ch + 1) * SGU_CHUNK)
                zp = zb[rows, cols]
                mixed = jnp.where(first, jnp.dot(w0, zp, preferred_element_type=F32),
                                  jnp.dot(w1, zp, preferred_element_type=F32)) + bias_ref[:, cols]
                o_ref[rows, cols] = gzu[rows, cols] * mixed

    return pl.pallas_call(
        body, name=name, grid=(s // tr,),
        in_specs=[pl.BlockSpec((tr, D_HALF), lambda i: (i, zu_blk)), pl.BlockSpec((tr, D_HALF), lambda i: (i, zv_blk)),
                  pl.BlockSpec((1, D_HALF), lambda i: (0, 0)), pl.BlockSpec((1, D_HALF), lambda i: (0, 0)),
                  pl.BlockSpec((N_HEADS, SGU_CHUNK, SGU_CHUNK), lambda i: (0, 0, 0)),
                  pl.BlockSpec((SGU_CHUNK, D_HALF), lambda i: (0, 0))],
        out_specs=pl.BlockSpec((tr, D_HALF), lambda i: (i, 0)),
        out_shape=jax.ShapeDtypeStruct((s, D_HALF), F32),
        compiler_params=_params("parallel"),
    )(h, h, ln_g, ln_b, w_s, bias_tile)


def _sgu_bwd(dsgu, h, ln_g, ln_b, w_s, bias_tile, *, tr, name):
    s = h.shape[0]
    n = s // tr
    zu_blk, zv_blk = 1536 // D_HALF, 2048 // D_HALF

    def body(ds_ref, zu_ref, zv_ref, lng_ref, lnb_ref, ws_ref, bias_ref,
             dzu_ref, dzv_ref, dws_ref, dlng_ref, dlnb_ref, dbs_ref, dgzu_sc, dzvn_sc, dbias_sc):
        step = pl.program_id(0)

        @pl.when(step == 0)
        def _():
            dws_ref[...] = jnp.zeros_like(dws_ref)
            dlng_ref[...] = jnp.zeros_like(dlng_ref)
            dlnb_ref[...] = jnp.zeros_like(dlnb_ref)
            dbias_sc[...] = jnp.zeros_like(dbias_sc)

        zu = zu_ref[...]
        zv = zv_ref[...]
        gzu = _gelu(zu)
        xh, rstd = _layernorm_stats(_gelu(zv))
        zb = (xh * lng_ref[...] + lnb_ref[...]).astype(BF16)
        ds = ds_ref[...]
        mask = _sgu_mask()
        first = _first_group_lanes()
        tn_dims = (((0,), (0,)), ((), ()))
        nt_dims = (((1,), (1,)), ((), ()))
        for pair in range(4):
            cols = slice(pair * LANES, (pair + 1) * LANES)
            w0 = (ws_ref[2 * pair] * mask).astype(BF16)
            w1 = (ws_ref[2 * pair + 1] * mask).astype(BF16)
            for ch in range(tr // SGU_CHUNK):
                rows = slice(ch * SGU_CHUNK, (ch + 1) * SGU_CHUNK)
                zp = zb[rows, cols]
                mixed = jnp.where(first, jnp.dot(w0, zp, preferred_element_type=F32),
                                  jnp.dot(w1, zp, preferred_element_type=F32)) + bias_ref[:, cols]
                dsp = ds[rows, cols]
                dgzu_sc[rows, cols] = dsp * mixed
                dm = dsp * gzu[rows, cols]
                dbias_sc[:, cols] += dm
                dmb = dm.astype(BF16)
                dm0 = jnp.where(first, dmb, jnp.zeros_like(dmb))
                dm1 = jnp.where(first, jnp.zeros_like(dmb), dmb)
                dws_ref[2 * pair] += lax.dot_general(dm0, zp, nt_dims, preferred_element_type=F32)
                dws_ref[2 * pair + 1] += lax.dot_general(dm1, zp, nt_dims, preferred_element_type=F32)
                dzvn_sc[rows, cols] = jnp.where(first, lax.dot_general(w0, dmb, tn_dims, preferred_element_type=F32),
                                                lax.dot_general(w1, dmb, tn_dims, preferred_element_type=F32))
        dzvn = dzvn_sc[...]
        dlng_ref[...] += jnp.sum(dzvn * xh, axis=0, keepdims=True)
        dlnb_ref[...] += jnp.sum(dzvn, axis=0, keepdims=True)
        dxh = dzvn * lng_ref[...]
        dgzv = rstd * (dxh - jnp.mean(dxh, axis=-1, keepdims=True) - xh * jnp.mean(dxh * xh, axis=-1, keepdims=True))
        dzv_ref[...] = (dgzv * _gelu_grad(zv)).astype(dzv_ref.dtype)
        dzu_ref[...] = (dgzu_sc[...] * _gelu_grad(zu)).astype(dzu_ref.dtype)

        @pl.when(step == n - 1)
        def _():
            for g in range(N_HEADS):
                dws_ref[g] = dws_ref[g] * mask
            lane = lax.broadcasted_iota(jnp.int32, (D_HALF, LANES), 0) // 64
            grp = lax.broadcasted_iota(jnp.int32, (D_HALF, LANES), 1)
            dbs_ref[...] = jnp.dot(dbias_sc[...], (lane == grp).astype(F32), precision=lax.Precision.HIGHEST,
                                   preferred_element_type=F32)

    const2 = lambda i: (0, 0)
    return pl.pallas_call(
        body, name=name, grid=(n,),
        in_specs=[pl.BlockSpec((tr, D_HALF), lambda i: (i, 0)),
                  pl.BlockSpec((tr, D_HALF), lambda i: (i, zu_blk)), pl.BlockSpec((tr, D_HALF), lambda i: (i, zv_blk)),
                  pl.BlockSpec((1, D_HALF), const2), pl.BlockSpec((1, D_HALF), const2),
                  pl.BlockSpec((N_HEADS, SGU_CHUNK, SGU_CHUNK), lambda i: (0, 0, 0)),
                  pl.BlockSpec((SGU_CHUNK, D_HALF), const2)],
        out_specs=[pl.BlockSpec((tr, D_HALF), lambda i: (i, 0)), pl.BlockSpec((tr, D_HALF), lambda i: (i, 0)),
                   pl.BlockSpec((N_HEADS, SGU_CHUNK, SGU_CHUNK), lambda i: (0, 0, 0)),
                   pl.BlockSpec((1, D_HALF), const2), pl.BlockSpec((1, D_HALF), const2),
                   pl.BlockSpec((SGU_CHUNK, LANES), const2)],
        out_shape=[jax.ShapeDtypeStruct((s, D_HALF), BF16), jax.ShapeDtypeStruct((s, D_HALF), BF16),
                   jax.ShapeDtypeStruct((N_HEADS, SGU_CHUNK, SGU_CHUNK), F32),
                   jax.ShapeDtypeStruct((1, D_HALF), F32), jax.ShapeDtypeStruct((1, D_HALF), F32),
                   jax.ShapeDtypeStruct((SGU_CHUNK, LANES), F32)],
        scratch_shapes=[pltpu.VMEM((tr, D_HALF), F32), pltpu.VMEM((tr, D_HALF), F32), pltpu.VMEM((SGU_CHUNK, D_HALF), F32)],
        compiler_params=_params("arbitrary"),
    )(dsgu, h, h, ln_g, ln_b, w_s, bias_tile)


FF_BLOCK = D_FF // 2
DSWIGLU_CHUNK = 384


def _matmul_swiglu(a, w, *, tm, name):
    s, k = a.shape

    def body(a_ref, wg_ref, wu_ref, g_ref, u_ref, act_ref):
        av = a_ref[...].astype(BF16)
        g = jnp.dot(av, wg_ref[...].astype(BF16), preferred_element_type=F32)
        u = jnp.dot(av, wu_ref[...].astype(BF16), preferred_element_type=F32)
        g_ref[...] = g.astype(g_ref.dtype)
        u_ref[...] = u.astype(u_ref.dtype)
        act_ref[...] = (g * jax.nn.sigmoid(g) * u).astype(act_ref.dtype)

    out = pl.BlockSpec((tm, FF_BLOCK), lambda j, i: (i, j))
    return pl.pallas_call(
        body, name=name, grid=(2, s // tm),
        in_specs=[pl.BlockSpec((tm, k), lambda j, i: (i, 0)), pl.BlockSpec((k, FF_BLOCK), lambda j, i: (0, j)),
                  pl.BlockSpec((k, FF_BLOCK), lambda j, i: (0, j + 2))],
        out_specs=[out, out, out],
        out_shape=[jax.ShapeDtypeStruct((s, D_FF), BF16)] * 3,
        compiler_params=_params("parallel", "parallel"),
    )(a, w, w)


def _matmul_dswiglu(dx, w_down, gate, up, *, tm, name):
    s, k = dx.shape

    def body(dx_ref, w_ref, g_ref, u_ref, dg_ref, du_ref):
        dxb = dx_ref[...].astype(BF16)
        for lo in range(0, FF_BLOCK, DSWIGLU_CHUNK):
            cols = slice(lo, min(lo + DSWIGLU_CHUNK, FF_BLOCK))
            d = lax.dot_general(dxb, w_ref[cols, :].astype(BF16), (((1,), (1,)), ((), ())), preferred_element_type=F32)
            g = g_ref[:, cols].astype(F32)
            sig = jax.nn.sigmoid(g)
            dg_ref[:, cols] = (d * u_ref[:, cols].astype(F32) * (sig * (1.0 + g * (1.0 - sig)))).astype(dg_ref.dtype)
            du_ref[:, cols] = (d * (g * sig)).astype(du_ref.dtype)

    blk = pl.BlockSpec((tm, FF_BLOCK), lambda j, i: (i, j))
    return pl.pallas_call(
        body, name=name, grid=(2, s // tm),
        in_specs=[pl.BlockSpec((tm, k), lambda j, i: (i, 0)), pl.BlockSpec((FF_BLOCK, k), lambda j, i: (j, 0)), blk, blk],
        out_specs=[blk, blk], out_shape=[jax.ShapeDtypeStruct((s, D_FF), BF16)] * 2,
        compiler_params=_params("parallel", "parallel"),
    )(dx, w_down, gate, up)


def _matmul_gu_dx(dgate, dup, w_gu, *, tm, name):
    s = dgate.shape[0]
    k = w_gu.shape[0]
    nt_dims = (((1,), (1,)), ((), ()))

    def body(dg_ref, du_ref, wg_ref, wu_ref, o_ref):
        o_ref[...] = (lax.dot_general(dg_ref[...], wg_ref[...], nt_dims, preferred_element_type=F32)
                      + lax.dot_general(du_ref[...], wu_ref[...], nt_dims, preferred_element_type=F32))

    return pl.pallas_call(
        body, name=name, grid=(s // tm,),
        in_specs=[pl.BlockSpec((tm, D_FF), lambda i: (i, 0)), pl.BlockSpec((tm, D_FF), lambda i: (i, 0)),
                  pl.BlockSpec((k, D_FF), lambda i: (0, 0)), pl.BlockSpec((k, D_FF), lambda i: (0, 1))],
        out_specs=pl.BlockSpec((tm, k), lambda i: (i, 0)),
        out_shape=jax.ShapeDtypeStruct((s, k), F32),
        compiler_params=_params("parallel"),
    )(dgate, dup, w_gu, w_gu)


def _sum_slots(stacked, *, tr, name):
    k, r, _ = stacked.shape

    def body(x_ref, o_ref):
        acc = x_ref[0].astype(F32)
        for idx in range(1, k):
            acc = acc + x_ref[idx].astype(F32)
        o_ref[...] = acc

    return pl.pallas_call(
        body, name=name, grid=(r // tr,),
        in_specs=[pl.BlockSpec((k, tr, LANES), lambda i: (0, i, 0))],
        out_specs=pl.BlockSpec((tr, LANES), lambda i: (i, 0)),
        out_shape=jax.ShapeDtypeStruct((r, LANES), F32),
        compiler_params=_params("parallel"),
    )(stacked)


def _adamw(w, g, m, v, *, tr, name):
    r = w.shape[0]

    def body(w_ref, g_ref, m_ref, v_ref, d_ref, m2_ref, v2_ref):
        gv = g_ref[...]
        m2 = ADAM_B1 * m_ref[...] + (1.0 - ADAM_B1) * gv
        v2 = ADAM_B2 * v_ref[...] + (1.0 - ADAM_B2) * jnp.square(gv)
        m_hat = m2 / (1.0 - ADAM_B1 ** ADAM_STEP)
        v_hat = v2 / (1.0 - ADAM_B2 ** ADAM_STEP)
        d_ref[...] = -ADAM_LR * (m_hat / (jnp.sqrt(v_hat) + ADAM_EPS) + ADAM_WD * w_ref[...])
        m2_ref[...] = m2
        v2_ref[...] = v2

    spec = pl.BlockSpec((tr, LANES), lambda i: (i, 0))
    shape = jax.ShapeDtypeStruct((r, LANES), F32)
    return pl.pallas_call(
        body, name=name, grid=(r // tr,), in_specs=[spec] * 4, out_specs=[spec] * 3, out_shape=[shape] * 3,
        compiler_params=_params("parallel"),
    )(w, g, m, v)


PAIR_CHUNKS = 5


def _coords():
    return lax.axis_index("x"), lax.axis_index("y"), lax.axis_index("c")


def _my_chip():
    return 2 * lax.axis_index("x") + lax.axis_index("y")


def _chip_peer(x, y, k):
    px = 1 - x if k & 2 else x
    py = 1 - y if k & 1 else y
    return px, py


def _allgather_chips(shard, *, name):
    r = shard.shape[0]
    rh = r // 2

    def body(src, out, send_sems, recv_sems):
        x, y, c = _coords()
        me = 2 * x + y
        half = pl.ds(c * rh, rh)

        def copy(k, src_ref, slot, to):
            return pltpu.make_async_remote_copy(src_ref=src_ref, dst_ref=out.at[slot, half, :], send_sem=send_sems.at[k],
                                                recv_sem=recv_sems.at[k], device_id=to, device_id_type=MESH)

        first, passed = [], []
        for k in (1, 2, 3):
            px, py = _chip_peer(x, y, k)
            first.append(copy(k - 1, src.at[half, :], me, (px, py, c)))
            first[-1].start()
        for k in (1, 2, 3):
            px, py = _chip_peer(x, y, k)
            slot = 2 * px + py
            first[k - 1].wait_recv()
            passed.append(copy(2 + k, out.at[slot, half, :], slot, (x, y, 1 - c)))
            passed[-1].start()
        for cp in passed:
            cp.wait_recv()
        for cp in first + passed:
            cp.wait_send()

    gathered = pl.pallas_call(
        body, name=name, in_specs=[ANY], out_specs=ANY,
        out_shape=jax.ShapeDtypeStruct((4, r, LANES), shard.dtype),
        scratch_shapes=[pltpu.SemaphoreType.DMA((6,)), pltpu.SemaphoreType.DMA((6,))],
    )(shard)
    return lax.dynamic_update_slice(gathered, shard[None], (_my_chip(), 0, 0))


def _pair_split(grads, *, name):
    _, r, _ = grads.shape
    rh = r // 2
    rc = rh // PAIR_CHUNKS
    nchunk = 4 * PAIR_CHUNKS

    def body(g_ref, theirs_ref, send_sems, recv_sems):
        x, y, c = _coords()
        copies = []
        for j in range(4):
            for q in range(PAIR_CHUNKS):
                idx = j * PAIR_CHUNKS + q
                cp = pltpu.make_async_remote_copy(
                    src_ref=g_ref.at[j, pl.ds((1 - c) * rh + q * rc, rc), :], dst_ref=theirs_ref.at[j, pl.ds(q * rc, rc), :],
                    send_sem=send_sems.at[idx], recv_sem=recv_sems.at[idx], device_id=(x, y, 1 - c), device_id_type=MESH)
                cp.start()
                copies.append(cp)
        for cp in copies:
            cp.wait()

    return pl.pallas_call(
        body, name=name, in_specs=[ANY], out_specs=ANY, out_shape=jax.ShapeDtypeStruct((4, rh, LANES), F32),
        scratch_shapes=[pltpu.SemaphoreType.DMA((nchunk,)), pltpu.SemaphoreType.DMA((nchunk,))],
    )(grads)


def _pair_sum(grads, theirs, half, *, tr, name):
    _, rh, _ = theirs.shape
    nrt = rh // tr

    def body(half_ref, g_ref, t_ref, o_ref):
        o_ref[...] = (g_ref[...] + t_ref[...]).astype(o_ref.dtype)

    return pl.pallas_call(
        body, name=name,
        grid_spec=pltpu.PrefetchScalarGridSpec(
            num_scalar_prefetch=1, grid=(4, nrt),
            in_specs=[pl.BlockSpec((1, tr, LANES), lambda j, i, half_ref: (j, half_ref[0] * nrt + i, 0)),
                      pl.BlockSpec((1, tr, LANES), lambda j, i, half_ref: (j, i, 0))],
            out_specs=pl.BlockSpec((1, tr, LANES), lambda j, i, half_ref: (j, i, 0))),
        out_shape=jax.ShapeDtypeStruct((4, rh, LANES), BF16),
        compiler_params=_params("parallel", "parallel"),
    )(half, grads, theirs)


def _scatter_chips(part, *, name):
    _, rh, _ = part.shape

    def body(p_ref, out, send_sems, recv_sems):
        x, y, c = _coords()
        me = 2 * x + y
        copies = []
        for k in (1, 2, 3):
            px, py = _chip_peer(x, y, k)
            cp = pltpu.make_async_remote_copy(src_ref=p_ref.at[2 * px + py], dst_ref=out.at[me], send_sem=send_sems.at[k - 1],
                                              recv_sem=recv_sems.at[k - 1], device_id=(px, py, c), device_id_type=MESH)
            cp.start()
            copies.append(cp)
        for cp in copies:
            cp.wait()

    from_chips = pl.pallas_call(
        body, name=name, in_specs=[ANY], out_specs=ANY, out_shape=jax.ShapeDtypeStruct((4, rh, LANES), part.dtype),
        scratch_shapes=[pltpu.SemaphoreType.DMA((3,)), pltpu.SemaphoreType.DMA((3,))],
    )(part)
    own = lax.dynamic_index_in_dim(part, _my_chip(), axis=0, keepdims=True)
    return lax.dynamic_update_slice(from_chips, own, (_my_chip(), 0, 0))


def _pair_join(half, *, name):
    rh = half.shape[0]
    nchunk = 2 * PAIR_CHUNKS
    rc = rh // nchunk

    def body(h_ref, out, send_sems, recv_sems):
        x, y, c = _coords()
        copies = []
        for q in range(nchunk):
            src = h_ref.at[pl.ds(q * rc, rc), :]
            rows = out.at[pl.ds(c * rh + q * rc, rc), :]
            cp = pltpu.make_async_remote_copy(src_ref=src, dst_ref=rows, send_sem=send_sems.at[q], recv_sem=recv_sems.at[q],
                                              device_id=(x, y, 1 - c), device_id_type=MESH)
            cp.start()
            copies.append(cp)
        for cp in copies:
            cp.wait()

    joined = pl.pallas_call(
        body, name=name, in_specs=[ANY], out_specs=ANY, out_shape=jax.ShapeDtypeStruct((2 * rh, LANES), F32),
        scratch_shapes=[pltpu.SemaphoreType.DMA((nchunk,)), pltpu.SemaphoreType.DMA((nchunk,))],
    )(half)
    return lax.dynamic_update_slice(joined, half, (lax.axis_index("c") * rh, 0))


def _allgather_all(block, *, name):
    r = block.shape[0]

    def body(src, out, send_sems, recv_sems):
        x, y, c = _coords()
        me = 4 * x + 2 * y + c
        copies = []
        for k in range(1, 8):
            px, py = _chip_peer(x, y, k >> 1)
            pc = 1 - c if k & 1 else c
            cp = pltpu.make_async_remote_copy(src_ref=src, dst_ref=out.at[me], send_sem=send_sems.at[k - 1],
                                              recv_sem=recv_sems.at[k - 1], device_id=(px, py, pc), device_id_type=MESH)
            cp.start()
            copies.append(cp)
        for cp in copies:
            cp.wait()

    gathered = pl.pallas_call(
        body, name=name, in_specs=[ANY], out_specs=ANY, out_shape=jax.ShapeDtypeStruct((8, r, LANES), F32),
        scratch_shapes=[pltpu.SemaphoreType.DMA((7,)), pltpu.SemaphoreType.DMA((7,))],
    )(block)
    return lax.dynamic_update_slice(gathered, block[None], (2 * _my_chip() + lax.axis_index("c"), 0, 0))


def _flatten(arrays, pad_rows=None):
    flat = jnp.concatenate([a.reshape(-1) for a in arrays])
    if pad_rows is not None:
        flat = jnp.pad(flat, (0, pad_rows * LANES - flat.shape[0]))
    return flat.reshape(-1, LANES)


def _unflatten(flat, shapes):
    flat = flat.reshape(-1)
    out, off = [], 0
    for shp in shapes:
        size = 1
        for dim in shp:
            size *= dim
        out.append(flat[off:off + size].reshape(shp))
        off += size
    return out


def _pad_w_in(w):
    pad = jnp.zeros(w.shape[:-1] + (D_IN_PAD - D_IN,), w.dtype)
    return jnp.concatenate([w[..., :1536], w[..., 1544:], w[..., 1536:1544], pad], axis=-1)


def _unpad_w_in(w):
    return jnp.concatenate([w[..., :1536], w[..., 2560:2568], w[..., 1536:2560]], axis=-1)


def _tile(s, want):
    return min(want, s)


def _layer_fwd(x, p, l):
    s = x.shape[0]
    tr = _tile(s, 512)
    tm = _tile(s, 1024)
    xn = _rms_fwd([x], p["mix_g"], out_dtype=BF16, tr=tr, name=f"rms_mix_fwd{l}")
    h = _matmul(xn, p["w_in"], tm=_tile(s, 512), tn=D_IN_PAD, out_dtype=F32, name=f"mm_in{l}")
    c = _gates_fwd(h, p["bf_pad"], tr=_tile(s, 256), name=f"gates_fwd{l}")
    attn, attn_res = _attention_fwd(h, c, tq=_tile(s, 2048), kb=_tile(s, 512), t=_tile(s, 512), name=f"attn_fwd{l}")
    sgu = _sgu_fwd(h, p["ln_g"], p["ln_b"], p["w_s"], p["bias_tile"], tr=_tile(s, 256), name=f"sgu_fwd{l}")
    merged = _rms_fwd([attn, sgu], p["out_g"], out_dtype=BF16, tr=tr, name=f"rms_out_fwd{l}")
    x1 = _matmul(merged, p["w_out"], tm=tm, tn=1024, out_dtype=F32, residual=x, name=f"mm_out{l}")
    xn2 = _rms_fwd([x1], p["ffn_g"], out_dtype=BF16, tr=tr, name=f"rms_ffn_fwd{l}")
    gate, up, act = _matmul_swiglu(xn2, p["w_gu"], tm=_tile(s, 512), name=f"mm_gu{l}")
    x2 = _matmul(act, p["w_down"], tm=tm, tn=1024, out_dtype=F32, residual=x1, name=f"mm_down{l}")
    saved = dict(x=x, xn=xn, h=h, attn_res=attn_res, attn=attn, sgu=sgu, merged=merged, x1=x1, xn2=xn2, gate=gate, up=up, act=act)
    return x2, saved


def _layer_bwd(dx2, p, sv, l):
    s = dx2.shape[0]
    tr = _tile(s, 512)
    tm = _tile(s, 1024)
    ts = _tile(s, 1024)
    g = {}
    g["w_down"] = _matmul_tn(sv["act"], dx2, tm=1408, tn=1024, ts=ts, name=f"mm_down_dw{l}")
    dgate, dup = _matmul_dswiglu(dx2, p["w_down"], sv["gate"], sv["up"], tm=_tile(s, 512), name=f"mm_down_dx{l}")
    g["w_gu"] = jnp.concatenate([_matmul_tn(sv["xn2"], dgate, tm=1024, tn=1408, ts=ts, name=f"mm_gate_dw{l}"),
                                 _matmul_tn(sv["xn2"], dup, tm=1024, tn=1408, ts=ts, name=f"mm_up_dw{l}")], axis=1)
    dxn2 = _matmul_gu_dx(dgate, dup, p["w_gu"], tm=_tile(s, 512), name=f"mm_gu_dx{l}")
    (dx1,), g["ffn_g"] = _rms_bwd(dxn2, [sv["x1"]], p["ffn_g"], residual=dx2, tr=tr, name=f"rms_ffn_bwd{l}")
    g["w_out"] = _matmul_tn(sv["merged"], dx1, tm=1024, tn=1024, ts=ts, name=f"mm_out_dw{l}")
    dmerged = _matmul(dx1, p["w_out"], trans_b=True, tm=tm, tn=1024, out_dtype=F32, name=f"mm_out_dx{l}")
    (dattn, dsgu), g["out_g"], delta = _rms_bwd(dmerged, [sv["attn"], sv["sgu"]], p["out_g"], head_dots=True, tr=tr,
                                                name=f"rms_out_bwd{l}")
    dzu, dzv, g["w_s"], g["ln_g"], g["ln_b"], dbs = _sgu_bwd(dsgu, sv["h"], p["ln_g"], p["ln_b"], p["w_s"], p["bias_tile"],
                                                           tr=_tile(s, 256), name=f"sgu_bwd{l}")
    g["b_s"] = dbs[:, :N_HEADS].T
    dqkv, dc = _attention_bwd(dattn, delta[:, :N_HEADS], sv["attn_res"], name=f"attn_bwd{l}")
    dfl, dbf = _gates_bwd(dc, sv["h"], p["bf_pad"], tr=_tile(s, 256), name=f"gates_bwd{l}")
    g["b_f"] = dbf[0, :N_HEADS]
    dh = jnp.concatenate([dqkv, dzu, dzv, dfl], axis=1)
    g["w_in"] = _matmul_tn(sv["xn"], dh, tm=1024, tn=896, ts=ts, name=f"mm_in_dw{l}")
    dxn = _matmul(dh, p["w_in"], trans_b=True, tm=tm, tn=1024, out_dtype=F32, name=f"mm_in_dx{l}")
    (dx,), g["mix_g"] = _rms_bwd(dxn, [sv["x"]], p["mix_g"], residual=dx1, tr=tr, name=f"rms_mix_bwd{l}")
    return dx, g


def _layer_params(l, w_in_pad, w_out, w_gu, w_down, mix_norm_g, b_f, sgu_ln_g, sgu_ln_b, w_s, b_s, out_norm_g, ffn_norm_g):
    return dict(
        w_in=w_in_pad[l], w_out=w_out[l], w_gu=w_gu[l], w_down=w_down[l],
        mix_g=mix_norm_g[l][None, :], out_g=out_norm_g[l][None, :], ffn_g=ffn_norm_g[l][None, :],
        bf_pad=jnp.pad(b_f[l], (0, LANES - N_HEADS))[None, :],
        ln_g=sgu_ln_g[l][None, :], ln_b=sgu_ln_b[l][None, :], w_s=w_s[l],
        bias_tile=jnp.repeat(b_s[l].T, 64, axis=1),
    )


def _local_step(x, tgt, w_in_pad, w_out, w_gu, w_down, mix_norm_g, b_f, sgu_ln_g, sgu_ln_b, w_s, b_s, out_norm_g,
                ffn_norm_g, final_norm_g):
    depth = w_in_pad.shape[0]
    s = x.shape[0]
    params = [_layer_params(l, w_in_pad, w_out, w_gu, w_down, mix_norm_g, b_f, sgu_ln_g, sgu_ln_b, w_s, b_s, out_norm_g,
                            ffn_norm_g) for l in range(depth)]
    saved = []
    for l in range(depth):
        x, sv = _layer_fwd(x, params[l], l)
        saved.append(sv)
    loss_tile, dx, dfinal = _loss_head(x, tgt, final_norm_g[None, :], tr=_tile(s, 512), name="loss_head")
    grads = [None] * depth
    for l in reversed(range(depth)):
        dx, grads[l] = _layer_bwd(dx, params[l], saved[l], l)
    return loss_tile[0, 0], dx, grads, dfinal[0]


SMALL_ROWS = 4272


def kernel(x, mix_norm_g, w_in, b_f, sgu_ln_g, sgu_ln_b, w_s, b_s, out_norm_g, w_out, ffn_norm_g, w_gate_up, w_down, final_norm_g, loss_target, m_mix_norm_g, m_w_in, m_b_f, m_sgu_ln_g, m_sgu_ln_b, m_w_s, m_b_s, m_out_norm_g, m_w_out, m_ffn_norm_g, m_w_gate_up, m_w_down, m_final_norm_g, v_mix_norm_g, v_w_in, v_b_f, v_sgu_ln_g, v_sgu_ln_b, v_w_s, v_b_s, v_out_norm_g, v_w_out, v_ffn_norm_g, v_w_gate_up, v_w_down, v_final_norm_g):
    big = [w_in, w_out, w_gate_up, w_down]
    big_shapes = [a.shape for a in big]
    small = [mix_norm_g, b_f, sgu_ln_g, sgu_ln_b, w_s, b_s, out_norm_g, ffn_norm_g, final_norm_g]
    small_shapes = [a.shape for a in small]

    gathered = _allgather_chips(_flatten([a.astype(BF16) for a in big]), name="ag_weights")
    per_chip = [_unflatten(gathered[j], big_shapes) for j in range(4)]
    w_in_full = _pad_w_in(jnp.concatenate([pc[0] for pc in per_chip], axis=2))
    w_out_full = jnp.concatenate([pc[1] for pc in per_chip], axis=1)
    w_gu_full = jnp.concatenate([pc[2] for pc in per_chip], axis=2)
    w_down_full = jnp.concatenate([pc[3] for pc in per_chip], axis=1)

    loss_part, dx, grads, dfinal = _local_step(
        x[0], loss_target[0], w_in_full, w_out_full, w_gu_full, w_down_full, mix_norm_g, b_f, sgu_ln_g, sgu_ln_b, w_s, b_s,
        out_norm_g, ffn_norm_g, final_norm_g)
    stack = lambda key: jnp.stack([g[key] for g in grads])
    g_in = _unpad_w_in(stack("w_in"))
    g_out, g_gu, g_down = stack("w_out"), stack("w_gu"), stack("w_down")

    send = jnp.stack([_flatten([g_in[:, :, 642 * j:642 * (j + 1)], g_out[:, 256 * j:256 * (j + 1), :],
                                g_gu[:, :, 1408 * j:1408 * (j + 1)], g_down[:, 704 * j:704 * (j + 1), :]]) for j in range(4)])
    theirs = _pair_split(send, name="rs_pair_split")
    pair_sum = _pair_sum(send, theirs, lax.axis_index("c").astype(jnp.int32).reshape(1), tr=3440, name="rs_pair_sum")
    from_chips = _scatter_chips(pair_sum, name="rs_scatter")
    half = _sum_slots(from_chips, tr=1120, name="rs_chip_sum")
    g_big_flat = _pair_join(half, name="rs_pair_join")

    g_small_local = [stack("mix_g")[:, 0], stack("b_f"), stack("ln_g")[:, 0], stack("ln_b")[:, 0], stack("w_s"), stack("b_s"),
                     stack("out_g")[:, 0], stack("ffn_g")[:, 0], dfinal]
    g_small_flat = _sum_slots(_allgather_all(_flatten(g_small_local, SMALL_ROWS), name="ar_small_gather"), tr=1424,
                              name="ar_small_sum")
    loss = lax.psum(loss_part, ("x", "y", "c"))

    d_big, m_big, v_big = _adamw(_flatten(big), g_big_flat, _flatten([m_w_in, m_w_out, m_w_gate_up, m_w_down]),
                                 _flatten([v_w_in, v_w_out, v_w_gate_up, v_w_down]), tr=2240, name="adamw_big")
    m_small = [m_mix_norm_g, m_b_f, m_sgu_ln_g, m_sgu_ln_b, m_w_s, m_b_s, m_out_norm_g, m_ffn_norm_g, m_final_norm_g]
    v_small = [v_mix_norm_g, v_b_f, v_sgu_ln_g, v_sgu_ln_b, v_w_s, v_b_s, v_out_norm_g, v_ffn_norm_g, v_final_norm_g]
    d_small, m_small2, v_small2 = _adamw(_flatten(small, SMALL_ROWS), g_small_flat, _flatten(m_small, SMALL_ROWS),
                                         _flatten(v_small, SMALL_ROWS), tr=1424, name="adamw_small")

    def in_order(big_flat, small_flat):
        b_in, b_out, b_gu, b_down = _unflatten(big_flat, big_shapes)
        s_mix, s_bf, s_lng, s_lnb, s_ws, s_bs, s_outg, s_ffn, s_fin = _unflatten(small_flat, small_shapes)
        return [s_mix, b_in, s_bf, s_lng, s_lnb, s_ws, s_bs, s_outg, b_out, s_ffn, b_gu, b_down, s_fin]

    return (loss, dx[None], *in_order(g_big_flat, g_small_flat), *in_order(d_big, d_small), *in_order(m_big, m_small2),
            *in_order(v_big, v_small2))
```

```python
import jax
import jax.numpy as jnp
from jax import lax
from jax.experimental import pallas as pl
from jax.experimental.pallas import tpu as pltpu

F32 = jnp.float32
BF16 = jnp.bfloat16

D_MODEL = 1024
D_HALF = 512
N_HEADS = 8
HEAD_DIM = 64
SGU_CHUNK = 128
CAUSAL_CHUNK = 64
D_FF = 2816
D_IN = 2568
D_IN_PAD = 2688
F_COL_BLOCK = 2560 // 128
EPS = 1e-6
NEG = -1e30
LANES = 128
VMEM_LIMIT = 56 * 1024 * 1024

ADAM_LR = 0.001
ADAM_B1 = 0.9
ADAM_B2 = 0.999
ADAM_EPS = 1e-08
ADAM_WD = 0.01
ADAM_STEP = 10

MESH = pl.DeviceIdType.MESH
ANY = pl.BlockSpec(memory_space=pl.ANY)


def _params(*sem):
    return pltpu.CompilerParams(dimension_semantics=sem, vmem_limit_bytes=VMEM_LIMIT)


def _matmul(a, b, *, trans_b=False, tm, tn, out_dtype, residual=None, name):
    m, k = a.shape
    n = b.shape[0] if trans_b else b.shape[1]
    dims = (((1,), (1,)), ((), ())) if trans_b else (((1,), (0,)), ((), ()))

    def body(*refs):
        a_ref, b_ref = refs[0], refs[1]
        o_ref = refs[-1]
        acc = lax.dot_general(a_ref[...].astype(BF16), b_ref[...].astype(BF16), dims,
                              preferred_element_type=F32)
        if residual is not None:
            acc = acc + refs[2][...]
        o_ref[...] = acc.astype(out_dtype)

    b_spec = pl.BlockSpec((tn, k), lambda i, j: (j, 0)) if trans_b else pl.BlockSpec((k, tn), lambda i, j: (0, j))
    in_specs = [pl.BlockSpec((tm, k), lambda i, j: (i, 0)), b_spec]
    args = [a, b]
    if residual is not None:
        in_specs.append(pl.BlockSpec((tm, tn), lambda i, j: (i, j)))
        args.append(residual)
    return pl.pallas_call(
        body, name=name, grid=(m // tm, n // tn), in_specs=in_specs,
        out_specs=pl.BlockSpec((tm, tn), lambda i, j: (i, j)),
        out_shape=jax.ShapeDtypeStruct((m, n), out_dtype),
        compiler_params=_params("parallel", "parallel"),
    )(*args)


def _matmul_tn(a, b, *, tm, tn, ts, name):
    s, m = a.shape
    n = b.shape[1]

    def body(a_ref, b_ref, o_ref):
        @pl.when(pl.program_id(2) == 0)
        def _():
            o_ref[...] = jnp.zeros_like(o_ref)

        o_ref[...] += lax.dot_general(a_ref[...].astype(BF16), b_ref[...].astype(BF16),
                                      (((0,), (0,)), ((), ())), preferred_element_type=F32)

    return pl.pallas_call(
        body, name=name, grid=(m // tm, n // tn, s // ts),
        in_specs=[pl.BlockSpec((ts, tm), lambda i, j, t: (t, i)), pl.BlockSpec((ts, tn), lambda i, j, t: (t, j))],
        out_specs=pl.BlockSpec((tm, tn), lambda i, j, t: (i, j)),
        out_shape=jax.ShapeDtypeStruct((m, n), F32),
        compiler_params=_params("parallel", "parallel", "arbitrary"),
    )(a, b)


def _rms_fwd(xs, g, *, out_dtype, tr, name):
    s = xs[0].shape[0]
    widths = [x.shape[1] for x in xs]
    wsum = sum(widths)
    nx = len(xs)

    def body(*refs):
        g_ref, o_ref = refs[nx], refs[nx + 1]
        off = 0
        for x_ref, w in zip(refs[:nx], widths):
            x = x_ref[...]
            r = lax.rsqrt(jnp.mean(x * x, axis=-1, keepdims=True) + EPS)
            o_ref[:, off:off + w] = (x * r * g_ref[:, off:off + w]).astype(out_dtype)
            off += w

    return pl.pallas_call(
        body, name=name, grid=(s // tr,),
        in_specs=[pl.BlockSpec((tr, w), lambda i: (i, 0)) for w in widths] + [pl.BlockSpec((1, wsum), lambda i: (0, 0))],
        out_specs=pl.BlockSpec((tr, wsum), lambda i: (i, 0)),
        out_shape=jax.ShapeDtypeStruct((s, wsum), out_dtype),
        compiler_params=_params("parallel"),
    )(*xs, g)


def _rms_bwd(dy, xs, g, *, residual=None, head_dots=False, tr, name):
    s = xs[0].shape[0]
    widths = [x.shape[1] for x in xs]
    wsum = sum(widths)
    nx = len(xs)
    nin = 2 + nx + (residual is not None)

    def body(*refs):
        dy_ref, g_ref = refs[0], refs[1 + nx]
        dx_refs, dg_ref = refs[nin:nin + nx], refs[nin + nx]

        @pl.when(pl.program_id(0) == 0)
        def _():
            dg_ref[...] = jnp.zeros_like(dg_ref)

        off = 0
        for idx, (x_ref, w) in enumerate(zip(refs[1:1 + nx], widths)):
            x = x_ref[...]
            r = lax.rsqrt(jnp.mean(x * x, axis=-1, keepdims=True) + EPS)
            xh = x * r
            dyv = dy_ref[:, off:off + w]
            dxh = dyv * g_ref[:, off:off + w]
            dx = r * (dxh - xh * jnp.mean(dxh * xh, axis=-1, keepdims=True))
            if residual is not None and idx == 0:
                dx = dx + refs[2 + nx][...]
            dx_refs[idx][...] = dx
            dg_ref[:, off:off + w] += jnp.sum(dyv * xh, axis=0, keepdims=True)
            if head_dots and idx == 0:
                col = lax.broadcasted_iota(jnp.int32, (w, LANES), 0) // HEAD_DIM
                head = lax.broadcasted_iota(jnp.int32, (w, LANES), 1)
                refs[nin + nx + 1][...] = jnp.dot(dx * x, (col == head).astype(F32), precision=lax.Precision.HIGHEST,
                                                  preferred_element_type=F32)
            off += w

    in_specs = ([pl.BlockSpec((tr, wsum), lambda i: (i, 0))]
                + [pl.BlockSpec((tr, w), lambda i: (i, 0)) for w in widths]
                + [pl.BlockSpec((1, wsum), lambda i: (0, 0))])
    args = [dy, *xs, g]
    if residual is not None:
        in_specs.append(pl.BlockSpec((tr, widths[0]), lambda i: (i, 0)))
        args.append(residual)
    out_specs = [pl.BlockSpec((tr, w), lambda i: (i, 0)) for w in widths] + [pl.BlockSpec((1, wsum), lambda i: (0, 0))]
    out_shape = [jax.ShapeDtypeStruct((s, w), F32) for w in widths] + [jax.ShapeDtypeStruct((1, wsum), F32)]
    if head_dots:
        out_specs.append(pl.BlockSpec((tr, LANES), lambda i: (i, 0)))
        out_shape.append(jax.ShapeDtypeStruct((s, LANES), F32))
    outs = pl.pallas_call(
        body, name=name, grid=(s // tr,), in_specs=in_specs, out_specs=out_specs, out_shape=out_shape,
        compiler_params=_params("arbitrary"),
    )(*args)
    if head_dots:
        return outs[:nx], outs[nx], outs[nx + 1]
    return outs[:nx], outs[nx]


def _loss_head(x, tgt, g, *, tr, name):
    s, d = x.shape

    def body(x_ref, t_ref, g_ref, loss_ref, dx_ref, dg_ref):
        @pl.when(pl.program_id(0) == 0)
        def _():
            loss_ref[...] = jnp.zeros_like(loss_ref)
            dg_ref[...] = jnp.zeros_like(dg_ref)

        xv = x_ref[...]
        r = lax.rsqrt(jnp.mean(xv * xv, axis=-1, keepdims=True) + EPS)
        xh = xv * r
        err = xh * g_ref[...] - t_ref[...]
        loss_ref[...] += 0.5 * jnp.sum(jnp.mean(err * err, axis=-1, keepdims=True))
        dy = err * (1.0 / d)
        dxh = dy * g_ref[...]
        dx_ref[...] = r * (dxh - xh * jnp.mean(dxh * xh, axis=-1, keepdims=True))
        dg_ref[...] += jnp.sum(dy * xh, axis=0, keepdims=True)

    return pl.pallas_call(
        body, name=name, grid=(s // tr,),
        in_specs=[pl.BlockSpec((tr, d), lambda i: (i, 0)), pl.BlockSpec((tr, d), lambda i: (i, 0)),
                  pl.BlockSpec((1, d), lambda i: (0, 0))],
        out_specs=[pl.BlockSpec((8, LANES), lambda i: (0, 0)), pl.BlockSpec((tr, d), lambda i: (i, 0)),
                   pl.BlockSpec((1, d), lambda i: (0, 0))],
        out_shape=[jax.ShapeDtypeStruct((8, LANES), F32), jax.ShapeDtypeStruct((s, d), F32),
                   jax.ShapeDtypeStruct((1, d), F32)],
        compiler_params=_params("arbitrary"),
    )(x, tgt, g)


def _gates_fwd(h, bf_pad, *, tr, name):
    s = h.shape[0]

    def body(fl_ref, b_ref, c_ref, carry_ref):
        @pl.when(pl.program_id(0) == 0)
        def _():
            carry_ref[...] = jnp.zeros_like(carry_ref)

        lf = jax.nn.log_sigmoid(fl_ref[...] + b_ref[...])
        row = lax.broadcasted_iota(jnp.int32, (tr, tr), 0)
        col = lax.broadcasted_iota(jnp.int32, (tr, tr), 1)
        tri = (col <= row).astype(F32)
        c_ref[...] = jnp.dot(tri, lf, precision=lax.Precision.HIGHEST, preferred_element_type=F32) + carry_ref[...]
        carry_ref[...] += jnp.sum(lf, axis=0, keepdims=True)

    return pl.pallas_call(
        body, name=name, grid=(s // tr,),
        in_specs=[pl.BlockSpec((tr, LANES), lambda i: (i, F_COL_BLOCK)), pl.BlockSpec((1, LANES), lambda i: (0, 0))],
        out_specs=pl.BlockSpec((tr, LANES), lambda i: (i, 0)),
        out_shape=jax.ShapeDtypeStruct((s, LANES), F32),
        scratch_shapes=[pltpu.VMEM((1, LANES), F32)],
        compiler_params=_params("arbitrary"),
    )(h, bf_pad)


def _gates_bwd(dc, h, bf_pad, *, tr, name):
    s = h.shape[0]
    n = s // tr

    def body(dc_ref, fl_ref, b_ref, dfl_ref, db_ref, carry_ref):
        @pl.when(pl.program_id(0) == 0)
        def _():
            carry_ref[...] = jnp.zeros_like(carry_ref)
            db_ref[...] = jnp.zeros_like(db_ref)

        dcv = dc_ref[...]
        row = lax.broadcasted_iota(jnp.int32, (tr, tr), 0)
        col = lax.broadcasted_iota(jnp.int32, (tr, tr), 1)
        triu = (col >= row).astype(F32)
        dlf = jnp.dot(triu, dcv, precision=lax.Precision.HIGHEST, preferred_element_type=F32) + carry_ref[...]
        carry_ref[...] += jnp.sum(dcv, axis=0, keepdims=True)
        dfl = dlf * jax.nn.sigmoid(-(fl_ref[...] + b_ref[...]))
        dfl_ref[...] = dfl.astype(dfl_ref.dtype)
        db_ref[...] += jnp.sum(dfl, axis=0, keepdims=True)

    return pl.pallas_call(
        body, name=name, grid=(n,),
        in_specs=[pl.BlockSpec((tr, LANES), lambda i: (n - 1 - i, 0)),
                  pl.BlockSpec((tr, LANES), lambda i: (n - 1 - i, F_COL_BLOCK)),
                  pl.BlockSpec((1, LANES), lambda i: (0, 0))],
        out_specs=[pl.BlockSpec((tr, LANES), lambda i: (n - 1 - i, 0)), pl.BlockSpec((1, LANES), lambda i: (0, 0))],
        out_shape=[jax.ShapeDtypeStruct((s, LANES), BF16), jax.ShapeDtypeStruct((1, LANES), F32)],
        scratch_shapes=[pltpu.VMEM((1, LANES), F32)],
        compiler_params=_params("arbitrary"),
    )(dc, h, bf_pad)


LOG2E = 1.4426950408889634
LN2 = 0.6931471805599453
V_ROWS = 80


def _transpose_bf16(a):
    return a.astype(F32).T.astype(BF16)


SKIP_MARGIN_FWD = 140.0
SKIP_MARGIN_BWD_BASE = 136.0
SMEM = pl.BlockSpec(memory_space=pltpu.SMEM)


def _attn_fwd(qat, ka, va, gnorm, cmax_q, cmin_k, *, tq, kb, name):
    s = ka.shape[0]
    nq = s // tq
    per_tile = tq // kb
    tt = qat.shape[3]
    sub = tq // tt

    def body(gn_ref, cq_ref, ck_ref, qat_ref, ka_ref, va_ref, o_ref, lse_ref):
        h, i = pl.program_id(0), pl.program_id(1)
        qat = jnp.concatenate([qat_ref[0, d] for d in range(sub)], axis=1)

        def blk(n, carry, masked):
            m, acc = carry
            rows = pl.ds(pl.multiple_of(n * kb, kb), kb)
            sc = jnp.dot(ka_ref[rows, :], qat, preferred_element_type=F32)
            if masked:
                key = n * kb + lax.broadcasted_iota(jnp.int32, (kb, tq), 0)
                qry = i * tq + lax.broadcasted_iota(jnp.int32, (kb, tq), 1)
                sc = jnp.where(key <= qry, sc, NEG)
            m_new = jnp.maximum(m, jnp.max(sc, axis=0, keepdims=True))
            p = jnp.exp2(sc - m_new)
            vt = _transpose_bf16(va_ref[rows, :])[:V_ROWS]
            acc = jnp.exp2(m - m_new) * acc + jnp.dot(vt, p.astype(BF16), preferred_element_type=F32)
            return m_new, acc

        carry = (jnp.full((1, tq), NEG, F32), jnp.zeros((V_ROWS, tq), F32))
        for d in reversed(range(per_tile)):
            carry = blk(i * per_tile + d, carry, True)
        top = gn_ref[h] + cq_ref[h, i]

        def live(state):
            n, m_min = state[0], state[1]
            return jnp.logical_and(n >= 0, top - ck_ref[h, jnp.maximum(n, 0)] >= m_min - SKIP_MARGIN_FWD)

        def step(state):
            n, _, m, acc = state
            m, acc = blk(n, (m, acc), False)
            return n - 1, jnp.min(m), m, acc

        _, _, m, acc = lax.while_loop(live, step, (i * per_tile - 1, jnp.min(carry[0]), *carry))
        l = acc[HEAD_DIM:HEAD_DIM + 1, :]
        padded = jnp.concatenate([acc / l, jnp.zeros((LANES - V_ROWS, tq), F32)], axis=0)
        o_ref[0, 0] = padded.T[:, :HEAD_DIM]
        lse_ref[0, 0] = m + jnp.log2(l)

    return pl.pallas_call(
        body, name=name, grid=(N_HEADS, nq),
        in_specs=[SMEM, SMEM, SMEM, pl.BlockSpec((1, sub, LANES, tt), lambda h, i: (h, i, 0, 0)),
                  pl.BlockSpec((s, LANES), lambda h, i: (0, h)),
                  pl.BlockSpec((s, LANES), lambda h, i: (0, h))],
        out_specs=[pl.BlockSpec((1, 1, tq, HEAD_DIM), lambda h, i: (h, i, 0, 0)),
                   pl.BlockSpec((1, 1, 1, tq), lambda h, i: (h, i, 0, 0))],
        out_shape=[jax.ShapeDtypeStruct((N_HEADS, nq, tq, HEAD_DIM), F32), jax.ShapeDtypeStruct((N_HEADS, nq, 1, tq), F32)],
        compiler_params=_params("parallel", "arbitrary"),
    )(gnorm, cmax_q, cmin_k, qat, ka, va)


def _attn_bwd(qa, qat, doa, dot, lse, dl, ka, va, gnorm, cmin_k, reach, margin, *, name):
    s = qa.shape[0]
    nt, t = qat.shape[1], qat.shape[3]

    def body(gn_ref, ck_ref, reach_ref, mg_ref, qa_ref, qat_ref, do_ref, dot_ref, lse_ref, dl_ref, ka_ref, va_ref,
             dq_ref, dk_ref, dv_ref):
        h, j = pl.program_id(0), pl.program_id(1)

        @pl.when(j == 0)
        def _():
            dq_ref[...] = jnp.zeros_like(dq_ref)

        ka_j, va_j = ka_ref[...], va_ref[...]

        def tile(i, carry, masked):
            dk, dv = carry
            rows = pl.ds(pl.multiple_of(i * t, t), t)
            qa_i, do_i = qa_ref[rows, :], do_ref[rows, :]
            st = jnp.dot(ka_j, qat_ref[0, i], preferred_element_type=F32) - lse_ref[0, i]
            if masked:
                key = lax.broadcasted_iota(jnp.int32, (t, t), 0)
                qry = lax.broadcasted_iota(jnp.int32, (t, t), 1)
                st = jnp.where(key <= qry, st, NEG)
            pt = jnp.exp2(st)
            dpt = jnp.dot(va_j, dot_ref[0, i], preferred_element_type=F32)
            dsb = (pt * (dpt - dl_ref[0, i])).astype(BF16)
            dv = dv + jnp.dot(pt.astype(BF16), do_i, preferred_element_type=F32)
            dk = dk + jnp.dot(dsb, qa_i, preferred_element_type=F32)
            dq_ref[rows, :] += lax.dot_general(dsb, ka_j, (((0,), (0,)), ((), ())), preferred_element_type=F32)
            return dk, dv

        carry = tile(j, (jnp.zeros((t, LANES), F32), jnp.zeros((t, LANES), F32)), True)
        base = gn_ref[h] - ck_ref[h, j]

        def live(state):
            i = state[0]
            return jnp.logical_and(i < nt, base + reach_ref[h, jnp.minimum(i, nt - 1)] >= -mg_ref[h])

        def step(state):
            i, dk, dv = state
            dk, dv = tile(i, (dk, dv), False)
            return i + 1, dk, dv

        _, dk, dv = lax.while_loop(live, step, (j + 1, *carry))
        dk_ref[...] = dk
        dv_ref[...] = dv

    res = pl.BlockSpec((s, LANES), lambda h, j: (0, h))
    rest = pl.BlockSpec((1, nt, LANES, t), lambda h, j: (h, 0, 0, 0))
    row = pl.BlockSpec((1, nt, 1, t), lambda h, j: (h, 0, 0, 0))
    blk = pl.BlockSpec((t, LANES), lambda h, j: (j, h))
    shape = jax.ShapeDtypeStruct((s, N_HEADS * LANES), F32)
    return pl.pallas_call(
        body, name=name, grid=(N_HEADS, nt),
        in_specs=[SMEM, SMEM, SMEM, SMEM, res, rest, res, rest, row, row, blk, blk], out_specs=[res, blk, blk],
        out_shape=[shape, shape, shape], compiler_params=_params("parallel", "arbitrary"),
    )(gnorm, cmin_k, reach, margin, qa, qat, doa, dot, lse, dl, ka, va)


def _spread_matrix():
    r = lax.broadcasted_iota(jnp.int32, (D_HALF, N_HEADS * LANES), 0)
    c = lax.broadcasted_iota(jnp.int32, (D_HALF, N_HEADS * LANES), 1)
    return jnp.logical_and(c // LANES == r // HEAD_DIM, c % LANES == r % HEAD_DIM).astype(BF16)


def _piece_matrix(base):
    r = lax.broadcasted_iota(jnp.int32, (3 * LANES, N_HEADS * LANES), 0)
    c = lax.broadcasted_iota(jnp.int32, (3 * LANES, N_HEADS * LANES), 1)
    return jnp.logical_and(r % LANES < N_HEADS, c == LANES * (r % LANES) + base + r // LANES).astype(BF16)


def _ones_columns(first, count):
    c = lax.broadcasted_iota(jnp.int32, (1, N_HEADS * LANES), 1) % LANES
    return jnp.logical_and(c >= first, c < first + count).astype(F32)


def _round_bf16(x):
    return x.astype(BF16).astype(F32)


def _write_transposed(a, out_ref):
    for h in range(N_HEADS):
        out_ref[h, 0] = _transpose_bf16(a[:, h * LANES:(h + 1) * LANES])


def _pack_qkv(h, c, *, tr, name):
    s = h.shape[0]
    wide = N_HEADS * LANES

    def body(q_ref, k_ref, v_ref, c_ref, qa_ref, ka_ref, va_ref, qat_ref):
        spread = _spread_matrix()
        c2 = c_ref[...] * LOG2E
        p1 = _round_bf16(c2)
        p2 = _round_bf16(c2 - p1)
        p3 = _round_bf16(c2 - p1 - p2)
        pieces = jnp.concatenate([p1, p2, p3], axis=1).astype(BF16)
        dot = lambda a, b: jnp.dot(a, b, preferred_element_type=F32)
        qa = (dot((q_ref[...] * 0.125).astype(BF16), spread) + dot(pieces, _piece_matrix(HEAD_DIM + 3))
              + _ones_columns(HEAD_DIM, 3)).astype(BF16)
        ka = (dot((k_ref[...] * LOG2E).astype(BF16), spread) - dot(pieces, _piece_matrix(HEAD_DIM))
              + _ones_columns(HEAD_DIM + 3, 3)).astype(BF16)
        qa_ref[...] = qa
        ka_ref[...] = ka
        va_ref[...] = (dot(v_ref[...].astype(BF16), spread) + _ones_columns(HEAD_DIM, 1)).astype(BF16)
        _write_transposed(qa, qat_ref)

    shape = jax.ShapeDtypeStruct((s, wide), BF16)
    return pl.pallas_call(
        body, name=name, grid=(s // tr,),
        in_specs=[pl.BlockSpec((tr, D_HALF), lambda i: (i, 0)), pl.BlockSpec((tr, D_HALF), lambda i: (i, 1)),
                  pl.BlockSpec((tr, D_HALF), lambda i: (i, 2)), pl.BlockSpec((tr, LANES), lambda i: (i, 0))],
        out_specs=[pl.BlockSpec((tr, wide), lambda i: (i, 0))] * 3 + [pl.BlockSpec((N_HEADS, 1, LANES, tr), lambda i: (0, i, 0, 0))],
        out_shape=[shape, shape, shape, jax.ShapeDtypeStruct((N_HEADS, s // tr, LANES, tr), BF16)],
        compiler_params=_params("parallel"),
    )(h, h, h, c)


def _pack_do(dattn, *, tr, name):
    s = dattn.shape[0]
    wide = N_HEADS * LANES

    def body(d_ref, doa_ref, dot_ref):
        doa = jnp.dot(d_ref[...].astype(BF16), _spread_matrix(), preferred_element_type=F32).astype(BF16)
        doa_ref[...] = doa
        _write_transposed(doa, dot_ref)

    return pl.pallas_call(
        body, name=name, grid=(s // tr,),
        in_specs=[pl.BlockSpec((tr, D_HALF), lambda i: (i, 0))],
        out_specs=[pl.BlockSpec((tr, wide), lambda i: (i, 0)), pl.BlockSpec((N_HEADS, 1, LANES, tr), lambda i: (0, i, 0, 0))],
        out_shape=[jax.ShapeDtypeStruct((s, wide), BF16), jax.ShapeDtypeStruct((N_HEADS, s // tr, LANES, tr), BF16)],
        compiler_params=_params("parallel"),
    )(dattn)


def _unpack_grads(dqa, dka, dva, *, tr, name):
    s = dqa.shape[0]
    wide = N_HEADS * LANES

    def body(dq_ref, dk_ref, dv_ref, o_ref, dc_ref):
        gather = _spread_matrix()
        nt_dims = (((1,), (1,)), ((), ()))
        pick = lambda a: lax.dot_general(a.astype(BF16), gather, nt_dims, preferred_element_type=F32).astype(BF16)
        dq, dk = dq_ref[...], dk_ref[...]
        o_ref[:, 0:D_HALF] = pick(dq * (LN2 * 0.125))
        o_ref[:, D_HALF:2 * D_HALF] = pick(dk)
        o_ref[:, 2 * D_HALF:3 * D_HALF] = pick(dv_ref[...])
        lane = lax.broadcasted_iota(jnp.int32, (1, wide), 1) % LANES
        both = jnp.where(lane == HEAD_DIM + 3, dq, 0.0) - jnp.where(lane == HEAD_DIM, dk, 0.0)
        r = lax.broadcasted_iota(jnp.int32, (wide, LANES), 0)
        head = lax.broadcasted_iota(jnp.int32, (wide, LANES), 1)
        dc_ref[...] = jnp.dot(both, (r // LANES == head).astype(F32), precision=lax.Precision.HIGHEST,
                              preferred_element_type=F32)

    spec = pl.BlockSpec((tr, wide), lambda i: (i, 0))
    return pl.pallas_call(
        body, name=name, grid=(s // tr,), in_specs=[spec, spec, spec],
        out_specs=[pl.BlockSpec((tr, 3 * D_HALF), lambda i: (i, 0)), pl.BlockSpec((tr, LANES), lambda i: (i, 0))],
        out_shape=[jax.ShapeDtypeStruct((s, 3 * D_HALF), BF16), jax.ShapeDtypeStruct((s, LANES), F32)],
        compiler_params=_params("parallel"),
    )(dqa, dka, dva)


def _attention_fwd(h, c, *, tq, kb, t, name):
    s = h.shape[0]
    qa, ka, va, qat = _pack_qkv(h, c, tr=t, name=name + "_pack")
    c2 = c[:, :N_HEADS] * LOG2E
    head_norm = lambda a: jnp.sqrt(jnp.max(jnp.sum(jnp.square(a.reshape(s, N_HEADS, HEAD_DIM)), axis=-1), axis=0))
    gnorm = head_norm(h[:, 0:D_HALF] * 0.125) * head_norm(h[:, D_HALF:2 * D_HALF] * LOG2E) * 1.01 + 1.0
    cmax_q = jnp.max(c2.reshape(s // tq, tq, N_HEADS), axis=1).T
    cmin_k = lax.cummin(jnp.min(c2.reshape(s // kb, kb, N_HEADS), axis=1), axis=0).T
    o4, lse2 = _attn_fwd(qat, ka, va, gnorm, cmax_q, cmin_k, tq=tq, kb=kb, name=name)
    attn = o4.reshape(N_HEADS, s, HEAD_DIM).transpose(1, 0, 2).reshape(s, N_HEADS * HEAD_DIM)
    return attn, dict(qa=qa, qat=qat, ka=ka, va=va, lse2=lse2, c2=c2, gnorm=gnorm)


def _attention_bwd(dattn, delta, res, *, name):
    s = dattn.shape[0]
    t = res["qat"].shape[3]
    doa, dot = _pack_do(dattn, tr=t, name=name + "_pack")
    c2t = res["c2"].reshape(s // t, t, N_HEADS)
    lse_t = res["lse2"].reshape(N_HEADS, s // t, t)
    reach = lax.cummax(jnp.max(c2t.transpose(2, 0, 1) - lse_t, axis=2), axis=1, reverse=True)
    head_norm = lambda a: jnp.sqrt(jnp.max(jnp.sum(jnp.square(a), axis=-1), axis=0))
    v_norm = head_norm(res["va"].reshape(s, N_HEADS, LANES)[:, :, :HEAD_DIM].astype(F32))
    spread = v_norm * head_norm(dattn.reshape(s, N_HEADS, HEAD_DIM)) * 1.01 + jnp.max(jnp.abs(delta), axis=0)
    margin = SKIP_MARGIN_BWD_BASE + jnp.maximum(0.0, jnp.log2(spread + 1e-30))
    dqa, dka, dva = _attn_bwd(res["qa"], res["qat"], doa, dot, lse_t.reshape(N_HEADS, s // t, 1, t),
                              delta.T.reshape(N_HEADS, s // t, 1, t), res["ka"], res["va"], res["gnorm"],
                              jnp.min(c2t, axis=1).T, reach, margin, name=name)
    return _unpack_grads(dqa, dka, dva, tr=t, name=name + "_unpack")


def _gelu(z):
    return 0.5 * z * (1.0 + lax.erf(z * 0.7071067811865476))


def _gelu_grad(z):
    return 0.5 * (1.0 + lax.erf(z * 0.7071067811865476)) + z * (0.3989422804014327 * jnp.exp(-0.5 * z * z))


def _sgu_mask():
    i = lax.broadcasted_iota(jnp.int32, (SGU_CHUNK, SGU_CHUNK), 0) // CAUSAL_CHUNK
    j = lax.broadcasted_iota(jnp.int32, (SGU_CHUNK, SGU_CHUNK), 1) // CAUSAL_CHUNK
    return (j <= i).astype(F32)


def _layernorm_stats(x):
    mu = jnp.mean(x, axis=-1, keepdims=True)
    xc = x - mu
    rstd = lax.rsqrt(jnp.mean(xc * xc, axis=-1, keepdims=True) + EPS)
    return xc * rstd, rstd


def _first_group_lanes():
    return lax.broadcasted_iota(jnp.int32, (SGU_CHUNK, LANES), 1) < 64


def _sgu_fwd(h, ln_g, ln_b, w_s, bias_tile, *, tr, name):
    s = h.shape[0]
    zu_blk, zv_blk = 1536 // D_HALF, 2048 // D_HALF

    def body(zu_ref, zv_ref, lng_ref, lnb_ref, ws_ref, bias_ref, o_ref):
        gzu = _gelu(zu_ref[...])
        xh, _ = _layernorm_stats(_gelu(zv_ref[...]))
        zb = (xh * lng_ref[...] + lnb_ref[...]).astype(BF16)
        mask = _sgu_mask()
        first = _first_group_lanes()
        for pair in range(4):
            cols = slice(pair * LANES, (pair + 1) * LANES)
            w0 = (ws_ref[2 * pair] * mask).astype(BF16)
            w1 = (ws_ref[2 * pair + 1] * mask).astype(BF16)
            for ch in range(tr // SGU_CHUNK):
                rows = slice(ch * SGU_CHUNK, (ch + 1) * SGU_CHUNK)
                zp = zb[rows, cols]
                mixed = jnp.where(first, jnp.dot(w0, zp, preferred_element_type=F32),
                                  jnp.dot(w1, zp, preferred_element_type=F32)) + bias_ref[:, cols]
                o_ref[rows, cols] = gzu[rows, cols] * mixed

    return pl.pallas_call(
        body, name=name, grid=(s // tr,),
        in_specs=[pl.BlockSpec((tr, D_HALF), lambda i: (i, zu_blk)), pl.BlockSpec((tr, D_HALF), lambda i: (i, zv_blk)),
                  pl.BlockSpec((1, D_HALF), lambda i: (0, 0)), pl.BlockSpec((1, D_HALF), lambda i: (0, 0)),
                  pl.BlockSpec((N_HEADS, SGU_CHUNK, SGU_CHUNK), lambda i: (0, 0, 0)),
                  pl.BlockSpec((SGU_CHUNK, D_HALF), lambda i: (0, 0))],
        out_specs=pl.BlockSpec((tr, D_HALF), lambda i: (i, 0)),
        out_shape=jax.ShapeDtypeStruct((s, D_HALF), F32),
        compiler_params=_params("parallel"),
    )(h, h, ln_g, ln_b, w_s, bias_tile)


def _sgu_bwd(dsgu, h, ln_g, ln_b, w_s, bias_tile, *, tr, name):
    s = h.shape[0]
    n = s // tr
    zu_blk, zv_blk = 1536 // D_HALF, 2048 // D_HALF

    def body(ds_ref, zu_ref, zv_ref, lng_ref, lnb_ref, ws_ref, bias_ref,
             dzu_ref, dzv_ref, dws_ref, dlng_ref, dlnb_ref, dbs_ref, dgzu_sc, dzvn_sc, dbias_sc):
        step = pl.program_id(0)

        @pl.when(step == 0)
        def _():
            dws_ref[...] = jnp.zeros_like(dws_ref)
            dlng_ref[...] = jnp.zeros_like(dlng_ref)
            dlnb_ref[...] = jnp.zeros_like(dlnb_ref)
            dbias_sc[...] = jnp.zeros_like(dbias_sc)

        zu = zu_ref[...]
        zv = zv_ref[...]
        gzu = _gelu(zu)
        xh, rstd = _layernorm_stats(_gelu(zv))
        zb = (xh * lng_ref[...] + lnb_ref[...]).astype(BF16)
        ds = ds_ref[...]
        mask = _sgu_mask()
        first = _first_group_lanes()
        tn_dims = (((0,), (0,)), ((), ()))
        nt_dims = (((1,), (1,)), ((), ()))
        for pair in range(4):
            cols = slice(pair * LANES, (pair + 1) * LANES)
            w0 = (ws_ref[2 * pair] * mask).astype(BF16)
            w1 = (ws_ref[2 * pair + 1] * mask).astype(BF16)
            for ch in range(tr // SGU_CHUNK):
                rows = slice(ch * SGU_CHUNK, (ch + 1) * SGU_CHUNK)
                zp = zb[rows, cols]
                mixed = jnp.where(first, jnp.dot(w0, zp, preferred_element_type=F32),
                                  jnp.dot(w1, zp, preferred_element_type=F32)) + bias_ref[:, cols]
                dsp = ds[rows, cols]
                dgzu_sc[rows, cols] = dsp * mixed
                dm = dsp * gzu[rows, cols]
                dbias_sc[:, cols] += dm
                dmb = dm.astype(BF16)
                dm0 = jnp.where(first, dmb, jnp.zeros_like(dmb))
                dm1 = jnp.where(first, jnp.zeros_like(dmb), dmb)
                dws_ref[2 * pair] += lax.dot_general(dm0, zp, nt_dims, preferred_element_type=F32)
                dws_ref[2 * pair + 1] += lax.dot_general(dm1, zp, nt_dims, preferred_element_type=F32)
                dzvn_sc[rows, cols] = jnp.where(first, lax.dot_general(w0, dmb, tn_dims, preferred_element_type=F32),
                                                lax.dot_general(w1, dmb, tn_dims, preferred_element_type=F32))
        dzvn = dzvn_sc[...]
        dlng_ref[...] += jnp.sum(dzvn * xh, axis=0, keepdims=True)
        dlnb_ref[...] += jnp.sum(dzvn, axis=0, keepdims=True)
        dxh = dzvn * lng_ref[...]
        dgzv = rstd * (dxh - jnp.mean(dxh, axis=-1, keepdims=True) - xh * jnp.mean(dxh * xh, axis=-1, keepdims=True))
        dzv_ref[...] = (dgzv * _gelu_grad(zv)).astype(dzv_ref.dtype)
        dzu_ref[...] = (dgzu_sc[...] * _gelu_grad(zu)).astype(dzu_ref.dtype)

        @pl.when(step == n - 1)
        def _():
            for g in range(N_HEADS):
                dws_ref[g] = dws_ref[g] * mask
            lane = lax.broadcasted_iota(jnp.int32, (D_HALF, LANES), 0) // 64
            grp = lax.broadcasted_iota(jnp.int32, (D_HALF, LANES), 1)
            dbs_ref[...] = jnp.dot(dbias_sc[...], (lane == grp).astype(F32), precision=lax.Precision.HIGHEST,
                                   preferred_element_type=F32)

    const2 = lambda i: (0, 0)
    return pl.pallas_call(
        body, name=name, grid=(n,),
        in_specs=[pl.BlockSpec((tr, D_HALF), lambda i: (i, 0)),
                  pl.BlockSpec((tr, D_HALF), lambda i: (i, zu_blk)), pl.BlockSpec((tr, D_HALF), lambda i: (i, zv_blk)),
                  pl.BlockSpec((1, D_HALF), const2), pl.BlockSpec((1, D_HALF), const2),
                  pl.BlockSpec((N_HEADS, SGU_CHUNK, SGU_CHUNK), lambda i: (0, 0, 0)),
                  pl.BlockSpec((SGU_CHUNK, D_HALF), const2)],
        out_specs=[pl.BlockSpec((tr, D_HALF), lambda i: (i, 0)), pl.BlockSpec((tr, D_HALF), lambda i: (i, 0)),
                   pl.BlockSpec((N_HEADS, SGU_CHUNK, SGU_CHUNK), lambda i: (0, 0, 0)),
                   pl.BlockSpec((1, D_HALF), const2), pl.BlockSpec((1, D_HALF), const2),
                   pl.BlockSpec((SGU_CHUNK, LANES), const2)],
        out_shape=[jax.ShapeDtypeStruct((s, D_HALF), BF16), jax.ShapeDtypeStruct((s, D_HALF), BF16),
                   jax.ShapeDtypeStruct((N_HEADS, SGU_CHUNK, SGU_CHUNK), F32),
                   jax.ShapeDtypeStruct((1, D_HALF), F32), jax.ShapeDtypeStruct((1, D_HALF), F32),
                   jax.ShapeDtypeStruct((SGU_CHUNK, LANES), F32)],
        scratch_shapes=[pltpu.VMEM((tr, D_HALF), F32), pltpu.VMEM((tr, D_HALF), F32), pltpu.VMEM((SGU_CHUNK, D_HALF), F32)],
        compiler_params=_params("arbitrary"),
    )(dsgu, h, h, ln_g, ln_b, w_s, bias_tile)


FF_BLOCK = D_FF // 2
DSWIGLU_CHUNK = 384


def _matmul_swiglu(a, w, *, tm, name):
    s, k = a.shape

    def body(a_ref, wg_ref, wu_ref, g_ref, u_ref, act_ref):
        av = a_ref[...].astype(BF16)
        g = jnp.dot(av, wg_ref[...].astype(BF16), preferred_element_type=F32)
        u = jnp.dot(av, wu_ref[...].astype(BF16), preferred_element_type=F32)
        g_ref[...] = g.astype(g_ref.dtype)
        u_ref[...] = u.astype(u_ref.dtype)
        act_ref[...] = (g * jax.nn.sigmoid(g) * u).astype(act_ref.dtype)

    out = pl.BlockSpec((tm, FF_BLOCK), lambda j, i: (i, j))
    return pl.pallas_call(
        body, name=name, grid=(2, s // tm),
        in_specs=[pl.BlockSpec((tm, k), lambda j, i: (i, 0)), pl.BlockSpec((k, FF_BLOCK), lambda j, i: (0, j)),
                  pl.BlockSpec((k, FF_BLOCK), lambda j, i: (0, j + 2))],
        out_specs=[out, out, out],
        out_shape=[jax.ShapeDtypeStruct((s, D_FF), BF16)] * 3,
        compiler_params=_params("parallel", "parallel"),
    )(a, w, w)


def _matmul_dswiglu(dx, w_down, gate, up, *, tm, name):
    s, k = dx.shape

    def body(dx_ref, w_ref, g_ref, u_ref, dg_ref, du_ref):
        dxb = dx_ref[...].astype(BF16)
        for lo in range(0, FF_BLOCK, DSWIGLU_CHUNK):
            cols = slice(lo, min(lo + DSWIGLU_CHUNK, FF_BLOCK))
            d = lax.dot_general(dxb, w_ref[cols, :].astype(BF16), (((1,), (1,)), ((), ())), preferred_element_type=F32)
            g = g_ref[:, cols].astype(F32)
            sig = jax.nn.sigmoid(g)
            dg_ref[:, cols] = (d * u_ref[:, cols].astype(F32) * (sig * (1.0 + g * (1.0 - sig)))).astype(dg_ref.dtype)
            du_ref[:, cols] = (d * (g * sig)).astype(du_ref.dtype)

    blk = pl.BlockSpec((tm, FF_BLOCK), lambda j, i: (i, j))
    return pl.pallas_call(
        body, name=name, grid=(2, s // tm),
        in_specs=[pl.BlockSpec((tm, k), lambda j, i: (i, 0)), pl.BlockSpec((FF_BLOCK, k), lambda j, i: (j, 0)), blk, blk],
        out_specs=[blk, blk], out_shape=[jax.ShapeDtypeStruct((s, D_FF), BF16)] * 2,
        compiler_params=_params("parallel", "parallel"),
    )(dx, w_down, gate, up)


def _matmul_gu_dx(dgate, dup, w_gu, *, tm, name):
    s = dgate.shape[0]
    k = w_gu.shape[0]
    nt_dims = (((1,), (1,)), ((), ()))

    def body(dg_ref, du_ref, wg_ref, wu_ref, o_ref):
        o_ref[...] = (lax.dot_general(dg_ref[...], wg_ref[...], nt_dims, preferred_element_type=F32)
                      + lax.dot_general(du_ref[...], wu_ref[...], nt_dims, preferred_element_type=F32))

    return pl.pallas_call(
        body, name=name, grid=(s // tm,),
        in_specs=[pl.BlockSpec((tm, D_FF), lambda i: (i, 0)), pl.BlockSpec((tm, D_FF), lambda i: (i, 0)),
                  pl.BlockSpec((k, D_FF), lambda i: (0, 0)), pl.BlockSpec((k, D_FF), lambda i: (0, 1))],
        out_specs=pl.BlockSpec((tm, k), lambda i: (i, 0)),
        out_shape=jax.ShapeDtypeStruct((s, k), F32),
        compiler_params=_params("parallel"),
    )(dgate, dup, w_gu, w_gu)


def _sum_slots(stacked, *, tr, name):
    k, r, _ = stacked.shape

    def body(x_ref, o_ref):
        acc = x_ref[0].astype(F32)
        for idx in range(1, k):
            acc = acc + x_ref[idx].astype(F32)
        o_ref[...] = acc

    return pl.pallas_call(
        body, name=name, grid=(r // tr,),
        in_specs=[pl.BlockSpec((k, tr, LANES), lambda i: (0, i, 0))],
        out_specs=pl.BlockSpec((tr, LANES), lambda i: (i, 0)),
        out_shape=jax.ShapeDtypeStruct((r, LANES), F32),
        compiler_params=_params("parallel"),
    )(stacked)


def _adamw(w, g, m, v, *, tr, name):
    r = w.shape[0]

    def body(w_ref, g_ref, m_ref, v_ref, d_ref, m2_ref, v2_ref):
        gv = g_ref[...]
        m2 = ADAM_B1 * m_ref[...] + (1.0 - ADAM_B1) * gv
        v2 = ADAM_B2 * v_ref[...] + (1.0 - ADAM_B2) * jnp.square(gv)
        m_hat = m2 / (1.0 - ADAM_B1 ** ADAM_STEP)
        v_hat = v2 / (1.0 - ADAM_B2 ** ADAM_STEP)
        d_ref[...] = -ADAM_LR * (m_hat / (jnp.sqrt(v_hat) + ADAM_EPS) + ADAM_WD * w_ref[...])
        m2_ref[...] = m2
        v2_ref[...] = v2

    spec = pl.BlockSpec((tr, LANES), lambda i: (i, 0))
    shape = jax.ShapeDtypeStruct((r, LANES), F32)
    return pl.pallas_call(
        body, name=name, grid=(r // tr,), in_specs=[spec] * 4, out_specs=[spec] * 3, out_shape=[shape] * 3,
        compiler_params=_params("parallel"),
    )(w, g, m, v)


PAIR_CHUNKS = 5


def _coords():
    return lax.axis_index("x"), lax.axis_index("y"), lax.axis_index("c")


def _my_chip():
    return 2 * lax.axis_index("x") + lax.axis_index("y")


def _chip_peer(x, y, k):
    px = 1 - x if k & 2 else x
    py = 1 - y if k & 1 else y
    return px, py


def _allgather_chips(shard, *, name):
    r = shard.shape[0]
    rh = r // 2

    def body(src, out, send_sems, recv_sems):
        x, y, c = _coords()
        me = 2 * x + y
        half = pl.ds(c * rh, rh)

        def copy(k, src_ref, slot, to):
            return pltpu.make_async_remote_copy(src_ref=src_ref, dst_ref=out.at[slot, half, :], send_sem=send_sems.at[k],
                                                recv_sem=recv_sems.at[k], device_id=to, device_id_type=MESH)

        first, passed = [], []
        for k in (1, 2, 3):
            px, py = _chip_peer(x, y, k)
            first.append(copy(k - 1, src.at[half, :], me, (px, py, c)))
            first[-1].start()
        for k in (1, 2, 3):
            px, py = _chip_peer(x, y, k)
            slot = 2 * px + py
            first[k - 1].wait_recv()
            passed.append(copy(2 + k, out.at[slot, half, :], slot, (x, y, 1 - c)))
            passed[-1].start()
        for cp in passed:
            cp.wait_recv()
        for cp in first + passed:
            cp.wait_send()

    gathered = pl.pallas_call(
        body, name=name, in_specs=[ANY], out_specs=ANY,
        out_shape=jax.ShapeDtypeStruct((4, r, LANES), shard.dtype),
        scratch_shapes=[pltpu.SemaphoreType.DMA((6,)), pltpu.SemaphoreType.DMA((6,))],
    )(shard)
    return lax.dynamic_update_slice(gathered, shard[None], (_my_chip(), 0, 0))


def _pair_split(grads, *, name):
    _, r, _ = grads.shape
    rh = r // 2
    rc = rh // PAIR_CHUNKS
    nchunk = 4 * PAIR_CHUNKS

    def body(g_ref, theirs_ref, send_sems, recv_sems):
        x, y, c = _coords()
        copies = []
        for j in range(4):
            for q in range(PAIR_CHUNKS):
                idx = j * PAIR_CHUNKS + q
                cp = pltpu.make_async_remote_copy(
                    src_ref=g_ref.at[j, pl.ds((1 - c) * rh + q * rc, rc), :], dst_ref=theirs_ref.at[j, pl.ds(q * rc, rc), :],
                    send_sem=send_sems.at[idx], recv_sem=recv_sems.at[idx], device_id=(x, y, 1 - c), device_id_type=MESH)
                cp.start()
                copies.append(cp)
        for cp in copies:
            cp.wait()

    return pl.pallas_call(
        body, name=name, in_specs=[ANY], out_specs=ANY, out_shape=jax.ShapeDtypeStruct((4, rh, LANES), F32),
        scratch_shapes=[pltpu.SemaphoreType.DMA((nchunk,)), pltpu.SemaphoreType.DMA((nchunk,))],
    )(grads)


def _pair_sum(grads, theirs, half, *, tr, name):
    _, rh, _ = theirs.shape
    nrt = rh // tr

    def body(half_ref, g_ref, t_ref, o_ref):
        o_ref[...] = (g_ref[...] + t_ref[...]).astype(o_ref.dtype)

    return pl.pallas_call(
        body, name=name,
        grid_spec=pltpu.PrefetchScalarGridSpec(
            num_scalar_prefetch=1, grid=(4, nrt),
            in_specs=[pl.BlockSpec((1, tr, LANES), lambda j, i, half_ref: (j, half_ref[0] * nrt + i, 0)),
                      pl.BlockSpec((1, tr, LANES), lambda j, i, half_ref: (j, i, 0))],
            out_specs=pl.BlockSpec((1, tr, LANES), lambda j, i, half_ref: (j, i, 0))),
        out_shape=jax.ShapeDtypeStruct((4, rh, LANES), BF16),
        compiler_params=_params("parallel", "parallel"),
    )(half, grads, theirs)


def _scatter_chips(part, *, name):
    _, rh, _ = part.shape

    def body(p_ref, out, send_sems, recv_sems):
        x, y, c = _coords()
        me = 2 * x + y
        copies = []
        for k in (1, 2, 3):
            px, py = _chip_peer(x, y, k)
            cp = pltpu.make_async_remote_copy(src_ref=p_ref.at[2 * px + py], dst_ref=out.at[me], send_sem=send_sems.at[k - 1],
                                              recv_sem=recv_sems.at[k - 1], device_id=(px, py, c), device_id_type=MESH)
            cp.start()
            copies.append(cp)
        for cp in copies:
            cp.wait()

    from_chips = pl.pallas_call(
        body, name=name, in_specs=[ANY], out_specs=ANY, out_shape=jax.ShapeDtypeStruct((4, rh, LANES), part.dtype),
        scratch_shapes=[pltpu.SemaphoreType.DMA((3,)), pltpu.SemaphoreType.DMA((3,))],
    )(part)
    own = lax.dynamic_index_in_dim(part, _my_chip(), axis=0, keepdims=True)
    return lax.dynamic_update_slice(from_chips, own, (_my_chip(), 0, 0))


def _pair_join(half, *, name):
    rh = half.shape[0]
    nchunk = 2 * PAIR_CHUNKS
    rc = rh // nchunk

    def body(h_ref, out, send_sems, recv_sems):
        x, y, c = _coords()
        copies = []
        for q in range(nchunk):
            src = h_ref.at[pl.ds(q * rc, rc), :]
            rows = out.at[pl.ds(c * rh + q * rc, rc), :]
            cp = pltpu.make_async_remote_copy(src_ref=src, dst_ref=rows, send_sem=send_sems.at[q], recv_sem=recv_sems.at[q],
                                              device_id=(x, y, 1 - c), device_id_type=MESH)
            cp.start()
            copies.append(cp)
        for cp in copies:
            cp.wait()

    joined = pl.pallas_call(
        body, name=name, in_specs=[ANY], out_specs=ANY, out_shape=jax.ShapeDtypeStruct((2 * rh, LANES), F32),
        scratch_shapes=[pltpu.SemaphoreType.DMA((nchunk,)), pltpu.SemaphoreType.DMA((nchunk,))],
    )(half)
    return lax.dynamic_update_slice(joined, half, (lax.axis_index("c") * rh, 0))


def _allgather_all(block, *, name):
    r = block.shape[0]

    def body(src, out, send_sems, recv_sems):
        x, y, c = _coords()
        me = 4 * x + 2 * y + c
        copies = []
        for k in range(1, 8):
            px, py = _chip_peer(x, y, k >> 1)
            pc = 1 - c if k & 1 else c
            cp = pltpu.make_async_remote_copy(src_ref=src, dst_ref=out.at[me], send_sem=send_sems.at[k - 1],
                                              recv_sem=recv_sems.at[k - 1], device_id=(px, py, pc), device_id_type=MESH)
            cp.start()
            copies.append(cp)
        for cp in copies:
            cp.wait()

    gathered = pl.pallas_call(
        body, name=name, in_specs=[ANY], out_specs=ANY, out_shape=jax.ShapeDtypeStruct((8, r, LANES), F32),
        scratch_shapes=[pltpu.SemaphoreType.DMA((7,)), pltpu.SemaphoreType.DMA((7,))],
    )(block)
    return lax.dynamic_update_slice(gathered, block[None], (2 * _my_chip() + lax.axis_index("c"), 0, 0))


def _flatten(arrays, pad_rows=None):
    flat = jnp.concatenate([a.reshape(-1) for a in arrays])
    if pad_rows is not None:
        flat = jnp.pad(flat, (0, pad_rows * LANES - flat.shape[0]))
    return flat.reshape(-1, LANES)


def _unflatten(flat, shapes):
    flat = flat.reshape(-1)
    out, off = [], 0
    for shp in shapes:
        size = 1
        for dim in shp:
            size *= dim
        out.append(flat[off:off + size].reshape(shp))
        off += size
    return out


def _pad_w_in(w):
    pad = jnp.zeros(w.shape[:-1] + (D_IN_PAD - D_IN,), w.dtype)
    return jnp.concatenate([w[..., :1536], w[..., 1544:], w[..., 1536:1544], pad], axis=-1)


def _unpad_w_in(w):
    return jnp.concatenate([w[..., :1536], w[..., 2560:2568], w[..., 1536:2560]], axis=-1)


def _tile(s, want):
    return min(want, s)


def _layer_fwd(x, p, l):
    s = x.shape[0]
    tr = _tile(s, 512)
    tm = _tile(s, 1024)
    xn = _rms_fwd([x], p["mix_g"], out_dtype=BF16, tr=tr, name=f"rms_mix_fwd{l}")
    h = _matmul(xn, p["w_in"], tm=_tile(s, 512), tn=D_IN_PAD, out_dtype=F32, name=f"mm_in{l}")
    c = _gates_fwd(h, p["bf_pad"], tr=_tile(s, 256), name=f"gates_fwd{l}")
    attn, attn_res = _attention_fwd(h, c, tq=_tile(s, 2048), kb=_tile(s, 512), t=_tile(s, 512), name=f"attn_fwd{l}")
    sgu = _sgu_fwd(h, p["ln_g"], p["ln_b"], p["w_s"], p["bias_tile"], tr=_tile(s, 256), name=f"sgu_fwd{l}")
    merged = _rms_fwd([attn, sgu], p["out_g"], out_dtype=BF16, tr=tr, name=f"rms_out_fwd{l}")
    x1 = _matmul(merged, p["w_out"], tm=tm, tn=1024, out_dtype=F32, residual=x, name=f"mm_out{l}")
    xn2 = _rms_fwd([x1], p["ffn_g"], out_dtype=BF16, tr=tr, name=f"rms_ffn_fwd{l}")
    gate, up, act = _matmul_swiglu(xn2, p["w_gu"], tm=_tile(s, 512), name=f"mm_gu{l}")
    x2 = _matmul(act, p["w_down"], tm=tm, tn=1024, out_dtype=F32, residual=x1, name=f"mm_down{l}")
    saved = dict(x=x, xn=xn, h=h, attn_res=attn_res, attn=attn, sgu=sgu, merged=merged, x1=x1, xn2=xn2, gate=gate, up=up, act=act)
    return x2, saved


def _layer_bwd(dx2, p, sv, l):
    s = dx2.shape[0]
    tr = _tile(s, 512)
    tm = _tile(s, 1024)
    ts = _tile(s, 1024)
    g = {}
    g["w_down"] = _matmul_tn(sv["act"], dx2, tm=1408, tn=1024, ts=ts, name=f"mm_down_dw{l}")
    dgate, dup = _matmul_dswiglu(dx2, p["w_down"], sv["gate"], sv["up"], tm=_tile(s, 512), name=f"mm_down_dx{l}")
    g["w_gu"] = jnp.concatenate([_matmul_tn(sv["xn2"], dgate, tm=1024, tn=1408, ts=ts, name=f"mm_gate_dw{l}"),
                                 _matmul_tn(sv["xn2"], dup, tm=1024, tn=1408, ts=ts, name=f"mm_up_dw{l}")], axis=1)
    dxn2 = _matmul_gu_dx(dgate, dup, p["w_gu"], tm=_tile(s, 512), name=f"mm_gu_dx{l}")
    (dx1,), g["ffn_g"] = _rms_bwd(dxn2, [sv["x1"]], p["ffn_g"], residual=dx2, tr=tr, name=f"rms_ffn_bwd{l}")
    g["w_out"] = _matmul_tn(sv["merged"], dx1, tm=1024, tn=1024, ts=ts, name=f"mm_out_dw{l}")
    dmerged = _matmul(dx1, p["w_out"], trans_b=True, tm=tm, tn=1024, out_dtype=F32, name=f"mm_out_dx{l}")
    (dattn, dsgu), g["out_g"], delta = _rms_bwd(dmerged, [sv["attn"], sv["sgu"]], p["out_g"], head_dots=True, tr=tr,
                                                name=f"rms_out_bwd{l}")
    dzu, dzv, g["w_s"], g["ln_g"], g["ln_b"], dbs = _sgu_bwd(dsgu, sv["h"], p["ln_g"], p["ln_b"], p["w_s"], p["bias_tile"],
                                                           tr=_tile(s, 256), name=f"sgu_bwd{l}")
    g["b_s"] = dbs[:, :N_HEADS].T
    dqkv, dc = _attention_bwd(dattn, delta[:, :N_HEADS], sv["attn_res"], name=f"attn_bwd{l}")
    dfl, dbf = _gates_bwd(dc, sv["h"], p["bf_pad"], tr=_tile(s, 256), name=f"gates_bwd{l}")
    g["b_f"] = dbf[0, :N_HEADS]
    dh = jnp.concatenate([dqkv, dzu, dzv, dfl], axis=1)
    g["w_in"] = _matmul_tn(sv["xn"], dh, tm=1024, tn=896, ts=ts, name=f"mm_in_dw{l}")
    dxn = _matmul(dh, p["w_in"], trans_b=True, tm=tm, tn=1024, out_dtype=F32, name=f"mm_in_dx{l}")
    (dx,), g["mix_g"] = _rms_bwd(dxn, [sv["x"]], p["mix_g"], residual=dx1, tr=tr, name=f"rms_mix_bwd{l}")
    return dx, g


def _layer_params(l, w_in_pad, w_out, w_gu, w_down, mix_norm_g, b_f, sgu_ln_g, sgu_ln_b, w_s, b_s, out_norm_g, ffn_norm_g):
    return dict(
        w_in=w_in_pad[l], w_out=w_out[l], w_gu=w_gu[l], w_down=w_down[l],
        mix_g=mix_norm_g[l][None, :], out_g=out_norm_g[l][None, :], ffn_g=ffn_norm_g[l][None, :],
        bf_pad=jnp.pad(b_f[l], (0, LANES - N_HEADS))[None, :],
        ln_g=sgu_ln_g[l][None, :], ln_b=sgu_ln_b[l][None, :], w_s=w_s[l],
        bias_tile=jnp.repeat(b_s[l].T, 64, axis=1),
    )


def _local_step(x, tgt, w_in_pad, w_out, w_gu, w_down, mix_norm_g, b_f, sgu_ln_g, sgu_ln_b, w_s, b_s, out_norm_g,
                ffn_norm_g, final_norm_g):
    depth = w_in_pad.shape[0]
    s = x.shape[0]
    params = [_layer_params(l, w_in_pad, w_out, w_gu, w_down, mix_norm_g, b_f, sgu_ln_g, sgu_ln_b, w_s, b_s, out_norm_g,
                            ffn_norm_g) for l in range(depth)]
    saved = []
    for l in range(depth):
        x, sv = _layer_fwd(x, params[l], l)
        saved.append(sv)
    loss_tile, dx, dfinal = _loss_head(x, tgt, final_norm_g[None, :], tr=_tile(s, 512), name="loss_head")
    grads = [None] * depth
    for l in reversed(range(depth)):
        dx, grads[l] = _layer_bwd(dx, params[l], saved[l], l)
    return loss_tile[0, 0], dx, grads, dfinal[0]


SMALL_ROWS = 4272


def kernel(x, mix_norm_g, w_in, b_f, sgu_ln_g, sgu_ln_b, w_s, b_s, out_norm_g, w_out, ffn_norm_g, w_gate_up, w_down, final_norm_g, loss_target, m_mix_norm_g, m_w_in, m_b_f, m_sgu_ln_g, m_sgu_ln_b, m_w_s, m_b_s, m_out_norm_g, m_w_out, m_ffn_norm_g, m_w_gate_up, m_w_down, m_final_norm_g, v_mix_norm_g, v_w_in, v_b_f, v_sgu_ln_g, v_sgu_ln_b, v_w_s, v_b_s, v_out_norm_g, v_w_out, v_ffn_norm_g, v_w_gate_up, v_w_down, v_final_norm_g):
    big = [w_in, w_out, w_gate_up, w_down]
    big_shapes = [a.shape for a in big]
    small = [mix_norm_g, b_f, sgu_ln_g, sgu_ln_b, w_s, b_s, out_norm_g, ffn_norm_g, final_norm_g]
    small_shapes = [a.shape for a in small]

    gathered = _allgather_chips(_flatten([a.astype(BF16) for a in big]), name="ag_weights")
    per_chip = [_unflatten(gathered[j], big_shapes) for j in range(4)]
    w_in_full = _pad_w_in(jnp.concatenate([pc[0] for pc in per_chip], axis=2))
    w_out_full = jnp.concatenate([pc[1] for pc in per_chip], axis=1)
    w_gu_full = jnp.concatenate([pc[2] for pc in per_chip], axis=2)
    w_down_full = jnp.concatenate([pc[3] for pc in per_chip], axis=1)

    loss_part, dx, grads, dfinal = _local_step(
        x[0], loss_target[0], w_in_full, w_out_full, w_gu_full, w_down_full, mix_norm_g, b_f, sgu_ln_g, sgu_ln_b, w_s, b_s,
        out_norm_g, ffn_norm_g, final_norm_g)
    stack = lambda key: jnp.stack([g[key] for g in grads])
    g_in = _unpad_w_in(stack("w_in"))
    g_out, g_gu, g_down = stack("w_out"), stack("w_gu"), stack("w_down")

    send = jnp.stack([_flatten([g_in[:, :, 642 * j:642 * (j + 1)], g_out[:, 256 * j:256 * (j + 1), :],
                                g_gu[:, :, 1408 * j:1408 * (j + 1)], g_down[:, 704 * j:704 * (j + 1), :]]) for j in range(4)])
    theirs = _pair_split(send, name="rs_pair_split")
    pair_sum = _pair_sum(send, theirs, lax.axis_index("c").astype(jnp.int32).reshape(1), tr=3440, name="rs_pair_sum")
    from_chips = _scatter_chips(pair_sum, name="rs_scatter")
    half = _sum_slots(from_chips, tr=1120, name="rs_chip_sum")
    g_big_flat = _pair_join(half, name="rs_pair_join")

    g_small_local = [stack("mix_g")[:, 0], stack("b_f"), stack("ln_g")[:, 0], stack("ln_b")[:, 0], stack("w_s"), stack("b_s"),
                     stack("out_g")[:, 0], stack("ffn_g")[:, 0], dfinal]
    g_small_flat = _sum_slots(_allgather_all(_flatten(g_small_local, SMALL_ROWS), name="ar_small_gather"), tr=1424,
                              name="ar_small_sum")
    loss = lax.psum(loss_part, ("x", "y", "c"))

    d_big, m_big, v_big = _adamw(_flatten(big), g_big_flat, _flatten([m_w_in, m_w_out, m_w_gate_up, m_w_down]),
                                 _flatten([v_w_in, v_w_out, v_w_gate_up, v_w_down]), tr=2240, name="adamw_big")
    m_small = [m_mix_norm_g, m_b_f, m_sgu_ln_g, m_sgu_ln_b, m_w_s, m_b_s, m_out_norm_g, m_ffn_norm_g, m_final_norm_g]
    v_small = [v_mix_norm_g, v_b_f, v_sgu_ln_g, v_sgu_ln_b, v_w_s, v_b_s, v_out_norm_g, v_ffn_norm_g, v_final_norm_g]
    d_small, m_small2, v_small2 = _adamw(_flatten(small, SMALL_ROWS), g_small_flat, _flatten(m_small, SMALL_ROWS),
                                         _flatten(v_small, SMALL_ROWS), tr=1424, name="adamw_small")

    def in_order(big_flat, small_flat):
        b_in, b_out, b_gu, b_down = _unflatten(big_flat, big_shapes)
        s_mix, s_bf, s_lng, s_lnb, s_ws, s_bs, s_outg, s_ffn, s_fin = _unflatten(small_flat, small_shapes)
        return [s_mix, b_in, s_bf, s_lng, s_lnb, s_ws, s_bs, s_outg, b_out, s_ffn, b_gu, b_down, s_fin]

    return (loss, dx[None], *in_order(g_big_flat, g_small_flat), *in_order(d_big, d_small), *in_order(m_big, m_small2),
            *in_order(v_big, v_small2))
```

```python
import jax
import jax.numpy as jnp
from jax import lax
from jax.experimental import pallas as pl
from jax.experimental.pallas import tpu as pltpu

F32 = jnp.float32
BF16 = jnp.bfloat16

D_MODEL = 1024
D_HALF = 512
N_HEADS = 8
HEAD_DIM = 64
SGU_CHUNK = 128
CAUSAL_CHUNK = 64
D_FF = 2816
D_IN = 2568
D_IN_PAD = 2688
F_COL_BLOCK = 2560 // 128
EPS = 1e-6
NEG = -1e30
LANES = 128
VMEM_LIMIT = 56 * 1024 * 1024

ADAM_LR = 0.001
ADAM_B1 = 0.9
ADAM_B2 = 0.999
ADAM_EPS = 1e-08
ADAM_WD = 0.01
ADAM_STEP = 10

MESH = pl.DeviceIdType.MESH
ANY = pl.BlockSpec(memory_space=pl.ANY)


def _params(*sem):
    return pltpu.CompilerParams(dimension_semantics=sem, vmem_limit_bytes=VMEM_LIMIT)


def _matmul(a, b, *, trans_b=False, tm, tn, out_dtype, residual=None, name):
    m, k = a.shape
    n = b.shape[0] if trans_b else b.shape[1]
    dims = (((1,), (1,)), ((), ())) if trans_b else (((1,), (0,)), ((), ()))

    def body(*refs):
        a_ref, b_ref = refs[0], refs[1]
        o_ref = refs[-1]
        acc = lax.dot_general(a_ref[...].astype(BF16), b_ref[...].astype(BF16), dims,
                              preferred_element_type=F32)
        if residual is not None:
            acc = acc + refs[2][...]
        o_ref[...] = acc.astype(out_dtype)

    b_spec = pl.BlockSpec((tn, k), lambda i, j: (j, 0)) if trans_b else pl.BlockSpec((k, tn), lambda i, j: (0, j))
    in_specs = [pl.BlockSpec((tm, k), lambda i, j: (i, 0)), b_spec]
    args = [a, b]
    if residual is not None:
        in_specs.append(pl.BlockSpec((tm, tn), lambda i, j: (i, j)))
        args.append(residual)
    return pl.pallas_call(
        body, name=name, grid=(m // tm, n // tn), in_specs=in_specs,
        out_specs=pl.BlockSpec((tm, tn), lambda i, j: (i, j)),
        out_shape=jax.ShapeDtypeStruct((m, n), out_dtype),
        compiler_params=_params("parallel", "parallel"),
    )(*args)


def _matmul_tn(a, b, *, tm, tn, ts, name):
    s, m = a.shape
    n = b.shape[1]

    def body(a_ref, b_ref, o_ref):
        @pl.when(pl.program_id(2) == 0)
        def _():
            o_ref[...] = jnp.zeros_like(o_ref)

        o_ref[...] += lax.dot_general(a_ref[...].astype(BF16), b_ref[...].astype(BF16),
                                      (((0,), (0,)), ((), ())), preferred_element_type=F32)

    return pl.pallas_call(
        body, name=name, grid=(m // tm, n // tn, s // ts),
        in_specs=[pl.BlockSpec((ts, tm), lambda i, j, t: (t, i)), pl.BlockSpec((ts, tn), lambda i, j, t: (t, j))],
        out_specs=pl.BlockSpec((tm, tn), lambda i, j, t: (i, j)),
        out_shape=jax.ShapeDtypeStruct((m, n), F32),
        compiler_params=_params("parallel", "parallel", "arbitrary"),
    )(a, b)


def _rms_fwd(xs, g, *, out_dtype, tr, name):
    s = xs[0].shape[0]
    widths = [x.shape[1] for x in xs]
    wsum = sum(widths)
    nx = len(xs)

    def body(*refs):
        g_ref, o_ref = refs[nx], refs[nx + 1]
        off = 0
        for x_ref, w in zip(refs[:nx], widths):
            x = x_ref[...]
            r = lax.rsqrt(jnp.mean(x * x, axis=-1, keepdims=True) + EPS)
            o_ref[:, off:off + w] = (x * r * g_ref[:, off:off + w]).astype(out_dtype)
            off += w

    return pl.pallas_call(
        body, name=name, grid=(s // tr,),
        in_specs=[pl.BlockSpec((tr, w), lambda i: (i, 0)) for w in widths] + [pl.BlockSpec((1, wsum), lambda i: (0, 0))],
        out_specs=pl.BlockSpec((tr, wsum), lambda i: (i, 0)),
        out_shape=jax.ShapeDtypeStruct((s, wsum), out_dtype),
        compiler_params=_params("parallel"),
    )(*xs, g)


def _rms_bwd(dy, xs, g, *, residual=None, head_dots=False, tr, name):
    s = xs[0].shape[0]
    widths = [x.shape[1] for x in xs]
    wsum = sum(widths)
    nx = len(xs)
    nin = 2 + nx + (residual is not None)

    def body(*refs):
        dy_ref, g_ref = refs[0], refs[1 + nx]
        dx_refs, dg_ref = refs[nin:nin + nx], refs[nin + nx]

        @pl.when(pl.program_id(0) == 0)
        def _():
            dg_ref[...] = jnp.zeros_like(dg_ref)
            if head_dots:
                refs[nin + nx + 2][...] = jnp.zeros_like(refs[nin + nx + 2])

        off = 0
        for idx, (x_ref, w) in enumerate(zip(refs[1:1 + nx], widths)):
            x = x_ref[...]
            r = lax.rsqrt(jnp.mean(x * x, axis=-1, keepdims=True) + EPS)
            xh = x * r
            dyv = dy_ref[:, off:off + w]
            dxh = dyv * g_ref[:, off:off + w]
            dx = r * (dxh - xh * jnp.mean(dxh * xh, axis=-1, keepdims=True))
            if residual is not None and idx == 0:
                dx = dx + refs[2 + nx][...]
            dx_refs[idx][...] = dx
            dg_ref[:, off:off + w] += jnp.sum(dyv * xh, axis=0, keepdims=True)
            if head_dots and idx == 0:
                col = lax.broadcasted_iota(jnp.int32, (w, LANES), 0) // HEAD_DIM
                head = lax.broadcasted_iota(jnp.int32, (w, LANES), 1)
                ind = (col == head).astype(F32)
                refs[nin + nx + 1][...] = jnp.dot(dx * x, ind, precision=lax.Precision.HIGHEST, preferred_element_type=F32)
                sq = jnp.dot(dx * dx, ind, precision=lax.Precision.HIGHEST, preferred_element_type=F32)
                refs[nin + nx + 2][...] = jnp.maximum(refs[nin + nx + 2][...], jnp.max(sq, axis=0, keepdims=True))
            off += w

    in_specs = ([pl.BlockSpec((tr, wsum), lambda i: (i, 0))]
                + [pl.BlockSpec((tr, w), lambda i: (i, 0)) for w in widths]
                + [pl.BlockSpec((1, wsum), lambda i: (0, 0))])
    args = [dy, *xs, g]
    if residual is not None:
        in_specs.append(pl.BlockSpec((tr, widths[0]), lambda i: (i, 0)))
        args.append(residual)
    out_specs = [pl.BlockSpec((tr, w), lambda i: (i, 0)) for w in widths] + [pl.BlockSpec((1, wsum), lambda i: (0, 0))]
    out_shape = [jax.ShapeDtypeStruct((s, w), F32) for w in widths] + [jax.ShapeDtypeStruct((1, wsum), F32)]
    if head_dots:
        out_specs.append(pl.BlockSpec((tr, LANES), lambda i: (i, 0)))
        out_shape.append(jax.ShapeDtypeStruct((s, LANES), F32))
        out_specs.append(pl.BlockSpec((1, LANES), lambda i: (0, 0)))
        out_shape.append(jax.ShapeDtypeStruct((1, LANES), F32))
    outs = pl.pallas_call(
        body, name=name, grid=(s // tr,), in_specs=in_specs, out_specs=out_specs, out_shape=out_shape,
        compiler_params=_params("arbitrary"),
    )(*args)
    if head_dots:
        return outs[:nx], outs[nx], outs[nx + 1], outs[nx + 2]
    return outs[:nx], outs[nx]


def _loss_head(x, tgt, g, *, tr, name):
    s, d = x.shape

    def body(x_ref, t_ref, g_ref, loss_ref, dx_ref, dg_ref):
        @pl.when(pl.program_id(0) == 0)
        def _():
            loss_ref[...] = jnp.zeros_like(loss_ref)
            dg_ref[...] = jnp.zeros_like(dg_ref)

        xv = x_ref[...]
        r = lax.rsqrt(jnp.mean(xv * xv, axis=-1, keepdims=True) + EPS)
        xh = xv * r
        err = xh * g_ref[...] - t_ref[...]
        loss_ref[...] += 0.5 * jnp.sum(jnp.mean(err * err, axis=-1, keepdims=True))
        dy = err * (1.0 / d)
        dxh = dy * g_ref[...]
        dx_ref[...] = r * (dxh - xh * jnp.mean(dxh * xh, axis=-1, keepdims=True))
        dg_ref[...] += jnp.sum(dy * xh, axis=0, keepdims=True)

    return pl.pallas_call(
        body, name=name, grid=(s // tr,),
        in_specs=[pl.BlockSpec((tr, d), lambda i: (i, 0)), pl.BlockSpec((tr, d), lambda i: (i, 0)),
                  pl.BlockSpec((1, d), lambda i: (0, 0))],
        out_specs=[pl.BlockSpec((8, LANES), lambda i: (0, 0)), pl.BlockSpec((tr, d), lambda i: (i, 0)),
                   pl.BlockSpec((1, d), lambda i: (0, 0))],
        out_shape=[jax.ShapeDtypeStruct((8, LANES), F32), jax.ShapeDtypeStruct((s, d), F32),
                   jax.ShapeDtypeStruct((1, d), F32)],
        compiler_params=_params("arbitrary"),
    )(x, tgt, g)


def _gates_fwd(h, bf_pad, *, tr, name):
    s = h.shape[0]

    def body(fl_ref, b_ref, c_ref, carry_ref):
        @pl.when(pl.program_id(0) == 0)
        def _():
            carry_ref[...] = jnp.zeros_like(carry_ref)

        lf = jax.nn.log_sigmoid(fl_ref[...] + b_ref[...])
        row = lax.broadcasted_iota(jnp.int32, (tr, tr), 0)
        col = lax.broadcasted_iota(jnp.int32, (tr, tr), 1)
        tri = (col <= row).astype(F32)
        c_ref[...] = jnp.dot(tri, lf, precision=lax.Precision.HIGHEST, preferred_element_type=F32) + carry_ref[...]
        carry_ref[...] += jnp.sum(lf, axis=0, keepdims=True)

    return pl.pallas_call(
        body, name=name, grid=(s // tr,),
        in_specs=[pl.BlockSpec((tr, LANES), lambda i: (i, F_COL_BLOCK)), pl.BlockSpec((1, LANES), lambda i: (0, 0))],
        out_specs=pl.BlockSpec((tr, LANES), lambda i: (i, 0)),
        out_shape=jax.ShapeDtypeStruct((s, LANES), F32),
        scratch_shapes=[pltpu.VMEM((1, LANES), F32)],
        compiler_params=_params("arbitrary"),
    )(h, bf_pad)


def _gates_bwd(dc, h, bf_pad, *, tr, name):
    s = h.shape[0]
    n = s // tr

    def body(dc_ref, fl_ref, b_ref, dfl_ref, db_ref, carry_ref):
        @pl.when(pl.program_id(0) == 0)
        def _():
            carry_ref[...] = jnp.zeros_like(carry_ref)
            db_ref[...] = jnp.zeros_like(db_ref)

        dcv = dc_ref[...]
        row = lax.broadcasted_iota(jnp.int32, (tr, tr), 0)
        col = lax.broadcasted_iota(jnp.int32, (tr, tr), 1)
        triu = (col >= row).astype(F32)
        dlf = jnp.dot(triu, dcv, precision=lax.Precision.HIGHEST, preferred_element_type=F32) + carry_ref[...]
        carry_ref[...] += jnp.sum(dcv, axis=0, keepdims=True)
        dfl = dlf * jax.nn.sigmoid(-(fl_ref[...] + b_ref[...]))
        dfl_ref[...] = dfl.astype(dfl_ref.dtype)
        db_ref[...] += jnp.sum(dfl, axis=0, keepdims=True)

    return pl.pallas_call(
        body, name=name, grid=(n,),
        in_specs=[pl.BlockSpec((tr, LANES), lambda i: (n - 1 - i, 0)),
                  pl.BlockSpec((tr, LANES), lambda i: (n - 1 - i, F_COL_BLOCK)),
                  pl.BlockSpec((1, LANES), lambda i: (0, 0))],
        out_specs=[pl.BlockSpec((tr, LANES), lambda i: (n - 1 - i, 0)), pl.BlockSpec((1, LANES), lambda i: (0, 0))],
        out_shape=[jax.ShapeDtypeStruct((s, LANES), BF16), jax.ShapeDtypeStruct((1, LANES), F32)],
        scratch_shapes=[pltpu.VMEM((1, LANES), F32)],
        compiler_params=_params("arbitrary"),
    )(dc, h, bf_pad)


LOG2E = 1.4426950408889634
LN2 = 0.6931471805599453
V_ROWS = 80


def _transpose_bf16(a):
    return a.astype(F32).T.astype(BF16)


SKIP_MARGIN_FWD = 140.0
SKIP_MARGIN_BWD_BASE = 136.0
SMEM = pl.BlockSpec(memory_space=pltpu.SMEM)


def _attn_fwd(qat, ka, va, gnorm, cmax_q, cmin_k, *, tq, kb, name):
    s = ka.shape[0]
    nq = s // tq
    per_tile = tq // kb
    tt = qat.shape[3]
    sub = tq // tt

    def body(gn_ref, cq_ref, ck_ref, qat_ref, ka_ref, va_ref, o_ref, lse_ref):
        h, i = pl.program_id(0), pl.program_id(1)
        qat = jnp.concatenate([qat_ref[0, d] for d in range(sub)], axis=1)

        def blk(n, carry, masked):
            m, acc = carry
            rows = pl.ds(pl.multiple_of(n * kb, kb), kb)
            sc = jnp.dot(ka_ref[rows, :], qat, preferred_element_type=F32)
            if masked:
                key = n * kb + lax.broadcasted_iota(jnp.int32, (kb, tq), 0)
                qry = i * tq + lax.broadcasted_iota(jnp.int32, (kb, tq), 1)
                sc = jnp.where(key <= qry, sc, NEG)
            m_new = jnp.maximum(m, jnp.max(sc, axis=0, keepdims=True))
            p = jnp.exp2(sc - m_new)
            vt = _transpose_bf16(va_ref[rows, :])[:V_ROWS]
            acc = jnp.exp2(m - m_new) * acc + jnp.dot(vt, p.astype(BF16), preferred_element_type=F32)
            return m_new, acc

        carry = (jnp.full((1, tq), NEG, F32), jnp.zeros((V_ROWS, tq), F32))
        for d in reversed(range(per_tile)):
            carry = blk(i * per_tile + d, carry, True)
        top = gn_ref[h] + cq_ref[h, i]

        def live(state):
            n, m_min = state[0], state[1]
            return jnp.logical_and(n >= 0, top - ck_ref[h, jnp.maximum(n, 0)] >= m_min - SKIP_MARGIN_FWD)

        def step(state):
            n, _, m, acc = state
            m, acc = blk(n, (m, acc), False)
            return n - 1, jnp.min(m), m, acc

        _, _, m, acc = lax.while_loop(live, step, (i * per_tile - 1, jnp.min(carry[0]), *carry))
        l = acc[HEAD_DIM:HEAD_DIM + 1, :]
        padded = jnp.concatenate([acc / l, jnp.zeros((LANES - V_ROWS, tq), F32)], axis=0)
        o_ref[0, 0] = padded.T[:, :HEAD_DIM]
        lse_ref[0, 0] = m + jnp.log2(l)

    return pl.pallas_call(
        body, name=name, grid=(N_HEADS, nq),
        in_specs=[SMEM, SMEM, SMEM, pl.BlockSpec((1, sub, LANES, tt), lambda h, i: (h, i, 0, 0)),
                  pl.BlockSpec((s, LANES), lambda h, i: (0, h)),
                  pl.BlockSpec((s, LANES), lambda h, i: (0, h))],
        out_specs=[pl.BlockSpec((1, 1, tq, HEAD_DIM), lambda h, i: (h, i, 0, 0)),
                   pl.BlockSpec((1, 1, 1, tq), lambda h, i: (h, i, 0, 0))],
        out_shape=[jax.ShapeDtypeStruct((N_HEADS, nq, tq, HEAD_DIM), F32), jax.ShapeDtypeStruct((N_HEADS, nq, 1, tq), F32)],
        compiler_params=_params("parallel", "arbitrary"),
    )(gnorm, cmax_q, cmin_k, qat, ka, va)


def _attn_bwd(qa, qat, doa, dot, lse, dl, ka, va, gnorm, cmin_k, reach, margin, *, name):
    s = qa.shape[0]
    nt, t = qat.shape[1], qat.shape[3]

    def body(gn_ref, ck_ref, reach_ref, mg_ref, qa_ref, qat_ref, do_ref, dot_ref, lse_ref, dl_ref, ka_ref, va_ref,
             dq_ref, dk_ref, dv_ref):
        h, j = pl.program_id(0), pl.program_id(1)

        @pl.when(j == 0)
        def _():
            dq_ref[...] = jnp.zeros_like(dq_ref)

        ka_j, va_j = ka_ref[...], va_ref[...]

        def tile(i, carry, masked):
            dk, dv = carry
            rows = pl.ds(pl.multiple_of(i * t, t), t)
            qa_i, do_i = qa_ref[rows, :], do_ref[rows, :]
            st = jnp.dot(ka_j, qat_ref[0, i], preferred_element_type=F32) - lse_ref[0, i]
            if masked:
                key = lax.broadcasted_iota(jnp.int32, (t, t), 0)
                qry = lax.broadcasted_iota(jnp.int32, (t, t), 1)
                st = jnp.where(key <= qry, st, NEG)
            pt = jnp.exp2(st)
            dpt = jnp.dot(va_j, dot_ref[0, i], preferred_element_type=F32)
            dsb = (pt * (dpt - dl_ref[0, i])).astype(BF16)
            dv = dv + jnp.dot(pt.astype(BF16), do_i, preferred_element_type=F32)
            dk = dk + jnp.dot(dsb, qa_i, preferred_element_type=F32)
            dq_ref[rows, :] += lax.dot_general(dsb, ka_j, (((0,), (0,)), ((), ())), preferred_element_type=F32)
            return dk, dv

        carry = tile(j, (jnp.zeros((t, LANES), F32), jnp.zeros((t, LANES), F32)), True)
        base = gn_ref[h] - ck_ref[h, j]

        def live(state):
            i = state[0]
            return jnp.logical_and(i < nt, base + reach_ref[h, jnp.minimum(i, nt - 1)] >= -mg_ref[h])

        def step(state):
            i, dk, dv = state
            dk, dv = tile(i, (dk, dv), False)
            return i + 1, dk, dv

        _, dk, dv = lax.while_loop(live, step, (j + 1, *carry))
        dk_ref[...] = dk
        dv_ref[...] = dv

    res = pl.BlockSpec((s, LANES), lambda h, j: (0, h))
    rest = pl.BlockSpec((1, nt, LANES, t), lambda h, j: (h, 0, 0, 0))
    row = pl.BlockSpec((1, nt, 1, t), lambda h, j: (h, 0, 0, 0))
    blk = pl.BlockSpec((t, LANES), lambda h, j: (j, h))
    shape = jax.ShapeDtypeStruct((s, N_HEADS * LANES), F32)
    return pl.pallas_call(
        body, name=name, grid=(N_HEADS, nt),
        in_specs=[SMEM, SMEM, SMEM, SMEM, res, rest, res, rest, row, row, blk, blk], out_specs=[res, blk, blk],
        out_shape=[shape, shape, shape], compiler_params=_params("parallel", "arbitrary"),
    )(gnorm, cmin_k, reach, margin, qa, qat, doa, dot, lse, dl, ka, va)


def _spread_matrix():
    r = lax.broadcasted_iota(jnp.int32, (D_HALF, N_HEADS * LANES), 0)
    c = lax.broadcasted_iota(jnp.int32, (D_HALF, N_HEADS * LANES), 1)
    return jnp.logical_and(c // LANES == r // HEAD_DIM, c % LANES == r % HEAD_DIM).astype(BF16)


def _piece_matrix(base):
    r = lax.broadcasted_iota(jnp.int32, (3 * LANES, N_HEADS * LANES), 0)
    c = lax.broadcasted_iota(jnp.int32, (3 * LANES, N_HEADS * LANES), 1)
    return jnp.logical_and(r % LANES < N_HEADS, c == LANES * (r % LANES) + base + r // LANES).astype(BF16)


def _ones_columns(first, count):
    c = lax.broadcasted_iota(jnp.int32, (1, N_HEADS * LANES), 1) % LANES
    return jnp.logical_and(c >= first, c < first + count).astype(F32)


def _round_bf16(x):
    return x.astype(BF16).astype(F32)


def _write_transposed(a, out_ref):
    for h in range(N_HEADS):
        out_ref[h, 0] = _transpose_bf16(a[:, h * LANES:(h + 1) * LANES])


def _pack_qkv(h, c, *, tr, name):
    s = h.shape[0]
    wide = N_HEADS * LANES

    def body(q_ref, k_ref, v_ref, c_ref, qa_ref, ka_ref, va_ref, qat_ref, vsq_ref):
        @pl.when(pl.program_id(0) == 0)
        def _():
            vsq_ref[...] = jnp.zeros_like(vsq_ref)

        vf = _round_bf16(v_ref[...])
        col = lax.broadcasted_iota(jnp.int32, (D_HALF, LANES), 0) // HEAD_DIM
        head = lax.broadcasted_iota(jnp.int32, (D_HALF, LANES), 1)
        sq = jnp.dot(vf * vf, (col == head).astype(F32), precision=lax.Precision.HIGHEST, preferred_element_type=F32)
        vsq_ref[...] = jnp.maximum(vsq_ref[...], jnp.max(sq, axis=0, keepdims=True))
        spread = _spread_matrix()
        c2 = c_ref[...] * LOG2E
        p1 = _round_bf16(c2)
        p2 = _round_bf16(c2 - p1)
        p3 = _round_bf16(c2 - p1 - p2)
        pieces = jnp.concatenate([p1, p2, p3], axis=1).astype(BF16)
        dot = lambda a, b: jnp.dot(a, b, preferred_element_type=F32)
        qa = (dot((q_ref[...] * 0.125).astype(BF16), spread) + dot(pieces, _piece_matrix(HEAD_DIM + 3))
              + _ones_columns(HEAD_DIM, 3)).astype(BF16)
        ka = (dot((k_ref[...] * LOG2E).astype(BF16), spread) - dot(pieces, _piece_matrix(HEAD_DIM))
              + _ones_columns(HEAD_DIM + 3, 3)).astype(BF16)
        qa_ref[...] = qa
        ka_ref[...] = ka
        va_ref[...] = (dot(v_ref[...].astype(BF16), spread) + _ones_columns(HEAD_DIM, 1)).astype(BF16)
        _write_transposed(qa, qat_ref)

    shape = jax.ShapeDtypeStruct((s, wide), BF16)
    return pl.pallas_call(
        body, name=name, grid=(s // tr,),
        in_specs=[pl.BlockSpec((tr, D_HALF), lambda i: (i, 0)), pl.BlockSpec((tr, D_HALF), lambda i: (i, 1)),
                  pl.BlockSpec((tr, D_HALF), lambda i: (i, 2)), pl.BlockSpec((tr, LANES), lambda i: (i, 0))],
        out_specs=[pl.BlockSpec((tr, wide), lambda i: (i, 0))] * 3 + [pl.BlockSpec((N_HEADS, 1, LANES, tr), lambda i: (0, i, 0, 0)),
                                                                   pl.BlockSpec((1, LANES), lambda i: (0, 0))],
        out_shape=[shape, shape, shape, jax.ShapeDtypeStruct((N_HEADS, s // tr, LANES, tr), BF16),
                   jax.ShapeDtypeStruct((1, LANES), F32)],
        compiler_params=_params("arbitrary"),
    )(h, h, h, c)


def _pack_do(dattn, *, tr, name):
    s = dattn.shape[0]
    wide = N_HEADS * LANES

    def body(d_ref, doa_ref, dot_ref):
        doa = jnp.dot(d_ref[...].astype(BF16), _spread_matrix(), preferred_element_type=F32).astype(BF16)
        doa_ref[...] = doa
        _write_transposed(doa, dot_ref)

    return pl.pallas_call(
        body, name=name, grid=(s // tr,),
        in_specs=[pl.BlockSpec((tr, D_HALF), lambda i: (i, 0))],
        out_specs=[pl.BlockSpec((tr, wide), lambda i: (i, 0)), pl.BlockSpec((N_HEADS, 1, LANES, tr), lambda i: (0, i, 0, 0))],
        out_shape=[jax.ShapeDtypeStruct((s, wide), BF16), jax.ShapeDtypeStruct((N_HEADS, s // tr, LANES, tr), BF16)],
        compiler_params=_params("parallel"),
    )(dattn)


def _unpack_grads(dqa, dka, dva, *, tr, name):
    s = dqa.shape[0]
    wide = N_HEADS * LANES

    def body(dq_ref, dk_ref, dv_ref, o_ref, dc_ref):
        gather = _spread_matrix()
        nt_dims = (((1,), (1,)), ((), ()))
        pick = lambda a: lax.dot_general(a.astype(BF16), gather, nt_dims, preferred_element_type=F32).astype(BF16)
        dq, dk = dq_ref[...], dk_ref[...]
        o_ref[:, 0:D_HALF] = pick(dq * (LN2 * 0.125))
        o_ref[:, D_HALF:2 * D_HALF] = pick(dk)
        o_ref[:, 2 * D_HALF:3 * D_HALF] = pick(dv_ref[...])
        lane = lax.broadcasted_iota(jnp.int32, (1, wide), 1) % LANES
        both = jnp.where(lane == HEAD_DIM + 3, dq, 0.0) - jnp.where(lane == HEAD_DIM, dk, 0.0)
        r = lax.broadcasted_iota(jnp.int32, (wide, LANES), 0)
        head = lax.broadcasted_iota(jnp.int32, (wide, LANES), 1)
        dc_ref[...] = jnp.dot(both, (r // LANES == head).astype(F32), precision=lax.Precision.HIGHEST,
                              preferred_element_type=F32)

    spec = pl.BlockSpec((tr, wide), lambda i: (i, 0))
    return pl.pallas_call(
        body, name=name, grid=(s // tr,), in_specs=[spec, spec, spec],
        out_specs=[pl.BlockSpec((tr, 3 * D_HALF), lambda i: (i, 0)), pl.BlockSpec((tr, LANES), lambda i: (i, 0))],
        out_shape=[jax.ShapeDtypeStruct((s, 3 * D_HALF), BF16), jax.ShapeDtypeStruct((s, LANES), F32)],
        compiler_params=_params("parallel"),
    )(dqa, dka, dva)


def _attention_fwd(h, c, *, tq, kb, t, name):
    s = h.shape[0]
    qa, ka, va, qat, vsq = _pack_qkv(h, c, tr=t, name=name + "_pack")
    c2 = c[:, :N_HEADS] * LOG2E
    head_norm = lambda a: jnp.sqrt(jnp.max(jnp.sum(jnp.square(a.reshape(s, N_HEADS, HEAD_DIM)), axis=-1), axis=0))
    gnorm = head_norm(h[:, 0:D_HALF] * 0.125) * head_norm(h[:, D_HALF:2 * D_HALF] * LOG2E) * 1.01 + 1.0
    cmax_q = jnp.max(c2.reshape(s // tq, tq, N_HEADS), axis=1).T
    cmin_k = lax.cummin(jnp.min(c2.reshape(s // kb, kb, N_HEADS), axis=1), axis=0).T
    o4, lse2 = _attn_fwd(qat, ka, va, gnorm, cmax_q, cmin_k, tq=tq, kb=kb, name=name)
    attn = o4.reshape(N_HEADS, s, HEAD_DIM).transpose(1, 0, 2).reshape(s, N_HEADS * HEAD_DIM)
    return attn, dict(qa=qa, qat=qat, ka=ka, va=va, lse2=lse2, c2=c2, gnorm=gnorm, vsq=vsq)


def _attention_bwd(dattn, delta, dosq, res, *, name):
    s = dattn.shape[0]
    t = res["qat"].shape[3]
    doa, dot = _pack_do(dattn, tr=t, name=name + "_pack")
    c2t = res["c2"].reshape(s // t, t, N_HEADS)
    lse_t = res["lse2"].reshape(N_HEADS, s // t, t)
    reach = lax.cummax(jnp.max(c2t.transpose(2, 0, 1) - lse_t, axis=2), axis=1, reverse=True)
    spread = jnp.sqrt(res["vsq"][0, :N_HEADS] * dosq[0, :N_HEADS]) * 1.01 + jnp.max(jnp.abs(delta), axis=0)
    margin = SKIP_MARGIN_BWD_BASE + jnp.maximum(0.0, jnp.log2(spread + 1e-30))
    dqa, dka, dva = _attn_bwd(res["qa"], res["qat"], doa, dot, lse_t.reshape(N_HEADS, s // t, 1, t),
                              delta.T.reshape(N_HEADS, s // t, 1, t), res["ka"], res["va"], res["gnorm"],
                              jnp.min(c2t, axis=1).T, reach, margin, name=name)
    return _unpack_grads(dqa, dka, dva, tr=t, name=name + "_unpack")


def _gelu(z):
    return 0.5 * z * (1.0 + lax.erf(z * 0.7071067811865476))


def _gelu_grad(z):
    return 0.5 * (1.0 + lax.erf(z * 0.7071067811865476)) + z * (0.3989422804014327 * jnp.exp(-0.5 * z * z))


def _sgu_mask():
    i = lax.broadcasted_iota(jnp.int32, (SGU_CHUNK, SGU_CHUNK), 0) // CAUSAL_CHUNK
    j = lax.broadcasted_iota(jnp.int32, (SGU_CHUNK, SGU_CHUNK), 1) // CAUSAL_CHUNK
    return (j <= i).astype(F32)


def _layernorm_stats(x):
    mu = jnp.mean(x, axis=-1, keepdims=True)
    xc = x - mu
    rstd = lax.rsqrt(jnp.mean(xc * xc, axis=-1, keepdims=True) + EPS)
    return xc * rstd, rstd


def _first_group_lanes():
    return lax.broadcasted_iota(jnp.int32, (SGU_CHUNK, LANES), 1) < 64


def _sgu_fwd(h, ln_g, ln_b, w_s, bias_tile, *, tr, name):
    s = h.shape[0]
    zu_blk, zv_blk = 1536 // D_HALF, 2048 // D_HALF

    def body(zu_ref, zv_ref, lng_ref, lnb_ref, ws_ref, bias_ref, o_ref):
        gzu = _gelu(zu_ref[...])
        xh, _ = _layernorm_stats(_gelu(zv_ref[...]))
        zb = (xh * lng_ref[...] + lnb_ref[...]).astype(BF16)
        mask = _sgu_mask()
        first = _first_group_lanes()
        for pair in range(4):
            cols = slice(pair * LANES, (pair + 1) * LANES)
            w0 = (ws_ref[2 * pair] * mask).astype(BF16)
            w1 = (ws_ref[2 * pair + 1] * mask).astype(BF16)
            for ch in range(tr // SGU_CHUNK):
                rows = slice(ch * SGU_CHUNK, (ch + 1) * SGU_CHUNK)
                zp = zb[rows, cols]
                mixed = jnp.where(first, jnp.dot(w0, zp, preferred_element_type=F32),
                                  jnp.dot(w1, zp, preferred_element_type=F32)) + bias_ref[:, cols]
                o_ref[rows, cols] = gzu[rows, cols] * mixed

    return pl.pallas_call(
        body, name=name, grid=(s // tr,),
        in_specs=[pl.BlockSpec((tr, D_HALF), lambda i: (i, zu_blk)), pl.BlockSpec((tr, D_HALF), lambda i: (i, zv_blk)),
                  pl.BlockSpec((1, D_HALF), lambda i: (0, 0)), pl.BlockSpec((1, D_HALF), lambda i: (0, 0)),
                  pl.BlockSpec((N_HEADS, SGU_CHUNK, SGU_CHUNK), lambda i: (0, 0, 0)),
                  pl.BlockSpec((SGU_CHUNK, D_HALF), lambda i: (0, 0))],
        out_specs=pl.BlockSpec((tr, D_HALF), lambda i: (i, 0)),
        out_shape=jax.ShapeDtypeStruct((s, D_HALF), F32),
        compiler_params=_params("parallel"),
    )(h, h, ln_g, ln_b, w_s, bias_tile)


def _sgu_bwd(dsgu, h, ln_g, ln_b, w_s, bias_tile, *, tr, name):
    s = h.shape[0]
    n = s // tr
    zu_blk, zv_blk = 1536 // D_HALF, 2048 // D_HALF

    def body(ds_ref, zu_ref, zv_ref, lng_ref, lnb_ref, ws_ref, bias_ref,
             dzu_ref, dzv_ref, dws_ref, dlng_ref, dlnb_ref, dbs_ref, dgzu_sc, dzvn_sc, dbias_sc):
        step = pl.program_id(0)

        @pl.when(step == 0)
        def _():
            dws_ref[...] = jnp.zeros_like(dws_ref)
            dlng_ref[...] = jnp.zeros_like(dlng_ref)
            dlnb_ref[...] = jnp.zeros_like(dlnb_ref)
            dbias_sc[...] = jnp.zeros_like(dbias_sc)

        zu = zu_ref[...]
        zv = zv_ref[...]
        gzu = _gelu(zu)
        xh, rstd = _layernorm_stats(_gelu(zv))
        zb = (xh * lng_ref[...] + lnb_ref[...]).astype(BF16)
        ds = ds_ref[...]
        mask = _sgu_mask()
        first = _first_group_lanes()
        tn_dims = (((0,), (0,)), ((), ()))
        nt_dims = (((1,), (1,)), ((), ()))
        for pair in range(4):
            cols = slice(pair * LANES, (pair + 1) * LANES)
            w0 = (ws_ref[2 * pair] * mask).astype(BF16)
            w1 = (ws_ref[2 * pair + 1] * mask).astype(BF16)
            for ch in range(tr // SGU_CHUNK):
                rows = slice(ch * SGU_CHUNK, (ch + 1) * SGU_CHUNK)
                zp = zb[rows, cols]
                mixed = jnp.where(first, jnp.dot(w0, zp, preferred_element_type=F32),
                                  jnp.dot(w1, zp, preferred_element_type=F32)) + bias_ref[:, cols]
                dsp = ds[rows, cols]
                dgzu_sc[rows, cols] = dsp * mixed
                dm = dsp * gzu[rows, cols]
                dbias_sc[:, cols] += dm
                dmb = dm.astype(BF16)
                dm0 = jnp.where(first, dmb, jnp.zeros_like(dmb))
                dm1 = jnp.where(first, jnp.zeros_like(dmb), dmb)
                dws_ref[2 * pair] += lax.dot_general(dm0, zp, nt_dims, preferred_element_type=F32)
                dws_ref[2 * pair + 1] += lax.dot_general(dm1, zp, nt_dims, preferred_element_type=F32)
                dzvn_sc[rows, cols] = jnp.where(first, lax.dot_general(w0, dmb, tn_dims, preferred_element_type=F32),
                                                lax.dot_general(w1, dmb, tn_dims, preferred_element_type=F32))
        dzvn = dzvn_sc[...]
        dlng_ref[...] += jnp.sum(dzvn * xh, axis=0, keepdims=True)
        dlnb_ref[...] += jnp.sum(dzvn, axis=0, keepdims=True)
        dxh = dzvn * lng_ref[...]
        dgzv = rstd * (dxh - jnp.mean(dxh, axis=-1, keepdims=True) - xh * jnp.mean(dxh * xh, axis=-1, keepdims=True))
        dzv_ref[...] = (dgzv * _gelu_grad(zv)).astype(dzv_ref.dtype)
        dzu_ref[...] = (dgzu_sc[...] * _gelu_grad(zu)).astype(dzu_ref.dtype)

        @pl.when(step == n - 1)
        def _():
            for g in range(N_HEADS):
                dws_ref[g] = dws_ref[g] * mask
            lane = lax.broadcasted_iota(jnp.int32, (D_HALF, LANES), 0) // 64
            grp = lax.broadcasted_iota(jnp.int32, (D_HALF, LANES), 1)
            dbs_ref[...] = jnp.dot(dbias_sc[...], (lane == grp).astype(F32), precision=lax.Precision.HIGHEST,
                                   preferred_element_type=F32)

    const2 = lambda i: (0, 0)
    return pl.pallas_call(
        body, name=name, grid=(n,),
        in_specs=[pl.BlockSpec((tr, D_HALF), lambda i: (i, 0)),
                  pl.BlockSpec((tr, D_HALF), lambda i: (i, zu_blk)), pl.BlockSpec((tr, D_HALF), lambda i: (i, zv_blk)),
                  pl.BlockSpec((1, D_HALF), const2), pl.BlockSpec((1, D_HALF), const2),
                  pl.BlockSpec((N_HEADS, SGU_CHUNK, SGU_CHUNK), lambda i: (0, 0, 0)),
                  pl.BlockSpec((SGU_CHUNK, D_HALF), const2)],
        out_specs=[pl.BlockSpec((tr, D_HALF), lambda i: (i, 0)), pl.BlockSpec((tr, D_HALF), lambda i: (i, 0)),
                   pl.BlockSpec((N_HEADS, SGU_CHUNK, SGU_CHUNK), lambda i: (0, 0, 0)),
                   pl.BlockSpec((1, D_HALF), const2), pl.BlockSpec((1, D_HALF), const2),
                   pl.BlockSpec((SGU_CHUNK, LANES), const2)],
        out_shape=[jax.ShapeDtypeStruct((s, D_HALF), BF16), jax.ShapeDtypeStruct((s, D_HALF), BF16),
                   jax.ShapeDtypeStruct((N_HEADS, SGU_CHUNK, SGU_CHUNK), F32),
                   jax.ShapeDtypeStruct((1, D_HALF), F32), jax.ShapeDtypeStruct((1, D_HALF), F32),
                   jax.ShapeDtypeStruct((SGU_CHUNK, LANES), F32)],
        scratch_shapes=[pltpu.VMEM((tr, D_HALF), F32), pltpu.VMEM((tr, D_HALF), F32), pltpu.VMEM((SGU_CHUNK, D_HALF), F32)],
        compiler_params=_params("arbitrary"),
    )(dsgu, h, h, ln_g, ln_b, w_s, bias_tile)


FF_BLOCK = D_FF // 2
DSWIGLU_CHUNK = 384


def _matmul_swiglu(a, w, *, tm, name):
    s, k = a.shape

    def body(a_ref, wg_ref, wu_ref, g_ref, u_ref, act_ref):
        av = a_ref[...].astype(BF16)
        g = jnp.dot(av, wg_ref[...].astype(BF16), preferred_element_type=F32)
        u = jnp.dot(av, wu_ref[...].astype(BF16), preferred_element_type=F32)
        g_ref[...] = g.astype(g_ref.dtype)
        u_ref[...] = u.astype(u_ref.dtype)
        act_ref[...] = (g * jax.nn.sigmoid(g) * u).astype(act_ref.dtype)

    out = pl.BlockSpec((tm, FF_BLOCK), lambda j, i: (i, j))
    return pl.pallas_call(
        body, name=name, grid=(2, s // tm),
        in_specs=[pl.BlockSpec((tm, k), lambda j, i: (i, 0)), pl.BlockSpec((k, FF_BLOCK), lambda j, i: (0, j)),
                  pl.BlockSpec((k, FF_BLOCK), lambda j, i: (0, j + 2))],
        out_specs=[out, out, out],
        out_shape=[jax.ShapeDtypeStruct((s, D_FF), BF16)] * 3,
        compiler_params=_params("parallel", "parallel"),
    )(a, w, w)


def _matmul_dswiglu(dx, w_down, gate, up, *, tm, name):
    s, k = dx.shape

    def body(dx_ref, w_ref, g_ref, u_ref, dg_ref, du_ref):
        dxb = dx_ref[...].astype(BF16)
        for lo in range(0, FF_BLOCK, DSWIGLU_CHUNK):
            cols = slice(lo, min(lo + DSWIGLU_CHUNK, FF_BLOCK))
            d = lax.dot_general(dxb, w_ref[cols, :].astype(BF16), (((1,), (1,)), ((), ())), preferred_element_type=F32)
            g = g_ref[:, cols].astype(F32)
            sig = jax.nn.sigmoid(g)
            dg_ref[:, cols] = (d * u_ref[:, cols].astype(F32) * (sig * (1.0 + g * (1.0 - sig)))).astype(dg_ref.dtype)
            du_ref[:, cols] = (d * (g * sig)).astype(du_ref.dtype)

    blk = pl.BlockSpec((tm, FF_BLOCK), lambda j, i: (i, j))
    return pl.pallas_call(
        body, name=name, grid=(2, s // tm),
        in_specs=[pl.BlockSpec((tm, k), lambda j, i: (i, 0)), pl.BlockSpec((FF_BLOCK, k), lambda j, i: (j, 0)), blk, blk],
        out_specs=[blk, blk], out_shape=[jax.ShapeDtypeStruct((s, D_FF), BF16)] * 2,
        compiler_params=_params("parallel", "parallel"),
    )(dx, w_down, gate, up)


def _matmul_gu_dx(dgate, dup, w_gu, *, tm, name):
    s = dgate.shape[0]
    k = w_gu.shape[0]
    nt_dims = (((1,), (1,)), ((), ()))

    def body(dg_ref, du_ref, wg_ref, wu_ref, o_ref):
        o_ref[...] = (lax.dot_general(dg_ref[...], wg_ref[...], nt_dims, preferred_element_type=F32)
                      + lax.dot_general(du_ref[...], wu_ref[...], nt_dims, preferred_element_type=F32))

    return pl.pallas_call(
        body, name=name, grid=(s // tm,),
        in_specs=[pl.BlockSpec((tm, D_FF), lambda i: (i, 0)), pl.BlockSpec((tm, D_FF), lambda i: (i, 0)),
                  pl.BlockSpec((k, D_FF), lambda i: (0, 0)), pl.BlockSpec((k, D_FF), lambda i: (0, 1))],
        out_specs=pl.BlockSpec((tm, k), lambda i: (i, 0)),
        out_shape=jax.ShapeDtypeStruct((s, k), F32),
        compiler_params=_params("parallel"),
    )(dgate, dup, w_gu, w_gu)


def _sum_slots(stacked, *, tr, name):
    k, r, _ = stacked.shape

    def body(x_ref, o_ref):
        acc = x_ref[0].astype(F32)
        for idx in range(1, k):
            acc = acc + x_ref[idx].astype(F32)
        o_ref[...] = acc

    return pl.pallas_call(
        body, name=name, grid=(r // tr,),
        in_specs=[pl.BlockSpec((k, tr, LANES), lambda i: (0, i, 0))],
        out_specs=pl.BlockSpec((tr, LANES), lambda i: (i, 0)),
        out_shape=jax.ShapeDtypeStruct((r, LANES), F32),
        compiler_params=_params("parallel"),
    )(stacked)


def _adamw(w, g, m, v, *, tr, name):
    r = w.shape[0]

    def body(w_ref, g_ref, m_ref, v_ref, d_ref, m2_ref, v2_ref):
        gv = g_ref[...]
        m2 = ADAM_B1 * m_ref[...] + (1.0 - ADAM_B1) * gv
        v2 = ADAM_B2 * v_ref[...] + (1.0 - ADAM_B2) * jnp.square(gv)
        m_hat = m2 / (1.0 - ADAM_B1 ** ADAM_STEP)
        v_hat = v2 / (1.0 - ADAM_B2 ** ADAM_STEP)
        d_ref[...] = -ADAM_LR * (m_hat / (jnp.sqrt(v_hat) + ADAM_EPS) + ADAM_WD * w_ref[...])
        m2_ref[...] = m2
        v2_ref[...] = v2

    spec = pl.BlockSpec((tr, LANES), lambda i: (i, 0))
    shape = jax.ShapeDtypeStruct((r, LANES), F32)
    return pl.pallas_call(
        body, name=name, grid=(r // tr,), in_specs=[spec] * 4, out_specs=[spec] * 3, out_shape=[shape] * 3,
        compiler_params=_params("parallel"),
    )(w, g, m, v)


PAIR_CHUNKS = 5


def _coords():
    return lax.axis_index("x"), lax.axis_index("y"), lax.axis_index("c")


def _my_chip():
    return 2 * lax.axis_index("x") + lax.axis_index("y")


def _chip_peer(x, y, k):
    px = 1 - x if k & 2 else x
    py = 1 - y if k & 1 else y
    return px, py


def _allgather_chips(shard, *, name):
    r = shard.shape[0]
    rh = r // 2

    def body(src, out, send_sems, recv_sems):
        x, y, c = _coords()
        me = 2 * x + y
        half = pl.ds(c * rh, rh)

        def copy(k, src_ref, slot, to):
            return pltpu.make_async_remote_copy(src_ref=src_ref, dst_ref=out.at[slot, half, :], send_sem=send_sems.at[k],
                                                recv_sem=recv_sems.at[k], device_id=to, device_id_type=MESH)

        first, passed = [], []
        for k in (1, 2, 3):
            px, py = _chip_peer(x, y, k)
            first.append(copy(k - 1, src.at[half, :], me, (px, py, c)))
            first[-1].start()
        for k in (1, 2, 3):
            px, py = _chip_peer(x, y, k)
            slot = 2 * px + py
            first[k - 1].wait_recv()
            passed.append(copy(2 + k, out.at[slot, half, :], slot, (x, y, 1 - c)))
            passed[-1].start()
        for cp in passed:
            cp.wait_recv()
        for cp in first + passed:
            cp.wait_send()

    gathered = pl.pallas_call(
        body, name=name, in_specs=[ANY], out_specs=ANY,
        out_shape=jax.ShapeDtypeStruct((4, r, LANES), shard.dtype),
        scratch_shapes=[pltpu.SemaphoreType.DMA((6,)), pltpu.SemaphoreType.DMA((6,))],
    )(shard)
    return lax.dynamic_update_slice(gathered, shard[None], (_my_chip(), 0, 0))


def _pair_split(grads, *, name):
    _, r, _ = grads.shape
    rh = r // 2
    rc = rh // PAIR_CHUNKS
    nchunk = 4 * PAIR_CHUNKS

    def body(g_ref, theirs_ref, send_sems, recv_sems):
        x, y, c = _coords()
        copies = []
        for j in range(4):
            for q in range(PAIR_CHUNKS):
                idx = j * PAIR_CHUNKS + q
                cp = pltpu.make_async_remote_copy(
                    src_ref=g_ref.at[j, pl.ds((1 - c) * rh + q * rc, rc), :], dst_ref=theirs_ref.at[j, pl.ds(q * rc, rc), :],
                    send_sem=send_sems.at[idx], recv_sem=recv_sems.at[idx], device_id=(x, y, 1 - c), device_id_type=MESH)
                cp.start()
                copies.append(cp)
        for cp in copies:
            cp.wait()

    return pl.pallas_call(
        body, name=name, in_specs=[ANY], out_specs=ANY, out_shape=jax.ShapeDtypeStruct((4, rh, LANES), F32),
        scratch_shapes=[pltpu.SemaphoreType.DMA((nchunk,)), pltpu.SemaphoreType.DMA((nchunk,))],
    )(grads)


def _pair_sum(grads, theirs, half, *, tr, name):
    _, rh, _ = theirs.shape
    nrt = rh // tr

    def body(half_ref, g_ref, t_ref, o_ref):
        o_ref[...] = (g_ref[...] + t_ref[...]).astype(o_ref.dtype)

    return pl.pallas_call(
        body, name=name,
        grid_spec=pltpu.PrefetchScalarGridSpec(
            num_scalar_prefetch=1, grid=(4, nrt),
            in_specs=[pl.BlockSpec((1, tr, LANES), lambda j, i, half_ref: (j, half_ref[0] * nrt + i, 0)),
                      pl.BlockSpec((1, tr, LANES), lambda j, i, half_ref: (j, i, 0))],
            out_specs=pl.BlockSpec((1, tr, LANES), lambda j, i, half_ref: (j, i, 0))),
        out_shape=jax.ShapeDtypeStruct((4, rh, LANES), BF16),
        compiler_params=_params("parallel", "parallel"),
    )(half, grads, theirs)


def _scatter_chips(part, *, name):
    _, rh, _ = part.shape

    def body(p_ref, out, send_sems, recv_sems):
        x, y, c = _coords()
        me = 2 * x + y
        copies = []
        for k in (1, 2, 3):
            px, py = _chip_peer(x, y, k)
            cp = pltpu.make_async_remote_copy(src_ref=p_ref.at[2 * px + py], dst_ref=out.at[me], send_sem=send_sems.at[k - 1],
                                              recv_sem=recv_sems.at[k - 1], device_id=(px, py, c), device_id_type=MESH)
            cp.start()
            copies.append(cp)
        for cp in copies:
            cp.wait()

    from_chips = pl.pallas_call(
        body, name=name, in_specs=[ANY], out_specs=ANY, out_shape=jax.ShapeDtypeStruct((4, rh, LANES), part.dtype),
        scratch_shapes=[pltpu.SemaphoreType.DMA((3,)), pltpu.SemaphoreType.DMA((3,))],
    )(part)
    own = lax.dynamic_index_in_dim(part, _my_chip(), axis=0, keepdims=True)
    return lax.dynamic_update_slice(from_chips, own, (_my_chip(), 0, 0))


def _pair_join(half, *, name):
    rh = half.shape[0]
    nchunk = 2 * PAIR_CHUNKS
    rc = rh // nchunk

    def body(h_ref, out, send_sems, recv_sems):
        x, y, c = _coords()
        copies = []
        for q in range(nchunk):
            src = h_ref.at[pl.ds(q * rc, rc), :]
            rows = out.at[pl.ds(c * rh + q * rc, rc), :]
            cp = pltpu.make_async_remote_copy(src_ref=src, dst_ref=rows, send_sem=send_sems.at[q], recv_sem=recv_sems.at[q],
                                              device_id=(x, y, 1 - c), device_id_type=MESH)
            cp.start()
            copies.append(cp)
        for cp in copies:
            cp.wait()

    joined = pl.pallas_call(
        body, name=name, in_specs=[ANY], out_specs=ANY, out_shape=jax.ShapeDtypeStruct((2 * rh, LANES), F32),
        scratch_shapes=[pltpu.SemaphoreType.DMA((nchunk,)), pltpu.SemaphoreType.DMA((nchunk,))],
    )(half)
    return lax.dynamic_update_slice(joined, half, (lax.axis_index("c") * rh, 0))


def _allgather_all(block, *, name):
    r = block.shape[0]

    def body(src, out, send_sems, recv_sems):
        x, y, c = _coords()
        me = 4 * x + 2 * y + c
        copies = []
        for k in range(1, 8):
            px, py = _chip_peer(x, y, k >> 1)
            pc = 1 - c if k & 1 else c
            cp = pltpu.make_async_remote_copy(src_ref=src, dst_ref=out.at[me], send_sem=send_sems.at[k - 1],
                                              recv_sem=recv_sems.at[k - 1], device_id=(px, py, pc), device_id_type=MESH)
            cp.start()
            copies.append(cp)
        for cp in copies:
            cp.wait()

    gathered = pl.pallas_call(
        body, name=name, in_specs=[ANY], out_specs=ANY, out_shape=jax.ShapeDtypeStruct((8, r, LANES), F32),
        scratch_shapes=[pltpu.SemaphoreType.DMA((7,)), pltpu.SemaphoreType.DMA((7,))],
    )(block)
    return lax.dynamic_update_slice(gathered, block[None], (2 * _my_chip() + lax.axis_index("c"), 0, 0))


def _flatten(arrays, pad_rows=None):
    flat = jnp.concatenate([a.reshape(-1) for a in arrays])
    if pad_rows is not None:
        flat = jnp.pad(flat, (0, pad_rows * LANES - flat.shape[0]))
    return flat.reshape(-1, LANES)


def _unflatten(flat, shapes):
    flat = flat.reshape(-1)
    out, off = [], 0
    for shp in shapes:
        size = 1
        for dim in shp:
            size *= dim
        out.append(flat[off:off + size].reshape(shp))
        off += size
    return out


def _pad_w_in(w):
    pad = jnp.zeros(w.shape[:-1] + (D_IN_PAD - D_IN,), w.dtype)
    return jnp.concatenate([w[..., :1536], w[..., 1544:], w[..., 1536:1544], pad], axis=-1)


def _unpad_w_in(w):
    return jnp.concatenate([w[..., :1536], w[..., 2560:2568], w[..., 1536:2560]], axis=-1)


def _tile(s, want):
    return min(want, s)


def _layer_fwd(x, p, l):
    s = x.shape[0]
    tr = _tile(s, 512)
    tm = _tile(s, 1024)
    xn = _rms_fwd([x], p["mix_g"], out_dtype=BF16, tr=tr, name=f"rms_mix_fwd{l}")
    h = _matmul(xn, p["w_in"], tm=_tile(s, 512), tn=D_IN_PAD, out_dtype=F32, name=f"mm_in{l}")
    c = _gates_fwd(h, p["bf_pad"], tr=_tile(s, 256), name=f"gates_fwd{l}")
    attn, attn_res = _attention_fwd(h, c, tq=_tile(s, 2048), kb=_tile(s, 512), t=_tile(s, 512), name=f"attn_fwd{l}")
    sgu = _sgu_fwd(h, p["ln_g"], p["ln_b"], p["w_s"], p["bias_tile"], tr=_tile(s, 256), name=f"sgu_fwd{l}")
    merged = _rms_fwd([attn, sgu], p["out_g"], out_dtype=BF16, tr=tr, name=f"rms_out_fwd{l}")
    x1 = _matmul(merged, p["w_out"], tm=tm, tn=1024, out_dtype=F32, residual=x, name=f"mm_out{l}")
    xn2 = _rms_fwd([x1], p["ffn_g"], out_dtype=BF16, tr=tr, name=f"rms_ffn_fwd{l}")
    gate, up, act = _matmul_swiglu(xn2, p["w_gu"], tm=_tile(s, 512), name=f"mm_gu{l}")
    x2 = _matmul(act, p["w_down"], tm=tm, tn=1024, out_dtype=F32, residual=x1, name=f"mm_down{l}")
    saved = dict(x=x, xn=xn, h=h, attn_res=attn_res, attn=attn, sgu=sgu, merged=merged, x1=x1, xn2=xn2, gate=gate, up=up, act=act)
    return x2, saved


def _layer_bwd(dx2, p, sv, l):
    s = dx2.shape[0]
    tr = _tile(s, 512)
    tm = _tile(s, 1024)
    ts = _tile(s, 1024)
    g = {}
    g["w_down"] = _matmul_tn(sv["act"], dx2, tm=1408, tn=1024, ts=ts, name=f"mm_down_dw{l}")
    dgate, dup = _matmul_dswiglu(dx2, p["w_down"], sv["gate"], sv["up"], tm=_tile(s, 512), name=f"mm_down_dx{l}")
    g["w_gu"] = jnp.concatenate([_matmul_tn(sv["xn2"], dgate, tm=1024, tn=1408, ts=ts, name=f"mm_gate_dw{l}"),
                                 _matmul_tn(sv["xn2"], dup, tm=1024, tn=1408, ts=ts, name=f"mm_up_dw{l}")], axis=1)
    dxn2 = _matmul_gu_dx(dgate, dup, p["w_gu"], tm=_tile(s, 512), name=f"mm_gu_dx{l}")
    (dx1,), g["ffn_g"] = _rms_bwd(dxn2, [sv["x1"]], p["ffn_g"], residual=dx2, tr=tr, name=f"rms_ffn_bwd{l}")
    g["w_out"] = _matmul_tn(sv["merged"], dx1, tm=1024, tn=1024, ts=ts, name=f"mm_out_dw{l}")
    dmerged = _matmul(dx1, p["w_out"], trans_b=True, tm=tm, tn=1024, out_dtype=F32, name=f"mm_out_dx{l}")
    (dattn, dsgu), g["out_g"], delta, dosq = _rms_bwd(dmerged, [sv["attn"], sv["sgu"]], p["out_g"], head_dots=True, tr=tr,
                                                      name=f"rms_out_bwd{l}")
    dzu, dzv, g["w_s"], g["ln_g"], g["ln_b"], dbs = _sgu_bwd(dsgu, sv["h"], p["ln_g"], p["ln_b"], p["w_s"], p["bias_tile"],
                                                           tr=_tile(s, 256), name=f"sgu_bwd{l}")
    g["b_s"] = dbs[:, :N_HEADS].T
    dqkv, dc = _attention_bwd(dattn, delta[:, :N_HEADS], dosq, sv["attn_res"], name=f"attn_bwd{l}")
    dfl, dbf = _gates_bwd(dc, sv["h"], p["bf_pad"], tr=_tile(s, 256), name=f"gates_bwd{l}")
    g["b_f"] = dbf[0, :N_HEADS]
    dh = jnp.concatenate([dqkv, dzu, dzv, dfl], axis=1)
    g["w_in"] = _matmul_tn(sv["xn"], dh, tm=1024, tn=896, ts=ts, name=f"mm_in_dw{l}")
    dxn = _matmul(dh, p["w_in"], trans_b=True, tm=tm, tn=1024, out_dtype=F32, name=f"mm_in_dx{l}")
    (dx,), g["mix_g"] = _rms_bwd(dxn, [sv["x"]], p["mix_g"], residual=dx1, tr=tr, name=f"rms_mix_bwd{l}")
    return dx, g


def _layer_params(l, w_in_pad, w_out, w_gu, w_down, mix_norm_g, b_f, sgu_ln_g, sgu_ln_b, w_s, b_s, out_norm_g, ffn_norm_g):
    return dict(
        w_in=w_in_pad[l], w_out=w_out[l], w_gu=w_gu[l], w_down=w_down[l],
        mix_g=mix_norm_g[l][None, :], out_g=out_norm_g[l][None, :], ffn_g=ffn_norm_g[l][None, :],
        bf_pad=jnp.pad(b_f[l], (0, LANES - N_HEADS))[None, :],
        ln_g=sgu_ln_g[l][None, :], ln_b=sgu_ln_b[l][None, :], w_s=w_s[l],
        bias_tile=jnp.repeat(b_s[l].T, 64, axis=1),
    )


def _local_step(x, tgt, w_in_pad, w_out, w_gu, w_down, mix_norm_g, b_f, sgu_ln_g, sgu_ln_b, w_s, b_s, out_norm_g,
                ffn_norm_g, final_norm_g):
    depth = w_in_pad.shape[0]
    s = x.shape[0]
    params = [_layer_params(l, w_in_pad, w_out, w_gu, w_down, mix_norm_g, b_f, sgu_ln_g, sgu_ln_b, w_s, b_s, out_norm_g,
                            ffn_norm_g) for l in range(depth)]
    saved = []
    for l in range(depth):
        x, sv = _layer_fwd(x, params[l], l)
        saved.append(sv)
    loss_tile, dx, dfinal = _loss_head(x, tgt, final_norm_g[None, :], tr=_tile(s, 512), name="loss_head")
    grads = [None] * depth
    for l in reversed(range(depth)):
        dx, grads[l] = _layer_bwd(dx, params[l], saved[l], l)
    return loss_tile[0, 0], dx, grads, dfinal[0]


SMALL_ROWS = 4272


def kernel(x, mix_norm_g, w_in, b_f, sgu_ln_g, sgu_ln_b, w_s, b_s, out_norm_g, w_out, ffn_norm_g, w_gate_up, w_down, final_norm_g, loss_target, m_mix_norm_g, m_w_in, m_b_f, m_sgu_ln_g, m_sgu_ln_b, m_w_s, m_b_s, m_out_norm_g, m_w_out, m_ffn_norm_g, m_w_gate_up, m_w_down, m_final_norm_g, v_mix_norm_g, v_w_in, v_b_f, v_sgu_ln_g, v_sgu_ln_b, v_w_s, v_b_s, v_out_norm_g, v_w_out, v_ffn_norm_g, v_w_gate_up, v_w_down, v_final_norm_g):
    big = [w_in, w_out, w_gate_up, w_down]
    big_shapes = [a.shape for a in big]
    small = [mix_norm_g, b_f, sgu_ln_g, sgu_ln_b, w_s, b_s, out_norm_g, ffn_norm_g, final_norm_g]
    small_shapes = [a.shape for a in small]

    gathered = _allgather_chips(_flatten([a.astype(BF16) for a in big]), name="ag_weights")
    per_chip = [_unflatten(gathered[j], big_shapes) for j in range(4)]
    w_in_full = _pad_w_in(jnp.concatenate([pc[0] for pc in per_chip], axis=2))
    w_out_full = jnp.concatenate([pc[1] for pc in per_chip], axis=1)
    w_gu_full = jnp.concatenate([pc[2] for pc in per_chip], axis=2)
    w_down_full = jnp.concatenate([pc[3] for pc in per_chip], axis=1)

    loss_part, dx, grads, dfinal = _local_step(
        x[0], loss_target[0], w_in_full, w_out_full, w_gu_full, w_down_full, mix_norm_g, b_f, sgu_ln_g, sgu_ln_b, w_s, b_s,
        out_norm_g, ffn_norm_g, final_norm_g)
    stack = lambda key: jnp.stack([g[key] for g in grads])
    g_in = _unpad_w_in(stack("w_in"))
    g_out, g_gu, g_down = stack("w_out"), stack("w_gu"), stack("w_down")

    send = jnp.stack([_flatten([g_in[:, :, 642 * j:642 * (j + 1)], g_out[:, 256 * j:256 * (j + 1), :],
                                g_gu[:, :, 1408 * j:1408 * (j + 1)], g_down[:, 704 * j:704 * (j + 1), :]]) for j in range(4)])
    theirs = _pair_split(send, name="rs_pair_split")
    pair_sum = _pair_sum(send, theirs, lax.axis_index("c").astype(jnp.int32).reshape(1), tr=3440, name="rs_pair_sum")
    from_chips = _scatter_chips(pair_sum, name="rs_scatter")
    half = _sum_slots(from_chips, tr=1120, name="rs_chip_sum")
    g_big_flat = _pair_join(half, name="rs_pair_join")

    g_small_local = [stack("mix_g")[:, 0], stack("b_f"), stack("ln_g")[:, 0], stack("ln_b")[:, 0], stack("w_s"), stack("b_s"),
                     stack("out_g")[:, 0], stack("ffn_g")[:, 0], dfinal]
    g_small_flat = _sum_slots(_allgather_all(_flatten(g_small_local, SMALL_ROWS), name="ar_small_gather"), tr=1424,
                              name="ar_small_sum")
    loss = lax.psum(loss_part, ("x", "y", "c"))

    d_big, m_big, v_big = _adamw(_flatten(big), g_big_flat, _flatten([m_w_in, m_w_out, m_w_gate_up, m_w_down]),
                                 _flatten([v_w_in, v_w_out, v_w_gate_up, v_w_down]), tr=2240, name="adamw_big")
    m_small = [m_mix_norm_g, m_b_f, m_sgu_ln_g, m_sgu_ln_b, m_w_s, m_b_s, m_out_norm_g, m_ffn_norm_g, m_final_norm_g]
    v_small = [v_mix_norm_g, v_b_f, v_sgu_ln_g, v_sgu_ln_b, v_w_s, v_b_s, v_out_norm_g, v_ffn_norm_g, v_final_norm_g]
    d_small, m_small2, v_small2 = _adamw(_flatten(small, SMALL_ROWS), g_small_flat, _flatten(m_small, SMALL_ROWS),
                                         _flatten(v_small, SMALL_ROWS), tr=1424, name="adamw_small")

    def in_order(big_flat, small_flat):
        b_in, b_out, b_gu, b_down = _unflatten(big_flat, big_shapes)
        s_mix, s_bf, s_lng, s_lnb, s_ws, s_bs, s_outg, s_ffn, s_fin = _unflatten(small_flat, small_shapes)
        return [s_mix, b_in, s_bf, s_lng, s_lnb, s_ws, s_bs, s_outg, b_out, s_ffn, b_gu, b_down, s_fin]

    return (loss, dx[None], *in_order(g_big_flat, g_small_flat), *in_order(d_big, d_small), *in_order(m_big, m_small2),
            *in_order(v_big, v_small2))
```
